```python
import math
import jax
import jax.numpy as jnp
from jax import lax
import numpy as np

D_MODEL = 2048
BATCH = 1
SEQ = 8192
DEPTH = 4

N_MIXERS = 4
NORM_EPS = 1e-6
NEG_INF = -1e30

LRU_WIDTH = D_MODEL
LRU_BLOCKS = 16
LRU_BLOCK_W = LRU_WIDTH // LRU_BLOCKS
CONV_W = 4
CONV_PAD = (2, 1)
RGLRU_C = 8.0

HEAD_DIM = 128
ROT_DIM = HEAD_DIM // 4
ROPE_THETA = 500000.0
Q_BLOCK = 128

DIFF_HEADS = 8
DIFF_V_DIM = 2 * HEAD_DIM
DIFF_QK_WIDTH = DIFF_HEADS * 2 * HEAD_DIM

WIN_Q_HEADS = 16
WIN_KV_HEADS = 4
WIN_GROUP = WIN_Q_HEADS // WIN_KV_HEADS
WINDOW = 128
WIN_BLOCK = WINDOW

SSM_WIDTH = D_MODEL // 2
SSM_GROUP_CH = 16
SSM_GROUPS = SSM_WIDTH // SSM_GROUP_CH
SSM_STATE = 64

MOE_GROUPS = 4
EXPERTS_PER_GROUP = 8
N_EXPERTS = MOE_GROUPS * EXPERTS_PER_GROUP
TOP_K = 2
D_EXPERT = D_MODEL // 4
EXPERT_BLOCK = 128

kernel_name = 'hybrid_interleaved_bidir_encoder'

F32 = jnp.float32


def rms_norm(x, g):
    xf = x.astype(F32)
    y = xf * lax.rsqrt(jnp.mean(xf * xf, axis=-1, keepdims=True) + NORM_EPS)
    return (y * g.astype(F32)).astype(x.dtype)


def rope_tables(positions):
    inv = ROPE_THETA ** (-jnp.arange(0, ROT_DIM, 2, dtype=F32) / ROT_DIM)
    ang = positions.astype(F32)[..., None] * inv
    return jnp.cos(ang), jnp.sin(ang)


def partial_rope(x, cos, sin):
    half = ROT_DIM // 2
    shp = cos.shape[:2] + (1,) * (x.ndim - 3) + (half,)
    c = cos.reshape(shp).astype(x.dtype)
    s = sin.reshape(shp).astype(x.dtype)
    x1 = x[..., :half]
    x2 = x[..., half:ROT_DIM]
    return jnp.concatenate([x1 * c - x2 * s, x2 * c + x1 * s, x[..., ROT_DIM:]], axis=-1)


def _flip(t):
    return jnp.flip(t, axis=1)


def _lin_comb(l, r):
    return (l[0] * r[0], r[0] * l[1] + r[1])


def _cplx_comb(l, r):
    lar, lai, lbr, lbi = l
    rar, rai, rbr, rbi = r
    return (lar * rar - lai * rai,
            lar * rai + lai * rar,
            rar * lbr - rai * lbi + rbr,
            rar * lbi + rai * lbr + rbi)


def rglru_direction(xr, w_a, w_x, b_a, b_x, lam):
    b_, s_, w_ = xr.shape
    xh = xr.reshape(b_, s_, LRU_BLOCKS, LRU_BLOCK_W)
    r = jax.nn.sigmoid(jnp.einsum('bshi,hij->bshj', xh, w_a.astype(F32)).reshape(b_, s_, w_) + b_a.astype(F32))
    i = jax.nn.sigmoid(jnp.einsum('bshi,hij->bshj', xh, w_x.astype(F32)).reshape(b_, s_, w_) + b_x.astype(F32))
    log_a = -RGLRU_C * r * jax.nn.softplus(-lam.astype(F32))
    a = jnp.exp(log_a)
    b = jnp.sqrt(-jnp.expm1(2.0 * log_a)) * (i * xr)
    return lax.associative_scan(_lin_comb, (a, b), axis=1)[1]


def rglru_mixer(h, w_in, conv_w, conv_b, w_a, w_x, b_a, b_x, lam, w_out):
    proj = h @ w_in
    gate, xr = jnp.split(proj, 2, axis=-1)
    xr = lax.conv_general_dilated(xr, conv_w[:, None, :], window_strides=(1,), padding=[CONV_PAD],
                                  dimension_numbers=('NWC', 'WIO', 'NWC'),
                                  feature_group_count=LRU_WIDTH) + conv_b
    xr = xr.astype(F32)
    y = (rglru_direction(xr, w_a[0], w_x[0], b_a[0], b_x[0], lam[0])
         + _flip(rglru_direction(_flip(xr), w_a[1], w_x[1], b_a[1], b_x[1], lam[1])))
    return (y.astype(h.dtype) * jax.nn.gelu(gate)) @ w_out


def diff_attn_mixer(h, cos, sin, w_qkv, lq1, lk1, lq2, lk2, subln_g, w_out, lambda_init):
    b_, s_, _ = h.shape
    q, k, v = jnp.split(h @ w_qkv, [DIFF_QK_WIDTH, 2 * DIFF_QK_WIDTH], axis=-1)
    q = partial_rope(q.reshape(b_, s_, DIFF_HEADS, 2, HEAD_DIM), cos, sin)
    k = partial_rope(k.reshape(b_, s_, DIFF_HEADS, 2, HEAD_DIM), cos, sin)
    v = v.reshape(b_, s_, DIFF_HEADS, DIFF_V_DIM)
    lam = (jnp.exp(jnp.sum(lq1.astype(F32) * lk1.astype(F32)))
           - jnp.exp(jnp.sum(lq2.astype(F32) * lk2.astype(F32))) + lambda_init)
    scale = HEAD_DIM ** -0.5
    n_qb = s_ // Q_BLOCK
    qb = jnp.moveaxis(q.reshape(b_, n_qb, Q_BLOCK, DIFF_HEADS, 2, HEAD_DIM), 1, 0)

    def block(qblk):
        s = jnp.einsum('bqhcd,bkhcd->bhcqk', qblk, k).astype(F32) * scale
        p = jax.nn.softmax(s, axis=-1)
        wdiff = p[:, :, 0] - lam * p[:, :, 1]
        return jnp.einsum('bhqk,bkhe->bqhe', wdiff.astype(v.dtype), v)

    o = jnp.moveaxis(lax.map(block, qb), 0, 1).reshape(b_, s_, DIFF_HEADS, DIFF_V_DIM)
    o = rms_norm(o, subln_g) * (1.0 - lambda_init)
    return o.reshape(b_, s_, DIFF_HEADS * DIFF_V_DIM) @ w_out


def window_gqa_mixer(h, cos, sin, w_qkv, sink, w_out):
    b_, s_, _ = h.shape
    qd = WIN_Q_HEADS * HEAD_DIM
    kd = WIN_KV_HEADS * HEAD_DIM
    q, k, v = jnp.split(h @ w_qkv, [qd, qd + kd], axis=-1)
    q = partial_rope(q.reshape(b_, s_, WIN_KV_HEADS, WIN_GROUP, HEAD_DIM), cos, sin)
    k = partial_rope(k.reshape(b_, s_, WIN_KV_HEADS, HEAD_DIM), cos, sin)
    v = v.reshape(b_, s_, WIN_KV_HEADS, HEAD_DIM)
    nb = s_ // WIN_BLOCK
    qb = q.reshape(b_, nb, WIN_BLOCK, WIN_KV_HEADS, WIN_GROUP, HEAD_DIM)

    def band(t):
        tp = jnp.pad(t, ((0, 0), (WIN_BLOCK, WIN_BLOCK), (0, 0), (0, 0)))
        tp = tp.reshape(b_, nb + 2, WIN_BLOCK, WIN_KV_HEADS, HEAD_DIM)
        return jnp.concatenate([tp[:, :-2], tp[:, 1:-1], tp[:, 2:]], axis=2)

    kb, vb = band(k), band(v)
    s = jnp.einsum('bnqkgd,bnmkd->bnkgqm', qb, kb).astype(F32) * (HEAD_DIM ** -0.5)
    qi = jnp.arange(WIN_BLOCK)[:, None]
    mi = jnp.arange(3 * WIN_BLOCK)[None, :]
    rel = mi - WIN_BLOCK - qi
    key_pos = jnp.arange(nb)[:, None, None] * WIN_BLOCK - WIN_BLOCK + mi[None]
    valid = (jnp.abs(rel) <= WINDOW)[None] & (key_pos >= 0) & (key_pos < s_)
    s = jnp.where(valid[None, :, None, None], s, NEG_INF)
    sink_l = sink.astype(F32).reshape(1, 1, WIN_KV_HEADS, WIN_GROUP, 1, 1)
    m = jnp.maximum(jnp.max(s, axis=-1, keepdims=True), sink_l)
    e = jnp.exp(s - m)
    p = e / (jnp.sum(e, axis=-1, keepdims=True) + jnp.exp(sink_l - m))
    o = jnp.einsum('bnkgqm,bnmkd->bnqkgd', p.astype(vb.dtype), vb)
    return o.reshape(b_, s_, qd) @ w_out


def s5_direction(u, a_re, a_im, log_dt, b_re, b_im, c_re, c_im):
    a_re, a_im = a_re.astype(F32), a_im.astype(F32)
    b_re, b_im = b_re.astype(F32), b_im.astype(F32)
    dt = jnp.exp(log_dt.astype(F32))[:, None]
    mag = jnp.exp(dt * a_re)
    lr, li = mag * jnp.cos(dt * a_im), mag * jnp.sin(dt * a_im)
    den = a_re * a_re + a_im * a_im
    nr, ni = lr - 1.0, li
    fr = (nr * a_re + ni * a_im) / den
    fi = (ni * a_re - nr * a_im) / den
    bbr = fr[..., None] * b_re - fi[..., None] * b_im
    bbi = fr[..., None] * b_im + fi[..., None] * b_re
    xr = jnp.einsum('gnc,bsgc->bsgn', bbr, u)
    xi = jnp.einsum('gnc,bsgc->bsgn', bbi, u)
    ar = jnp.broadcast_to(lr, xr.shape)
    ai = jnp.broadcast_to(li, xr.shape)
    _, _, sr, si = lax.associative_scan(_cplx_comb, (ar, ai, xr, xi), axis=1)
    return (jnp.einsum('gcn,bsgn->bsgc', c_re.astype(F32), sr)
            - jnp.einsum('gcn,bsgn->bsgc', c_im.astype(F32), si))


def s5_mixer(h, w_in, a_re, a_im, log_dt, b_re, b_im, c_re, c_im, d_skip, w_glu, w_out):
    b_, s_, _ = h.shape
    u = (h @ w_in).astype(F32)
    ug = u.reshape(b_, s_, SSM_GROUPS, SSM_GROUP_CH)
    y = (s5_direction(ug, a_re[0], a_im[0], log_dt[0], b_re[0], b_im[0], c_re[0], c_im[0])
         + _flip(s5_direction(_flip(ug), a_re[1], a_im[1], log_dt[1], b_re[1], b_im[1], c_re[1], c_im[1])))
    y = y.reshape(b_, s_, SSM_WIDTH) + d_skip.astype(F32) * u
    g = jax.nn.gelu(y).astype(h.dtype)
    z = g * jax.nn.sigmoid(g @ w_glu)
    return z @ w_out


def hier_moe(x2d, w_group, w_expert, w_gate, w_up, w_down):
    n_tok, d = x2d.shape
    g_prob = jax.nn.softmax((x2d @ w_group).astype(F32), axis=-1)
    g_p, g_idx = lax.top_k(g_prob, 1)
    e_logits = (x2d @ w_expert).astype(F32).reshape(n_tok, MOE_GROUPS, EXPERTS_PER_GROUP)
    e_sel = e_logits[jnp.arange(n_tok), g_idx[:, 0]]
    e_p, e_idx = lax.top_k(jax.nn.softmax(e_sel, axis=-1), TOP_K)
    e_p = e_p / jnp.sum(e_p, axis=-1, keepdims=True)
    weights = (g_p * e_p).reshape(-1)
    expert = (g_idx * EXPERTS_PER_GROUP + e_idx).reshape(-1).astype(jnp.int32)
    n_asg = n_tok * TOP_K
    tok = jnp.arange(n_asg, dtype=jnp.int32) // TOP_K
    order = jnp.argsort(expert, stable=True)
    se, stok, sw = expert[order], tok[order], weights[order]
    counts = jax.ops.segment_sum(jnp.ones_like(expert), expert, num_segments=N_EXPERTS)
    starts = jnp.cumsum(counts) - counts
    padded = ((counts + EXPERT_BLOCK - 1) // EXPERT_BLOCK) * EXPERT_BLOCK
    pends = jnp.cumsum(padded)
    pstarts = pends - padded
    dest = pstarts[se] + jnp.arange(n_asg, dtype=jnp.int32) - starts[se]
    n_rows = -(-(n_asg + N_EXPERTS * (EXPERT_BLOCK - 1)) // EXPERT_BLOCK) * EXPERT_BLOCK
    n_blk = n_rows // EXPERT_BLOCK
    row_tok = jnp.full((n_rows,), n_tok, jnp.int32).at[dest].set(stok)
    row_w = jnp.zeros((n_rows,), F32).at[dest].set(sw)
    blk_start = jnp.arange(n_blk, dtype=jnp.int32) * EXPERT_BLOCK
    blk_e = jnp.minimum(jnp.searchsorted(pends, blk_start, side='right'), N_EXPERTS - 1)
    x_pad = jnp.concatenate([x2d, jnp.zeros((1, d), x2d.dtype)], axis=0)
    xs = x_pad[row_tok].reshape(n_blk, EXPERT_BLOCK, d)

    def expert_block(args):
        xb, e = args
        hdn = jax.nn.silu(xb @ w_gate[e]) * (xb @ w_up[e])
        return hdn @ w_down[e]

    ys = lax.map(expert_block, (xs, blk_e)).reshape(n_rows, d)
    out = jnp.zeros((n_tok + 1, d), F32).at[row_tok].add(ys.astype(F32) * row_w[:, None])
    return out[:n_tok].astype(x2d.dtype)


def _normal(k, shape, scale):
    return jax.random.normal(k, shape, F32) * scale


def setup_inputs(seed: int = 0) -> dict:
    key = jax.random.key(seed)
    ks = list(jax.random.split(key, 48))
    n_a = (DEPTH + 3) // 4
    n_b = (DEPTH + 2) // 4
    n_c = (DEPTH + 1) // 4
    n_d = DEPTH // 4
    d = D_MODEL
    x = jax.random.normal(ks[0], (BATCH, SEQ, d), F32)
    positions = jnp.broadcast_to(jnp.arange(SEQ, dtype=jnp.int32), (BATCH, SEQ))
    ln_mix = 1.0 + _normal(ks[1], (DEPTH, d), 0.01)
    ln_ffn = 1.0 + _normal(ks[2], (DEPTH, d), 0.01)
    ln_final = 1.0 + _normal(ks[3], (d,), 0.01)
    lru_w_in = _normal(ks[4], (n_a, d, 2 * LRU_WIDTH), d ** -0.5)
    lru_conv_w = _normal(ks[5], (n_a, CONV_W, LRU_WIDTH), CONV_W ** -0.5)
    lru_conv_b = _normal(ks[6], (n_a, LRU_WIDTH), 0.01)
    lru_w_a = _normal(ks[7], (n_a, 2, LRU_BLOCKS, LRU_BLOCK_W, LRU_BLOCK_W), LRU_BLOCK_W ** -0.5)
    lru_w_x = _normal(ks[8], (n_a, 2, LRU_BLOCKS, LRU_BLOCK_W, LRU_BLOCK_W), LRU_BLOCK_W ** -0.5)
    lru_b_a = _normal(ks[9], (n_a, 2, LRU_WIDTH), 0.1)
    lru_b_x = _normal(ks[10], (n_a, 2, LRU_WIDTH), 0.1)
    a_pow = jax.random.uniform(ks[11], (n_a, 2, LRU_WIDTH), F32, 0.9, 0.999) ** (1.0 / RGLRU_C)
    lru_lambda = jnp.log(a_pow) - jnp.log1p(-a_pow)
    lru_w_out = _normal(ks[12], (n_a, LRU_WIDTH, d), LRU_WIDTH ** -0.5)
    diff_w_qkv = _normal(ks[13], (n_b, d, 3 * DIFF_QK_WIDTH), d ** -0.5)
    diff_lq1 = _normal(ks[14], (n_b, HEAD_DIM), 0.1)
    diff_lk1 = _normal(ks[15], (n_b, HEAD_DIM), 0.1)
    diff_lq2 = _normal(ks[16], (n_b, HEAD_DIM), 0.1)
    diff_lk2 = _normal(ks[17], (n_b, HEAD_DIM), 0.1)
    diff_subln = 1.0 + _normal(ks[18], (n_b, DIFF_V_DIM), 0.01)
    diff_w_out = _normal(ks[19], (n_b, DIFF_HEADS * DIFF_V_DIM, d), (DIFF_HEADS * DIFF_V_DIM) ** -0.5)
    win_w_qkv = _normal(ks[20], (n_c, d, (WIN_Q_HEADS + 2 * WIN_KV_HEADS) * HEAD_DIM), d ** -0.5)
    win_sink = _normal(ks[21], (n_c, WIN_Q_HEADS), 0.5)
    win_w_out = _normal(ks[22], (n_c, WIN_Q_HEADS * HEAD_DIM, d), (WIN_Q_HEADS * HEAD_DIM) ** -0.5)
    s5_w_in = _normal(ks[23], (n_d, d, SSM_WIDTH), d ** -0.5)
    st_shape = (n_d, 2, SSM_GROUPS, SSM_STATE)
    s5_a_re = -0.5 + _normal(ks[24], st_shape, 0.01)
    s5_a_im = jnp.pi * jnp.arange(SSM_STATE, dtype=F32) + _normal(ks[25], st_shape, 0.01)
    s5_log_dt = jax.random.uniform(ks[26], (n_d, 2, SSM_GROUPS), F32, math.log(1e-3), math.log(1e-1))
    s5_b_re = _normal(ks[27], (n_d, 2, SSM_GROUPS, SSM_STATE, SSM_GROUP_CH), (2 * SSM_GROUP_CH) ** -0.5)
    s5_b_im = _normal(ks[28], (n_d, 2, SSM_GROUPS, SSM_STATE, SSM_GROUP_CH), (2 * SSM_GROUP_CH) ** -0.5)
    s5_c_re = _normal(ks[29], (n_d, 2, SSM_GROUPS, SSM_GROUP_CH, SSM_STATE), 0.5)
    s5_c_im = _normal(ks[30], (n_d, 2, SSM_GROUPS, SSM_GROUP_CH, SSM_STATE), 0.5)
    s5_d = 1.0 + _normal(ks[31], (n_d, SSM_WIDTH), 0.1)
    s5_w_glu = _normal(ks[32], (n_d, SSM_WIDTH, SSM_WIDTH), SSM_WIDTH ** -0.5)
    s5_w_out = _normal(ks[33], (n_d, SSM_WIDTH, d), SSM_WIDTH ** -0.5)
    moe_w_group = _normal(ks[34], (DEPTH, d, MOE_GROUPS), d ** -0.5)
    moe_w_expert = _normal(ks[35], (DEPTH, d, N_EXPERTS), d ** -0.5)
    moe_w_gate = _normal(ks[36], (DEPTH, N_EXPERTS, d, D_EXPERT), d ** -0.5)
    moe_w_up = _normal(ks[37], (DEPTH, N_EXPERTS, d, D_EXPERT), d ** -0.5)
    moe_w_down = _normal(ks[38], (DEPTH, N_EXPERTS, D_EXPERT, d), D_EXPERT ** -0.5)
    return {'x': x, 'positions': positions, 'ln_mix': ln_mix, 'ln_ffn': ln_ffn, 'ln_final': ln_final,
            'lru_w_in': lru_w_in, 'lru_conv_w': lru_conv_w, 'lru_conv_b': lru_conv_b,
            'lru_w_a': lru_w_a, 'lru_w_x': lru_w_x, 'lru_b_a': lru_b_a, 'lru_b_x': lru_b_x,
            'lru_lambda': lru_lambda, 'lru_w_out': lru_w_out,
            'diff_w_qkv': diff_w_qkv, 'diff_lq1': diff_lq1, 'diff_lk1': diff_lk1,
            'diff_lq2': diff_lq2, 'diff_lk2': diff_lk2, 'diff_subln': diff_subln, 'diff_w_out': diff_w_out,
            'win_w_qkv': win_w_qkv, 'win_sink': win_sink, 'win_w_out': win_w_out,
            's5_w_in': s5_w_in, 's5_a_re': s5_a_re, 's5_a_im': s5_a_im, 's5_log_dt': s5_log_dt,
            's5_b_re': s5_b_re, 's5_b_im': s5_b_im, 's5_c_re': s5_c_re, 's5_c_im': s5_c_im,
            's5_d': s5_d, 's5_w_glu': s5_w_glu, 's5_w_out': s5_w_out,
            'moe_w_group': moe_w_group, 'moe_w_expert': moe_w_expert, 'moe_w_gate': moe_w_gate,
            'moe_w_up': moe_w_up, 'moe_w_down': moe_w_down}


def reference(x, positions, ln_mix, ln_ffn, ln_final,
              lru_w_in, lru_conv_w, lru_conv_b, lru_w_a, lru_w_x, lru_b_a, lru_b_x, lru_lambda, lru_w_out,
              diff_w_qkv, diff_lq1, diff_lk1, diff_lq2, diff_lk2, diff_subln, diff_w_out,
              win_w_qkv, win_sink, win_w_out,
              s5_w_in, s5_a_re, s5_a_im, s5_log_dt, s5_b_re, s5_b_im, s5_c_re, s5_c_im, s5_d, s5_w_glu, s5_w_out,
              moe_w_group, moe_w_expert, moe_w_gate, moe_w_up, moe_w_down):
    cos, sin = rope_tables(positions)
    for i in range(DEPTH):
        kind, j = i % N_MIXERS, i // N_MIXERS
        h = rms_norm(x, ln_mix[i])
        if kind == 0:
            y = rglru_mixer(h, lru_w_in[j], lru_conv_w[j], lru_conv_b[j], lru_w_a[j], lru_w_x[j],
                            lru_b_a[j], lru_b_x[j], lru_lambda[j], lru_w_out[j])
        elif kind == 1:
            y = diff_attn_mixer(h, cos, sin, diff_w_qkv[j], diff_lq1[j], diff_lk1[j], diff_lq2[j],
                                diff_lk2[j], diff_subln[j], diff_w_out[j],
                                0.8 - 0.6 * math.exp(-0.3 * i))
        elif kind == 2:
            y = window_gqa_mixer(h, cos, sin, win_w_qkv[j], win_sink[j], win_w_out[j])
        else:
            y = s5_mixer(h, s5_w_in[j], s5_a_re[j], s5_a_im[j], s5_log_dt[j], s5_b_re[j], s5_b_im[j],
                         s5_c_re[j], s5_c_im[j], s5_d[j], s5_w_glu[j], s5_w_out[j])
        x = x + y.astype(x.dtype)
        h = rms_norm(x, ln_ffn[i])
        moe_out = hier_moe(h.reshape(-1, D_MODEL), moe_w_group[i], moe_w_expert[i],
                           moe_w_gate[i], moe_w_up[i], moe_w_down[i])
        x = x + moe_out.reshape(x.shape).astype(x.dtype)
    return rms_norm(x, ln_final)
```

```python
import functools
import math

import jax
import jax.numpy as jnp
from jax import lax
from jax.experimental import pallas as pl
from jax.experimental.pallas import tpu as pltpu

F32 = jnp.float32
BF16 = jnp.bfloat16

NORM_EPS = 1e-6
NEG_INF = -1e30
LANES = 128
SUBLANES = 8
VMEM_LIMIT = 56 * 1024 * 1024

HEAD_DIM = 128
ROT_DIM = HEAD_DIM // 4
ROPE_THETA = 500000.0
RGLRU_C = 8.0
CONV_W = 4
LRU_BLOCK_W = 128
WINDOW = 128
DIFF_HEADS = 8
WIN_Q_HEADS = 16
WIN_KV_HEADS = 4
SSM_GROUP_CH = 16
SSM_STATE = 64
MOE_GROUPS = 4
EXPERTS_PER_GROUP = 8
N_EXPERTS = MOE_GROUPS * EXPERTS_PER_GROUP
TOP_K = 2
EXPERT_BLOCK = 128


def _params(*sem):
    return pltpu.CompilerParams(dimension_semantics=sem, vmem_limit_bytes=VMEM_LIMIT)


def _rms(x, g):
    ms = jnp.mean(x * x, axis=-1, keepdims=True)
    return x * lax.rsqrt(ms + NORM_EPS) * g


def _gelu_tanh(x):
    return 0.5 * x * (1.0 + jnp.tanh(math.sqrt(2.0 / math.pi) * (x + 0.044715 * (x * x * x))))


def _sigmoid(x):
    return 1.0 / (1.0 + jnp.exp(-x))


def _fused_mm_kernel(*refs, n_row, n_vec, n_epi, prologue, epilogue, keep_f32):
    row_refs = refs[:n_row]
    vec_refs = refs[n_row:n_row + n_vec]
    w_ref = refs[n_row + n_vec]
    epi_refs = refs[n_row + n_vec + 1:n_row + n_vec + 1 + n_epi]
    o_ref = refs[n_row + n_vec + 1 + n_epi]
    a_s = refs[n_row + n_vec + 2 + n_epi]
    a32_s = refs[n_row + n_vec + 3 + n_epi] if keep_f32 else None
    j = pl.program_id(1)

    @pl.when(j == 0)
    def _():
        a = prologue([r[...] for r in row_refs], [v[...] for v in vec_refs])
        a_s[...] = a.astype(BF16)
        if keep_f32:
            a32_s[...] = a

    acc = jnp.dot(a_s[...], w_ref[...].astype(BF16), preferred_element_type=F32)
    epilogue(acc, epi_refs, o_ref, j, a32_s)


def _fused_mm(row_inputs, vec_inputs, w, epi_inputs, prologue, epilogue, *, out_dtype, tm, tn,
              k_dim, keep_f32=False, name):
    s = row_inputs[0][0].shape[0]
    n = w.shape[1]
    tm = min(tm, s)
    tn = min(tn, n)
    in_specs = []
    args = []
    for arr, width, cb in row_inputs:
        in_specs.append(pl.BlockSpec((tm, width), lambda i, j, cb=cb: (i, cb)))
        args.append(arr)
    for arr in vec_inputs:
        in_specs.append(pl.BlockSpec(arr.shape, lambda i, j: (0, 0)))
        args.append(arr)
    in_specs.append(pl.BlockSpec((k_dim, tn), lambda i, j: (0, j)))
    args.append(w)
    for arr, width, per_tile in epi_inputs:
        if per_tile:
            in_specs.append(pl.BlockSpec((tm, width), lambda i, j: (i, j)))
        else:
            in_specs.append(pl.BlockSpec((tm, width), lambda i, j: (i, 0)))
        args.append(arr)
    scratch = [pltpu.VMEM((tm, k_dim), BF16)]
    if keep_f32:
        scratch.append(pltpu.VMEM((tm, k_dim), F32))
    kern = functools.partial(_fused_mm_kernel, n_row=len(row_inputs), n_vec=len(vec_inputs),
                             n_epi=len(epi_inputs), prologue=prologue, epilogue=epilogue,
                             keep_f32=keep_f32)
    return pl.pallas_call(
        kern,
        out_shape=jax.ShapeDtypeStruct((s, n), out_dtype),
        grid=(s // tm, n // tn),
        in_specs=in_specs,
        out_specs=pl.BlockSpec((tm, tn), lambda i, j: (i, j)),
        scratch_shapes=scratch,
        compiler_params=_params("parallel", "arbitrary"),
        name=name,
    )(*args)


def _pro_rms(rows, vecs):
    return _rms(rows[0], vecs[0])


def _epi_store(acc, epi_refs, o_ref, j, a32_s):
    o_ref[...] = acc.astype(o_ref.dtype)


def _epi_residual(acc, epi_refs, o_ref, j, a32_s):
    o_ref[...] = (epi_refs[0][...] + acc).astype(o_ref.dtype)


def _make_epi_rope(n_rope_tiles, tn):
    def epi(acc, epi_refs, o_ref, j, a32_s):
        c_ref, s1_ref, s2_ref = epi_refs

        @pl.when(j < n_rope_tiles)
        def _():
            c = c_ref[...]
            s1 = s1_ref[...]
            s2 = s2_ref[...]
            for hh in range(tn // HEAD_DIM):
                xs = acc[:, hh * HEAD_DIM:(hh + 1) * HEAD_DIM]
                rot = (xs * c + pltpu.roll(xs, HEAD_DIM - ROT_DIM // 2, 1) * s1
                       + pltpu.roll(xs, ROT_DIM // 2, 1) * s2)
                o_ref[:, hh * HEAD_DIM:(hh + 1) * HEAD_DIM] = rot.astype(o_ref.dtype)

        @pl.when(j >= n_rope_tiles)
        def _():
            o_ref[...] = acc.astype(o_ref.dtype)

    return epi


def _rope_tables(positions):
    half = ROT_DIM // 2
    inv = ROPE_THETA ** (-jnp.arange(0, ROT_DIM, 2, dtype=F32) / ROT_DIM)
    ang = positions.astype(F32)[:, None] * inv
    cos, sin = jnp.cos(ang), jnp.sin(ang)
    s = positions.shape[0]
    ones = jnp.ones((s, HEAD_DIM - ROT_DIM), F32)
    zeros = jnp.zeros((s, HEAD_DIM - ROT_DIM), F32)
    zh = jnp.zeros((s, half), F32)
    c_tab = jnp.concatenate([cos, cos, ones], axis=1)
    s1_tab = jnp.concatenate([-sin, zh, zeros], axis=1)
    s2_tab = jnp.concatenate([zh, sin, zeros], axis=1)
    return c_tab, s1_tab, s2_tab


def _norm_matmul(x, g, w, *, rope=None, n_rope_cols=0, out_dtype=F32, tm=1024, tn=512, name):
    d = x.shape[1]
    if rope is None:
        epi, epi_inputs = _epi_store, []
    else:
        tn = min(tn, w.shape[1])
        assert n_rope_cols % tn == 0
        epi = _make_epi_rope(n_rope_cols // tn, tn)
        epi_inputs = [(t, HEAD_DIM, False) for t in rope]
    return _fused_mm([(x, d, 0)], [g.reshape(1, d)], w, epi_inputs, _pro_rms, epi,
                     out_dtype=out_dtype, tm=tm, tn=tn, k_dim=d, name=name)


def _matmul_residual(row_inputs, vec_inputs, w, res, prologue, *, k_dim, tm=1024, tn=512, name):
    return _fused_mm(row_inputs, vec_inputs, w, [(res, min(tn, w.shape[1]), True)], prologue,
                     _epi_residual, out_dtype=F32, tm=tm, tn=tn, k_dim=k_dim, name=name)


def _rglru_kernel(xf_ref, xfp_ref, xfn_ref, xb_ref, xbp_ref, xbn_ref, cw_ref, cb_ref, wa_ref,
                  wx_ref, ba_ref, bx_ref, lam_ref, yf_ref, yb_ref,
                  ext_s, af_s, bf_s, ab_s, bb_s, hf_s, hb_s, *, tc, cw, n_t):
    i = pl.program_id(1)
    halo = SUBLANES

    @pl.when(i == 0)
    def _():
        hf_s[...] = jnp.zeros_like(hf_s)
        hb_s[...] = jnp.zeros_like(hb_s)

    def gates(x_ref, xp_ref, xn_ref, chunk, d, a_s, b_s):
        prev = jnp.where(chunk == 0, 0.0, xp_ref[...])
        nxt = jnp.where(chunk == n_t - 1, 0.0, xn_ref[...])
        ext_s[0:halo, :] = prev
        ext_s[halo:halo + tc, :] = x_ref[...]
        ext_s[halo + tc:halo + tc + halo, :] = nxt
        xc = cb_ref[...] + sum(
            cw_ref[k:k + 1, :] * ext_s[halo - 2 + k:halo - 2 + k + tc, :] for k in range(CONV_W))
        lam = lam_ref[d:d + 1, :]
        z = -lam
        sp = jnp.maximum(z, 0.0) + jnp.log1p(jnp.exp(-jnp.abs(z)))
        for blk in range(cw // LRU_BLOCK_W):
            sl = slice(blk * LRU_BLOCK_W, (blk + 1) * LRU_BLOCK_W)
            xb = xc[:, sl]
            xbh = xb.astype(BF16)
            r = _sigmoid(jnp.dot(xbh, wa_ref[d, blk].astype(BF16), preferred_element_type=F32)
                         + ba_ref[d:d + 1, sl])
            ig = _sigmoid(jnp.dot(xbh, wx_ref[d, blk].astype(BF16), preferred_element_type=F32)
                          + bx_ref[d:d + 1, sl])
            log_a = (-RGLRU_C) * r * sp[:, sl]
            a_s[:, sl] = jnp.exp(log_a)
            th = jnp.tanh(log_a)
            b_s[:, sl] = jnp.sqrt(-2.0 * th / (1.0 - th)) * (ig * xb)

    gates(xf_ref, xfp_ref, xfn_ref, i, 0, af_s, bf_s)
    gates(xb_ref, xbp_ref, xbn_ref, n_t - 1 - i, 1, ab_s, bb_s)

    def body(r, carry):
        hf, hb = carry
        hf = af_s[pl.ds(r, 1), :] * hf + bf_s[pl.ds(r, 1), :]
        yf_ref[pl.ds(r, 1), :] = hf
        rb = tc - 1 - r
        hb = ab_s[pl.ds(rb, 1), :] * hb + bb_s[pl.ds(rb, 1), :]
        yb_ref[pl.ds(rb, 1), :] = hb
        return hf, hb

    hf, hb = lax.fori_loop(0, tc, body, (hf_s[...], hb_s[...]), unroll=8)
    hf_s[...] = hf
    hb_s[...] = hb


def _rglru_scan(proj, conv_w, conv_b, w_a, w_x, b_a, b_x, lam, *, tc=256, cw=512):
    s = proj.shape[0]
    c = conv_w.shape[1]
    tc = min(tc, s)
    n_t = s // tc
    n_c = c // cw
    xoff = c // cw
    hb = tc // SUBLANES
    last_h = s // SUBLANES - 1

    specs = [
        pl.BlockSpec((tc, cw), lambda ci, i: (i, xoff + ci)),
        pl.BlockSpec((SUBLANES, cw), lambda ci, i: (jnp.maximum(i * hb - 1, 0), xoff + ci)),
        pl.BlockSpec((SUBLANES, cw), lambda ci, i: (jnp.minimum((i + 1) * hb, last_h), xoff + ci)),
        pl.BlockSpec((tc, cw), lambda ci, i: (n_t - 1 - i, xoff + ci)),
        pl.BlockSpec((SUBLANES, cw),
                     lambda ci, i: (jnp.maximum((n_t - 1 - i) * hb - 1, 0), xoff + ci)),
        pl.BlockSpec((SUBLANES, cw),
                     lambda ci, i: (jnp.minimum((n_t - i) * hb, last_h), xoff + ci)),
        pl.BlockSpec((CONV_W, cw), lambda ci, i: (0, ci)),
        pl.BlockSpec((1, cw), lambda ci, i: (0, ci)),
        pl.BlockSpec((2, cw // LRU_BLOCK_W, LRU_BLOCK_W, LRU_BLOCK_W), lambda ci, i: (0, ci, 0, 0)),
        pl.BlockSpec((2, cw // LRU_BLOCK_W, LRU_BLOCK_W, LRU_BLOCK_W), lambda ci, i: (0, ci, 0, 0)),
        pl.BlockSpec((2, cw), lambda ci, i: (0, ci)),
        pl.BlockSpec((2, cw), lambda ci, i: (0, ci)),
        pl.BlockSpec((2, cw), lambda ci, i: (0, ci)),
    ]
    kern = functools.partial(_rglru_kernel, tc=tc, cw=cw, n_t=n_t)
    return pl.pallas_call(
        kern,
        out_shape=(jax.ShapeDtypeStruct((s, c), F32), jax.ShapeDtypeStruct((s, c), F32)),
        grid=(n_c, n_t),
        in_specs=specs,
        out_specs=(pl.BlockSpec((tc, cw), lambda ci, i: (i, ci)),
                   pl.BlockSpec((tc, cw), lambda ci, i: (n_t - 1 - i, ci))),
        scratch_shapes=[pltpu.VMEM((tc + 2 * SUBLANES, cw), F32)]
        + [pltpu.VMEM((tc, cw), F32) for _ in range(4)]
        + [pltpu.VMEM((1, cw), F32) for _ in range(2)],
        compiler_params=_params("parallel", "arbitrary"),
        name="rglru_scan",
    )(proj, proj, proj, proj, proj, proj, conv_w, conv_b.reshape(1, c), w_a, w_x, b_a, b_x, lam)


def _pro_rglru_out(rows, vecs):
    yf, yb, gate = rows
    return (yf + yb) * _gelu_tanh(gate)


def _diff_attn_kernel(q_ref, k_ref, v_ref, lq1_ref, lk1_ref, lq2_ref, lk2_ref, g_ref, o_ref,
                      m_s, l_s, acc_s, *, tq, tk, n_kv, lambda_init):
    scale = HEAD_DIM ** -0.5
    m_s[...] = jnp.full_like(m_s, -jnp.inf)
    l_s[...] = jnp.zeros_like(l_s)
    acc_s[...] = jnp.zeros_like(acc_s)
    q = q_ref[...]
    qs = (q[:, :HEAD_DIM], q[:, HEAD_DIM:])

    def kv_step(j, _):
        off = pl.multiple_of(j * tk, tk)
        kb = k_ref[pl.ds(off, tk), :]
        vb = v_ref[pl.ds(off, tk), :]
        for c in range(2):
            kc = kb[:, c * HEAD_DIM:(c + 1) * HEAD_DIM]
            sc = lax.dot_general(qs[c], kc, (((1,), (1,)), ((), ())),
                                 preferred_element_type=F32) * scale
            m_old = m_s[c]
            m_new = jnp.maximum(m_old, jnp.max(sc, axis=-1, keepdims=True))
            alpha = jnp.exp(m_old - m_new)
            p = jnp.exp(sc - m_new)
            l_s[c] = alpha * l_s[c] + jnp.sum(p, axis=-1, keepdims=True)
            acc_s[c] = alpha * acc_s[c] + jnp.dot(p.astype(BF16), vb, preferred_element_type=F32)
            m_s[c] = m_new
        return 0

    lax.fori_loop(0, n_kv, kv_step, 0)
    lam = (jnp.exp(jnp.sum(lq1_ref[...] * lk1_ref[...], axis=-1, keepdims=True))
           - jnp.exp(jnp.sum(lq2_ref[...] * lk2_ref[...], axis=-1, keepdims=True)) + lambda_init)
    o = acc_s[0] / l_s[0] - lam * (acc_s[1] / l_s[1])
    o_ref[...] = (_rms(o, g_ref[...]) * (1.0 - lambda_init)).astype(o_ref.dtype)


def _diff_attention(qkv, lq1, lk1, lq2, lk2, subln_g, lambda_init, *, tq=512, tk=512):
    s = qkv.shape[0]
    vd = 2 * HEAD_DIM
    tq = min(tq, s)
    tk = min(tk, s)
    kern = functools.partial(_diff_attn_kernel, tq=tq, tk=tk, n_kv=s // tk,
                             lambda_init=lambda_init)
    vec = lambda a: a.reshape(1, -1)
    vspec = lambda w: pl.BlockSpec((1, w), lambda h, i: (0, 0))
    return pl.pallas_call(
        kern,
        out_shape=jax.ShapeDtypeStruct((s, DIFF_HEADS * vd), BF16),
        grid=(DIFF_HEADS, s // tq),
        in_specs=[
            pl.BlockSpec((tq, vd), lambda h, i: (i, h)),
            pl.BlockSpec((s, vd), lambda h, i: (0, DIFF_HEADS + h)),
            pl.BlockSpec((s, vd), lambda h, i: (0, 2 * DIFF_HEADS + h)),
            vspec(HEAD_DIM), vspec(HEAD_DIM), vspec(HEAD_DIM), vspec(HEAD_DIM), vspec(vd),
        ],
        out_specs=pl.BlockSpec((tq, vd), lambda h, i: (i, h)),
        scratch_shapes=[pltpu.VMEM((2, tq, 1), F32), pltpu.VMEM((2, tq, 1), F32),
                        pltpu.VMEM((2, tq, vd), F32)],
        compiler_params=_params("parallel", "arbitrary"),
        name="diff_attention",
    )(qkv, qkv, qkv, vec(lq1), vec(lk1), vec(lq2), vec(lk2), vec(subln_g))


def _pro_identity(rows, vecs):
    return rows[0]


def _win_attn_kernel(sink_ref, q_ref, k_ref, v_ref, o_ref, *, tq, win, s_len, group):
    kvh = pl.program_id(0)
    i = pl.program_id(1)
    scale = HEAD_DIM ** -0.5
    start = jnp.clip(i * tq - WINDOW, 0, s_len - win)
    start = pl.multiple_of(start, WINDOW)
    kw = k_ref[pl.ds(start, win), :]
    vw = v_ref[pl.ds(start, win), :]
    qpos = i * tq + lax.broadcasted_iota(jnp.int32, (tq, win), 0)
    kpos = start + lax.broadcasted_iota(jnp.int32, (tq, win), 1)
    valid = jnp.abs(kpos - qpos) <= WINDOW
    for g in range(group):
        qg = q_ref[:, g * HEAD_DIM:(g + 1) * HEAD_DIM]
        sc = lax.dot_general(qg, kw, (((1,), (1,)), ((), ())), preferred_element_type=F32) * scale
        sc = jnp.where(valid, sc, NEG_INF)
        sink = sink_ref[kvh * group + g]
        m = jnp.maximum(jnp.max(sc, axis=-1, keepdims=True), sink)
        e = jnp.exp(sc - m)
        p = e / (jnp.sum(e, axis=-1, keepdims=True) + jnp.exp(sink - m))
        o = jnp.dot(p.astype(BF16), vw, preferred_element_type=F32)
        o_ref[:, g * HEAD_DIM:(g + 1) * HEAD_DIM] = o.astype(o_ref.dtype)


def _window_attention(qkv, sink, *, tq=256):
    s = qkv.shape[0]
    group = WIN_Q_HEADS // WIN_KV_HEADS
    tq = min(tq, s)
    win = min(tq + 2 * WINDOW, s)
    qw = group * HEAD_DIM
    k0 = WIN_Q_HEADS
    v0 = WIN_Q_HEADS + WIN_KV_HEADS
    kern = functools.partial(_win_attn_kernel, tq=tq, win=win, s_len=s, group=group)
    return pl.pallas_call(
        kern,
        out_shape=jax.ShapeDtypeStruct((s, WIN_Q_HEADS * HEAD_DIM), BF16),
        grid=(WIN_KV_HEADS, s // tq),
        in_specs=[
            pl.BlockSpec(memory_space=pltpu.SMEM),
            pl.BlockSpec((tq, qw), lambda h, i: (i, h)),
            pl.BlockSpec((s, HEAD_DIM), lambda h, i: (0, k0 + h)),
            pl.BlockSpec((s, HEAD_DIM), lambda h, i: (0, v0 + h)),
        ],
        out_specs=pl.BlockSpec((tq, qw), lambda h, i: (i, h)),
        compiler_params=_params("parallel", "arbitrary"),
        name="window_attention",
    )(sink.astype(F32), qkv, qkv, qkv)


def _s5_discretize(a_re, a_im, log_dt, b_re, b_im):
    dt = jnp.exp(log_dt)[:, None]
    mag = jnp.exp(dt * a_re)
    lr, li = mag * jnp.cos(dt * a_im), mag * jnp.sin(dt * a_im)
    den = a_re * a_re + a_im * a_im
    nr, ni = lr - 1.0, li
    fr = (nr * a_re + ni * a_im) / den
    fi = (ni * a_re - nr * a_im) / den
    bbr = fr[..., None] * b_re - fi[..., None] * b_im
    bbi = fr[..., None] * b_im + fi[..., None] * b_re
    return lr, li, bbr, bbi


def _s5_block_weights(lr, li, bbr, bbi, c_re, c_im, gpc):
    g, n, c = bbr.shape
    n_k = g // gpc
    eye = jnp.eye(gpc, dtype=F32)

    def w_in(bb):
        t = bb.reshape(n_k, gpc, n, c)
        return jnp.einsum('kgnc,gh->kgchn', t, eye).reshape(n_k, gpc * c, gpc * n)

    def w_out(cc):
        t = cc.reshape(n_k, gpc, c, n)
        return jnp.einsum('kgcn,gh->khngc', t, eye).reshape(n_k, gpc * n, gpc * c)

    win = jnp.concatenate([w_in(bbr), w_in(bbi)], axis=2)
    wout = jnp.concatenate([w_out(c_re), -w_out(c_im)], axis=1)
    lam = jnp.stack([lr.reshape(n_k, gpc * n), li.reshape(n_k, gpc * n)], axis=1)
    return win, wout, lam


def _s5_kernel(uf_ref, ub_ref, win_ref, wout_ref, lam_ref, yf_ref, yb_ref,
               xf_s, xb_s, st_s, *, tc, ns):
    i = pl.program_id(1)

    @pl.when(i == 0)
    def _():
        st_s[...] = jnp.zeros_like(st_s)

    xf_s[...] = jnp.dot(uf_ref[...].astype(BF16), win_ref[0, 0].astype(BF16),
                        preferred_element_type=F32)
    xb_s[...] = jnp.dot(ub_ref[...].astype(BF16), win_ref[1, 0].astype(BF16),
                        preferred_element_type=F32)
    lrf, lif = lam_ref[0, 0, 0:1, :], lam_ref[0, 0, 1:2, :]
    lrb, lib = lam_ref[1, 0, 0:1, :], lam_ref[1, 0, 1:2, :]

    def body(r, carry):
        srf, sif, srb, sib = carry
        nrf = lrf * srf - lif * sif + xf_s[pl.ds(r, 1), 0:ns]
        nif = lrf * sif + lif * srf + xf_s[pl.ds(r, 1), ns:2 * ns]
        xf_s[pl.ds(r, 1), 0:ns] = nrf
        xf_s[pl.ds(r, 1), ns:2 * ns] = nif
        rb = tc - 1 - r
        nrb = lrb * srb - lib * sib + xb_s[pl.ds(rb, 1), 0:ns]
        nib = lrb * sib + lib * srb + xb_s[pl.ds(rb, 1), ns:2 * ns]
        xb_s[pl.ds(rb, 1), 0:ns] = nrb
        xb_s[pl.ds(rb, 1), ns:2 * ns] = nib
        return nrf, nif, nrb, nib

    init = (st_s[0:1, :], st_s[1:2, :], st_s[2:3, :], st_s[3:4, :])
    srf, sif, srb, sib = lax.fori_loop(0, tc, body, init, unroll=8)
    st_s[0:1, :] = srf
    st_s[1:2, :] = sif
    st_s[2:3, :] = srb
    st_s[3:4, :] = sib
    yf_ref[...] = jnp.dot(xf_s[...].astype(BF16), wout_ref[0, 0].astype(BF16),
                          preferred_element_type=F32)
    yb_ref[...] = jnp.dot(xb_s[...].astype(BF16), wout_ref[1, 0].astype(BF16),
                          preferred_element_type=F32)


def _s5_scan(u, win, wout, lam, *, tc=256):
    s, w = u.shape
    _, n_k, cw, ns2 = win.shape
    ns = ns2 // 2
    tc = min(tc, s)
    n_t = s // tc
    kern = functools.partial(_s5_kernel, tc=tc, ns=ns)
    return pl.pallas_call(
        kern,
        out_shape=(jax.ShapeDtypeStruct((s, w), F32), jax.ShapeDtypeStruct((s, w), F32)),
        grid=(n_k, n_t),
        in_specs=[
            pl.BlockSpec((tc, cw), lambda k, i: (i, k)),
            pl.BlockSpec((tc, cw), lambda k, i: (n_t - 1 - i, k)),
            pl.BlockSpec((2, 1, cw, ns2), lambda k, i: (0, k, 0, 0)),
            pl.BlockSpec((2, 1, ns2, cw), lambda k, i: (0, k, 0, 0)),
            pl.BlockSpec((2, 1, 2, ns), lambda k, i: (0, k, 0, 0)),
        ],
        out_specs=(pl.BlockSpec((tc, cw), lambda k, i: (i, k)),
                   pl.BlockSpec((tc, cw), lambda k, i: (n_t - 1 - i, k))),
        scratch_shapes=[pltpu.VMEM((tc, ns2), F32), pltpu.VMEM((tc, ns2), F32),
                        pltpu.VMEM((4, ns), F32)],
        compiler_params=_params("parallel", "arbitrary"),
        name="s5_scan",
    )(u, u, win, wout, lam)


def _pro_s5_glu(rows, vecs):
    yf, yb, u = rows
    return _gelu_tanh(yf + yb + vecs[0] * u)


def _epi_glu(acc, epi_refs, o_ref, j, a32_s):
    o_ref[...] = (a32_s[...] * _sigmoid(acc)).astype(o_ref.dtype)


def _router_kernel(x_ref, g_ref, w_ref, h_ref, idx_ref, wt_ref):
    h = _rms(x_ref[...], g_ref[...])
    h_ref[...] = h
    logits = jnp.dot(h, w_ref[...], preferred_element_type=F32, precision=lax.Precision.HIGHEST)
    tm = logits.shape[0]
    lane = lax.broadcasted_iota(jnp.int32, (tm, LANES), 1)
    big = jnp.int32(LANES)
    ninf = -jnp.inf
    gl = jnp.where(lane < MOE_GROUPS, logits, ninf)
    gm = jnp.max(gl, axis=-1, keepdims=True)
    ge = jnp.exp(gl - gm)
    g_prob = ge / jnp.sum(ge, axis=-1, keepdims=True)
    g_p = jnp.max(g_prob, axis=-1, keepdims=True)
    g_idx = jnp.min(jnp.where(g_prob == g_p, lane, big), axis=-1, keepdims=True)
    lo = MOE_GROUPS + g_idx * EXPERTS_PER_GROUP
    in_grp = (lane >= lo) & (lane < lo + EXPERTS_PER_GROUP)
    el = jnp.where(in_grp, logits, ninf)
    em = jnp.max(el, axis=-1, keepdims=True)
    ee = jnp.exp(el - em)
    e_prob = jnp.where(in_grp, ee / jnp.sum(ee, axis=-1, keepdims=True), -1.0)
    p1 = jnp.max(e_prob, axis=-1, keepdims=True)
    i1 = jnp.min(jnp.where(e_prob == p1, lane, big), axis=-1, keepdims=True)
    rest = jnp.where(lane == i1, -1.0, e_prob)
    p2 = jnp.max(rest, axis=-1, keepdims=True)
    i2 = jnp.min(jnp.where(rest == p2, lane, big), axis=-1, keepdims=True)
    denom = p1 + p2
    w1 = g_p * (p1 / denom)
    w2 = g_p * (p2 / denom)
    idx_ref[...] = jnp.where(lane == 0, i1 - MOE_GROUPS, jnp.where(lane == 1, i2 - MOE_GROUPS, 0))
    wt_ref[...] = jnp.where(lane == 0, w1, jnp.where(lane == 1, w2, 0.0))


def _router(x, g, w_group, w_expert, *, tm=512):
    s, d = x.shape
    tm = min(tm, s)
    w_r = jnp.concatenate(
        [w_group, w_expert, jnp.zeros((d, LANES - MOE_GROUPS - N_EXPERTS), F32)], axis=1)
    return pl.pallas_call(
        _router_kernel,
        out_shape=(jax.ShapeDtypeStruct((s, d), F32), jax.ShapeDtypeStruct((s, LANES), jnp.int32),
                   jax.ShapeDtypeStruct((s, LANES), F32)),
        grid=(s // tm,),
        in_specs=[pl.BlockSpec((tm, d), lambda i: (i, 0)), pl.BlockSpec((1, d), lambda i: (0, 0)),
                  pl.BlockSpec((d, LANES), lambda i: (0, 0))],
        out_specs=(pl.BlockSpec((tm, d), lambda i: (i, 0)), pl.BlockSpec((tm, LANES), lambda i: (i, 0)),
                   pl.BlockSpec((tm, LANES), lambda i: (i, 0))),
        compiler_params=_params("parallel"),
        name="moe_router",
    )(x, g.reshape(1, d), w_r)


def _dispatch_plan(expert, n_tok):
    n_asg = n_tok * TOP_K
    e_flat = expert.reshape(-1)
    onehot = (e_flat[:, None] == jnp.arange(N_EXPERTS, dtype=jnp.int32)[None, :]).astype(jnp.int32)
    csum = jnp.cumsum(onehot, axis=0)
    rank = jnp.sum(onehot * csum, axis=1) - 1
    counts = csum[-1]
    padded = ((counts + EXPERT_BLOCK - 1) // EXPERT_BLOCK) * EXPERT_BLOCK
    pends = jnp.cumsum(padded)
    pstarts = pends - padded
    dest = (pstarts[e_flat] + rank).astype(jnp.int32)
    n_rows = -(-(n_asg + N_EXPERTS * (EXPERT_BLOCK - 1)) // EXPERT_BLOCK) * EXPERT_BLOCK
    n_blk = n_rows // EXPERT_BLOCK
    tok = jnp.arange(n_asg, dtype=jnp.int32) // TOP_K
    row_tok = jnp.zeros((n_rows,), jnp.int32).at[dest].set(tok)
    blk_start = jnp.arange(n_blk, dtype=jnp.int32) * EXPERT_BLOCK
    blk_e = jnp.minimum(jnp.searchsorted(pends, blk_start, side='right'),
                        N_EXPERTS - 1).astype(jnp.int32)
    return dest, row_tok, blk_e, n_blk


def _row_copy(src_hbm, src_row, dst_buf, slot, dst_row, sem):
    return pltpu.make_async_copy(src_hbm.at[pl.ds(src_row, 1), :],
                                 dst_buf.at[slot, pl.ds(dst_row, 1), :], sem.at[slot])


def _gather_start(idx_ref, base, n, src_hbm, dst_buf, slot, sem):
    def body(r, _):
        _row_copy(src_hbm, idx_ref[base + r], dst_buf, slot, r, sem).start()
        return 0
    lax.fori_loop(0, n, body, 0, unroll=8)


def _gather_wait(n, src_hbm, dst_buf, slot, sem):
    def body(r, _):
        _row_copy(src_hbm, 0, dst_buf, slot, r, sem).wait()
        return 0
    lax.fori_loop(0, n, body, 0, unroll=8)


def _expert_kernel(be_ref, rt_ref, h_hbm, wg_ref, wu_ref, wd_ref, y_ref,
                   xbuf, sem, wg_s, wu_s, wd_s, *, n_blk):
    b = pl.program_id(0)
    slot = b % 2

    @pl.when(b == 0)
    def _():
        _gather_start(rt_ref, 0, EXPERT_BLOCK, h_hbm, xbuf, 0, sem)

    @pl.when(b + 1 < n_blk)
    def _():
        _gather_start(rt_ref, (b + 1) * EXPERT_BLOCK, EXPERT_BLOCK, h_hbm, xbuf, 1 - slot, sem)

    changed = jnp.logical_or(b == 0, be_ref[b] != be_ref[jnp.maximum(b - 1, 0)])

    @pl.when(changed)
    def _():
        wg_s[...] = wg_ref[...].astype(BF16)
        wu_s[...] = wu_ref[...].astype(BF16)
        wd_s[...] = wd_ref[...].astype(BF16)

    _gather_wait(EXPERT_BLOCK, h_hbm, xbuf, slot, sem)
    x = xbuf[slot].astype(BF16)
    hg = jnp.dot(x, wg_s[...], preferred_element_type=F32)
    hu = jnp.dot(x, wu_s[...], preferred_element_type=F32)
    hdn = (hg * _sigmoid(hg)) * hu
    y_ref[...] = jnp.dot(hdn.astype(BF16), wd_s[...], preferred_element_type=F32)


def _expert_mlp(h, row_tok, blk_e, n_blk, w_gate, w_up, w_down):
    s, d = h.shape
    de = w_gate.shape[2]
    kern = functools.partial(_expert_kernel, n_blk=n_blk)
    grid_spec = pltpu.PrefetchScalarGridSpec(
        num_scalar_prefetch=2,
        grid=(n_blk,),
        in_specs=[
            pl.BlockSpec(memory_space=pl.ANY),
            pl.BlockSpec((None, d, de), lambda b, be, rt: (be[b], 0, 0)),
            pl.BlockSpec((None, d, de), lambda b, be, rt: (be[b], 0, 0)),
            pl.BlockSpec((None, de, d), lambda b, be, rt: (be[b], 0, 0)),
        ],
        out_specs=pl.BlockSpec((EXPERT_BLOCK, d), lambda b, be, rt: (b, 0)),
        scratch_shapes=[pltpu.VMEM((2, EXPERT_BLOCK, d), F32), pltpu.SemaphoreType.DMA((2,)),
                        pltpu.VMEM((d, de), BF16), pltpu.VMEM((d, de), BF16),
                        pltpu.VMEM((de, d), BF16)],
    )
    return pl.pallas_call(
        kern,
        out_shape=jax.ShapeDtypeStruct((n_blk * EXPERT_BLOCK, d), F32),
        grid_spec=grid_spec,
        compiler_params=_params("arbitrary"),
        name="moe_experts",
    )(blk_e, row_tok, h, w_gate, w_up, w_down)


def _combine_kernel(dest_ref, x_ref, wt_ref, ys_hbm, g_ref, o_ref, ybuf, sem, *, tb, n_b, final):
    b = pl.program_id(0)
    slot = b % 2
    n = tb * TOP_K

    @pl.when(b == 0)
    def _():
        _gather_start(dest_ref, 0, n, ys_hbm, ybuf, 0, sem)

    @pl.when(b + 1 < n_b)
    def _():
        _gather_start(dest_ref, (b + 1) * n, n, ys_hbm, ybuf, 1 - slot, sem)

    _gather_wait(n, ys_hbm, ybuf, slot, sem)
    y0 = ybuf[slot, 0:tb, :]
    y1 = ybuf[slot, tb:2 * tb, :]
    wt = wt_ref[...]
    out = x_ref[...] + (y0 * wt[:, 0:1] + y1 * wt[:, 1:2])
    if final:
        out = _rms(out, g_ref[...])
    o_ref[...] = out


def _moe_combine(x, wt, ys, dest, g_final, *, final, tb=128):
    s, d = x.shape
    tb = min(tb, s)
    n_b = s // tb
    dest = dest.reshape(n_b, tb, TOP_K).transpose(0, 2, 1).reshape(-1)
    kern = functools.partial(_combine_kernel, tb=tb, n_b=n_b, final=final)
    grid_spec = pltpu.PrefetchScalarGridSpec(
        num_scalar_prefetch=1,
        grid=(n_b,),
        in_specs=[
            pl.BlockSpec((tb, d), lambda b, dr: (b, 0)),
            pl.BlockSpec((tb, LANES), lambda b, dr: (b, 0)),
            pl.BlockSpec(memory_space=pl.ANY),
            pl.BlockSpec((1, d), lambda b, dr: (0, 0)),
        ],
        out_specs=pl.BlockSpec((tb, d), lambda b, dr: (b, 0)),
        scratch_shapes=[pltpu.VMEM((2, tb * TOP_K, d), F32), pltpu.SemaphoreType.DMA((2,))],
    )
    return pl.pallas_call(
        kern,
        out_shape=jax.ShapeDtypeStruct((s, d), F32),
        grid_spec=grid_spec,
        compiler_params=_params("arbitrary"),
        name="moe_combine",
    )(dest, x, wt, ys, g_final.reshape(1, d))


def _hier_moe(x, ln_g, w_group, w_expert, w_gate, w_up, w_down, g_final, *, final):
    s = x.shape[0]
    h, idx, wt = _router(x, ln_g, w_group, w_expert)
    dest, row_tok, blk_e, n_blk = _dispatch_plan(idx[:, :TOP_K], s)
    ys = _expert_mlp(h, row_tok, blk_e, n_blk, w_gate, w_up, w_down)
    return _moe_combine(x, wt, ys, dest, g_final, final=final)


def _rglru_layer(x, ln_g, w_in, conv_w, conv_b, w_a, w_x, b_a, b_x, lam, w_out):
    d = x.shape[1]
    c = conv_w.shape[1]
    proj = _norm_matmul(x, ln_g, w_in, name="lru_in_proj")
    yf, yb = _rglru_scan(proj, conv_w, conv_b, w_a, w_x, b_a, b_x, lam)
    return _matmul_residual([(yf, c, 0), (yb, c, 0), (proj, c, 0)], [], w_out, x, _pro_rglru_out,
                            k_dim=c, tm=512, name="lru_out_proj")


def _diff_layer(x, ln_g, rope, w_qkv, lq1, lk1, lq2, lk2, subln_g, w_out, lambda_init):
    qk_cols = 2 * DIFF_HEADS * 2 * HEAD_DIM
    qkv = _norm_matmul(x, ln_g, w_qkv, rope=rope, n_rope_cols=qk_cols, out_dtype=BF16,
                       name="diff_qkv_proj")
    o = _diff_attention(qkv, lq1, lk1, lq2, lk2, subln_g, lambda_init)
    return _matmul_residual([(o, o.shape[1], 0)], [], w_out, x, _pro_identity,
                            k_dim=o.shape[1], name="diff_out_proj")


def _window_layer(x, ln_g, rope, w_qkv, sink, w_out):
    qk_cols = (WIN_Q_HEADS + WIN_KV_HEADS) * HEAD_DIM
    qkv = _norm_matmul(x, ln_g, w_qkv, rope=rope, n_rope_cols=qk_cols, out_dtype=BF16,
                       name="win_qkv_proj")
    o = _window_attention(qkv, sink)
    return _matmul_residual([(o, o.shape[1], 0)], [], w_out, x, _pro_identity,
                            k_dim=o.shape[1], name="win_out_proj")


def _s5_layer(x, ln_g, w_in, a_re, a_im, log_dt, b_re, b_im, c_re, c_im, d_skip, w_glu, w_out,
              *, gpc=8):
    u = _norm_matmul(x, ln_g, w_in, name="s5_in_proj")
    w = u.shape[1]
    wins, wouts, lams = [], [], []
    for dd in range(2):
        lr, li, bbr, bbi = _s5_discretize(a_re[dd], a_im[dd], log_dt[dd], b_re[dd], b_im[dd])
        wi, wo, lm = _s5_block_weights(lr, li, bbr, bbi, c_re[dd], c_im[dd], gpc)
        wins.append(wi)
        wouts.append(wo)
        lams.append(lm)
    yf, yb = _s5_scan(u, jnp.stack(wins), jnp.stack(wouts), jnp.stack(lams))
    z = _fused_mm([(yf, w, 0), (yb, w, 0), (u, w, 0)], [d_skip.reshape(1, w)], w_glu, [],
                  _pro_s5_glu, _epi_glu, out_dtype=BF16, tm=512, tn=w, k_dim=w, keep_f32=True,
                  name="s5_glu")
    return _matmul_residual([(z, w, 0)], [], w_out, x, _pro_identity, k_dim=w, name="s5_out_proj")


def kernel(x, positions, ln_mix, ln_ffn, ln_final, lru_w_in, lru_conv_w, lru_conv_b, lru_w_a, lru_w_x, lru_b_a, lru_b_x, lru_lambda, lru_w_out, diff_w_qkv, diff_lq1, diff_lk1, diff_lq2, diff_lk2, diff_subln, diff_w_out, win_w_qkv, win_sink, win_w_out, s5_w_in, s5_a_re, s5_a_im, s5_log_dt, s5_b_re, s5_b_im, s5_c_re, s5_c_im, s5_d, s5_w_glu, s5_w_out, moe_w_group, moe_w_expert, moe_w_gate, moe_w_up, moe_w_down):
    batch, s, d = x.shape
    depth = ln_mix.shape[0]
    outs = []
    for b in range(batch):
        xb = x[b]
        rope = _rope_tables(positions[b])
        for i in range(depth):
            kind, j = i % 4, i // 4
            if kind == 0:
                xb = _rglru_layer(xb, ln_mix[i], lru_w_in[j], lru_conv_w[j], lru_conv_b[j],
                                  lru_w_a[j], lru_w_x[j], lru_b_a[j], lru_b_x[j], lru_lambda[j],
                                  lru_w_out[j])
            elif kind == 1:
                xb = _diff_layer(xb, ln_mix[i], rope, diff_w_qkv[j], diff_lq1[j], diff_lk1[j],
                                 diff_lq2[j], diff_lk2[j], diff_subln[j], diff_w_out[j],
                                 0.8 - 0.6 * math.exp(-0.3 * i))
            elif kind == 2:
                xb = _window_layer(xb, ln_mix[i], rope, win_w_qkv[j], win_sink[j], win_w_out[j])
            else:
                xb = _s5_layer(xb, ln_mix[i], s5_w_in[j], s5_a_re[j], s5_a_im[j], s5_log_dt[j],
                               s5_b_re[j], s5_b_im[j], s5_c_re[j], s5_c_im[j], s5_d[j],
                               s5_w_glu[j], s5_w_out[j])
            xb = _hier_moe(xb, ln_ffn[i], moe_w_group[i], moe_w_expert[i], moe_w_gate[i],
                           moe_w_up[i], moe_w_down[i], ln_final, final=(i == depth - 1))
        outs.append(xb)
    return jnp.stack(outs)
```

```python
import functools
import math

import jax
import jax.numpy as jnp
from jax import lax
from jax.experimental import pallas as pl
from jax.experimental.pallas import tpu as pltpu

F32 = jnp.float32
BF16 = jnp.bfloat16

NORM_EPS = 1e-6
NEG_INF = -1e30
LANES = 128
SUBLANES = 8
VMEM_LIMIT = 56 * 1024 * 1024

HEAD_DIM = 128
ROT_DIM = HEAD_DIM // 4
ROPE_THETA = 500000.0
RGLRU_C = 8.0
CONV_W = 4
LRU_BLOCK_W = 128
WINDOW = 128
DIFF_HEADS = 8
WIN_Q_HEADS = 16
WIN_KV_HEADS = 4
SSM_GROUP_CH = 16
SSM_STATE = 64
MOE_GROUPS = 4
EXPERTS_PER_GROUP = 8
N_EXPERTS = MOE_GROUPS * EXPERTS_PER_GROUP
TOP_K = 2
EXPERT_BLOCK = 128


def _params(*sem):
    return pltpu.CompilerParams(dimension_semantics=sem, vmem_limit_bytes=VMEM_LIMIT)


def _rms(x, g):
    ms = jnp.mean(x * x, axis=-1, keepdims=True)
    return x * lax.rsqrt(ms + NORM_EPS) * g


def _gelu_tanh(x):
    return 0.5 * x * (1.0 + jnp.tanh(math.sqrt(2.0 / math.pi) * (x + 0.044715 * (x * x * x))))


def _sigmoid(x):
    return 1.0 / (1.0 + jnp.exp(-x))


def _fused_mm_kernel(*refs, n_row, n_vec, n_epi, prologue, epilogue, keep_f32):
    row_refs = refs[:n_row]
    vec_refs = refs[n_row:n_row + n_vec]
    w_ref = refs[n_row + n_vec]
    epi_refs = refs[n_row + n_vec + 1:n_row + n_vec + 1 + n_epi]
    o_ref = refs[n_row + n_vec + 1 + n_epi]
    a_s = refs[n_row + n_vec + 2 + n_epi]
    a32_s = refs[n_row + n_vec + 3 + n_epi] if keep_f32 else None
    j = pl.program_id(1)

    @pl.when(j == 0)
    def _():
        a = prologue([r[...] for r in row_refs], [v[...] for v in vec_refs])
        a_s[...] = a.astype(BF16)
        if keep_f32:
            a32_s[...] = a

    acc = jnp.dot(a_s[...], w_ref[...].astype(BF16), preferred_element_type=F32)
    epilogue(acc, epi_refs, o_ref, j, a32_s)


def _fused_mm(row_inputs, vec_inputs, w, epi_inputs, prologue, epilogue, *, out_dtype, tm, tn,
              k_dim, keep_f32=False, name):
    s = row_inputs[0][0].shape[0]
    n = w.shape[1]
    tm = min(tm, s)
    tn = min(tn, n)
    in_specs = []
    args = []
    for arr, width, cb in row_inputs:
        in_specs.append(pl.BlockSpec((tm, width), lambda i, j, cb=cb: (i, cb)))
        args.append(arr)
    for arr in vec_inputs:
        in_specs.append(pl.BlockSpec(arr.shape, lambda i, j: (0, 0)))
        args.append(arr)
    in_specs.append(pl.BlockSpec((k_dim, tn), lambda i, j: (0, j)))
    args.append(w)
    for arr, width, per_tile in epi_inputs:
        if per_tile:
            in_specs.append(pl.BlockSpec((tm, width), lambda i, j: (i, j)))
        else:
            in_specs.append(pl.BlockSpec((tm, width), lambda i, j: (i, 0)))
        args.append(arr)
    scratch = [pltpu.VMEM((tm, k_dim), BF16)]
    if keep_f32:
        scratch.append(pltpu.VMEM((tm, k_dim), F32))
    kern = functools.partial(_fused_mm_kernel, n_row=len(row_inputs), n_vec=len(vec_inputs),
                             n_epi=len(epi_inputs), prologue=prologue, epilogue=epilogue,
                             keep_f32=keep_f32)
    return pl.pallas_call(
        kern,
        out_shape=jax.ShapeDtypeStruct((s, n), out_dtype),
        grid=(s // tm, n // tn),
        in_specs=in_specs,
        out_specs=pl.BlockSpec((tm, tn), lambda i, j: (i, j)),
        scratch_shapes=scratch,
        compiler_params=_params("parallel", "arbitrary"),
        name=name,
    )(*args)


def _pro_rms(rows, vecs):
    return _rms(rows[0], vecs[0])


def _epi_store(acc, epi_refs, o_ref, j, a32_s):
    o_ref[...] = acc.astype(o_ref.dtype)


def _epi_residual(acc, epi_refs, o_ref, j, a32_s):
    o_ref[...] = (epi_refs[0][...] + acc).astype(o_ref.dtype)


def _make_epi_rope(n_rope_tiles, n_q_tiles, q_scale, tn):
    def epi(acc, epi_refs, o_ref, j, a32_s):
        c_ref, s1_ref, s2_ref = epi_refs

        @pl.when(j < n_rope_tiles)
        def _():
            c = c_ref[...]
            s1 = s1_ref[...]
            s2 = s2_ref[...]
            sc = jnp.where(j < n_q_tiles, q_scale, 1.0).astype(F32)
            for hh in range(tn // HEAD_DIM):
                xs = acc[:, hh * HEAD_DIM:(hh + 1) * HEAD_DIM]
                rot = (xs * c + pltpu.roll(xs, HEAD_DIM - ROT_DIM // 2, 1) * s1
                       + pltpu.roll(xs, ROT_DIM // 2, 1) * s2) * sc
                o_ref[:, hh * HEAD_DIM:(hh + 1) * HEAD_DIM] = rot.astype(o_ref.dtype)

        @pl.when(j >= n_rope_tiles)
        def _():
            o_ref[...] = acc.astype(o_ref.dtype)

    return epi


def _rope_tables(positions):
    half = ROT_DIM // 2
    inv = ROPE_THETA ** (-jnp.arange(0, ROT_DIM, 2, dtype=F32) / ROT_DIM)
    ang = positions.astype(F32)[:, None] * inv
    cos, sin = jnp.cos(ang), jnp.sin(ang)
    s = positions.shape[0]
    ones = jnp.ones((s, HEAD_DIM - ROT_DIM), F32)
    zeros = jnp.zeros((s, HEAD_DIM - ROT_DIM), F32)
    zh = jnp.zeros((s, half), F32)
    c_tab = jnp.concatenate([cos, cos, ones], axis=1)
    s1_tab = jnp.concatenate([-sin, zh, zeros], axis=1)
    s2_tab = jnp.concatenate([zh, sin, zeros], axis=1)
    return c_tab, s1_tab, s2_tab


def _norm_matmul(x, g, w, *, rope=None, n_rope_cols=0, n_q_cols=0, q_scale=1.0, out_dtype=F32,
                 tm=1024, tn=512, name):
    d = x.shape[1]
    if rope is None:
        epi, epi_inputs = _epi_store, []
    else:
        tn = min(tn, w.shape[1])
        assert n_rope_cols % tn == 0 and n_q_cols % tn == 0
        epi = _make_epi_rope(n_rope_cols // tn, n_q_cols // tn, q_scale, tn)
        epi_inputs = [(t, HEAD_DIM, False) for t in rope]
    return _fused_mm([(x, d, 0)], [g.reshape(1, d)], w, epi_inputs, _pro_rms, epi,
                     out_dtype=out_dtype, tm=tm, tn=tn, k_dim=d, name=name)


def _matmul_residual(row_inputs, vec_inputs, w, res, prologue, *, k_dim, tm=1024, tn=512, name):
    return _fused_mm(row_inputs, vec_inputs, w, [(res, min(tn, w.shape[1]), True)], prologue,
                     _epi_residual, out_dtype=F32, tm=tm, tn=tn, k_dim=k_dim, name=name)


def _rglru_kernel(xf_ref, xfp_ref, xfn_ref, xb_ref, xbp_ref, xbn_ref, cw_ref, cb_ref, wa_ref,
                  wx_ref, ba_ref, bx_ref, lam_ref, yf_ref, yb_ref,
                  ext_s, af_s, bf_s, ab_s, bb_s, hf_s, hb_s, *, tc, cw, n_t):
    i = pl.program_id(1)
    halo = SUBLANES

    @pl.when(i == 0)
    def _():
        hf_s[...] = jnp.zeros_like(hf_s)
        hb_s[...] = jnp.zeros_like(hb_s)

    def gates(x_ref, xp_ref, xn_ref, chunk, d, a_s, b_s):
        prev = jnp.where(chunk == 0, 0.0, xp_ref[...])
        nxt = jnp.where(chunk == n_t - 1, 0.0, xn_ref[...])
        ext_s[0:halo, :] = prev
        ext_s[halo:halo + tc, :] = x_ref[...]
        ext_s[halo + tc:halo + tc + halo, :] = nxt
        xc = cb_ref[...] + sum(
            cw_ref[k:k + 1, :] * ext_s[halo - 2 + k:halo - 2 + k + tc, :] for k in range(CONV_W))
        lam = lam_ref[d:d + 1, :]
        z = -lam
        sp = jnp.maximum(z, 0.0) + jnp.log1p(jnp.exp(-jnp.abs(z)))
        for blk in range(cw // LRU_BLOCK_W):
            sl = slice(blk * LRU_BLOCK_W, (blk + 1) * LRU_BLOCK_W)
            xb = xc[:, sl]
            xbh = xb.astype(BF16)
            r = _sigmoid(jnp.dot(xbh, wa_ref[d, blk].astype(BF16), preferred_element_type=F32)
                         + ba_ref[d:d + 1, sl])
            ig = _sigmoid(jnp.dot(xbh, wx_ref[d, blk].astype(BF16), preferred_element_type=F32)
                          + bx_ref[d:d + 1, sl])
            log_a = (-RGLRU_C) * r * sp[:, sl]
            a_s[:, sl] = jnp.exp(log_a)
            th = jnp.tanh(log_a)
            b_s[:, sl] = jnp.sqrt(-2.0 * th / (1.0 - th)) * (ig * xb)

    gates(xf_ref, xfp_ref, xfn_ref, i, 0, af_s, bf_s)
    gates(xb_ref, xbp_ref, xbn_ref, n_t - 1 - i, 1, ab_s, bb_s)

    def body(r, carry):
        hf, hb = carry
        hf = af_s[pl.ds(r, 1), :] * hf + bf_s[pl.ds(r, 1), :]
        yf_ref[pl.ds(r, 1), :] = hf
        rb = tc - 1 - r
        hb = ab_s[pl.ds(rb, 1), :] * hb + bb_s[pl.ds(rb, 1), :]
        yb_ref[pl.ds(rb, 1), :] = hb
        return hf, hb

    hf, hb = lax.fori_loop(0, tc, body, (hf_s[...], hb_s[...]), unroll=8)
    hf_s[...] = hf
    hb_s[...] = hb


def _rglru_scan(proj, conv_w, conv_b, w_a, w_x, b_a, b_x, lam, *, tc=256, cw=512):
    s = proj.shape[0]
    c = conv_w.shape[1]
    tc = min(tc, s)
    n_t = s // tc
    n_c = c // cw
    xoff = c // cw
    hb = tc // SUBLANES
    last_h = s // SUBLANES - 1

    specs = [
        pl.BlockSpec((tc, cw), lambda ci, i: (i, xoff + ci)),
        pl.BlockSpec((SUBLANES, cw), lambda ci, i: (jnp.maximum(i * hb - 1, 0), xoff + ci)),
        pl.BlockSpec((SUBLANES, cw), lambda ci, i: (jnp.minimum((i + 1) * hb, last_h), xoff + ci)),
        pl.BlockSpec((tc, cw), lambda ci, i: (n_t - 1 - i, xoff + ci)),
        pl.BlockSpec((SUBLANES, cw),
                     lambda ci, i: (jnp.maximum((n_t - 1 - i) * hb - 1, 0), xoff + ci)),
        pl.BlockSpec((SUBLANES, cw),
                     lambda ci, i: (jnp.minimum((n_t - i) * hb, last_h), xoff + ci)),
        pl.BlockSpec((CONV_W, cw), lambda ci, i: (0, ci)),
        pl.BlockSpec((1, cw), lambda ci, i: (0, ci)),
        pl.BlockSpec((2, cw // LRU_BLOCK_W, LRU_BLOCK_W, LRU_BLOCK_W), lambda ci, i: (0, ci, 0, 0)),
        pl.BlockSpec((2, cw // LRU_BLOCK_W, LRU_BLOCK_W, LRU_BLOCK_W), lambda ci, i: (0, ci, 0, 0)),
        pl.BlockSpec((2, cw), lambda ci, i: (0, ci)),
        pl.BlockSpec((2, cw), lambda ci, i: (0, ci)),
        pl.BlockSpec((2, cw), lambda ci, i: (0, ci)),
    ]
    kern = functools.partial(_rglru_kernel, tc=tc, cw=cw, n_t=n_t)
    return pl.pallas_call(
        kern,
        out_shape=(jax.ShapeDtypeStruct((s, c), F32), jax.ShapeDtypeStruct((s, c), F32)),
        grid=(n_c, n_t),
        in_specs=specs,
        out_specs=(pl.BlockSpec((tc, cw), lambda ci, i: (i, ci)),
                   pl.BlockSpec((tc, cw), lambda ci, i: (n_t - 1 - i, ci))),
        scratch_shapes=[pltpu.VMEM((tc + 2 * SUBLANES, cw), F32)]
        + [pltpu.VMEM((tc, cw), F32) for _ in range(4)]
        + [pltpu.VMEM((1, cw), F32) for _ in range(2)],
        compiler_params=_params("parallel", "arbitrary"),
        name="rglru_scan",
    )(proj, proj, proj, proj, proj, proj, conv_w, conv_b.reshape(1, c), w_a, w_x, b_a, b_x, lam)


def _pro_rglru_out(rows, vecs):
    yf, yb, gate = rows
    return (yf + yb) * _gelu_tanh(gate)


def _diff_attn_kernel(q_ref, k_ref, v_ref, lq1_ref, lk1_ref, lq2_ref, lk2_ref, g_ref, o_ref,
                      s_buf, p_buf, m_s, a_s, l_s, acc_s, *, tq, tk, rc, n_kv, lambda_init):
    m_s[...] = jnp.full_like(m_s, -jnp.inf)
    l_s[...] = jnp.zeros_like(l_s)
    acc_s[...] = jnp.zeros_like(acc_s)
    q = q_ref[...]
    qs = (q[:, :HEAD_DIM], q[:, HEAD_DIM:])

    def scores(j, slot):
        off = pl.multiple_of(j * tk, tk)
        kb = k_ref[pl.ds(off, tk), :]
        for c in range(2):
            kc = kb[:, c * HEAD_DIM:(c + 1) * HEAD_DIM]
            s_buf[slot, c * tq:(c + 1) * tq, :] = lax.dot_general(
                qs[c], kc, (((1,), (1,)), ((), ())), preferred_element_type=F32)

    def update(j, slot):
        off = pl.multiple_of(j * tk, tk)
        vb = v_ref[pl.ds(off, tk), :]
        for r0 in range(0, 2 * tq, rc):
            rows = slice(r0, r0 + rc)
            sc = s_buf[slot, rows, :]
            m_old = m_s[rows, :]
            m_new = jnp.maximum(m_old, jnp.max(sc, axis=-1, keepdims=True))
            alpha = jnp.exp2(m_old - m_new)
            p = jnp.exp2(sc - m_new)
            l_s[rows, :] = alpha * l_s[rows, :] + sum(
                p[:, t * LANES:(t + 1) * LANES] for t in range(tk // LANES))
            p_buf[rows, :] = p.astype(BF16)
            m_s[rows, :] = m_new
            a_s[rows, :] = alpha
        acc_s[...] = a_s[...] * acc_s[...] + jnp.dot(p_buf[...], vb, preferred_element_type=F32)

    scores(0, 0)

    def kv_pair(jj, _):
        j = 2 * jj
        scores(j + 1, 1)
        update(j, 0)
        scores(jnp.minimum(j + 2, n_kv - 1), 0)
        update(j + 1, 1)
        return 0

    lax.fori_loop(0, n_kv // 2, kv_pair, 0)
    lam = (jnp.exp(jnp.sum(lq1_ref[...] * lk1_ref[...], axis=-1, keepdims=True))
           - jnp.exp(jnp.sum(lq2_ref[...] * lk2_ref[...], axis=-1, keepdims=True)) + lambda_init)
    l = jnp.sum(l_s[...], axis=-1, keepdims=True)
    o = acc_s[0:tq] / l[0:tq] - lam * (acc_s[tq:2 * tq] / l[tq:2 * tq])
    o_ref[...] = (_rms(o, g_ref[...]) * (1.0 - lambda_init)).astype(o_ref.dtype)


def _diff_attention(qkv, lq1, lk1, lq2, lk2, subln_g, lambda_init, *, tq=512, tk=1024, rc=32):
    s = qkv.shape[0]
    vd = 2 * HEAD_DIM
    tq = min(tq, s)
    tk = min(tk, s // 2)
    assert (s // tk) % 2 == 0
    kern = functools.partial(_diff_attn_kernel, tq=tq, tk=tk, rc=rc, n_kv=s // tk,
                             lambda_init=lambda_init)
    vec = lambda a: a.reshape(1, -1)
    vspec = lambda w: pl.BlockSpec((1, w), lambda h, i: (0, 0))
    return pl.pallas_call(
        kern,
        out_shape=jax.ShapeDtypeStruct((s, DIFF_HEADS * vd), BF16),
        grid=(DIFF_HEADS, s // tq),
        in_specs=[
            pl.BlockSpec((tq, vd), lambda h, i: (i, h)),
            pl.BlockSpec((s, vd), lambda h, i: (0, DIFF_HEADS + h)),
            pl.BlockSpec((s, vd), lambda h, i: (0, 2 * DIFF_HEADS + h)),
            vspec(HEAD_DIM), vspec(HEAD_DIM), vspec(HEAD_DIM), vspec(HEAD_DIM), vspec(vd),
        ],
        out_specs=pl.BlockSpec((tq, vd), lambda h, i: (i, h)),
        scratch_shapes=[pltpu.VMEM((2, 2 * tq, tk), F32), pltpu.VMEM((2 * tq, tk), BF16),
                        pltpu.VMEM((2 * tq, 1), F32), pltpu.VMEM((2 * tq, 1), F32),
                        pltpu.VMEM((2 * tq, LANES), F32), pltpu.VMEM((2 * tq, vd), F32)],
        compiler_params=_params("parallel", "arbitrary"),
        name="diff_attention",
    )(qkv, qkv, qkv, vec(lq1), vec(lk1), vec(lq2), vec(lk2), vec(subln_g))


def _pro_identity(rows, vecs):
    return rows[0]


def _win_attn_kernel(sink_ref, q_ref, k_ref, v_ref, o_ref, *, tq, win, s_len, group):
    kvh = pl.program_id(0)
    i = pl.program_id(1)
    scale = HEAD_DIM ** -0.5
    start = jnp.clip(i * tq - WINDOW, 0, s_len - win)
    start = pl.multiple_of(start, WINDOW)
    kw = k_ref[pl.ds(start, win), :]
    vw = v_ref[pl.ds(start, win), :]
    qpos = i * tq + lax.broadcasted_iota(jnp.int32, (tq, win), 0)
    kpos = start + lax.broadcasted_iota(jnp.int32, (tq, win), 1)
    valid = jnp.abs(kpos - qpos) <= WINDOW
    for g in range(group):
        qg = q_ref[:, g * HEAD_DIM:(g + 1) * HEAD_DIM]
        sc = lax.dot_general(qg, kw, (((1,), (1,)), ((), ())), preferred_element_type=F32) * scale
        sc = jnp.where(valid, sc, NEG_INF)
        sink = sink_ref[kvh * group + g]
        m = jnp.maximum(jnp.max(sc, axis=-1, keepdims=True), sink)
        e = jnp.exp(sc - m)
        p = e / (jnp.sum(e, axis=-1, keepdims=True) + jnp.exp(sink - m))
        o = jnp.dot(p.astype(BF16), vw, preferred_element_type=F32)
        o_ref[:, g * HEAD_DIM:(g + 1) * HEAD_DIM] = o.astype(o_ref.dtype)


def _window_attention(qkv, sink, *, tq=256):
    s = qkv.shape[0]
    group = WIN_Q_HEADS // WIN_KV_HEADS
    tq = min(tq, s)
    win = min(tq + 2 * WINDOW, s)
    qw = group * HEAD_DIM
    k0 = WIN_Q_HEADS
    v0 = WIN_Q_HEADS + WIN_KV_HEADS
    kern = functools.partial(_win_attn_kernel, tq=tq, win=win, s_len=s, group=group)
    return pl.pallas_call(
        kern,
        out_shape=jax.ShapeDtypeStruct((s, WIN_Q_HEADS * HEAD_DIM), BF16),
        grid=(WIN_KV_HEADS, s // tq),
        in_specs=[
            pl.BlockSpec(memory_space=pltpu.SMEM),
            pl.BlockSpec((tq, qw), lambda h, i: (i, h)),
            pl.BlockSpec((s, HEAD_DIM), lambda h, i: (0, k0 + h)),
            pl.BlockSpec((s, HEAD_DIM), lambda h, i: (0, v0 + h)),
        ],
        out_specs=pl.BlockSpec((tq, qw), lambda h, i: (i, h)),
        compiler_params=_params("parallel", "arbitrary"),
        name="window_attention",
    )(sink.astype(F32), qkv, qkv, qkv)


def _s5_discretize(a_re, a_im, log_dt, b_re, b_im):
    dt = jnp.exp(log_dt)[:, None]
    mag = jnp.exp(dt * a_re)
    lr, li = mag * jnp.cos(dt * a_im), mag * jnp.sin(dt * a_im)
    den = a_re * a_re + a_im * a_im
    nr, ni = lr - 1.0, li
    fr = (nr * a_re + ni * a_im) / den
    fi = (ni * a_re - nr * a_im) / den
    bbr = fr[..., None] * b_re - fi[..., None] * b_im
    bbi = fr[..., None] * b_im + fi[..., None] * b_re
    return lr, li, bbr, bbi


def _s5_block_weights(lr, li, bbr, bbi, c_re, c_im, gpc):
    g, n, c = bbr.shape
    n_k = g // gpc
    eye = jnp.eye(gpc, dtype=F32)

    def w_in(bb):
        t = bb.reshape(n_k, gpc, n, c)
        return jnp.einsum('kgnc,gh->kgchn', t, eye).reshape(n_k, gpc * c, gpc * n)

    def w_out(cc):
        t = cc.reshape(n_k, gpc, c, n)
        return jnp.einsum('kgcn,gh->khngc', t, eye).reshape(n_k, gpc * n, gpc * c)

    win = jnp.concatenate([w_in(bbr), w_in(bbi)], axis=2)
    wout = jnp.concatenate([w_out(c_re), -w_out(c_im)], axis=1)
    lam = jnp.stack([lr.reshape(n_k, gpc * n), li.reshape(n_k, gpc * n)], axis=1)
    return win, wout, lam


def _s5_kernel(uf_ref, ub_ref, win_ref, wout_ref, lam_ref, yf_ref, yb_ref,
               xf_s, xb_s, st_s, *, tc, ns):
    i = pl.program_id(1)

    @pl.when(i == 0)
    def _():
        st_s[...] = jnp.zeros_like(st_s)

    xf_s[...] = jnp.dot(uf_ref[...].astype(BF16), win_ref[0, 0].astype(BF16),
                        preferred_element_type=F32)
    xb_s[...] = jnp.dot(ub_ref[...].astype(BF16), win_ref[1, 0].astype(BF16),
                        preferred_element_type=F32)
    lrf, lif = lam_ref[0, 0, 0:1, :], lam_ref[0, 0, 1:2, :]
    lrb, lib = lam_ref[1, 0, 0:1, :], lam_ref[1, 0, 1:2, :]

    def body(r, carry):
        srf, sif, srb, sib = carry
        nrf = lrf * srf - lif * sif + xf_s[pl.ds(r, 1), 0:ns]
        nif = lrf * sif + lif * srf + xf_s[pl.ds(r, 1), ns:2 * ns]
        xf_s[pl.ds(r, 1), 0:ns] = nrf
        xf_s[pl.ds(r, 1), ns:2 * ns] = nif
        rb = tc - 1 - r
        nrb = lrb * srb - lib * sib + xb_s[pl.ds(rb, 1), 0:ns]
        nib = lrb * sib + lib * srb + xb_s[pl.ds(rb, 1), ns:2 * ns]
        xb_s[pl.ds(rb, 1), 0:ns] = nrb
        xb_s[pl.ds(rb, 1), ns:2 * ns] = nib
        return nrf, nif, nrb, nib

    init = (st_s[0:1, :], st_s[1:2, :], st_s[2:3, :], st_s[3:4, :])
    srf, sif, srb, sib = lax.fori_loop(0, tc, body, init, unroll=8)
    st_s[0:1, :] = srf
    st_s[1:2, :] = sif
    st_s[2:3, :] = srb
    st_s[3:4, :] = sib
    yf_ref[...] = jnp.dot(xf_s[...].astype(BF16), wout_ref[0, 0].astype(BF16),
                          preferred_element_type=F32)
    yb_ref[...] = jnp.dot(xb_s[...].astype(BF16), wout_ref[1, 0].astype(BF16),
                          preferred_element_type=F32)


def _s5_scan(u, win, wout, lam, *, tc=256):
    s, w = u.shape
    _, n_k, cw, ns2 = win.shape
    ns = ns2 // 2
    tc = min(tc, s)
    n_t = s // tc
    kern = functools.partial(_s5_kernel, tc=tc, ns=ns)
    return pl.pallas_call(
        kern,
        out_shape=(jax.ShapeDtypeStruct((s, w), F32), jax.ShapeDtypeStruct((s, w), F32)),
        grid=(n_k, n_t),
        in_specs=[
            pl.BlockSpec((tc, cw), lambda k, i: (i, k)),
            pl.BlockSpec((tc, cw), lambda k, i: (n_t - 1 - i, k)),
            pl.BlockSpec((2, 1, cw, ns2), lambda k, i: (0, k, 0, 0)),
            pl.BlockSpec((2, 1, ns2, cw), lambda k, i: (0, k, 0, 0)),
            pl.BlockSpec((2, 1, 2, ns), lambda k, i: (0, k, 0, 0)),
        ],
        out_specs=(pl.BlockSpec((tc, cw), lambda k, i: (i, k)),
                   pl.BlockSpec((tc, cw), lambda k, i: (n_t - 1 - i, k))),
        scratch_shapes=[pltpu.VMEM((tc, ns2), F32), pltpu.VMEM((tc, ns2), F32),
                        pltpu.VMEM((4, ns), F32)],
        compiler_params=_params("parallel", "arbitrary"),
        name="s5_scan",
    )(u, u, win, wout, lam)


def _pro_s5_glu(rows, vecs):
    yf, yb, u = rows
    return _gelu_tanh(yf + yb + vecs[0] * u)


def _epi_glu(acc, epi_refs, o_ref, j, a32_s):
    o_ref[...] = (a32_s[...] * _sigmoid(acc)).astype(o_ref.dtype)


def _router_kernel(x_ref, g_ref, w_ref, h_ref, idx_ref, wt_ref):
    h = _rms(x_ref[...], g_ref[...])
    h_ref[...] = h
    logits = jnp.dot(h, w_ref[...], preferred_element_type=F32, precision=lax.Precision.HIGHEST)
    tm = logits.shape[0]
    lane = lax.broadcasted_iota(jnp.int32, (tm, LANES), 1)
    big = jnp.int32(LANES)
    ninf = -jnp.inf
    gl = jnp.where(lane < MOE_GROUPS, logits, ninf)
    gm = jnp.max(gl, axis=-1, keepdims=True)
    ge = jnp.exp(gl - gm)
    g_prob = ge / jnp.sum(ge, axis=-1, keepdims=True)
    g_p = jnp.max(g_prob, axis=-1, keepdims=True)
    g_idx = jnp.min(jnp.where(g_prob == g_p, lane, big), axis=-1, keepdims=True)
    lo = MOE_GROUPS + g_idx * EXPERTS_PER_GROUP
    in_grp = (lane >= lo) & (lane < lo + EXPERTS_PER_GROUP)
    el = jnp.where(in_grp, logits, ninf)
    em = jnp.max(el, axis=-1, keepdims=True)
    ee = jnp.exp(el - em)
    e_prob = jnp.where(in_grp, ee / jnp.sum(ee, axis=-1, keepdims=True), -1.0)
    p1 = jnp.max(e_prob, axis=-1, keepdims=True)
    i1 = jnp.min(jnp.where(e_prob == p1, lane, big), axis=-1, keepdims=True)
    rest = jnp.where(lane == i1, -1.0, e_prob)
    p2 = jnp.max(rest, axis=-1, keepdims=True)
    i2 = jnp.min(jnp.where(rest == p2, lane, big), axis=-1, keepdims=True)
    denom = p1 + p2
    w1 = g_p * (p1 / denom)
    w2 = g_p * (p2 / denom)
    idx_ref[...] = jnp.where(lane == 0, i1 - MOE_GROUPS, jnp.where(lane == 1, i2 - MOE_GROUPS, 0))
    wt_ref[...] = jnp.where(lane == 0, w1, jnp.where(lane == 1, w2, 0.0))


def _router(x, g, w_group, w_expert, *, tm=512):
    s, d = x.shape
    tm = min(tm, s)
    w_r = jnp.concatenate(
        [w_group, w_expert, jnp.zeros((d, LANES - MOE_GROUPS - N_EXPERTS), F32)], axis=1)
    return pl.pallas_call(
        _router_kernel,
        out_shape=(jax.ShapeDtypeStruct((s, d), F32), jax.ShapeDtypeStruct((s, LANES), jnp.int32),
                   jax.ShapeDtypeStruct((s, LANES), F32)),
        grid=(s // tm,),
        in_specs=[pl.BlockSpec((tm, d), lambda i: (i, 0)), pl.BlockSpec((1, d), lambda i: (0, 0)),
                  pl.BlockSpec((d, LANES), lambda i: (0, 0))],
        out_specs=(pl.BlockSpec((tm, d), lambda i: (i, 0)), pl.BlockSpec((tm, LANES), lambda i: (i, 0)),
                   pl.BlockSpec((tm, LANES), lambda i: (i, 0))),
        compiler_params=_params("parallel"),
        name="moe_router",
    )(x, g.reshape(1, d), w_r)


def _dispatch_plan(expert, n_tok):
    n_asg = n_tok * TOP_K
    e_flat = expert.reshape(-1)
    onehot = (e_flat[:, None] == jnp.arange(N_EXPERTS, dtype=jnp.int32)[None, :]).astype(jnp.int32)
    csum = jnp.cumsum(onehot, axis=0)
    rank = jnp.sum(onehot * csum, axis=1) - 1
    counts = csum[-1]
    padded = ((counts + EXPERT_BLOCK - 1) // EXPERT_BLOCK) * EXPERT_BLOCK
    pends = jnp.cumsum(padded)
    pstarts = pends - padded
    dest = (pstarts[e_flat] + rank).astype(jnp.int32)
    n_rows = -(-(n_asg + N_EXPERTS * (EXPERT_BLOCK - 1)) // EXPERT_BLOCK) * EXPERT_BLOCK
    n_blk = n_rows // EXPERT_BLOCK
    tok = jnp.arange(n_asg, dtype=jnp.int32) // TOP_K
    row_tok = jnp.zeros((n_rows + EXPERT_BLOCK,), jnp.int32).at[dest].set(tok)
    blk_start = jnp.arange(n_blk, dtype=jnp.int32) * EXPERT_BLOCK
    blk_e = jnp.minimum(jnp.searchsorted(pends, blk_start, side='right'),
                        N_EXPERTS - 1).astype(jnp.int32)
    return dest, row_tok, blk_e, n_blk


def _row_copy(src_hbm, src_row, dst_buf, slot, dst_row, sem):
    return pltpu.make_async_copy(src_hbm.at[pl.ds(src_row, 1), :],
                                 dst_buf.at[slot, pl.ds(dst_row, 1), :], sem.at[slot])


def _gather_start(idx_ref, base, n, src_hbm, dst_buf, slot, sem):
    for r in range(n):
        _row_copy(src_hbm, idx_ref[base + r], dst_buf, slot, r, sem).start()


def _gather_wait(n, src_hbm, dst_buf, slot, sem):
    def body(r, _):
        _row_copy(src_hbm, 0, dst_buf, slot, r, sem).wait()
        return 0
    lax.fori_loop(0, n, body, 0, unroll=8)


def _expert_kernel(be_ref, rt_ref, h_hbm, wg_ref, wu_ref, wd_ref, y_ref,
                   xbuf, sem, wg_s, wu_s, wd_s, *, n_blk):
    b = pl.program_id(0)
    slot = b % 2

    @pl.when(b == 0)
    def _():
        _gather_start(rt_ref, 0, EXPERT_BLOCK, h_hbm, xbuf, 0, sem)

    changed = jnp.logical_or(b == 0, be_ref[b] != be_ref[jnp.maximum(b - 1, 0)])

    @pl.when(changed)
    def _():
        wg_s[...] = wg_ref[...].astype(BF16)
        wu_s[...] = wu_ref[...].astype(BF16)
        wd_s[...] = wd_ref[...].astype(BF16)

    _gather_wait(EXPERT_BLOCK, h_hbm, xbuf, slot, sem)
    _gather_start(rt_ref, (b + 1) * EXPERT_BLOCK, EXPERT_BLOCK, h_hbm, xbuf, 1 - slot, sem)
    x = xbuf[slot].astype(BF16)
    hg = jnp.dot(x, wg_s[...], preferred_element_type=F32)
    hu = jnp.dot(x, wu_s[...], preferred_element_type=F32)
    hdn = (hg * _sigmoid(hg)) * hu
    y_ref[...] = jnp.dot(hdn.astype(BF16), wd_s[...], preferred_element_type=F32)

    @pl.when(b == n_blk - 1)
    def _():
        _gather_wait(EXPERT_BLOCK, h_hbm, xbuf, 1 - slot, sem)


def _expert_mlp(h, row_tok, blk_e, n_blk, layer, w_gate, w_up, w_down):
    s, d = h.shape
    de = w_gate.shape[3]
    kern = functools.partial(_expert_kernel, n_blk=n_blk)
    grid_spec = pltpu.PrefetchScalarGridSpec(
        num_scalar_prefetch=2,
        grid=(n_blk,),
        in_specs=[
            pl.BlockSpec(memory_space=pl.ANY),
            pl.BlockSpec((None, None, d, de), lambda b, be, rt: (layer, be[b], 0, 0)),
            pl.BlockSpec((None, None, d, de), lambda b, be, rt: (layer, be[b], 0, 0)),
            pl.BlockSpec((None, None, de, d), lambda b, be, rt: (layer, be[b], 0, 0)),
        ],
        out_specs=pl.BlockSpec((EXPERT_BLOCK, d), lambda b, be, rt: (b, 0)),
        scratch_shapes=[pltpu.VMEM((2, EXPERT_BLOCK, d), F32), pltpu.SemaphoreType.DMA((2,)),
                        pltpu.VMEM((d, de), BF16), pltpu.VMEM((d, de), BF16),
                        pltpu.VMEM((de, d), BF16)],
    )
    return pl.pallas_call(
        kern,
        out_shape=jax.ShapeDtypeStruct((n_blk * EXPERT_BLOCK, d), F32),
        grid_spec=grid_spec,
        compiler_params=_params("arbitrary"),
        name="moe_experts",
    )(blk_e, row_tok, h, w_gate, w_up, w_down)


def _combine_kernel(dest_ref, x_ref, wt_ref, ys_hbm, g_ref, o_ref, ybuf, sem, *, tb, n_b, final):
    b = pl.program_id(0)
    slot = b % 2
    n = tb * TOP_K

    @pl.when(b == 0)
    def _():
        _gather_start(dest_ref, 0, n, ys_hbm, ybuf, 0, sem)

    _gather_wait(n, ys_hbm, ybuf, slot, sem)
    _gather_start(dest_ref, (b + 1) * n, n, ys_hbm, ybuf, 1 - slot, sem)
    y0 = ybuf[slot, 0:tb, :]
    y1 = ybuf[slot, tb:2 * tb, :]
    wt = wt_ref[...]
    out = x_ref[...] + (y0 * wt[:, 0:1] + y1 * wt[:, 1:2])
    if final:
        out = _rms(out, g_ref[...])
    o_ref[...] = out

    @pl.when(b == n_b - 1)
    def _():
        _gather_wait(n, ys_hbm, ybuf, 1 - slot, sem)


def _moe_combine(x, wt, ys, dest, g_final, *, final, tb=128):
    s, d = x.shape
    tb = min(tb, s)
    n_b = s // tb
    dest = dest.reshape(n_b, tb, TOP_K).transpose(0, 2, 1).reshape(-1)
    dest = jnp.concatenate([dest, jnp.zeros((tb * TOP_K,), jnp.int32)])
    kern = functools.partial(_combine_kernel, tb=tb, n_b=n_b, final=final)
    grid_spec = pltpu.PrefetchScalarGridSpec(
        num_scalar_prefetch=1,
        grid=(n_b,),
        in_specs=[
            pl.BlockSpec((tb, d), lambda b, dr: (b, 0)),
            pl.BlockSpec((tb, LANES), lambda b, dr: (b, 0)),
            pl.BlockSpec(memory_space=pl.ANY),
            pl.BlockSpec((1, d), lambda b, dr: (0, 0)),
        ],
        out_specs=pl.BlockSpec((tb, d), lambda b, dr: (b, 0)),
        scratch_shapes=[pltpu.VMEM((2, tb * TOP_K, d), F32), pltpu.SemaphoreType.DMA((2,))],
    )
    return pl.pallas_call(
        kern,
        out_shape=jax.ShapeDtypeStruct((s, d), F32),
        grid_spec=grid_spec,
        compiler_params=_params("arbitrary"),
        name="moe_combine",
    )(dest, x, wt, ys, g_final.reshape(1, d))


def _hier_moe(x, ln_g, w_group, w_expert, layer, w_gate, w_up, w_down, g_final, *, final):
    s = x.shape[0]
    h, idx, wt = _router(x, ln_g, w_group, w_expert)
    dest, row_tok, blk_e, n_blk = _dispatch_plan(idx[:, :TOP_K], s)
    ys = _expert_mlp(h, row_tok, blk_e, n_blk, layer, w_gate, w_up, w_down)
    return _moe_combine(x, wt, ys, dest, g_final, final=final)


def _rglru_layer(x, ln_g, w_in, conv_w, conv_b, w_a, w_x, b_a, b_x, lam, w_out):
    d = x.shape[1]
    c = conv_w.shape[1]
    proj = _norm_matmul(x, ln_g, w_in, name="lru_in_proj")
    yf, yb = _rglru_scan(proj, conv_w, conv_b, w_a, w_x, b_a, b_x, lam)
    return _matmul_residual([(yf, c, 0), (yb, c, 0), (proj, c, 0)], [], w_out, x, _pro_rglru_out,
                            k_dim=c, tm=512, name="lru_out_proj")


def _diff_layer(x, ln_g, rope, w_qkv, lq1, lk1, lq2, lk2, subln_g, w_out, lambda_init):
    qk_cols = 2 * DIFF_HEADS * 2 * HEAD_DIM
    qkv = _norm_matmul(x, ln_g, w_qkv, rope=rope, n_rope_cols=qk_cols, n_q_cols=qk_cols // 2,
                       q_scale=HEAD_DIM ** -0.5 * math.log2(math.e), out_dtype=BF16,
                       name="diff_qkv_proj")
    o = _diff_attention(qkv, lq1, lk1, lq2, lk2, subln_g, lambda_init)
    return _matmul_residual([(o, o.shape[1], 0)], [], w_out, x, _pro_identity,
                            k_dim=o.shape[1], name="diff_out_proj")


def _window_layer(x, ln_g, rope, w_qkv, sink, w_out):
    qk_cols = (WIN_Q_HEADS + WIN_KV_HEADS) * HEAD_DIM
    qkv = _norm_matmul(x, ln_g, w_qkv, rope=rope, n_rope_cols=qk_cols, out_dtype=BF16,
                       name="win_qkv_proj")
    o = _window_attention(qkv, sink)
    return _matmul_residual([(o, o.shape[1], 0)], [], w_out, x, _pro_identity,
                            k_dim=o.shape[1], name="win_out_proj")


def _s5_layer(x, ln_g, w_in, a_re, a_im, log_dt, b_re, b_im, c_re, c_im, d_skip, w_glu, w_out,
              *, gpc=8):
    u = _norm_matmul(x, ln_g, w_in, name="s5_in_proj")
    w = u.shape[1]
    wins, wouts, lams = [], [], []
    for dd in range(2):
        lr, li, bbr, bbi = _s5_discretize(a_re[dd], a_im[dd], log_dt[dd], b_re[dd], b_im[dd])
        wi, wo, lm = _s5_block_weights(lr, li, bbr, bbi, c_re[dd], c_im[dd], gpc)
        wins.append(wi)
        wouts.append(wo)
        lams.append(lm)
    yf, yb = _s5_scan(u, jnp.stack(wins), jnp.stack(wouts), jnp.stack(lams))
    z = _fused_mm([(yf, w, 0), (yb, w, 0), (u, w, 0)], [d_skip.reshape(1, w)], w_glu, [],
                  _pro_s5_glu, _epi_glu, out_dtype=BF16, tm=512, tn=w, k_dim=w, keep_f32=True,
                  name="s5_glu")
    return _matmul_residual([(z, w, 0)], [], w_out, x, _pro_identity, k_dim=w, name="s5_out_proj")


def kernel(x, positions, ln_mix, ln_ffn, ln_final, lru_w_in, lru_conv_w, lru_conv_b, lru_w_a, lru_w_x, lru_b_a, lru_b_x, lru_lambda, lru_w_out, diff_w_qkv, diff_lq1, diff_lk1, diff_lq2, diff_lk2, diff_subln, diff_w_out, win_w_qkv, win_sink, win_w_out, s5_w_in, s5_a_re, s5_a_im, s5_log_dt, s5_b_re, s5_b_im, s5_c_re, s5_c_im, s5_d, s5_w_glu, s5_w_out, moe_w_group, moe_w_expert, moe_w_gate, moe_w_up, moe_w_down):
    batch, s, d = x.shape
    depth = ln_mix.shape[0]
    outs = []
    for b in range(batch):
        xb = x[b]
        rope = _rope_tables(positions[b])
        for i in range(depth):
            kind, j = i % 4, i // 4
            if kind == 0:
                xb = _rglru_layer(xb, ln_mix[i], lru_w_in[j], lru_conv_w[j], lru_conv_b[j],
                                  lru_w_a[j], lru_w_x[j], lru_b_a[j], lru_b_x[j], lru_lambda[j],
                                  lru_w_out[j])
            elif kind == 1:
                xb = _diff_layer(xb, ln_mix[i], rope, diff_w_qkv[j], diff_lq1[j], diff_lk1[j],
                                 diff_lq2[j], diff_lk2[j], diff_subln[j], diff_w_out[j],
                                 0.8 - 0.6 * math.exp(-0.3 * i))
            elif kind == 2:
                xb = _window_layer(xb, ln_mix[i], rope, win_w_qkv[j], win_sink[j], win_w_out[j])
            else:
                xb = _s5_layer(xb, ln_mix[i], s5_w_in[j], s5_a_re[j], s5_a_im[j], s5_log_dt[j],
                               s5_b_re[j], s5_b_im[j], s5_c_re[j], s5_c_im[j], s5_d[j],
                               s5_w_glu[j], s5_w_out[j])
            xb = _hier_moe(xb, ln_ffn[i], moe_w_group[i], moe_w_expert[i], i, moe_w_gate,
                           moe_w_up, moe_w_down, ln_final, final=(i == depth - 1))
        outs.append(xb)
    return jnp.stack(outs)
```

```python
import functools
import math

import jax
import jax.numpy as jnp
from jax import lax
from jax.experimental import pallas as pl
from jax.experimental.pallas import tpu as pltpu

F32 = jnp.float32
BF16 = jnp.bfloat16

NORM_EPS = 1e-6
NEG_INF = -1e30
LANES = 128
SUBLANES = 8
VMEM_LIMIT = 56 * 1024 * 1024

HEAD_DIM = 128
ROT_DIM = HEAD_DIM // 4
ROPE_THETA = 500000.0
RGLRU_C = 8.0
CONV_W = 4
LRU_BLOCK_W = 128
WINDOW = 128
DIFF_HEADS = 8
WIN_Q_HEADS = 16
WIN_KV_HEADS = 4
SSM_GROUP_CH = 16
SSM_STATE = 64
MOE_GROUPS = 4
EXPERTS_PER_GROUP = 8
N_EXPERTS = MOE_GROUPS * EXPERTS_PER_GROUP
TOP_K = 2
EXPERT_BLOCK = 128


def _params(*sem):
    return pltpu.CompilerParams(dimension_semantics=sem, vmem_limit_bytes=VMEM_LIMIT)


def _rms(x, g):
    ms = jnp.mean(x * x, axis=-1, keepdims=True)
    return x * lax.rsqrt(ms + NORM_EPS) * g


def _gelu_tanh(x):
    return 0.5 * x * (1.0 + jnp.tanh(math.sqrt(2.0 / math.pi) * (x + 0.044715 * (x * x * x))))


def _sigmoid(x):
    return 1.0 / (1.0 + jnp.exp(-x))


def _fused_mm_kernel(*refs, n_row, n_vec, n_epi, prologue, epilogue, keep_f32):
    row_refs = refs[:n_row]
    vec_refs = refs[n_row:n_row + n_vec]
    w_ref = refs[n_row + n_vec]
    epi_refs = refs[n_row + n_vec + 1:n_row + n_vec + 1 + n_epi]
    o_ref = refs[n_row + n_vec + 1 + n_epi]
    a_s = refs[n_row + n_vec + 2 + n_epi]
    a32_s = refs[n_row + n_vec + 3 + n_epi] if keep_f32 else None
    j = pl.program_id(1)

    @pl.when(j == 0)
    def _():
        a = prologue([r[...] for r in row_refs], [v[...] for v in vec_refs])
        a_s[...] = a.astype(BF16)
        if keep_f32:
            a32_s[...] = a

    acc = jnp.dot(a_s[...], w_ref[...].astype(BF16), preferred_element_type=F32)
    epilogue(acc, epi_refs, o_ref, j, a32_s)


def _fused_mm(row_inputs, vec_inputs, w, epi_inputs, prologue, epilogue, *, out_dtype, tm, tn,
              k_dim, keep_f32=False, name):
    s = row_inputs[0][0].shape[0]
    n = w.shape[1]
    tm = min(tm, s)
    tn = min(tn, n)
    in_specs = []
    args = []
    for arr, width, cb in row_inputs:
        in_specs.append(pl.BlockSpec((tm, width), lambda i, j, cb=cb: (i, cb)))
        args.append(arr)
    for arr in vec_inputs:
        in_specs.append(pl.BlockSpec(arr.shape, lambda i, j: (0, 0)))
        args.append(arr)
    in_specs.append(pl.BlockSpec((k_dim, tn), lambda i, j: (0, j)))
    args.append(w)
    for arr, width, per_tile in epi_inputs:
        if per_tile:
            in_specs.append(pl.BlockSpec((tm, width), lambda i, j: (i, j)))
        else:
            in_specs.append(pl.BlockSpec((tm, width), lambda i, j: (i, 0)))
        args.append(arr)
    scratch = [pltpu.VMEM((tm, k_dim), BF16)]
    if keep_f32:
        scratch.append(pltpu.VMEM((tm, k_dim), F32))
    kern = functools.partial(_fused_mm_kernel, n_row=len(row_inputs), n_vec=len(vec_inputs),
                             n_epi=len(epi_inputs), prologue=prologue, epilogue=epilogue,
                             keep_f32=keep_f32)
    return pl.pallas_call(
        kern,
        out_shape=jax.ShapeDtypeStruct((s, n), out_dtype),
        grid=(s // tm, n // tn),
        in_specs=in_specs,
        out_specs=pl.BlockSpec((tm, tn), lambda i, j: (i, j)),
        scratch_shapes=scratch,
        compiler_params=_params("parallel", "arbitrary"),
        name=name,
    )(*args)


def _pro_rms(rows, vecs):
    return _rms(rows[0], vecs[0])


def _epi_store(acc, epi_refs, o_ref, j, a32_s):
    o_ref[...] = acc.astype(o_ref.dtype)


def _epi_residual(acc, epi_refs, o_ref, j, a32_s):
    o_ref[...] = (epi_refs[0][...] + acc).astype(o_ref.dtype)


def _make_epi_rope(n_rope_tiles, n_q_tiles, q_scale, tn):
    def epi(acc, epi_refs, o_ref, j, a32_s):
        c_ref, s1_ref, s2_ref = epi_refs

        @pl.when(j < n_rope_tiles)
        def _():
            c = c_ref[...]
            s1 = s1_ref[...]
            s2 = s2_ref[...]
            sc = jnp.where(j < n_q_tiles, q_scale, 1.0).astype(F32)
            for hh in range(tn // HEAD_DIM):
                xs = acc[:, hh * HEAD_DIM:(hh + 1) * HEAD_DIM]
                rot = (xs * c + pltpu.roll(xs, HEAD_DIM - ROT_DIM // 2, 1) * s1
                       + pltpu.roll(xs, ROT_DIM // 2, 1) * s2) * sc
                o_ref[:, hh * HEAD_DIM:(hh + 1) * HEAD_DIM] = rot.astype(o_ref.dtype)

        @pl.when(j >= n_rope_tiles)
        def _():
            o_ref[...] = acc.astype(o_ref.dtype)

    return epi


def _rope_tables(positions):
    half = ROT_DIM // 2
    inv = ROPE_THETA ** (-jnp.arange(0, ROT_DIM, 2, dtype=F32) / ROT_DIM)
    ang = positions.astype(F32)[:, None] * inv
    cos, sin = jnp.cos(ang), jnp.sin(ang)
    s = positions.shape[0]
    ones = jnp.ones((s, HEAD_DIM - ROT_DIM), F32)
    zeros = jnp.zeros((s, HEAD_DIM - ROT_DIM), F32)
    zh = jnp.zeros((s, half), F32)
    c_tab = jnp.concatenate([cos, cos, ones], axis=1)
    s1_tab = jnp.concatenate([-sin, zh, zeros], axis=1)
    s2_tab = jnp.concatenate([zh, sin, zeros], axis=1)
    return c_tab, s1_tab, s2_tab


def _norm_matmul(x, g, w, *, rope=None, n_rope_cols=0, n_q_cols=0, q_scale=1.0, out_dtype=F32,
                 tm=1024, tn=512, name):
    d = x.shape[1]
    if rope is None:
        epi, epi_inputs = _epi_store, []
    else:
        tn = min(tn, w.shape[1])
        assert n_rope_cols % tn == 0 and n_q_cols % tn == 0
        epi = _make_epi_rope(n_rope_cols // tn, n_q_cols // tn, q_scale, tn)
        epi_inputs = [(t, HEAD_DIM, False) for t in rope]
    return _fused_mm([(x, d, 0)], [g.reshape(1, d)], w, epi_inputs, _pro_rms, epi,
                     out_dtype=out_dtype, tm=tm, tn=tn, k_dim=d, name=name)


def _matmul_residual(row_inputs, vec_inputs, w, res, prologue, *, k_dim, tm=1024, tn=512, name):
    return _fused_mm(row_inputs, vec_inputs, w, [(res, min(tn, w.shape[1]), True)], prologue,
                     _epi_residual, out_dtype=F32, tm=tm, tn=tn, k_dim=k_dim, name=name)


def _rglru_kernel(xf_ref, xfp_ref, xfn_ref, xb_ref, xbp_ref, xbn_ref, cw_ref, cb_ref, wa_ref,
                  wx_ref, ba_ref, bx_ref, lam_ref, yf_ref, yb_ref,
                  ext_s, af_s, bf_s, ab_s, bb_s, hf_s, hb_s, *, tc, cw, n_t):
    i = pl.program_id(1)
    halo = SUBLANES

    @pl.when(i == 0)
    def _():
        hf_s[...] = jnp.zeros_like(hf_s)
        hb_s[...] = jnp.zeros_like(hb_s)

    def gates(x_ref, xp_ref, xn_ref, chunk, d, a_s, b_s):
        prev = jnp.where(chunk == 0, 0.0, xp_ref[...])
        nxt = jnp.where(chunk == n_t - 1, 0.0, xn_ref[...])
        ext_s[0:halo, :] = prev
        ext_s[halo:halo + tc, :] = x_ref[...]
        ext_s[halo + tc:halo + tc + halo, :] = nxt
        xc = cb_ref[...] + sum(
            cw_ref[k:k + 1, :] * ext_s[halo - 2 + k:halo - 2 + k + tc, :] for k in range(CONV_W))
        lam = lam_ref[d:d + 1, :]
        z = -lam
        sp = jnp.maximum(z, 0.0) + jnp.log1p(jnp.exp(-jnp.abs(z)))
        for blk in range(cw // LRU_BLOCK_W):
            sl = slice(blk * LRU_BLOCK_W, (blk + 1) * LRU_BLOCK_W)
            xb = xc[:, sl]
            xbh = xb.astype(BF16)
            r = _sigmoid(jnp.dot(xbh, wa_ref[d, blk].astype(BF16), preferred_element_type=F32)
                         + ba_ref[d:d + 1, sl])
            ig = _sigmoid(jnp.dot(xbh, wx_ref[d, blk].astype(BF16), preferred_element_type=F32)
                          + bx_ref[d:d + 1, sl])
            log_a = (-RGLRU_C) * r * sp[:, sl]
            a_s[:, sl] = jnp.exp(log_a)
            th = jnp.tanh(log_a)
            b_s[:, sl] = jnp.sqrt(-2.0 * th / (1.0 - th)) * (ig * xb)

    gates(xf_ref, xfp_ref, xfn_ref, i, 0, af_s, bf_s)
    gates(xb_ref, xbp_ref, xbn_ref, n_t - 1 - i, 1, ab_s, bb_s)

    def body(r, carry):
        hf, hb = carry
        hf = af_s[pl.ds(r, 1), :] * hf + bf_s[pl.ds(r, 1), :]
        yf_ref[pl.ds(r, 1), :] = hf
        rb = tc - 1 - r
        hb = ab_s[pl.ds(rb, 1), :] * hb + bb_s[pl.ds(rb, 1), :]
        yb_ref[pl.ds(rb, 1), :] = hb
        return hf, hb

    hf, hb = lax.fori_loop(0, tc, body, (hf_s[...], hb_s[...]), unroll=8)
    hf_s[...] = hf
    hb_s[...] = hb


def _rglru_scan(proj, conv_w, conv_b, w_a, w_x, b_a, b_x, lam, *, tc=256, cw=512):
    s = proj.shape[0]
    c = conv_w.shape[1]
    tc = min(tc, s)
    n_t = s // tc
    n_c = c // cw
    xoff = c // cw
    hb = tc // SUBLANES
    last_h = s // SUBLANES - 1

    specs = [
        pl.BlockSpec((tc, cw), lambda ci, i: (i, xoff + ci)),
        pl.BlockSpec((SUBLANES, cw), lambda ci, i: (jnp.maximum(i * hb - 1, 0), xoff + ci)),
        pl.BlockSpec((SUBLANES, cw), lambda ci, i: (jnp.minimum((i + 1) * hb, last_h), xoff + ci)),
        pl.BlockSpec((tc, cw), lambda ci, i: (n_t - 1 - i, xoff + ci)),
        pl.BlockSpec((SUBLANES, cw),
                     lambda ci, i: (jnp.maximum((n_t - 1 - i) * hb - 1, 0), xoff + ci)),
        pl.BlockSpec((SUBLANES, cw),
                     lambda ci, i: (jnp.minimum((n_t - i) * hb, last_h), xoff + ci)),
        pl.BlockSpec((CONV_W, cw), lambda ci, i: (0, ci)),
        pl.BlockSpec((1, cw), lambda ci, i: (0, ci)),
        pl.BlockSpec((2, cw // LRU_BLOCK_W, LRU_BLOCK_W, LRU_BLOCK_W), lambda ci, i: (0, ci, 0, 0)),
        pl.BlockSpec((2, cw // LRU_BLOCK_W, LRU_BLOCK_W, LRU_BLOCK_W), lambda ci, i: (0, ci, 0, 0)),
        pl.BlockSpec((2, cw), lambda ci, i: (0, ci)),
        pl.BlockSpec((2, cw), lambda ci, i: (0, ci)),
        pl.BlockSpec((2, cw), lambda ci, i: (0, ci)),
    ]
    kern = functools.partial(_rglru_kernel, tc=tc, cw=cw, n_t=n_t)
    return pl.pallas_call(
        kern,
        out_shape=(jax.ShapeDtypeStruct((s, c), F32), jax.ShapeDtypeStruct((s, c), F32)),
        grid=(n_c, n_t),
        in_specs=specs,
        out_specs=(pl.BlockSpec((tc, cw), lambda ci, i: (i, ci)),
                   pl.BlockSpec((tc, cw), lambda ci, i: (n_t - 1 - i, ci))),
        scratch_shapes=[pltpu.VMEM((tc + 2 * SUBLANES, cw), F32)]
        + [pltpu.VMEM((tc, cw), F32) for _ in range(4)]
        + [pltpu.VMEM((1, cw), F32) for _ in range(2)],
        compiler_params=_params("parallel", "arbitrary"),
        name="rglru_scan",
    )(proj, proj, proj, proj, proj, proj, conv_w, conv_b.reshape(1, c), w_a, w_x, b_a, b_x, lam)


def _pro_rglru_out(rows, vecs):
    yf, yb, gate = rows
    return (yf + yb) * _gelu_tanh(gate)


def _diff_attn_kernel(q_ref, k_ref, v_ref, lq1_ref, lk1_ref, lq2_ref, lk2_ref, g_ref, o_ref,
                      s_buf, p_buf, m_s, a_s, l_s, acc_s, *, tq, tk, rc, n_kv, lambda_init):
    m_s[...] = jnp.full_like(m_s, -jnp.inf)
    l_s[...] = jnp.zeros_like(l_s)
    acc_s[...] = jnp.zeros_like(acc_s)
    q = q_ref[...]
    qs = (q[:, :HEAD_DIM], q[:, HEAD_DIM:])

    def scores(j, slot):
        off = pl.multiple_of(j * tk, tk)
        kb = k_ref[pl.ds(off, tk), :]
        for c in range(2):
            kc = kb[:, c * HEAD_DIM:(c + 1) * HEAD_DIM]
            s_buf[slot, c * tq:(c + 1) * tq, :] = lax.dot_general(
                qs[c], kc, (((1,), (1,)), ((), ())), preferred_element_type=F32)

    def update(j, slot):
        off = pl.multiple_of(j * tk, tk)
        vb = v_ref[pl.ds(off, tk), :]
        for r0 in range(0, 2 * tq, rc):
            rows = slice(r0, r0 + rc)
            sc = s_buf[slot, rows, :]
            m_old = m_s[rows, :]
            m_new = jnp.maximum(m_old, jnp.max(sc, axis=-1, keepdims=True))
            alpha = jnp.exp2(m_old - m_new)
            p = jnp.exp2(sc - m_new)
            l_s[rows, :] = alpha * l_s[rows, :] + sum(
                p[:, t * LANES:(t + 1) * LANES] for t in range(tk // LANES))
            p_buf[rows, :] = p.astype(BF16)
            m_s[rows, :] = m_new
            a_s[rows, :] = alpha
        acc_s[...] = a_s[...] * acc_s[...] + jnp.dot(p_buf[...], vb, preferred_element_type=F32)

    scores(0, 0)

    def kv_pair(jj, _):
        j = 2 * jj
        scores(j + 1, 1)
        update(j, 0)
        scores(jnp.minimum(j + 2, n_kv - 1), 0)
        update(j + 1, 1)
        return 0

    lax.fori_loop(0, n_kv // 2, kv_pair, 0)
    lam = (jnp.exp(jnp.sum(lq1_ref[...] * lk1_ref[...], axis=-1, keepdims=True))
           - jnp.exp(jnp.sum(lq2_ref[...] * lk2_ref[...], axis=-1, keepdims=True)) + lambda_init)
    l = jnp.sum(l_s[...], axis=-1, keepdims=True)
    o = acc_s[0:tq] / l[0:tq] - lam * (acc_s[tq:2 * tq] / l[tq:2 * tq])
    o_ref[...] = (_rms(o, g_ref[...]) * (1.0 - lambda_init)).astype(o_ref.dtype)


def _diff_attention(qkv, lq1, lk1, lq2, lk2, subln_g, lambda_init, *, tq=512, tk=1024, rc=32):
    s = qkv.shape[0]
    vd = 2 * HEAD_DIM
    tq = min(tq, s)
    tk = min(tk, s // 2)
    assert (s // tk) % 2 == 0
    kern = functools.partial(_diff_attn_kernel, tq=tq, tk=tk, rc=rc, n_kv=s // tk,
                             lambda_init=lambda_init)
    vec = lambda a: a.reshape(1, -1)
    vspec = lambda w: pl.BlockSpec((1, w), lambda h, i: (0, 0))
    return pl.pallas_call(
        kern,
        out_shape=jax.ShapeDtypeStruct((s, DIFF_HEADS * vd), BF16),
        grid=(DIFF_HEADS, s // tq),
        in_specs=[
            pl.BlockSpec((tq, vd), lambda h, i: (i, h)),
            pl.BlockSpec((s, vd), lambda h, i: (0, DIFF_HEADS + h)),
            pl.BlockSpec((s, vd), lambda h, i: (0, 2 * DIFF_HEADS + h)),
            vspec(HEAD_DIM), vspec(HEAD_DIM), vspec(HEAD_DIM), vspec(HEAD_DIM), vspec(vd),
        ],
        out_specs=pl.BlockSpec((tq, vd), lambda h, i: (i, h)),
        scratch_shapes=[pltpu.VMEM((2, 2 * tq, tk), F32), pltpu.VMEM((2 * tq, tk), BF16),
                        pltpu.VMEM((2 * tq, 1), F32), pltpu.VMEM((2 * tq, 1), F32),
                        pltpu.VMEM((2 * tq, LANES), F32), pltpu.VMEM((2 * tq, vd), F32)],
        compiler_params=_params("parallel", "arbitrary"),
        name="diff_attention",
    )(qkv, qkv, qkv, vec(lq1), vec(lk1), vec(lq2), vec(lk2), vec(subln_g))


def _pro_identity(rows, vecs):
    return rows[0]


def _win_attn_kernel(sink_ref, q_ref, k_ref, v_ref, o_ref, *, tq, win, s_len, group):
    kvh = pl.program_id(0)
    i = pl.program_id(1)
    scale = HEAD_DIM ** -0.5
    start = jnp.clip(i * tq - WINDOW, 0, s_len - win)
    start = pl.multiple_of(start, WINDOW)
    kw = k_ref[pl.ds(start, win), :]
    vw = v_ref[pl.ds(start, win), :]
    qpos = i * tq + lax.broadcasted_iota(jnp.int32, (tq, win), 0)
    kpos = start + lax.broadcasted_iota(jnp.int32, (tq, win), 1)
    valid = jnp.abs(kpos - qpos) <= WINDOW
    for g in range(group):
        qg = q_ref[:, g * HEAD_DIM:(g + 1) * HEAD_DIM]
        sc = lax.dot_general(qg, kw, (((1,), (1,)), ((), ())), preferred_element_type=F32) * scale
        sc = jnp.where(valid, sc, NEG_INF)
        sink = sink_ref[kvh * group + g]
        m = jnp.maximum(jnp.max(sc, axis=-1, keepdims=True), sink)
        e = jnp.exp(sc - m)
        p = e / (jnp.sum(e, axis=-1, keepdims=True) + jnp.exp(sink - m))
        o = jnp.dot(p.astype(BF16), vw, preferred_element_type=F32)
        o_ref[:, g * HEAD_DIM:(g + 1) * HEAD_DIM] = o.astype(o_ref.dtype)


def _window_attention(qkv, sink, *, tq=256):
    s = qkv.shape[0]
    group = WIN_Q_HEADS // WIN_KV_HEADS
    tq = min(tq, s)
    win = min(tq + 2 * WINDOW, s)
    qw = group * HEAD_DIM
    k0 = WIN_Q_HEADS
    v0 = WIN_Q_HEADS + WIN_KV_HEADS
    kern = functools.partial(_win_attn_kernel, tq=tq, win=win, s_len=s, group=group)
    return pl.pallas_call(
        kern,
        out_shape=jax.ShapeDtypeStruct((s, WIN_Q_HEADS * HEAD_DIM), BF16),
        grid=(WIN_KV_HEADS, s // tq),
        in_specs=[
            pl.BlockSpec(memory_space=pltpu.SMEM),
            pl.BlockSpec((tq, qw), lambda h, i: (i, h)),
            pl.BlockSpec((s, HEAD_DIM), lambda h, i: (0, k0 + h)),
            pl.BlockSpec((s, HEAD_DIM), lambda h, i: (0, v0 + h)),
        ],
        out_specs=pl.BlockSpec((tq, qw), lambda h, i: (i, h)),
        compiler_params=_params("parallel", "arbitrary"),
        name="window_attention",
    )(sink.astype(F32), qkv, qkv, qkv)


def _s5_discretize(a_re, a_im, log_dt, b_re, b_im):
    dt = jnp.exp(log_dt)[:, None]
    mag = jnp.exp(dt * a_re)
    lr, li = mag * jnp.cos(dt * a_im), mag * jnp.sin(dt * a_im)
    den = a_re * a_re + a_im * a_im
    nr, ni = lr - 1.0, li
    fr = (nr * a_re + ni * a_im) / den
    fi = (ni * a_re - nr * a_im) / den
    bbr = fr[..., None] * b_re - fi[..., None] * b_im
    bbi = fr[..., None] * b_im + fi[..., None] * b_re
    return lr, li, bbr, bbi


def _s5_block_weights(lr, li, bbr, bbi, c_re, c_im, gpc):
    g, n, c = bbr.shape
    n_k = g // gpc
    eye = jnp.eye(gpc, dtype=F32)

    def w_in(bb):
        t = bb.reshape(n_k, gpc, n, c)
        return jnp.einsum('kgnc,gh->kgchn', t, eye).reshape(n_k, gpc * c, gpc * n)

    def w_out(cc):
        t = cc.reshape(n_k, gpc, c, n)
        return jnp.einsum('kgcn,gh->khngc', t, eye).reshape(n_k, gpc * n, gpc * c)

    win = jnp.concatenate([w_in(bbr), w_in(bbi)], axis=2)
    wout = jnp.concatenate([w_out(c_re), -w_out(c_im)], axis=1)
    lam = jnp.stack([lr.reshape(n_k, gpc * n), li.reshape(n_k, gpc * n)], axis=1)
    return win, wout, lam


def _s5_kernel(uf_ref, ub_ref, win_ref, wout_ref, lam_ref, yf_ref, yb_ref,
               xf_s, xb_s, st_s, *, tc, ns):
    i = pl.program_id(1)

    @pl.when(i == 0)
    def _():
        st_s[...] = jnp.zeros_like(st_s)

    xf_s[...] = jnp.dot(uf_ref[...].astype(BF16), win_ref[0, 0].astype(BF16),
                        preferred_element_type=F32)
    xb_s[...] = jnp.dot(ub_ref[...].astype(BF16), win_ref[1, 0].astype(BF16),
                        preferred_element_type=F32)
    lrf, lif = lam_ref[0, 0, 0:1, :], lam_ref[0, 0, 1:2, :]
    lrb, lib = lam_ref[1, 0, 0:1, :], lam_ref[1, 0, 1:2, :]

    def body(r, carry):
        srf, sif, srb, sib = carry
        nrf = lrf * srf - lif * sif + xf_s[pl.ds(r, 1), 0:ns]
        nif = lrf * sif + lif * srf + xf_s[pl.ds(r, 1), ns:2 * ns]
        xf_s[pl.ds(r, 1), 0:ns] = nrf
        xf_s[pl.ds(r, 1), ns:2 * ns] = nif
        rb = tc - 1 - r
        nrb = lrb * srb - lib * sib + xb_s[pl.ds(rb, 1), 0:ns]
        nib = lrb * sib + lib * srb + xb_s[pl.ds(rb, 1), ns:2 * ns]
        xb_s[pl.ds(rb, 1), 0:ns] = nrb
        xb_s[pl.ds(rb, 1), ns:2 * ns] = nib
        return nrf, nif, nrb, nib

    init = (st_s[0:1, :], st_s[1:2, :], st_s[2:3, :], st_s[3:4, :])
    srf, sif, srb, sib = lax.fori_loop(0, tc, body, init, unroll=8)
    st_s[0:1, :] = srf
    st_s[1:2, :] = sif
    st_s[2:3, :] = srb
    st_s[3:4, :] = sib
    yf_ref[...] = jnp.dot(xf_s[...].astype(BF16), wout_ref[0, 0].astype(BF16),
                          preferred_element_type=F32)
    yb_ref[...] = jnp.dot(xb_s[...].astype(BF16), wout_ref[1, 0].astype(BF16),
                          preferred_element_type=F32)


def _s5_scan(u, win, wout, lam, *, tc=256):
    s, w = u.shape
    _, n_k, cw, ns2 = win.shape
    ns = ns2 // 2
    tc = min(tc, s)
    n_t = s // tc
    kern = functools.partial(_s5_kernel, tc=tc, ns=ns)
    return pl.pallas_call(
        kern,
        out_shape=(jax.ShapeDtypeStruct((s, w), F32), jax.ShapeDtypeStruct((s, w), F32)),
        grid=(n_k, n_t),
        in_specs=[
            pl.BlockSpec((tc, cw), lambda k, i: (i, k)),
            pl.BlockSpec((tc, cw), lambda k, i: (n_t - 1 - i, k)),
            pl.BlockSpec((2, 1, cw, ns2), lambda k, i: (0, k, 0, 0)),
            pl.BlockSpec((2, 1, ns2, cw), lambda k, i: (0, k, 0, 0)),
            pl.BlockSpec((2, 1, 2, ns), lambda k, i: (0, k, 0, 0)),
        ],
        out_specs=(pl.BlockSpec((tc, cw), lambda k, i: (i, k)),
                   pl.BlockSpec((tc, cw), lambda k, i: (n_t - 1 - i, k))),
        scratch_shapes=[pltpu.VMEM((tc, ns2), F32), pltpu.VMEM((tc, ns2), F32),
                        pltpu.VMEM((4, ns), F32)],
        compiler_params=_params("parallel", "arbitrary"),
        name="s5_scan",
    )(u, u, win, wout, lam)


def _pro_s5_glu(rows, vecs):
    yf, yb, u = rows
    return _gelu_tanh(yf + yb + vecs[0] * u)


def _epi_glu(acc, epi_refs, o_ref, j, a32_s):
    o_ref[...] = (a32_s[...] * _sigmoid(acc)).astype(o_ref.dtype)


def _router_kernel(x_ref, g_ref, w_ref, h_ref, idx_ref, wt_ref):
    h = _rms(x_ref[...], g_ref[...])
    h_ref[...] = h
    logits = jnp.dot(h, w_ref[...], preferred_element_type=F32, precision=lax.Precision.HIGHEST)
    tm = logits.shape[0]
    lane = lax.broadcasted_iota(jnp.int32, (tm, LANES), 1)
    big = jnp.int32(LANES)
    ninf = -jnp.inf
    gl = jnp.where(lane < MOE_GROUPS, logits, ninf)
    gm = jnp.max(gl, axis=-1, keepdims=True)
    ge = jnp.exp(gl - gm)
    g_prob = ge / jnp.sum(ge, axis=-1, keepdims=True)
    g_p = jnp.max(g_prob, axis=-1, keepdims=True)
    g_idx = jnp.min(jnp.where(g_prob == g_p, lane, big), axis=-1, keepdims=True)
    lo = MOE_GROUPS + g_idx * EXPERTS_PER_GROUP
    in_grp = (lane >= lo) & (lane < lo + EXPERTS_PER_GROUP)
    el = jnp.where(in_grp, logits, ninf)
    em = jnp.max(el, axis=-1, keepdims=True)
    ee = jnp.exp(el - em)
    e_prob = jnp.where(in_grp, ee / jnp.sum(ee, axis=-1, keepdims=True), -1.0)
    p1 = jnp.max(e_prob, axis=-1, keepdims=True)
    i1 = jnp.min(jnp.where(e_prob == p1, lane, big), axis=-1, keepdims=True)
    rest = jnp.where(lane == i1, -1.0, e_prob)
    p2 = jnp.max(rest, axis=-1, keepdims=True)
    i2 = jnp.min(jnp.where(rest == p2, lane, big), axis=-1, keepdims=True)
    denom = p1 + p2
    w1 = g_p * (p1 / denom)
    w2 = g_p * (p2 / denom)
    idx_ref[...] = jnp.where(lane == 0, i1 - MOE_GROUPS, jnp.where(lane == 1, i2 - MOE_GROUPS, 0))
    wt_ref[...] = jnp.where(lane == 0, w1, jnp.where(lane == 1, w2, 0.0))


def _router(x, g, w_group, w_expert, *, tm=512):
    s, d = x.shape
    tm = min(tm, s)
    w_r = jnp.concatenate(
        [w_group, w_expert, jnp.zeros((d, LANES - MOE_GROUPS - N_EXPERTS), F32)], axis=1)
    return pl.pallas_call(
        _router_kernel,
        out_shape=(jax.ShapeDtypeStruct((s, d), F32), jax.ShapeDtypeStruct((s, LANES), jnp.int32),
                   jax.ShapeDtypeStruct((s, LANES), F32)),
        grid=(s // tm,),
        in_specs=[pl.BlockSpec((tm, d), lambda i: (i, 0)), pl.BlockSpec((1, d), lambda i: (0, 0)),
                  pl.BlockSpec((d, LANES), lambda i: (0, 0))],
        out_specs=(pl.BlockSpec((tm, d), lambda i: (i, 0)), pl.BlockSpec((tm, LANES), lambda i: (i, 0)),
                   pl.BlockSpec((tm, LANES), lambda i: (i, 0))),
        compiler_params=_params("parallel"),
        name="moe_router",
    )(x, g.reshape(1, d), w_r)


def _rank_kernel(idx_ref, dest_ref, cnt_ref, tot_s, pst_s, run_s, *, tm):
    ph = pl.program_id(0)
    i = pl.program_id(1)
    lane = lax.broadcasted_iota(jnp.int32, (tm, LANES), 1)
    idx = idx_ref[...]
    oh0 = (lane == idx[:, 0:1]).astype(F32)
    oh1 = (lane == idx[:, 1:2]).astype(F32)
    c = oh0 + oh1
    csum = jnp.sum(c, axis=0, keepdims=True)

    @pl.when(jnp.logical_and(ph == 0, i == 0))
    def _():
        tot_s[...] = jnp.zeros_like(tot_s)

    @pl.when(ph == 0)
    def _():
        tot_s[...] += csum

    @pl.when(jnp.logical_and(ph == 1, i == 0))
    def _():
        counts = tot_s[...]
        nblk = jnp.floor((counts + (EXPERT_BLOCK - 1)) * (1.0 / EXPERT_BLOCK))
        r = lax.broadcasted_iota(jnp.int32, (LANES, LANES), 0)
        cc = lax.broadcasted_iota(jnp.int32, (LANES, LANES), 1)
        upper = (r < cc).astype(F32)
        excl = jnp.dot(jnp.broadcast_to(nblk, (SUBLANES, LANES)), upper,
                       preferred_element_type=F32, precision=lax.Precision.HIGHEST)
        pst_s[...] = excl[0:1, :] * EXPERT_BLOCK
        run_s[...] = jnp.zeros_like(run_s)
        cnt_ref[...] = jnp.broadcast_to(counts, cnt_ref.shape)

    @pl.when(ph == 1)
    def _():
        rr = lax.broadcasted_iota(jnp.int32, (tm, tm), 0)
        cr = lax.broadcasted_iota(jnp.int32, (tm, tm), 1)
        lower = (rr > cr).astype(BF16)
        before = jnp.dot(lower, c.astype(BF16), preferred_element_type=F32) + run_s[...]
        base = pst_s[...] + before
        d0 = jnp.sum(oh0 * base, axis=-1, keepdims=True)
        d1 = jnp.sum(oh1 * (base + oh0), axis=-1, keepdims=True)
        dest_ref[...] = jnp.where(lane == 0, d0, jnp.where(lane == 1, d1, 0.0)).astype(jnp.int32)
        run_s[...] += csum


def _dispatch_rank(idx, *, tm=512):
    s = idx.shape[0]
    tm = min(tm, s)
    kern = functools.partial(_rank_kernel, tm=tm)
    return pl.pallas_call(
        kern,
        out_shape=(jax.ShapeDtypeStruct((s, LANES), jnp.int32),
                   jax.ShapeDtypeStruct((SUBLANES, LANES), F32)),
        grid=(2, s // tm),
        in_specs=[pl.BlockSpec((tm, LANES), lambda ph, i: (i, 0))],
        out_specs=(pl.BlockSpec((tm, LANES), lambda ph, i: (i * ph, 0)),
                   pl.BlockSpec((SUBLANES, LANES), lambda ph, i: (0, 0))),
        scratch_shapes=[pltpu.VMEM((1, LANES), F32) for _ in range(3)],
        compiler_params=_params("arbitrary", "arbitrary"),
        name="moe_rank",
    )(idx)


def _block_experts(counts, n_blk):
    cnt = counts[0, :N_EXPERTS].astype(jnp.int32)
    padded = ((cnt + EXPERT_BLOCK - 1) // EXPERT_BLOCK) * EXPERT_BLOCK
    pends = jnp.cumsum(padded)
    blk_start = jnp.arange(n_blk, dtype=jnp.int32) * EXPERT_BLOCK
    return jnp.minimum(jnp.searchsorted(pends, blk_start, side='right'),
                       N_EXPERTS - 1).astype(jnp.int32)


def _scatter_kernel(dest_ref, h_hbm, xs_in_hbm, xs_hbm, sem, *, tb, n_b):
    del xs_in_hbm
    b = pl.program_id(0)
    slot = b % 2
    n = tb * TOP_K

    def copy(tok, row, sl):
        return pltpu.make_async_copy(h_hbm.at[pl.ds(tok, 1), :], xs_hbm.at[pl.ds(row, 1), :],
                                     sem.at[sl])

    def wait_all(sl):
        def body(r, _):
            copy(0, 0, sl).wait()
            return 0
        lax.fori_loop(0, n, body, 0, unroll=8)

    for r in range(tb):
        tok = b * tb + r
        for k in range(TOP_K):
            copy(tok, dest_ref[tok * TOP_K + k], slot).start()

    @pl.when(b > 0)
    def _():
        wait_all(1 - slot)

    @pl.when(b == n_b - 1)
    def _():
        wait_all(slot)


def _dispatch_scatter(h, dest_flat, n_rows, *, tb=128):
    s, d = h.shape
    tb = min(tb, s)
    n_b = s // tb
    kern = functools.partial(_scatter_kernel, tb=tb, n_b=n_b)
    grid_spec = pltpu.PrefetchScalarGridSpec(
        num_scalar_prefetch=1,
        grid=(n_b,),
        in_specs=[pl.BlockSpec(memory_space=pl.ANY), pl.BlockSpec(memory_space=pl.ANY)],
        out_specs=pl.BlockSpec(memory_space=pl.ANY),
        scratch_shapes=[pltpu.SemaphoreType.DMA((2,))],
    )
    return pl.pallas_call(
        kern,
        out_shape=jax.ShapeDtypeStruct((n_rows, d), h.dtype),
        grid_spec=grid_spec,
        input_output_aliases={2: 0},
        compiler_params=_params("arbitrary"),
        name="moe_scatter",
    )(dest_flat, h, jnp.zeros((n_rows, d), h.dtype))


def _row_copy(src_hbm, src_row, dst_buf, slot, dst_row, sem):
    return pltpu.make_async_copy(src_hbm.at[pl.ds(src_row, 1), :],
                                 dst_buf.at[slot, pl.ds(dst_row, 1), :], sem.at[slot])


def _gather_start(idx_ref, base, n, src_hbm, dst_buf, slot, sem):
    for r in range(n):
        _row_copy(src_hbm, idx_ref[base + r], dst_buf, slot, r, sem).start()


def _gather_wait(n, src_hbm, dst_buf, slot, sem):
    def body(r, _):
        _row_copy(src_hbm, 0, dst_buf, slot, r, sem).wait()
        return 0
    lax.fori_loop(0, n, body, 0, unroll=8)


def _expert_kernel(be_ref, x_ref, wg_ref, wu_ref, wd_ref, y_ref, wg_s, wu_s, wd_s):
    b = pl.program_id(0)
    changed = jnp.logical_or(b == 0, be_ref[b] != be_ref[jnp.maximum(b - 1, 0)])

    @pl.when(changed)
    def _():
        wg_s[...] = wg_ref[...].astype(BF16)
        wu_s[...] = wu_ref[...].astype(BF16)
        wd_s[...] = wd_ref[...].astype(BF16)

    x = x_ref[...].astype(BF16)
    hg = jnp.dot(x, wg_s[...], preferred_element_type=F32)
    hu = jnp.dot(x, wu_s[...], preferred_element_type=F32)
    hdn = (hg * _sigmoid(hg)) * hu
    y_ref[...] = jnp.dot(hdn.astype(BF16), wd_s[...], preferred_element_type=F32)


def _expert_mlp(xs, blk_e, layer, w_gate, w_up, w_down):
    n_rows, d = xs.shape
    de = w_gate.shape[3]
    grid_spec = pltpu.PrefetchScalarGridSpec(
        num_scalar_prefetch=1,
        grid=(n_rows // EXPERT_BLOCK,),
        in_specs=[
            pl.BlockSpec((EXPERT_BLOCK, d), lambda b, be: (b, 0)),
            pl.BlockSpec((None, None, d, de), lambda b, be: (layer, be[b], 0, 0)),
            pl.BlockSpec((None, None, d, de), lambda b, be: (layer, be[b], 0, 0)),
            pl.BlockSpec((None, None, de, d), lambda b, be: (layer, be[b], 0, 0)),
        ],
        out_specs=pl.BlockSpec((EXPERT_BLOCK, d), lambda b, be: (b, 0)),
        scratch_shapes=[pltpu.VMEM((d, de), BF16), pltpu.VMEM((d, de), BF16),
                        pltpu.VMEM((de, d), BF16)],
    )
    return pl.pallas_call(
        _expert_kernel,
        out_shape=jax.ShapeDtypeStruct((n_rows, d), F32),
        grid_spec=grid_spec,
        compiler_params=_params("arbitrary"),
        name="moe_experts",
    )(blk_e, xs, w_gate, w_up, w_down)


def _combine_kernel(dest_ref, x_ref, wt_ref, ys_hbm, g_ref, o_ref, ybuf, sem, *, tb, n_b, final):
    b = pl.program_id(0)
    slot = b % 2
    n = tb * TOP_K

    @pl.when(b == 0)
    def _():
        _gather_start(dest_ref, 0, n, ys_hbm, ybuf, 0, sem)

    _gather_wait(n, ys_hbm, ybuf, slot, sem)
    _gather_start(dest_ref, (b + 1) * n, n, ys_hbm, ybuf, 1 - slot, sem)
    y0 = ybuf[slot, 0:tb, :]
    y1 = ybuf[slot, tb:2 * tb, :]
    wt = wt_ref[...]
    out = x_ref[...] + (y0 * wt[:, 0:1] + y1 * wt[:, 1:2])
    if final:
        out = _rms(out, g_ref[...])
    o_ref[...] = out

    @pl.when(b == n_b - 1)
    def _():
        _gather_wait(n, ys_hbm, ybuf, 1 - slot, sem)


def _moe_combine(x, wt, ys, dest, g_final, *, final, tb=128):
    s, d = x.shape
    tb = min(tb, s)
    n_b = s // tb
    dest = dest.reshape(n_b, tb, TOP_K).transpose(0, 2, 1).reshape(-1)
    dest = jnp.concatenate([dest, jnp.zeros((tb * TOP_K,), jnp.int32)])
    kern = functools.partial(_combine_kernel, tb=tb, n_b=n_b, final=final)
    grid_spec = pltpu.PrefetchScalarGridSpec(
        num_scalar_prefetch=1,
        grid=(n_b,),
        in_specs=[
            pl.BlockSpec((tb, d), lambda b, dr: (b, 0)),
            pl.BlockSpec((tb, LANES), lambda b, dr: (b, 0)),
            pl.BlockSpec(memory_space=pl.ANY),
            pl.BlockSpec((1, d), lambda b, dr: (0, 0)),
        ],
        out_specs=pl.BlockSpec((tb, d), lambda b, dr: (b, 0)),
        scratch_shapes=[pltpu.VMEM((2, tb * TOP_K, d), F32), pltpu.SemaphoreType.DMA((2,))],
    )
    return pl.pallas_call(
        kern,
        out_shape=jax.ShapeDtypeStruct((s, d), F32),
        grid_spec=grid_spec,
        compiler_params=_params("arbitrary"),
        name="moe_combine",
    )(dest, x, wt, ys, g_final.reshape(1, d))


def _hier_moe(x, ln_g, w_group, w_expert, layer, w_gate, w_up, w_down, g_final, *, final):
    s = x.shape[0]
    h, idx, wt = _router(x, ln_g, w_group, w_expert)
    dest, counts = _dispatch_rank(idx)
    dest = dest[:, :TOP_K].reshape(-1)
    n_rows = -(-(s * TOP_K + N_EXPERTS * (EXPERT_BLOCK - 1)) // EXPERT_BLOCK) * EXPERT_BLOCK
    blk_e = _block_experts(counts, n_rows // EXPERT_BLOCK)
    xs = _dispatch_scatter(h, dest, n_rows)
    ys = _expert_mlp(xs, blk_e, layer, w_gate, w_up, w_down)
    return _moe_combine(x, wt, ys, dest, g_final, final=final)


def _rglru_layer(x, ln_g, w_in, conv_w, conv_b, w_a, w_x, b_a, b_x, lam, w_out):
    d = x.shape[1]
    c = conv_w.shape[1]
    proj = _norm_matmul(x, ln_g, w_in, name="lru_in_proj")
    yf, yb = _rglru_scan(proj, conv_w, conv_b, w_a, w_x, b_a, b_x, lam)
    return _matmul_residual([(yf, c, 0), (yb, c, 0), (proj, c, 0)], [], w_out, x, _pro_rglru_out,
                            k_dim=c, tm=512, name="lru_out_proj")


def _diff_layer(x, ln_g, rope, w_qkv, lq1, lk1, lq2, lk2, subln_g, w_out, lambda_init):
    qk_cols = 2 * DIFF_HEADS * 2 * HEAD_DIM
    qkv = _norm_matmul(x, ln_g, w_qkv, rope=rope, n_rope_cols=qk_cols, n_q_cols=qk_cols // 2,
                       q_scale=HEAD_DIM ** -0.5 * math.log2(math.e), out_dtype=BF16,
                       name="diff_qkv_proj")
    o = _diff_attention(qkv, lq1, lk1, lq2, lk2, subln_g, lambda_init)
    return _matmul_residual([(o, o.shape[1], 0)], [], w_out, x, _pro_identity,
                            k_dim=o.shape[1], name="diff_out_proj")


def _window_layer(x, ln_g, rope, w_qkv, sink, w_out):
    qk_cols = (WIN_Q_HEADS + WIN_KV_HEADS) * HEAD_DIM
    qkv = _norm_matmul(x, ln_g, w_qkv, rope=rope, n_rope_cols=qk_cols, out_dtype=BF16,
                       name="win_qkv_proj")
    o = _window_attention(qkv, sink)
    return _matmul_residual([(o, o.shape[1], 0)], [], w_out, x, _pro_identity,
                            k_dim=o.shape[1], name="win_out_proj")


def _s5_layer(x, ln_g, w_in, a_re, a_im, log_dt, b_re, b_im, c_re, c_im, d_skip, w_glu, w_out,
              *, gpc=8):
    u = _norm_matmul(x, ln_g, w_in, name="s5_in_proj")
    w = u.shape[1]
    wins, wouts, lams = [], [], []
    for dd in range(2):
        lr, li, bbr, bbi = _s5_discretize(a_re[dd], a_im[dd], log_dt[dd], b_re[dd], b_im[dd])
        wi, wo, lm = _s5_block_weights(lr, li, bbr, bbi, c_re[dd], c_im[dd], gpc)
        wins.append(wi)
        wouts.append(wo)
        lams.append(lm)
    yf, yb = _s5_scan(u, jnp.stack(wins), jnp.stack(wouts), jnp.stack(lams))
    z = _fused_mm([(yf, w, 0), (yb, w, 0), (u, w, 0)], [d_skip.reshape(1, w)], w_glu, [],
                  _pro_s5_glu, _epi_glu, out_dtype=BF16, tm=512, tn=w, k_dim=w, keep_f32=True,
                  name="s5_glu")
    return _matmul_residual([(z, w, 0)], [], w_out, x, _pro_identity, k_dim=w, name="s5_out_proj")


def kernel(x, positions, ln_mix, ln_ffn, ln_final, lru_w_in, lru_conv_w, lru_conv_b, lru_w_a, lru_w_x, lru_b_a, lru_b_x, lru_lambda, lru_w_out, diff_w_qkv, diff_lq1, diff_lk1, diff_lq2, diff_lk2, diff_subln, diff_w_out, win_w_qkv, win_sink, win_w_out, s5_w_in, s5_a_re, s5_a_im, s5_log_dt, s5_b_re, s5_b_im, s5_c_re, s5_c_im, s5_d, s5_w_glu, s5_w_out, moe_w_group, moe_w_expert, moe_w_gate, moe_w_up, moe_w_down):
    batch, s, d = x.shape
    depth = ln_mix.shape[0]
    outs = []
    for b in range(batch):
        xb = x[b]
        rope = _rope_tables(positions[b])
        for i in range(depth):
            kind, j = i % 4, i // 4
            if kind == 0:
                xb = _rglru_layer(xb, ln_mix[i], lru_w_in[j], lru_conv_w[j], lru_conv_b[j],
                                  lru_w_a[j], lru_w_x[j], lru_b_a[j], lru_b_x[j], lru_lambda[j],
                                  lru_w_out[j])
            elif kind == 1:
                xb = _diff_layer(xb, ln_mix[i], rope, diff_w_qkv[j], diff_lq1[j], diff_lk1[j],
                                 diff_lq2[j], diff_lk2[j], diff_subln[j], diff_w_out[j],
                                 0.8 - 0.6 * math.exp(-0.3 * i))
            elif kind == 2:
                xb = _window_layer(xb, ln_mix[i], rope, win_w_qkv[j], win_sink[j], win_w_out[j])
            else:
                xb = _s5_layer(xb, ln_mix[i], s5_w_in[j], s5_a_re[j], s5_a_im[j], s5_log_dt[j],
                               s5_b_re[j], s5_b_im[j], s5_c_re[j], s5_c_im[j], s5_d[j],
                               s5_w_glu[j], s5_w_out[j])
            xb = _hier_moe(xb, ln_ffn[i], moe_w_group[i], moe_w_expert[i], i, moe_w_gate,
                           moe_w_up, moe_w_down, ln_final, final=(i == depth - 1))
        outs.append(xb)
    return jnp.stack(outs)
```

```python
import functools
import math

import jax
import jax.numpy as jnp
from jax import lax
from jax.experimental import pallas as pl
from jax.experimental.pallas import tpu as pltpu

F32 = jnp.float32
BF16 = jnp.bfloat16

NORM_EPS = 1e-6
NEG_INF = -1e30
LANES = 128
SUBLANES = 8
VMEM_LIMIT = 56 * 1024 * 1024

HEAD_DIM = 128
ROT_DIM = HEAD_DIM // 4
ROPE_THETA = 500000.0
RGLRU_C = 8.0
CONV_W = 4
LRU_BLOCK_W = 128
WINDOW = 128
DIFF_HEADS = 8
WIN_Q_HEADS = 16
WIN_KV_HEADS = 4
SSM_GROUP_CH = 16
SSM_STATE = 64
MOE_GROUPS = 4
EXPERTS_PER_GROUP = 8
N_EXPERTS = MOE_GROUPS * EXPERTS_PER_GROUP
TOP_K = 2
EXPERT_BLOCK = 128


def _params(*sem):
    return pltpu.CompilerParams(dimension_semantics=sem, vmem_limit_bytes=VMEM_LIMIT)


def _rms(x, g):
    ms = jnp.mean(x * x, axis=-1, keepdims=True)
    return x * lax.rsqrt(ms + NORM_EPS) * g


def _gelu_tanh(x):
    return 0.5 * x * (1.0 + jnp.tanh(math.sqrt(2.0 / math.pi) * (x + 0.044715 * (x * x * x))))


def _sigmoid(x):
    return 1.0 / (1.0 + jnp.exp(-x))


def _fused_mm_kernel(*refs, n_row, n_vec, n_epi, prologue, epilogue, keep_f32):
    row_refs = refs[:n_row]
    vec_refs = refs[n_row:n_row + n_vec]
    w_ref = refs[n_row + n_vec]
    epi_refs = refs[n_row + n_vec + 1:n_row + n_vec + 1 + n_epi]
    o_ref = refs[n_row + n_vec + 1 + n_epi]
    a_s = refs[n_row + n_vec + 2 + n_epi]
    a32_s = refs[n_row + n_vec + 3 + n_epi] if keep_f32 else None
    j = pl.program_id(1)

    @pl.when(j == 0)
    def _():
        a = prologue([r[...] for r in row_refs], [v[...] for v in vec_refs])
        a_s[...] = a.astype(BF16)
        if keep_f32:
            a32_s[...] = a

    acc = jnp.dot(a_s[...], w_ref[...].astype(BF16), preferred_element_type=F32)
    epilogue(acc, epi_refs, o_ref, j, a32_s)


def _fused_mm(row_inputs, vec_inputs, w, epi_inputs, prologue, epilogue, *, out_dtype, tm, tn,
              k_dim, keep_f32=False, name):
    s = row_inputs[0][0].shape[0]
    n = w.shape[1]
    tm = min(tm, s)
    tn = min(tn, n)
    in_specs = []
    args = []
    for arr, width, cb in row_inputs:
        in_specs.append(pl.BlockSpec((tm, width), lambda i, j, cb=cb: (i, cb)))
        args.append(arr)
    for arr in vec_inputs:
        in_specs.append(pl.BlockSpec(arr.shape, lambda i, j: (0, 0)))
        args.append(arr)
    in_specs.append(pl.BlockSpec((k_dim, tn), lambda i, j: (0, j)))
    args.append(w)
    for arr, width, per_tile in epi_inputs:
        if per_tile:
            in_specs.append(pl.BlockSpec((tm, width), lambda i, j: (i, j)))
        else:
            in_specs.append(pl.BlockSpec((tm, width), lambda i, j: (i, 0)))
        args.append(arr)
    scratch = [pltpu.VMEM((tm, k_dim), BF16)]
    if keep_f32:
        scratch.append(pltpu.VMEM((tm, k_dim), F32))
    kern = functools.partial(_fused_mm_kernel, n_row=len(row_inputs), n_vec=len(vec_inputs),
                             n_epi=len(epi_inputs), prologue=prologue, epilogue=epilogue,
                             keep_f32=keep_f32)
    return pl.pallas_call(
        kern,
        out_shape=jax.ShapeDtypeStruct((s, n), out_dtype),
        grid=(s // tm, n // tn),
        in_specs=in_specs,
        out_specs=pl.BlockSpec((tm, tn), lambda i, j: (i, j)),
        scratch_shapes=scratch,
        compiler_params=_params("parallel", "arbitrary"),
        name=name,
    )(*args)


def _pro_rms(rows, vecs):
    return _rms(rows[0], vecs[0])


def _epi_store(acc, epi_refs, o_ref, j, a32_s):
    o_ref[...] = acc.astype(o_ref.dtype)


def _epi_residual(acc, epi_refs, o_ref, j, a32_s):
    o_ref[...] = (epi_refs[0][...] + acc).astype(o_ref.dtype)


def _make_epi_rope(n_rope_tiles, n_q_tiles, q_scale, tn):
    def epi(acc, epi_refs, o_ref, j, a32_s):
        c_ref, s1_ref, s2_ref = epi_refs

        @pl.when(j < n_rope_tiles)
        def _():
            c = c_ref[...]
            s1 = s1_ref[...]
            s2 = s2_ref[...]
            sc = jnp.where(j < n_q_tiles, q_scale, 1.0).astype(F32)
            for hh in range(tn // HEAD_DIM):
                xs = acc[:, hh * HEAD_DIM:(hh + 1) * HEAD_DIM]
                rot = (xs * c + pltpu.roll(xs, HEAD_DIM - ROT_DIM // 2, 1) * s1
                       + pltpu.roll(xs, ROT_DIM // 2, 1) * s2) * sc
                o_ref[:, hh * HEAD_DIM:(hh + 1) * HEAD_DIM] = rot.astype(o_ref.dtype)

        @pl.when(j >= n_rope_tiles)
        def _():
            o_ref[...] = acc.astype(o_ref.dtype)

    return epi


def _rope_tables(positions):
    half = ROT_DIM // 2
    inv = ROPE_THETA ** (-jnp.arange(0, ROT_DIM, 2, dtype=F32) / ROT_DIM)
    ang = positions.astype(F32)[:, None] * inv
    cos, sin = jnp.cos(ang), jnp.sin(ang)
    s = positions.shape[0]
    ones = jnp.ones((s, HEAD_DIM - ROT_DIM), F32)
    zeros = jnp.zeros((s, HEAD_DIM - ROT_DIM), F32)
    zh = jnp.zeros((s, half), F32)
    c_tab = jnp.concatenate([cos, cos, ones], axis=1)
    s1_tab = jnp.concatenate([-sin, zh, zeros], axis=1)
    s2_tab = jnp.concatenate([zh, sin, zeros], axis=1)
    return c_tab, s1_tab, s2_tab


def _norm_matmul(x, g, w, *, rope=None, n_rope_cols=0, n_q_cols=0, q_scale=1.0, out_dtype=F32,
                 tm=1024, tn=512, name):
    d = x.shape[1]
    if rope is None:
        epi, epi_inputs = _epi_store, []
    else:
        tn = min(tn, w.shape[1])
        assert n_rope_cols % tn == 0 and n_q_cols % tn == 0
        epi = _make_epi_rope(n_rope_cols // tn, n_q_cols // tn, q_scale, tn)
        epi_inputs = [(t, HEAD_DIM, False) for t in rope]
    return _fused_mm([(x, d, 0)], [g.reshape(1, d)], w, epi_inputs, _pro_rms, epi,
                     out_dtype=out_dtype, tm=tm, tn=tn, k_dim=d, name=name)


def _matmul_residual(row_inputs, vec_inputs, w, res, prologue, *, k_dim, tm=1024, tn=512, name):
    return _fused_mm(row_inputs, vec_inputs, w, [(res, min(tn, w.shape[1]), True)], prologue,
                     _epi_residual, out_dtype=F32, tm=tm, tn=tn, k_dim=k_dim, name=name)


def _rglru_kernel(xf_ref, xfp_ref, xfn_ref, xb_ref, xbp_ref, xbn_ref, cw_ref, cb_ref, wa_ref,
                  wx_ref, ba_ref, bx_ref, lam_ref, yf_ref, yb_ref,
                  ext_s, af_s, bf_s, ab_s, bb_s, hf_s, hb_s, *, tc, cw, n_t):
    i = pl.program_id(1)
    halo = SUBLANES

    @pl.when(i == 0)
    def _():
        hf_s[...] = jnp.zeros_like(hf_s)
        hb_s[...] = jnp.zeros_like(hb_s)

    def gates(x_ref, xp_ref, xn_ref, chunk, d, a_s, b_s):
        prev = jnp.where(chunk == 0, 0.0, xp_ref[...])
        nxt = jnp.where(chunk == n_t - 1, 0.0, xn_ref[...])
        ext_s[0:halo, :] = prev
        ext_s[halo:halo + tc, :] = x_ref[...]
        ext_s[halo + tc:halo + tc + halo, :] = nxt
        xc = cb_ref[...] + sum(
            cw_ref[k:k + 1, :] * ext_s[halo - 2 + k:halo - 2 + k + tc, :] for k in range(CONV_W))
        lam = lam_ref[d:d + 1, :]
        z = -lam
        sp = jnp.maximum(z, 0.0) + jnp.log1p(jnp.exp(-jnp.abs(z)))
        for blk in range(cw // LRU_BLOCK_W):
            sl = slice(blk * LRU_BLOCK_W, (blk + 1) * LRU_BLOCK_W)
            xb = xc[:, sl]
            xbh = xb.astype(BF16)
            r = _sigmoid(jnp.dot(xbh, wa_ref[d, blk].astype(BF16), preferred_element_type=F32)
                         + ba_ref[d:d + 1, sl])
            ig = _sigmoid(jnp.dot(xbh, wx_ref[d, blk].astype(BF16), preferred_element_type=F32)
                          + bx_ref[d:d + 1, sl])
            log_a = (-RGLRU_C) * r * sp[:, sl]
            a_s[:, sl] = jnp.exp(log_a)
            th = jnp.tanh(log_a)
            b_s[:, sl] = jnp.sqrt(-2.0 * th / (1.0 - th)) * (ig * xb)

    gates(xf_ref, xfp_ref, xfn_ref, i, 0, af_s, bf_s)
    gates(xb_ref, xbp_ref, xbn_ref, n_t - 1 - i, 1, ab_s, bb_s)

    def body(r, carry):
        hf, hb = carry
        hf = af_s[pl.ds(r, 1), :] * hf + bf_s[pl.ds(r, 1), :]
        yf_ref[pl.ds(r, 1), :] = hf
        rb = tc - 1 - r
        hb = ab_s[pl.ds(rb, 1), :] * hb + bb_s[pl.ds(rb, 1), :]
        yb_ref[pl.ds(rb, 1), :] = hb
        return hf, hb

    hf, hb = lax.fori_loop(0, tc, body, (hf_s[...], hb_s[...]), unroll=8)
    hf_s[...] = hf
    hb_s[...] = hb


def _rglru_scan(proj, conv_w, conv_b, w_a, w_x, b_a, b_x, lam, *, tc=256, cw=512):
    s = proj.shape[0]
    c = conv_w.shape[1]
    tc = min(tc, s)
    n_t = s // tc
    n_c = c // cw
    xoff = c // cw
    hb = tc // SUBLANES
    last_h = s // SUBLANES - 1

    specs = [
        pl.BlockSpec((tc, cw), lambda ci, i: (i, xoff + ci)),
        pl.BlockSpec((SUBLANES, cw), lambda ci, i: (jnp.maximum(i * hb - 1, 0), xoff + ci)),
        pl.BlockSpec((SUBLANES, cw), lambda ci, i: (jnp.minimum((i + 1) * hb, last_h), xoff + ci)),
        pl.BlockSpec((tc, cw), lambda ci, i: (n_t - 1 - i, xoff + ci)),
        pl.BlockSpec((SUBLANES, cw),
                     lambda ci, i: (jnp.maximum((n_t - 1 - i) * hb - 1, 0), xoff + ci)),
        pl.BlockSpec((SUBLANES, cw),
                     lambda ci, i: (jnp.minimum((n_t - i) * hb, last_h), xoff + ci)),
        pl.BlockSpec((CONV_W, cw), lambda ci, i: (0, ci)),
        pl.BlockSpec((1, cw), lambda ci, i: (0, ci)),
        pl.BlockSpec((2, cw // LRU_BLOCK_W, LRU_BLOCK_W, LRU_BLOCK_W), lambda ci, i: (0, ci, 0, 0)),
        pl.BlockSpec((2, cw // LRU_BLOCK_W, LRU_BLOCK_W, LRU_BLOCK_W), lambda ci, i: (0, ci, 0, 0)),
        pl.BlockSpec((2, cw), lambda ci, i: (0, ci)),
        pl.BlockSpec((2, cw), lambda ci, i: (0, ci)),
        pl.BlockSpec((2, cw), lambda ci, i: (0, ci)),
    ]
    kern = functools.partial(_rglru_kernel, tc=tc, cw=cw, n_t=n_t)
    return pl.pallas_call(
        kern,
        out_shape=(jax.ShapeDtypeStruct((s, c), F32), jax.ShapeDtypeStruct((s, c), F32)),
        grid=(n_c, n_t),
        in_specs=specs,
        out_specs=(pl.BlockSpec((tc, cw), lambda ci, i: (i, ci)),
                   pl.BlockSpec((tc, cw), lambda ci, i: (n_t - 1 - i, ci))),
        scratch_shapes=[pltpu.VMEM((tc + 2 * SUBLANES, cw), F32)]
        + [pltpu.VMEM((tc, cw), F32) for _ in range(4)]
        + [pltpu.VMEM((1, cw), F32) for _ in range(2)],
        compiler_params=_params("parallel", "arbitrary"),
        name="rglru_scan",
    )(proj, proj, proj, proj, proj, proj, conv_w, conv_b.reshape(1, c), w_a, w_x, b_a, b_x, lam)


def _pro_rglru_out(rows, vecs):
    yf, yb, gate = rows
    return (yf + yb) * _gelu_tanh(gate)


def _diff_attn_kernel(q_ref, k_ref, v_ref, lq1_ref, lk1_ref, lq2_ref, lk2_ref, g_ref, o_ref,
                      s_buf, p_buf, m_s, a_s, l_s, acc_s, *, tq, tk, rc, n_kv, lambda_init):
    m_s[...] = jnp.full_like(m_s, -jnp.inf)
    l_s[...] = jnp.zeros_like(l_s)
    acc_s[...] = jnp.zeros_like(acc_s)
    q = q_ref[...]
    qs = (q[:, :HEAD_DIM], q[:, HEAD_DIM:])

    def scores(j, slot):
        off = pl.multiple_of(j * tk, tk)
        kb = k_ref[pl.ds(off, tk), :]
        for c in range(2):
            kc = kb[:, c * HEAD_DIM:(c + 1) * HEAD_DIM]
            s_buf[slot, c * tq:(c + 1) * tq, :] = lax.dot_general(
                qs[c], kc, (((1,), (1,)), ((), ())), preferred_element_type=F32)

    def update(j, slot):
        off = pl.multiple_of(j * tk, tk)
        vb = v_ref[pl.ds(off, tk), :]
        for r0 in range(0, 2 * tq, rc):
            rows = slice(r0, r0 + rc)
            sc = s_buf[slot, rows, :]
            m_old = m_s[rows, :]
            m_new = jnp.maximum(m_old, jnp.max(sc, axis=-1, keepdims=True))
            alpha = jnp.exp2(m_old - m_new)
            p = jnp.exp2(sc - m_new)
            l_s[rows, :] = alpha * l_s[rows, :] + sum(
                p[:, t * LANES:(t + 1) * LANES] for t in range(tk // LANES))
            p_buf[rows, :] = p.astype(BF16)
            m_s[rows, :] = m_new
            a_s[rows, :] = alpha
        acc_s[...] = a_s[...] * acc_s[...] + jnp.dot(p_buf[...], vb, preferred_element_type=F32)

    scores(0, 0)

    def kv_pair(jj, _):
        j = 2 * jj
        scores(j + 1, 1)
        update(j, 0)
        scores(jnp.minimum(j + 2, n_kv - 1), 0)
        update(j + 1, 1)
        return 0

    lax.fori_loop(0, n_kv // 2, kv_pair, 0)
    lam = (jnp.exp(jnp.sum(lq1_ref[...] * lk1_ref[...], axis=-1, keepdims=True))
           - jnp.exp(jnp.sum(lq2_ref[...] * lk2_ref[...], axis=-1, keepdims=True)) + lambda_init)
    l = jnp.sum(l_s[...], axis=-1, keepdims=True)
    o = acc_s[0:tq] / l[0:tq] - lam * (acc_s[tq:2 * tq] / l[tq:2 * tq])
    o_ref[...] = (_rms(o, g_ref[...]) * (1.0 - lambda_init)).astype(o_ref.dtype)


def _diff_attention(qkv, lq1, lk1, lq2, lk2, subln_g, lambda_init, *, tq=512, tk=1024, rc=32):
    s = qkv.shape[0]
    vd = 2 * HEAD_DIM
    tq = min(tq, s)
    tk = min(tk, s // 2)
    assert (s // tk) % 2 == 0
    kern = functools.partial(_diff_attn_kernel, tq=tq, tk=tk, rc=rc, n_kv=s // tk,
                             lambda_init=lambda_init)
    vec = lambda a: a.reshape(1, -1)
    vspec = lambda w: pl.BlockSpec((1, w), lambda h, i: (0, 0))
    return pl.pallas_call(
        kern,
        out_shape=jax.ShapeDtypeStruct((s, DIFF_HEADS * vd), BF16),
        grid=(DIFF_HEADS, s // tq),
        in_specs=[
            pl.BlockSpec((tq, vd), lambda h, i: (i, h)),
            pl.BlockSpec((s, vd), lambda h, i: (0, DIFF_HEADS + h)),
            pl.BlockSpec((s, vd), lambda h, i: (0, 2 * DIFF_HEADS + h)),
            vspec(HEAD_DIM), vspec(HEAD_DIM), vspec(HEAD_DIM), vspec(HEAD_DIM), vspec(vd),
        ],
        out_specs=pl.BlockSpec((tq, vd), lambda h, i: (i, h)),
        scratch_shapes=[pltpu.VMEM((2, 2 * tq, tk), F32), pltpu.VMEM((2 * tq, tk), BF16),
                        pltpu.VMEM((2 * tq, 1), F32), pltpu.VMEM((2 * tq, 1), F32),
                        pltpu.VMEM((2 * tq, LANES), F32), pltpu.VMEM((2 * tq, vd), F32)],
        compiler_params=_params("parallel", "arbitrary"),
        name="diff_attention",
    )(qkv, qkv, qkv, vec(lq1), vec(lk1), vec(lq2), vec(lk2), vec(subln_g))


def _pro_identity(rows, vecs):
    return rows[0]


def _win_attn_kernel(sink_ref, q_ref, k_ref, v_ref, o_ref, *, tq, win, s_len, group):
    kvh = pl.program_id(0)
    i = pl.program_id(1)
    scale = HEAD_DIM ** -0.5
    start = jnp.clip(i * tq - WINDOW, 0, s_len - win)
    start = pl.multiple_of(start, WINDOW)
    kw = k_ref[pl.ds(start, win), :]
    vw = v_ref[pl.ds(start, win), :]
    qpos = i * tq + lax.broadcasted_iota(jnp.int32, (tq, win), 0)
    kpos = start + lax.broadcasted_iota(jnp.int32, (tq, win), 1)
    valid = jnp.abs(kpos - qpos) <= WINDOW
    for g in range(group):
        qg = q_ref[:, g * HEAD_DIM:(g + 1) * HEAD_DIM]
        sc = lax.dot_general(qg, kw, (((1,), (1,)), ((), ())), preferred_element_type=F32) * scale
        sc = jnp.where(valid, sc, NEG_INF)
        sink = sink_ref[kvh * group + g]
        m = jnp.maximum(jnp.max(sc, axis=-1, keepdims=True), sink)
        e = jnp.exp(sc - m)
        p = e / (jnp.sum(e, axis=-1, keepdims=True) + jnp.exp(sink - m))
        o = jnp.dot(p.astype(BF16), vw, preferred_element_type=F32)
        o_ref[:, g * HEAD_DIM:(g + 1) * HEAD_DIM] = o.astype(o_ref.dtype)


def _window_attention(qkv, sink, *, tq=256):
    s = qkv.shape[0]
    group = WIN_Q_HEADS // WIN_KV_HEADS
    tq = min(tq, s)
    win = min(tq + 2 * WINDOW, s)
    qw = group * HEAD_DIM
    k0 = WIN_Q_HEADS
    v0 = WIN_Q_HEADS + WIN_KV_HEADS
    kern = functools.partial(_win_attn_kernel, tq=tq, win=win, s_len=s, group=group)
    return pl.pallas_call(
        kern,
        out_shape=jax.ShapeDtypeStruct((s, WIN_Q_HEADS * HEAD_DIM), BF16),
        grid=(WIN_KV_HEADS, s // tq),
        in_specs=[
            pl.BlockSpec(memory_space=pltpu.SMEM),
            pl.BlockSpec((tq, qw), lambda h, i: (i, h)),
            pl.BlockSpec((s, HEAD_DIM), lambda h, i: (0, k0 + h)),
            pl.BlockSpec((s, HEAD_DIM), lambda h, i: (0, v0 + h)),
        ],
        out_specs=pl.BlockSpec((tq, qw), lambda h, i: (i, h)),
        compiler_params=_params("parallel", "arbitrary"),
        name="window_attention",
    )(sink.astype(F32), qkv, qkv, qkv)


def _s5_discretize(a_re, a_im, log_dt, b_re, b_im):
    dt = jnp.exp(log_dt)[:, None]
    mag = jnp.exp(dt * a_re)
    lr, li = mag * jnp.cos(dt * a_im), mag * jnp.sin(dt * a_im)
    den = a_re * a_re + a_im * a_im
    nr, ni = lr - 1.0, li
    fr = (nr * a_re + ni * a_im) / den
    fi = (ni * a_re - nr * a_im) / den
    bbr = fr[..., None] * b_re - fi[..., None] * b_im
    bbi = fr[..., None] * b_im + fi[..., None] * b_re
    return lr, li, bbr, bbi


def _s5_block_weights(lr, li, bbr, bbi, c_re, c_im, gpc):
    g, n, c = bbr.shape
    n_k = g // gpc
    eye = jnp.eye(gpc, dtype=F32)

    def w_in(bb):
        t = bb.reshape(n_k, gpc, n, c)
        return jnp.einsum('kgnc,gh->kgchn', t, eye).reshape(n_k, gpc * c, gpc * n)

    def w_out(cc):
        t = cc.reshape(n_k, gpc, c, n)
        return jnp.einsum('kgcn,gh->khngc', t, eye).reshape(n_k, gpc * n, gpc * c)

    win = jnp.concatenate([w_in(bbr), w_in(bbi)], axis=2)
    wout = jnp.concatenate([w_out(c_re), -w_out(c_im)], axis=1)
    lam = jnp.stack([lr.reshape(n_k, gpc * n), li.reshape(n_k, gpc * n)], axis=1)
    return win, wout, lam


def _s5_kernel(uf_ref, ub_ref, win_ref, wout_ref, lam_ref, yf_ref, yb_ref,
               xf_s, xb_s, st_s, *, tc, ns):
    i = pl.program_id(1)

    @pl.when(i == 0)
    def _():
        st_s[...] = jnp.zeros_like(st_s)

    xf_s[...] = jnp.dot(uf_ref[...].astype(BF16), win_ref[0, 0].astype(BF16),
                        preferred_element_type=F32)
    xb_s[...] = jnp.dot(ub_ref[...].astype(BF16), win_ref[1, 0].astype(BF16),
                        preferred_element_type=F32)
    lrf, lif = lam_ref[0, 0, 0:1, :], lam_ref[0, 0, 1:2, :]
    lrb, lib = lam_ref[1, 0, 0:1, :], lam_ref[1, 0, 1:2, :]

    def body(r, carry):
        srf, sif, srb, sib = carry
        nrf = lrf * srf - lif * sif + xf_s[pl.ds(r, 1), 0:ns]
        nif = lrf * sif + lif * srf + xf_s[pl.ds(r, 1), ns:2 * ns]
        xf_s[pl.ds(r, 1), 0:ns] = nrf
        xf_s[pl.ds(r, 1), ns:2 * ns] = nif
        rb = tc - 1 - r
        nrb = lrb * srb - lib * sib + xb_s[pl.ds(rb, 1), 0:ns]
        nib = lrb * sib + lib * srb + xb_s[pl.ds(rb, 1), ns:2 * ns]
        xb_s[pl.ds(rb, 1), 0:ns] = nrb
        xb_s[pl.ds(rb, 1), ns:2 * ns] = nib
        return nrf, nif, nrb, nib

    init = (st_s[0:1, :], st_s[1:2, :], st_s[2:3, :], st_s[3:4, :])
    srf, sif, srb, sib = lax.fori_loop(0, tc, body, init, unroll=8)
    st_s[0:1, :] = srf
    st_s[1:2, :] = sif
    st_s[2:3, :] = srb
    st_s[3:4, :] = sib
    yf_ref[...] = jnp.dot(xf_s[...].astype(BF16), wout_ref[0, 0].astype(BF16),
                          preferred_element_type=F32)
    yb_ref[...] = jnp.dot(xb_s[...].astype(BF16), wout_ref[1, 0].astype(BF16),
                          preferred_element_type=F32)


def _s5_scan(u, win, wout, lam, *, tc=256):
    s, w = u.shape
    _, n_k, cw, ns2 = win.shape
    ns = ns2 // 2
    tc = min(tc, s)
    n_t = s // tc
    kern = functools.partial(_s5_kernel, tc=tc, ns=ns)
    return pl.pallas_call(
        kern,
        out_shape=(jax.ShapeDtypeStruct((s, w), F32), jax.ShapeDtypeStruct((s, w), F32)),
        grid=(n_k, n_t),
        in_specs=[
            pl.BlockSpec((tc, cw), lambda k, i: (i, k)),
            pl.BlockSpec((tc, cw), lambda k, i: (n_t - 1 - i, k)),
            pl.BlockSpec((2, 1, cw, ns2), lambda k, i: (0, k, 0, 0)),
            pl.BlockSpec((2, 1, ns2, cw), lambda k, i: (0, k, 0, 0)),
            pl.BlockSpec((2, 1, 2, ns), lambda k, i: (0, k, 0, 0)),
        ],
        out_specs=(pl.BlockSpec((tc, cw), lambda k, i: (i, k)),
                   pl.BlockSpec((tc, cw), lambda k, i: (n_t - 1 - i, k))),
        scratch_shapes=[pltpu.VMEM((tc, ns2), F32), pltpu.VMEM((tc, ns2), F32),
                        pltpu.VMEM((4, ns), F32)],
        compiler_params=_params("parallel", "arbitrary"),
        name="s5_scan",
    )(u, u, win, wout, lam)


def _pro_s5_glu(rows, vecs):
    yf, yb, u = rows
    return _gelu_tanh(yf + yb + vecs[0] * u)


def _epi_glu(acc, epi_refs, o_ref, j, a32_s):
    o_ref[...] = (a32_s[...] * _sigmoid(acc)).astype(o_ref.dtype)


def _router_kernel(x_ref, g_ref, w_ref, h_ref, idx_ref, wt_ref):
    h = _rms(x_ref[...], g_ref[...])
    h_ref[...] = h
    logits = jnp.dot(h, w_ref[...], preferred_element_type=F32, precision=lax.Precision.HIGHEST)
    tm = logits.shape[0]
    lane = lax.broadcasted_iota(jnp.int32, (tm, LANES), 1)
    big = jnp.int32(LANES)
    ninf = -jnp.inf
    gl = jnp.where(lane < MOE_GROUPS, logits, ninf)
    gm = jnp.max(gl, axis=-1, keepdims=True)
    ge = jnp.exp(gl - gm)
    g_prob = ge / jnp.sum(ge, axis=-1, keepdims=True)
    g_p = jnp.max(g_prob, axis=-1, keepdims=True)
    g_idx = jnp.min(jnp.where(g_prob == g_p, lane, big), axis=-1, keepdims=True)
    lo = MOE_GROUPS + g_idx * EXPERTS_PER_GROUP
    in_grp = (lane >= lo) & (lane < lo + EXPERTS_PER_GROUP)
    el = jnp.where(in_grp, logits, ninf)
    em = jnp.max(el, axis=-1, keepdims=True)
    ee = jnp.exp(el - em)
    e_prob = jnp.where(in_grp, ee / jnp.sum(ee, axis=-1, keepdims=True), -1.0)
    p1 = jnp.max(e_prob, axis=-1, keepdims=True)
    i1 = jnp.min(jnp.where(e_prob == p1, lane, big), axis=-1, keepdims=True)
    rest = jnp.where(lane == i1, -1.0, e_prob)
    p2 = jnp.max(rest, axis=-1, keepdims=True)
    i2 = jnp.min(jnp.where(rest == p2, lane, big), axis=-1, keepdims=True)
    denom = p1 + p2
    w1 = g_p * (p1 / denom)
    w2 = g_p * (p2 / denom)
    idx_ref[...] = jnp.where(lane == 0, i1 - MOE_GROUPS, jnp.where(lane == 1, i2 - MOE_GROUPS, 0))
    wt_ref[...] = jnp.where(lane == 0, w1, jnp.where(lane == 1, w2, 0.0))


def _router(x, g, w_group, w_expert, *, tm=512):
    s, d = x.shape
    tm = min(tm, s)
    w_r = jnp.concatenate(
        [w_group, w_expert, jnp.zeros((d, LANES - MOE_GROUPS - N_EXPERTS), F32)], axis=1)
    return pl.pallas_call(
        _router_kernel,
        out_shape=(jax.ShapeDtypeStruct((s, d), F32), jax.ShapeDtypeStruct((s, LANES), jnp.int32),
                   jax.ShapeDtypeStruct((s, LANES), F32)),
        grid=(s // tm,),
        in_specs=[pl.BlockSpec((tm, d), lambda i: (i, 0)), pl.BlockSpec((1, d), lambda i: (0, 0)),
                  pl.BlockSpec((d, LANES), lambda i: (0, 0))],
        out_specs=(pl.BlockSpec((tm, d), lambda i: (i, 0)), pl.BlockSpec((tm, LANES), lambda i: (i, 0)),
                   pl.BlockSpec((tm, LANES), lambda i: (i, 0))),
        compiler_params=_params("parallel"),
        name="moe_router",
    )(x, g.reshape(1, d), w_r)


def _rank_kernel(idx_ref, dest_ref, cnt_ref, tot_s, pst_s, run_s, *, tm):
    ph = pl.program_id(0)
    i = pl.program_id(1)
    lane = lax.broadcasted_iota(jnp.int32, (tm, LANES), 1)
    idx = idx_ref[...]
    oh0 = (lane == idx[:, 0:1]).astype(F32)
    oh1 = (lane == idx[:, 1:2]).astype(F32)
    c = oh0 + oh1
    csum = jnp.sum(c, axis=0, keepdims=True)

    @pl.when(jnp.logical_and(ph == 0, i == 0))
    def _():
        tot_s[...] = jnp.zeros_like(tot_s)

    @pl.when(ph == 0)
    def _():
        tot_s[...] += csum

    @pl.when(jnp.logical_and(ph == 1, i == 0))
    def _():
        counts = tot_s[...]
        nblk = jnp.floor((counts + (EXPERT_BLOCK - 1)) * (1.0 / EXPERT_BLOCK))
        r = lax.broadcasted_iota(jnp.int32, (LANES, LANES), 0)
        cc = lax.broadcasted_iota(jnp.int32, (LANES, LANES), 1)
        upper = (r < cc).astype(F32)
        excl = jnp.dot(jnp.broadcast_to(nblk, (SUBLANES, LANES)), upper,
                       preferred_element_type=F32, precision=lax.Precision.HIGHEST)
        pst_s[...] = excl[0:1, :] * EXPERT_BLOCK
        run_s[...] = jnp.zeros_like(run_s)
        cnt_ref[...] = jnp.broadcast_to(counts, cnt_ref.shape)

    @pl.when(ph == 1)
    def _():
        rr = lax.broadcasted_iota(jnp.int32, (tm, tm), 0)
        cr = lax.broadcasted_iota(jnp.int32, (tm, tm), 1)
        lower = (rr > cr).astype(BF16)
        before = jnp.dot(lower, c.astype(BF16), preferred_element_type=F32) + run_s[...]
        base = pst_s[...] + before
        d0 = jnp.sum(oh0 * base, axis=-1, keepdims=True)
        d1 = jnp.sum(oh1 * (base + oh0), axis=-1, keepdims=True)
        dest_ref[...] = jnp.where(lane == 0, d0, jnp.where(lane == 1, d1, 0.0)).astype(jnp.int32)
        run_s[...] += csum


def _dispatch_rank(idx, *, tm=512):
    s = idx.shape[0]
    tm = min(tm, s)
    kern = functools.partial(_rank_kernel, tm=tm)
    return pl.pallas_call(
        kern,
        out_shape=(jax.ShapeDtypeStruct((s, LANES), jnp.int32),
                   jax.ShapeDtypeStruct((SUBLANES, LANES), F32)),
        grid=(2, s // tm),
        in_specs=[pl.BlockSpec((tm, LANES), lambda ph, i: (i, 0))],
        out_specs=(pl.BlockSpec((tm, LANES), lambda ph, i: (i * ph, 0)),
                   pl.BlockSpec((SUBLANES, LANES), lambda ph, i: (0, 0))),
        scratch_shapes=[pltpu.VMEM((1, LANES), F32) for _ in range(3)],
        compiler_params=_params("arbitrary", "arbitrary"),
        name="moe_rank",
    )(idx)


def _block_experts(counts, n_blk):
    cnt = counts[0, :N_EXPERTS].astype(jnp.int32)
    padded = ((cnt + EXPERT_BLOCK - 1) // EXPERT_BLOCK) * EXPERT_BLOCK
    pends = jnp.cumsum(padded)
    blk_start = jnp.arange(n_blk, dtype=jnp.int32) * EXPERT_BLOCK
    owner = jnp.sum((pends[None, :] <= blk_start[:, None]).astype(jnp.int32), axis=1)
    return jnp.minimum(owner, N_EXPERTS - 1).astype(jnp.int32)


def _scatter_kernel(dest_ref, h_ref, xs_in_hbm, xs_hbm, h_s, sem, *, tb, n_b):
    del xs_in_hbm
    b = pl.program_id(0)
    slot = b % 2
    n = tb * TOP_K

    def copy(r, row, sl):
        return pltpu.make_async_copy(h_s.at[sl, pl.ds(r, 1), :], xs_hbm.at[pl.ds(row, 1), :],
                                     sem.at[sl])

    def wait_all(sl):
        def body(r, _):
            copy(0, 0, sl).wait()
            return 0
        lax.fori_loop(0, n, body, 0, unroll=8)

    h_s[slot] = h_ref[...]
    for r in range(tb):
        tok = b * tb + r
        for k in range(TOP_K):
            copy(r, dest_ref[tok * TOP_K + k], slot).start()

    @pl.when(b > 0)
    def _():
        wait_all(1 - slot)

    @pl.when(b == n_b - 1)
    def _():
        wait_all(slot)


def _dispatch_scatter(h, dest_flat, n_rows, *, tb=128):
    s, d = h.shape
    tb = min(tb, s)
    n_b = s // tb
    kern = functools.partial(_scatter_kernel, tb=tb, n_b=n_b)
    grid_spec = pltpu.PrefetchScalarGridSpec(
        num_scalar_prefetch=1,
        grid=(n_b,),
        in_specs=[pl.BlockSpec((tb, d), lambda b, dr: (b, 0)), pl.BlockSpec(memory_space=pl.ANY)],
        out_specs=pl.BlockSpec(memory_space=pl.ANY),
        scratch_shapes=[pltpu.VMEM((2, tb, d), h.dtype), pltpu.SemaphoreType.DMA((2,))],
    )
    return pl.pallas_call(
        kern,
        out_shape=jax.ShapeDtypeStruct((n_rows, d), h.dtype),
        grid_spec=grid_spec,
        input_output_aliases={2: 0},
        compiler_params=_params("arbitrary"),
        name="moe_scatter",
    )(dest_flat, h, jnp.zeros((n_rows, d), h.dtype))


def _row_copy(src_hbm, src_row, dst_buf, slot, dst_row, sem):
    return pltpu.make_async_copy(src_hbm.at[pl.ds(src_row, 1), :],
                                 dst_buf.at[slot, pl.ds(dst_row, 1), :], sem.at[slot])


def _gather_start(idx_ref, base, n, src_hbm, dst_buf, slot, sem):
    for r in range(n):
        _row_copy(src_hbm, idx_ref[base + r], dst_buf, slot, r, sem).start()


def _gather_wait(n, src_hbm, dst_buf, slot, sem):
    def body(r, _):
        _row_copy(src_hbm, 0, dst_buf, slot, r, sem).wait()
        return 0
    lax.fori_loop(0, n, body, 0, unroll=8)


def _expert_kernel(be_ref, x_ref, wg_ref, wu_ref, wd_ref, y_ref, wg_s, wu_s, wd_s):
    b = pl.program_id(0)
    changed = jnp.logical_or(b == 0, be_ref[b] != be_ref[jnp.maximum(b - 1, 0)])

    @pl.when(changed)
    def _():
        wg_s[...] = wg_ref[...].astype(BF16)
        wu_s[...] = wu_ref[...].astype(BF16)
        wd_s[...] = wd_ref[...].astype(BF16)

    x = x_ref[...].astype(BF16)
    hg = jnp.dot(x, wg_s[...], preferred_element_type=F32)
    hu = jnp.dot(x, wu_s[...], preferred_element_type=F32)
    hdn = (hg * _sigmoid(hg)) * hu
    y_ref[...] = jnp.dot(hdn.astype(BF16), wd_s[...], preferred_element_type=F32)


def _expert_mlp(xs, blk_e, layer, w_gate, w_up, w_down):
    n_rows, d = xs.shape
    de = w_gate.shape[3]
    grid_spec = pltpu.PrefetchScalarGridSpec(
        num_scalar_prefetch=1,
        grid=(n_rows // EXPERT_BLOCK,),
        in_specs=[
            pl.BlockSpec((EXPERT_BLOCK, d), lambda b, be: (b, 0)),
            pl.BlockSpec((None, None, d, de), lambda b, be: (layer, be[b], 0, 0)),
            pl.BlockSpec((None, None, d, de), lambda b, be: (layer, be[b], 0, 0)),
            pl.BlockSpec((None, None, de, d), lambda b, be: (layer, be[b], 0, 0)),
        ],
        out_specs=pl.BlockSpec((EXPERT_BLOCK, d), lambda b, be: (b, 0)),
        scratch_shapes=[pltpu.VMEM((d, de), BF16), pltpu.VMEM((d, de), BF16),
                        pltpu.VMEM((de, d), BF16)],
    )
    return pl.pallas_call(
        _expert_kernel,
        out_shape=jax.ShapeDtypeStruct((n_rows, d), F32),
        grid_spec=grid_spec,
        compiler_params=_params("arbitrary"),
        name="moe_experts",
    )(blk_e, xs, w_gate, w_up, w_down)


def _combine_kernel(dest_ref, x_ref, wt_ref, ys_hbm, g_ref, o_ref, ybuf, sem, *, tb, n_b, final):
    b = pl.program_id(0)
    slot = b % 2
    n = tb * TOP_K

    @pl.when(b == 0)
    def _():
        _gather_start(dest_ref, 0, n, ys_hbm, ybuf, 0, sem)

    _gather_wait(n, ys_hbm, ybuf, slot, sem)
    _gather_start(dest_ref, (b + 1) * n, n, ys_hbm, ybuf, 1 - slot, sem)
    y0 = ybuf[slot, 0:tb, :]
    y1 = ybuf[slot, tb:2 * tb, :]
    wt = wt_ref[...]
    out = x_ref[...] + (y0 * wt[:, 0:1] + y1 * wt[:, 1:2])
    if final:
        out = _rms(out, g_ref[...])
    o_ref[...] = out

    @pl.when(b == n_b - 1)
    def _():
        _gather_wait(n, ys_hbm, ybuf, 1 - slot, sem)


def _moe_combine(x, wt, ys, dest, g_final, *, final, tb=128):
    s, d = x.shape
    tb = min(tb, s)
    n_b = s // tb
    dest = dest.reshape(n_b, tb, TOP_K).transpose(0, 2, 1).reshape(-1)
    dest = jnp.concatenate([dest, jnp.zeros((tb * TOP_K,), jnp.int32)])
    kern = functools.partial(_combine_kernel, tb=tb, n_b=n_b, final=final)
    grid_spec = pltpu.PrefetchScalarGridSpec(
        num_scalar_prefetch=1,
        grid=(n_b,),
        in_specs=[
            pl.BlockSpec((tb, d), lambda b, dr: (b, 0)),
            pl.BlockSpec((tb, LANES), lambda b, dr: (b, 0)),
            pl.BlockSpec(memory_space=pl.ANY),
            pl.BlockSpec((1, d), lambda b, dr: (0, 0)),
        ],
        out_specs=pl.BlockSpec((tb, d), lambda b, dr: (b, 0)),
        scratch_shapes=[pltpu.VMEM((2, tb * TOP_K, d), F32), pltpu.SemaphoreType.DMA((2,))],
    )
    return pl.pallas_call(
        kern,
        out_shape=jax.ShapeDtypeStruct((s, d), F32),
        grid_spec=grid_spec,
        compiler_params=_params("arbitrary"),
        name="moe_combine",
    )(dest, x, wt, ys, g_final.reshape(1, d))


def _hier_moe(x, ln_g, w_group, w_expert, layer, w_gate, w_up, w_down, g_final, *, final):
    s = x.shape[0]
    h, idx, wt = _router(x, ln_g, w_group, w_expert)
    dest, counts = _dispatch_rank(idx)
    dest = dest[:, :TOP_K].reshape(-1)
    n_rows = -(-(s * TOP_K + N_EXPERTS * (EXPERT_BLOCK - 1)) // EXPERT_BLOCK) * EXPERT_BLOCK
    blk_e = _block_experts(counts, n_rows // EXPERT_BLOCK)
    xs = _dispatch_scatter(h, dest, n_rows)
    ys = _expert_mlp(xs, blk_e, layer, w_gate, w_up, w_down)
    return _moe_combine(x, wt, ys, dest, g_final, final=final)


def _rglru_layer(x, ln_g, w_in, conv_w, conv_b, w_a, w_x, b_a, b_x, lam, w_out):
    d = x.shape[1]
    c = conv_w.shape[1]
    proj = _norm_matmul(x, ln_g, w_in, name="lru_in_proj")
    yf, yb = _rglru_scan(proj, conv_w, conv_b, w_a, w_x, b_a, b_x, lam)
    return _matmul_residual([(yf, c, 0), (yb, c, 0), (proj, c, 0)], [], w_out, x, _pro_rglru_out,
                            k_dim=c, tm=512, name="lru_out_proj")


def _diff_layer(x, ln_g, rope, w_qkv, lq1, lk1, lq2, lk2, subln_g, w_out, lambda_init):
    qk_cols = 2 * DIFF_HEADS * 2 * HEAD_DIM
    qkv = _norm_matmul(x, ln_g, w_qkv, rope=rope, n_rope_cols=qk_cols, n_q_cols=qk_cols // 2,
                       q_scale=HEAD_DIM ** -0.5 * math.log2(math.e), out_dtype=BF16,
                       name="diff_qkv_proj")
    o = _diff_attention(qkv, lq1, lk1, lq2, lk2, subln_g, lambda_init)
    return _matmul_residual([(o, o.shape[1], 0)], [], w_out, x, _pro_identity,
                            k_dim=o.shape[1], name="diff_out_proj")


def _window_layer(x, ln_g, rope, w_qkv, sink, w_out):
    qk_cols = (WIN_Q_HEADS + WIN_KV_HEADS) * HEAD_DIM
    qkv = _norm_matmul(x, ln_g, w_qkv, rope=rope, n_rope_cols=qk_cols, out_dtype=BF16,
                       name="win_qkv_proj")
    o = _window_attention(qkv, sink)
    return _matmul_residual([(o, o.shape[1], 0)], [], w_out, x, _pro_identity,
                            k_dim=o.shape[1], name="win_out_proj")


def _s5_layer(x, ln_g, w_in, a_re, a_im, log_dt, b_re, b_im, c_re, c_im, d_skip, w_glu, w_out,
              *, gpc=8):
    u = _norm_matmul(x, ln_g, w_in, name="s5_in_proj")
    w = u.shape[1]
    wins, wouts, lams = [], [], []
    for dd in range(2):
        lr, li, bbr, bbi = _s5_discretize(a_re[dd], a_im[dd], log_dt[dd], b_re[dd], b_im[dd])
        wi, wo, lm = _s5_block_weights(lr, li, bbr, bbi, c_re[dd], c_im[dd], gpc)
        wins.append(wi)
        wouts.append(wo)
        lams.append(lm)
    yf, yb = _s5_scan(u, jnp.stack(wins), jnp.stack(wouts), jnp.stack(lams))
    z = _fused_mm([(yf, w, 0), (yb, w, 0), (u, w, 0)], [d_skip.reshape(1, w)], w_glu, [],
                  _pro_s5_glu, _epi_glu, out_dtype=BF16, tm=512, tn=w, k_dim=w, keep_f32=True,
                  name="s5_glu")
    return _matmul_residual([(z, w, 0)], [], w_out, x, _pro_identity, k_dim=w, name="s5_out_proj")


def kernel(x, positions, ln_mix, ln_ffn, ln_final, lru_w_in, lru_conv_w, lru_conv_b, lru_w_a, lru_w_x, lru_b_a, lru_b_x, lru_lambda, lru_w_out, diff_w_qkv, diff_lq1, diff_lk1, diff_lq2, diff_lk2, diff_subln, diff_w_out, win_w_qkv, win_sink, win_w_out, s5_w_in, s5_a_re, s5_a_im, s5_log_dt, s5_b_re, s5_b_im, s5_c_re, s5_c_im, s5_d, s5_w_glu, s5_w_out, moe_w_group, moe_w_expert, moe_w_gate, moe_w_up, moe_w_down):
    batch, s, d = x.shape
    depth = ln_mix.shape[0]
    outs = []
    for b in range(batch):
        xb = x[b]
        rope = _rope_tables(positions[b])
        for i in range(depth):
            kind, j = i % 4, i // 4
            if kind == 0:
                xb = _rglru_layer(xb, ln_mix[i], lru_w_in[j], lru_conv_w[j], lru_conv_b[j],
                                  lru_w_a[j], lru_w_x[j], lru_b_a[j], lru_b_x[j], lru_lambda[j],
                                  lru_w_out[j])
            elif kind == 1:
                xb = _diff_layer(xb, ln_mix[i], rope, diff_w_qkv[j], diff_lq1[j], diff_lk1[j],
                                 diff_lq2[j], diff_lk2[j], diff_subln[j], diff_w_out[j],
                                 0.8 - 0.6 * math.exp(-0.3 * i))
            elif kind == 2:
                xb = _window_layer(xb, ln_mix[i], rope, win_w_qkv[j], win_sink[j], win_w_out[j])
            else:
                xb = _s5_layer(xb, ln_mix[i], s5_w_in[j], s5_a_re[j], s5_a_im[j], s5_log_dt[j],
                               s5_b_re[j], s5_b_im[j], s5_c_re[j], s5_c_im[j], s5_d[j],
                               s5_w_glu[j], s5_w_out[j])
            xb = _hier_moe(xb, ln_ffn[i], moe_w_group[i], moe_w_expert[i], i, moe_w_gate,
                           moe_w_up, moe_w_down, ln_final, final=(i == depth - 1))
        outs.append(xb)
    return jnp.stack(outs)
```

```python
import functools
import math

import jax
import jax.numpy as jnp
from jax import lax
from jax.experimental import pallas as pl
from jax.experimental.pallas import tpu as pltpu

F32 = jnp.float32
BF16 = jnp.bfloat16

NORM_EPS = 1e-6
NEG_INF = -1e30
LANES = 128
SUBLANES = 8
VMEM_LIMIT = 56 * 1024 * 1024

HEAD_DIM = 128
ROT_DIM = HEAD_DIM // 4
ROPE_THETA = 500000.0
RGLRU_C = 8.0
CONV_W = 4
LRU_BLOCK_W = 128
WINDOW = 128
DIFF_HEADS = 8
WIN_Q_HEADS = 16
WIN_KV_HEADS = 4
SSM_GROUP_CH = 16
SSM_STATE = 64
MOE_GROUPS = 4
EXPERTS_PER_GROUP = 8
N_EXPERTS = MOE_GROUPS * EXPERTS_PER_GROUP
TOP_K = 2
EXPERT_BLOCK = 256


def _params(*sem):
    return pltpu.CompilerParams(dimension_semantics=sem, vmem_limit_bytes=VMEM_LIMIT)


def _rms(x, g):
    ms = jnp.mean(x * x, axis=-1, keepdims=True)
    return x * lax.rsqrt(ms + NORM_EPS) * g


def _gelu_tanh(x):
    return 0.5 * x * (1.0 + jnp.tanh(math.sqrt(2.0 / math.pi) * (x + 0.044715 * (x * x * x))))


def _sigmoid(x):
    return 1.0 / (1.0 + jnp.exp(-x))


def _fused_mm_kernel(*refs, n_row, n_vec, n_epi, prologue, epilogue, keep_f32):
    row_refs = refs[:n_row]
    vec_refs = refs[n_row:n_row + n_vec]
    w_ref = refs[n_row + n_vec]
    epi_refs = refs[n_row + n_vec + 1:n_row + n_vec + 1 + n_epi]
    o_ref = refs[n_row + n_vec + 1 + n_epi]
    a_s = refs[n_row + n_vec + 2 + n_epi]
    a32_s = refs[n_row + n_vec + 3 + n_epi] if keep_f32 else None
    j = pl.program_id(1)

    @pl.when(j == 0)
    def _():
        a = prologue([r[...] for r in row_refs], [v[...] for v in vec_refs])
        a_s[...] = a.astype(BF16)
        if keep_f32:
            a32_s[...] = a

    acc = jnp.dot(a_s[...], w_ref[...].astype(BF16), preferred_element_type=F32)
    epilogue(acc, epi_refs, o_ref, j, a32_s)


def _fused_mm(row_inputs, vec_inputs, w, epi_inputs, prologue, epilogue, *, out_dtype, tm, tn,
              k_dim, keep_f32=False, name):
    s = row_inputs[0][0].shape[0]
    n = w.shape[1]
    tm = min(tm, s)
    tn = min(tn, n)
    in_specs = []
    args = []
    for arr, width, cb in row_inputs:
        in_specs.append(pl.BlockSpec((tm, width), lambda i, j, cb=cb: (i, cb)))
        args.append(arr)
    for arr in vec_inputs:
        in_specs.append(pl.BlockSpec(arr.shape, lambda i, j: (0, 0)))
        args.append(arr)
    in_specs.append(pl.BlockSpec((k_dim, tn), lambda i, j: (0, j)))
    args.append(w)
    for arr, width, per_tile in epi_inputs:
        if per_tile:
            in_specs.append(pl.BlockSpec((tm, width), lambda i, j: (i, j)))
        else:
            in_specs.append(pl.BlockSpec((tm, width), lambda i, j: (i, 0)))
        args.append(arr)
    scratch = [pltpu.VMEM((tm, k_dim), BF16)]
    if keep_f32:
        scratch.append(pltpu.VMEM((tm, k_dim), F32))
    kern = functools.partial(_fused_mm_kernel, n_row=len(row_inputs), n_vec=len(vec_inputs),
                             n_epi=len(epi_inputs), prologue=prologue, epilogue=epilogue,
                             keep_f32=keep_f32)
    return pl.pallas_call(
        kern,
        out_shape=jax.ShapeDtypeStruct((s, n), out_dtype),
        grid=(s // tm, n // tn),
        in_specs=in_specs,
        out_specs=pl.BlockSpec((tm, tn), lambda i, j: (i, j)),
        scratch_shapes=scratch,
        compiler_params=_params("parallel", "arbitrary"),
        name=name,
    )(*args)


def _pro_rms(rows, vecs):
    return _rms(rows[0], vecs[0])


def _epi_store(acc, epi_refs, o_ref, j, a32_s):
    o_ref[...] = acc.astype(o_ref.dtype)


def _epi_residual(acc, epi_refs, o_ref, j, a32_s):
    o_ref[...] = (epi_refs[0][...] + acc).astype(o_ref.dtype)


def _make_epi_rope(n_rope_tiles, n_q_tiles, q_scale, tn):
    def epi(acc, epi_refs, o_ref, j, a32_s):
        c_ref, s1_ref, s2_ref = epi_refs

        @pl.when(j < n_rope_tiles)
        def _():
            c = c_ref[...]
            s1 = s1_ref[...]
            s2 = s2_ref[...]
            sc = jnp.where(j < n_q_tiles, q_scale, 1.0).astype(F32)
            for hh in range(tn // HEAD_DIM):
                xs = acc[:, hh * HEAD_DIM:(hh + 1) * HEAD_DIM]
                rot = (xs * c + pltpu.roll(xs, HEAD_DIM - ROT_DIM // 2, 1) * s1
                       + pltpu.roll(xs, ROT_DIM // 2, 1) * s2) * sc
                o_ref[:, hh * HEAD_DIM:(hh + 1) * HEAD_DIM] = rot.astype(o_ref.dtype)

        @pl.when(j >= n_rope_tiles)
        def _():
            o_ref[...] = acc.astype(o_ref.dtype)

    return epi


def _rope_tables(positions):
    half = ROT_DIM // 2
    inv = ROPE_THETA ** (-jnp.arange(0, ROT_DIM, 2, dtype=F32) / ROT_DIM)
    ang = positions.astype(F32)[:, None] * inv
    cos, sin = jnp.cos(ang), jnp.sin(ang)
    s = positions.shape[0]
    ones = jnp.ones((s, HEAD_DIM - ROT_DIM), F32)
    zeros = jnp.zeros((s, HEAD_DIM - ROT_DIM), F32)
    zh = jnp.zeros((s, half), F32)
    c_tab = jnp.concatenate([cos, cos, ones], axis=1)
    s1_tab = jnp.concatenate([-sin, zh, zeros], axis=1)
    s2_tab = jnp.concatenate([zh, sin, zeros], axis=1)
    return c_tab, s1_tab, s2_tab


def _norm_matmul(x, g, w, *, rope=None, n_rope_cols=0, n_q_cols=0, q_scale=1.0, out_dtype=F32,
                 tm=1024, tn=512, name):
    d = x.shape[1]
    if rope is None:
        epi, epi_inputs = _epi_store, []
    else:
        tn = min(tn, w.shape[1])
        assert n_rope_cols % tn == 0 and n_q_cols % tn == 0
        epi = _make_epi_rope(n_rope_cols // tn, n_q_cols // tn, q_scale, tn)
        epi_inputs = [(t, HEAD_DIM, False) for t in rope]
    return _fused_mm([(x, d, 0)], [g.reshape(1, d)], w, epi_inputs, _pro_rms, epi,
                     out_dtype=out_dtype, tm=tm, tn=tn, k_dim=d, name=name)


def _matmul_residual(row_inputs, vec_inputs, w, res, prologue, *, k_dim, tm=1024, tn=512, name):
    return _fused_mm(row_inputs, vec_inputs, w, [(res, min(tn, w.shape[1]), True)], prologue,
                     _epi_residual, out_dtype=F32, tm=tm, tn=tn, k_dim=k_dim, name=name)


def _rglru_kernel(xf_ref, xfp_ref, xfn_ref, xb_ref, xbp_ref, xbn_ref, cw_ref, cb_ref, wa_ref,
                  wx_ref, ba_ref, bx_ref, lam_ref, yf_ref, yb_ref,
                  ext_s, af_s, bf_s, ab_s, bb_s, hf_s, hb_s, *, tc, cw, n_t):
    i = pl.program_id(1)
    halo = SUBLANES

    @pl.when(i == 0)
    def _():
        hf_s[...] = jnp.zeros_like(hf_s)
        hb_s[...] = jnp.zeros_like(hb_s)

    def gates(x_ref, xp_ref, xn_ref, chunk, d, a_s, b_s):
        prev = jnp.where(chunk == 0, 0.0, xp_ref[...])
        nxt = jnp.where(chunk == n_t - 1, 0.0, xn_ref[...])
        ext_s[0:halo, :] = prev
        ext_s[halo:halo + tc, :] = x_ref[...]
        ext_s[halo + tc:halo + tc + halo, :] = nxt
        xc = cb_ref[...] + sum(
            cw_ref[k:k + 1, :] * ext_s[halo - 2 + k:halo - 2 + k + tc, :] for k in range(CONV_W))
        lam = lam_ref[d:d + 1, :]
        z = -lam
        sp = jnp.maximum(z, 0.0) + jnp.log1p(jnp.exp(-jnp.abs(z)))
        for blk in range(cw // LRU_BLOCK_W):
            sl = slice(blk * LRU_BLOCK_W, (blk + 1) * LRU_BLOCK_W)
            xb = xc[:, sl]
            xbh = xb.astype(BF16)
            r = _sigmoid(jnp.dot(xbh, wa_ref[d, blk].astype(BF16), preferred_element_type=F32)
                         + ba_ref[d:d + 1, sl])
            ig = _sigmoid(jnp.dot(xbh, wx_ref[d, blk].astype(BF16), preferred_element_type=F32)
                          + bx_ref[d:d + 1, sl])
            log_a = (-RGLRU_C) * r * sp[:, sl]
            a_s[:, sl] = jnp.exp(log_a)
            th = jnp.tanh(log_a)
            b_s[:, sl] = jnp.sqrt(-2.0 * th / (1.0 - th)) * (ig * xb)

    gates(xf_ref, xfp_ref, xfn_ref, i, 0, af_s, bf_s)
    gates(xb_ref, xbp_ref, xbn_ref, n_t - 1 - i, 1, ab_s, bb_s)

    def body(r, carry):
        hf, hb = carry
        hf = af_s[pl.ds(r, 1), :] * hf + bf_s[pl.ds(r, 1), :]
        yf_ref[pl.ds(r, 1), :] = hf
        rb = tc - 1 - r
        hb = ab_s[pl.ds(rb, 1), :] * hb + bb_s[pl.ds(rb, 1), :]
        yb_ref[pl.ds(rb, 1), :] = hb
        return hf, hb

    hf, hb = lax.fori_loop(0, tc, body, (hf_s[...], hb_s[...]), unroll=8)
    hf_s[...] = hf
    hb_s[...] = hb


def _rglru_scan(proj, conv_w, conv_b, w_a, w_x, b_a, b_x, lam, *, tc=256, cw=1024):
    s = proj.shape[0]
    c = conv_w.shape[1]
    tc = min(tc, s)
    n_t = s // tc
    n_c = c // cw
    xoff = c // cw
    hb = tc // SUBLANES
    last_h = s // SUBLANES - 1

    specs = [
        pl.BlockSpec((tc, cw), lambda ci, i: (i, xoff + ci)),
        pl.BlockSpec((SUBLANES, cw), lambda ci, i: (jnp.maximum(i * hb - 1, 0), xoff + ci)),
        pl.BlockSpec((SUBLANES, cw), lambda ci, i: (jnp.minimum((i + 1) * hb, last_h), xoff + ci)),
        pl.BlockSpec((tc, cw), lambda ci, i: (n_t - 1 - i, xoff + ci)),
        pl.BlockSpec((SUBLANES, cw),
                     lambda ci, i: (jnp.maximum((n_t - 1 - i) * hb - 1, 0), xoff + ci)),
        pl.BlockSpec((SUBLANES, cw),
                     lambda ci, i: (jnp.minimum((n_t - i) * hb, last_h), xoff + ci)),
        pl.BlockSpec((CONV_W, cw), lambda ci, i: (0, ci)),
        pl.BlockSpec((1, cw), lambda ci, i: (0, ci)),
        pl.BlockSpec((2, cw // LRU_BLOCK_W, LRU_BLOCK_W, LRU_BLOCK_W), lambda ci, i: (0, ci, 0, 0)),
        pl.BlockSpec((2, cw // LRU_BLOCK_W, LRU_BLOCK_W, LRU_BLOCK_W), lambda ci, i: (0, ci, 0, 0)),
        pl.BlockSpec((2, cw), lambda ci, i: (0, ci)),
        pl.BlockSpec((2, cw), lambda ci, i: (0, ci)),
        pl.BlockSpec((2, cw), lambda ci, i: (0, ci)),
    ]
    kern = functools.partial(_rglru_kernel, tc=tc, cw=cw, n_t=n_t)
    return pl.pallas_call(
        kern,
        out_shape=(jax.ShapeDtypeStruct((s, c), F32), jax.ShapeDtypeStruct((s, c), F32)),
        grid=(n_c, n_t),
        in_specs=specs,
        out_specs=(pl.BlockSpec((tc, cw), lambda ci, i: (i, ci)),
                   pl.BlockSpec((tc, cw), lambda ci, i: (n_t - 1 - i, ci))),
        scratch_shapes=[pltpu.VMEM((tc + 2 * SUBLANES, cw), F32)]
        + [pltpu.VMEM((tc, cw), F32) for _ in range(4)]
        + [pltpu.VMEM((1, cw), F32) for _ in range(2)],
        compiler_params=_params("parallel", "arbitrary"),
        name="rglru_scan",
    )(proj, proj, proj, proj, proj, proj, conv_w, conv_b.reshape(1, c), w_a, w_x, b_a, b_x, lam)


def _pro_rglru_out(rows, vecs):
    yf, yb, gate = rows
    return (yf + yb) * _gelu_tanh(gate)


def _diff_attn_kernel(q_ref, k_ref, v_ref, lq1_ref, lk1_ref, lq2_ref, lk2_ref, g_ref, o_ref,
                      s_buf, p_buf, m_s, a_s, l_s, acc_s, *, tq, tk, rc, n_kv, lambda_init):
    m_s[...] = jnp.full_like(m_s, -jnp.inf)
    l_s[...] = jnp.zeros_like(l_s)
    acc_s[...] = jnp.zeros_like(acc_s)
    q = q_ref[...]
    qs = (q[:, :HEAD_DIM], q[:, HEAD_DIM:])

    def scores(j, slot):
        off = pl.multiple_of(j * tk, tk)
        kb = k_ref[pl.ds(off, tk), :]
        for c in range(2):
            kc = kb[:, c * HEAD_DIM:(c + 1) * HEAD_DIM]
            s_buf[slot, c * tq:(c + 1) * tq, :] = lax.dot_general(
                qs[c], kc, (((1,), (1,)), ((), ())), preferred_element_type=F32)

    def update(j, slot):
        off = pl.multiple_of(j * tk, tk)
        vb = v_ref[pl.ds(off, tk), :]
        for r0 in range(0, 2 * tq, rc):
            rows = slice(r0, r0 + rc)
            sc = s_buf[slot, rows, :]
            m_old = m_s[rows, :]
            m_new = jnp.maximum(m_old, jnp.max(sc, axis=-1, keepdims=True))
            alpha = jnp.exp2(m_old - m_new)
            p = jnp.exp2(sc - m_new)
            l_s[rows, :] = alpha * l_s[rows, :] + sum(
                p[:, t * LANES:(t + 1) * LANES] for t in range(tk // LANES))
            p_buf[rows, :] = p.astype(BF16)
            m_s[rows, :] = m_new
            a_s[rows, :] = alpha
        acc_s[...] = a_s[...] * acc_s[...] + jnp.dot(p_buf[...], vb, preferred_element_type=F32)

    scores(0, 0)

    def kv_pair(jj, _):
        j = 2 * jj
        scores(j + 1, 1)
        update(j, 0)
        scores(jnp.minimum(j + 2, n_kv - 1), 0)
        update(j + 1, 1)
        return 0

    lax.fori_loop(0, n_kv // 2, kv_pair, 0)
    lam = (jnp.exp(jnp.sum(lq1_ref[...] * lk1_ref[...], axis=-1, keepdims=True))
           - jnp.exp(jnp.sum(lq2_ref[...] * lk2_ref[...], axis=-1, keepdims=True)) + lambda_init)
    l = jnp.sum(l_s[...], axis=-1, keepdims=True)
    o = acc_s[0:tq] / l[0:tq] - lam * (acc_s[tq:2 * tq] / l[tq:2 * tq])
    o_ref[...] = (_rms(o, g_ref[...]) * (1.0 - lambda_init)).astype(o_ref.dtype)


def _diff_attention(qkv, lq1, lk1, lq2, lk2, subln_g, lambda_init, *, tq=512, tk=1024, rc=32):
    s = qkv.shape[0]
    vd = 2 * HEAD_DIM
    tq = min(tq, s)
    tk = min(tk, s // 2)
    assert (s // tk) % 2 == 0
    kern = functools.partial(_diff_attn_kernel, tq=tq, tk=tk, rc=rc, n_kv=s // tk,
                             lambda_init=lambda_init)
    vec = lambda a: a.reshape(1, -1)
    vspec = lambda w: pl.BlockSpec((1, w), lambda h, i: (0, 0))
    return pl.pallas_call(
        kern,
        out_shape=jax.ShapeDtypeStruct((s, DIFF_HEADS * vd), BF16),
        grid=(DIFF_HEADS, s // tq),
        in_specs=[
            pl.BlockSpec((tq, vd), lambda h, i: (i, h)),
            pl.BlockSpec((s, vd), lambda h, i: (0, DIFF_HEADS + h)),
            pl.BlockSpec((s, vd), lambda h, i: (0, 2 * DIFF_HEADS + h)),
            vspec(HEAD_DIM), vspec(HEAD_DIM), vspec(HEAD_DIM), vspec(HEAD_DIM), vspec(vd),
        ],
        out_specs=pl.BlockSpec((tq, vd), lambda h, i: (i, h)),
        scratch_shapes=[pltpu.VMEM((2, 2 * tq, tk), F32), pltpu.VMEM((2 * tq, tk), BF16),
                        pltpu.VMEM((2 * tq, 1), F32), pltpu.VMEM((2 * tq, 1), F32),
                        pltpu.VMEM((2 * tq, LANES), F32), pltpu.VMEM((2 * tq, vd), F32)],
        compiler_params=_params("parallel", "arbitrary"),
        name="diff_attention",
    )(qkv, qkv, qkv, vec(lq1), vec(lk1), vec(lq2), vec(lk2), vec(subln_g))


def _pro_identity(rows, vecs):
    return rows[0]


def _win_attn_kernel(sink_ref, q_ref, k_ref, v_ref, o_ref, *, tq, win, s_len, group):
    kvh = pl.program_id(0)
    i = pl.program_id(1)
    scale = HEAD_DIM ** -0.5
    start = jnp.clip(i * tq - WINDOW, 0, s_len - win)
    start = pl.multiple_of(start, WINDOW)
    kw = k_ref[pl.ds(start, win), :]
    vw = v_ref[pl.ds(start, win), :]
    qpos = i * tq + lax.broadcasted_iota(jnp.int32, (tq, win), 0)
    kpos = start + lax.broadcasted_iota(jnp.int32, (tq, win), 1)
    valid = jnp.abs(kpos - qpos) <= WINDOW
    for g in range(group):
        qg = q_ref[:, g * HEAD_DIM:(g + 1) * HEAD_DIM]
        sc = lax.dot_general(qg, kw, (((1,), (1,)), ((), ())), preferred_element_type=F32) * scale
        sc = jnp.where(valid, sc, NEG_INF)
        sink = sink_ref[kvh * group + g]
        m = jnp.maximum(jnp.max(sc, axis=-1, keepdims=True), sink)
        e = jnp.exp(sc - m)
        p = e / (jnp.sum(e, axis=-1, keepdims=True) + jnp.exp(sink - m))
        o = jnp.dot(p.astype(BF16), vw, preferred_element_type=F32)
        o_ref[:, g * HEAD_DIM:(g + 1) * HEAD_DIM] = o.astype(o_ref.dtype)


def _window_attention(qkv, sink, *, tq=256):
    s = qkv.shape[0]
    group = WIN_Q_HEADS // WIN_KV_HEADS
    tq = min(tq, s)
    win = min(tq + 2 * WINDOW, s)
    qw = group * HEAD_DIM
    k0 = WIN_Q_HEADS
    v0 = WIN_Q_HEADS + WIN_KV_HEADS
    kern = functools.partial(_win_attn_kernel, tq=tq, win=win, s_len=s, group=group)
    return pl.pallas_call(
        kern,
        out_shape=jax.ShapeDtypeStruct((s, WIN_Q_HEADS * HEAD_DIM), BF16),
        grid=(WIN_KV_HEADS, s // tq),
        in_specs=[
            pl.BlockSpec(memory_space=pltpu.SMEM),
            pl.BlockSpec((tq, qw), lambda h, i: (i, h)),
            pl.BlockSpec((s, HEAD_DIM), lambda h, i: (0, k0 + h)),
            pl.BlockSpec((s, HEAD_DIM), lambda h, i: (0, v0 + h)),
        ],
        out_specs=pl.BlockSpec((tq, qw), lambda h, i: (i, h)),
        compiler_params=_params("parallel", "arbitrary"),
        name="window_attention",
    )(sink.astype(F32), qkv, qkv, qkv)


def _s5_discretize(a_re, a_im, log_dt, b_re, b_im):
    dt = jnp.exp(log_dt)[:, None]
    mag = jnp.exp(dt * a_re)
    lr, li = mag * jnp.cos(dt * a_im), mag * jnp.sin(dt * a_im)
    den = a_re * a_re + a_im * a_im
    nr, ni = lr - 1.0, li
    fr = (nr * a_re + ni * a_im) / den
    fi = (ni * a_re - nr * a_im) / den
    bbr = fr[..., None] * b_re - fi[..., None] * b_im
    bbi = fr[..., None] * b_im + fi[..., None] * b_re
    return lr, li, bbr, bbi


def _s5_block_weights(lr, li, bbr, bbi, c_re, c_im, gpc):
    g, n, c = bbr.shape
    n_k = g // gpc
    eye = jnp.eye(gpc, dtype=F32)

    def w_in(bb):
        t = bb.reshape(n_k, gpc, n, c)
        return jnp.einsum('kgnc,gh->kgchn', t, eye).reshape(n_k, gpc * c, gpc * n)

    def w_out(cc):
        t = cc.reshape(n_k, gpc, c, n)
        return jnp.einsum('kgcn,gh->khngc', t, eye).reshape(n_k, gpc * n, gpc * c)

    win = jnp.concatenate([w_in(bbr), w_in(bbi)], axis=2)
    wout = jnp.concatenate([w_out(c_re), -w_out(c_im)], axis=1)
    lam = jnp.stack([lr.reshape(n_k, gpc * n), li.reshape(n_k, gpc * n)], axis=1)
    return win, wout, lam


def _s5_kernel(uf_ref, ub_ref, win_ref, wout_ref, lam_ref, yf_ref, yb_ref,
               xf_s, xb_s, st_s, *, tc, ns):
    i = pl.program_id(1)

    @pl.when(i == 0)
    def _():
        st_s[...] = jnp.zeros_like(st_s)

    xf_s[...] = jnp.dot(uf_ref[...].astype(BF16), win_ref[0, 0].astype(BF16),
                        preferred_element_type=F32)
    xb_s[...] = jnp.dot(ub_ref[...].astype(BF16), win_ref[1, 0].astype(BF16),
                        preferred_element_type=F32)
    lrf, lif = lam_ref[0, 0, 0:1, :], lam_ref[0, 0, 1:2, :]
    lrb, lib = lam_ref[1, 0, 0:1, :], lam_ref[1, 0, 1:2, :]

    def body(r, carry):
        srf, sif, srb, sib = carry
        nrf = lrf * srf - lif * sif + xf_s[pl.ds(r, 1), 0:ns]
        nif = lrf * sif + lif * srf + xf_s[pl.ds(r, 1), ns:2 * ns]
        xf_s[pl.ds(r, 1), 0:ns] = nrf
        xf_s[pl.ds(r, 1), ns:2 * ns] = nif
        rb = tc - 1 - r
        nrb = lrb * srb - lib * sib + xb_s[pl.ds(rb, 1), 0:ns]
        nib = lrb * sib + lib * srb + xb_s[pl.ds(rb, 1), ns:2 * ns]
        xb_s[pl.ds(rb, 1), 0:ns] = nrb
        xb_s[pl.ds(rb, 1), ns:2 * ns] = nib
        return nrf, nif, nrb, nib

    init = (st_s[0:1, :], st_s[1:2, :], st_s[2:3, :], st_s[3:4, :])
    srf, sif, srb, sib = lax.fori_loop(0, tc, body, init, unroll=8)
    st_s[0:1, :] = srf
    st_s[1:2, :] = sif
    st_s[2:3, :] = srb
    st_s[3:4, :] = sib
    yf_ref[...] = jnp.dot(xf_s[...].astype(BF16), wout_ref[0, 0].astype(BF16),
                          preferred_element_type=F32)
    yb_ref[...] = jnp.dot(xb_s[...].astype(BF16), wout_ref[1, 0].astype(BF16),
                          preferred_element_type=F32)


def _s5_scan(u, win, wout, lam, *, tc=256):
    s, w = u.shape
    _, n_k, cw, ns2 = win.shape
    ns = ns2 // 2
    tc = min(tc, s)
    n_t = s // tc
    kern = functools.partial(_s5_kernel, tc=tc, ns=ns)
    return pl.pallas_call(
        kern,
        out_shape=(jax.ShapeDtypeStruct((s, w), F32), jax.ShapeDtypeStruct((s, w), F32)),
        grid=(n_k, n_t),
        in_specs=[
            pl.BlockSpec((tc, cw), lambda k, i: (i, k)),
            pl.BlockSpec((tc, cw), lambda k, i: (n_t - 1 - i, k)),
            pl.BlockSpec((2, 1, cw, ns2), lambda k, i: (0, k, 0, 0)),
            pl.BlockSpec((2, 1, ns2, cw), lambda k, i: (0, k, 0, 0)),
            pl.BlockSpec((2, 1, 2, ns), lambda k, i: (0, k, 0, 0)),
        ],
        out_specs=(pl.BlockSpec((tc, cw), lambda k, i: (i, k)),
                   pl.BlockSpec((tc, cw), lambda k, i: (n_t - 1 - i, k))),
        scratch_shapes=[pltpu.VMEM((tc, ns2), F32), pltpu.VMEM((tc, ns2), F32),
                        pltpu.VMEM((4, ns), F32)],
        compiler_params=_params("parallel", "arbitrary"),
        name="s5_scan",
    )(u, u, win, wout, lam)


def _pro_s5_glu(rows, vecs):
    yf, yb, u = rows
    return _gelu_tanh(yf + yb + vecs[0] * u)


def _epi_glu(acc, epi_refs, o_ref, j, a32_s):
    o_ref[...] = (a32_s[...] * _sigmoid(acc)).astype(o_ref.dtype)


def _router_kernel(x_ref, g_ref, w_ref, h_ref, idx_ref, wt_ref):
    h = _rms(x_ref[...], g_ref[...])
    h_ref[...] = h
    logits = jnp.dot(h, w_ref[...], preferred_element_type=F32, precision=lax.Precision.HIGHEST)
    tm = logits.shape[0]
    lane = lax.broadcasted_iota(jnp.int32, (tm, LANES), 1)
    big = jnp.int32(LANES)
    ninf = -jnp.inf
    gl = jnp.where(lane < MOE_GROUPS, logits, ninf)
    gm = jnp.max(gl, axis=-1, keepdims=True)
    ge = jnp.exp(gl - gm)
    g_prob = ge / jnp.sum(ge, axis=-1, keepdims=True)
    g_p = jnp.max(g_prob, axis=-1, keepdims=True)
    g_idx = jnp.min(jnp.where(g_prob == g_p, lane, big), axis=-1, keepdims=True)
    lo = MOE_GROUPS + g_idx * EXPERTS_PER_GROUP
    in_grp = (lane >= lo) & (lane < lo + EXPERTS_PER_GROUP)
    el = jnp.where(in_grp, logits, ninf)
    em = jnp.max(el, axis=-1, keepdims=True)
    ee = jnp.exp(el - em)
    e_prob = jnp.where(in_grp, ee / jnp.sum(ee, axis=-1, keepdims=True), -1.0)
    p1 = jnp.max(e_prob, axis=-1, keepdims=True)
    i1 = jnp.min(jnp.where(e_prob == p1, lane, big), axis=-1, keepdims=True)
    rest = jnp.where(lane == i1, -1.0, e_prob)
    p2 = jnp.max(rest, axis=-1, keepdims=True)
    i2 = jnp.min(jnp.where(rest == p2, lane, big), axis=-1, keepdims=True)
    denom = p1 + p2
    w1 = g_p * (p1 / denom)
    w2 = g_p * (p2 / denom)
    idx_ref[...] = jnp.where(lane == 0, i1 - MOE_GROUPS, jnp.where(lane == 1, i2 - MOE_GROUPS, 0))
    wt_ref[...] = jnp.where(lane == 0, w1, jnp.where(lane == 1, w2, 0.0))


def _router(x, g, w_group, w_expert, *, tm=512):
    s, d = x.shape
    tm = min(tm, s)
    w_r = jnp.concatenate(
        [w_group, w_expert, jnp.zeros((d, LANES - MOE_GROUPS - N_EXPERTS), F32)], axis=1)
    return pl.pallas_call(
        _router_kernel,
        out_shape=(jax.ShapeDtypeStruct((s, d), F32), jax.ShapeDtypeStruct((s, LANES), jnp.int32),
                   jax.ShapeDtypeStruct((s, LANES), F32)),
        grid=(s // tm,),
        in_specs=[pl.BlockSpec((tm, d), lambda i: (i, 0)), pl.BlockSpec((1, d), lambda i: (0, 0)),
                  pl.BlockSpec((d, LANES), lambda i: (0, 0))],
        out_specs=(pl.BlockSpec((tm, d), lambda i: (i, 0)), pl.BlockSpec((tm, LANES), lambda i: (i, 0)),
                   pl.BlockSpec((tm, LANES), lambda i: (i, 0))),
        compiler_params=_params("parallel"),
        name="moe_router",
    )(x, g.reshape(1, d), w_r)


def _rank_kernel(idx_ref, dest_ref, cnt_ref, tot_s, pst_s, run_s, *, tm):
    ph = pl.program_id(0)
    i = pl.program_id(1)
    lane = lax.broadcasted_iota(jnp.int32, (tm, LANES), 1)
    idx = idx_ref[...]
    oh0 = (lane == idx[:, 0:1]).astype(F32)
    oh1 = (lane == idx[:, 1:2]).astype(F32)
    c = oh0 + oh1
    csum = jnp.sum(c, axis=0, keepdims=True)

    @pl.when(jnp.logical_and(ph == 0, i == 0))
    def _():
        tot_s[...] = jnp.zeros_like(tot_s)

    @pl.when(ph == 0)
    def _():
        tot_s[...] += csum

    @pl.when(jnp.logical_and(ph == 1, i == 0))
    def _():
        counts = tot_s[...]
        nblk = jnp.floor((counts + (EXPERT_BLOCK - 1)) * (1.0 / EXPERT_BLOCK))
        r = lax.broadcasted_iota(jnp.int32, (LANES, LANES), 0)
        cc = lax.broadcasted_iota(jnp.int32, (LANES, LANES), 1)
        upper = (r < cc).astype(F32)
        excl = jnp.dot(jnp.broadcast_to(nblk, (SUBLANES, LANES)), upper,
                       preferred_element_type=F32, precision=lax.Precision.HIGHEST)
        pst_s[...] = excl[0:1, :] * EXPERT_BLOCK
        run_s[...] = jnp.zeros_like(run_s)
        cnt_ref[...] = jnp.broadcast_to(counts, cnt_ref.shape)

    @pl.when(ph == 1)
    def _():
        rr = lax.broadcasted_iota(jnp.int32, (tm, tm), 0)
        cr = lax.broadcasted_iota(jnp.int32, (tm, tm), 1)
        lower = (rr > cr).astype(BF16)
        before = jnp.dot(lower, c.astype(BF16), preferred_element_type=F32) + run_s[...]
        base = pst_s[...] + before
        d0 = jnp.sum(oh0 * base, axis=-1, keepdims=True)
        d1 = jnp.sum(oh1 * (base + oh0), axis=-1, keepdims=True)
        dest_ref[...] = jnp.where(lane == 0, d0, jnp.where(lane == 1, d1, 0.0)).astype(jnp.int32)
        run_s[...] += csum


def _dispatch_rank(idx, *, tm=512):
    s = idx.shape[0]
    tm = min(tm, s)
    kern = functools.partial(_rank_kernel, tm=tm)
    return pl.pallas_call(
        kern,
        out_shape=(jax.ShapeDtypeStruct((s, LANES), jnp.int32),
                   jax.ShapeDtypeStruct((SUBLANES, LANES), F32)),
        grid=(2, s // tm),
        in_specs=[pl.BlockSpec((tm, LANES), lambda ph, i: (i, 0))],
        out_specs=(pl.BlockSpec((tm, LANES), lambda ph, i: (i * ph, 0)),
                   pl.BlockSpec((SUBLANES, LANES), lambda ph, i: (0, 0))),
        scratch_shapes=[pltpu.VMEM((1, LANES), F32) for _ in range(3)],
        compiler_params=_params("arbitrary", "arbitrary"),
        name="moe_rank",
    )(idx)


def _block_experts(counts, n_blk):
    cnt = counts[0, :N_EXPERTS].astype(jnp.int32)
    padded = ((cnt + EXPERT_BLOCK - 1) // EXPERT_BLOCK) * EXPERT_BLOCK
    pends = jnp.cumsum(padded)
    blk_start = jnp.arange(n_blk, dtype=jnp.int32) * EXPERT_BLOCK
    owner = jnp.sum((pends[None, :] <= blk_start[:, None]).astype(jnp.int32), axis=1)
    last_e = jnp.max(jnp.where(cnt > 0, jnp.arange(N_EXPERTS, dtype=jnp.int32), 0))
    n_used = (pends[-1] // EXPERT_BLOCK).reshape(1)
    return jnp.minimum(owner, last_e).astype(jnp.int32), n_used.astype(jnp.int32)


def _scatter_kernel(dest_ref, h_ref, xs_in_hbm, xs_hbm, h_s, sem, *, tb, n_b):
    del xs_in_hbm
    b = pl.program_id(0)
    slot = b % 2
    n = tb * TOP_K

    def copy(r, row, sl):
        return pltpu.make_async_copy(h_s.at[sl, pl.ds(r, 1), :], xs_hbm.at[pl.ds(row, 1), :],
                                     sem.at[sl])

    def wait_all(sl):
        def body(r, _):
            copy(0, 0, sl).wait()
            return 0
        lax.fori_loop(0, n, body, 0, unroll=8)

    h_s[slot] = h_ref[...]
    for r in range(tb):
        tok = b * tb + r
        for k in range(TOP_K):
            copy(r, dest_ref[tok * TOP_K + k], slot).start()

    @pl.when(b > 0)
    def _():
        wait_all(1 - slot)

    @pl.when(b == n_b - 1)
    def _():
        wait_all(slot)


def _dispatch_scatter(h, dest_flat, n_rows, *, tb=128):
    s, d = h.shape
    tb = min(tb, s)
    n_b = s // tb
    kern = functools.partial(_scatter_kernel, tb=tb, n_b=n_b)
    grid_spec = pltpu.PrefetchScalarGridSpec(
        num_scalar_prefetch=1,
        grid=(n_b,),
        in_specs=[pl.BlockSpec((tb, d), lambda b, dr: (b, 0)), pl.BlockSpec(memory_space=pl.ANY)],
        out_specs=pl.BlockSpec(memory_space=pl.ANY),
        scratch_shapes=[pltpu.VMEM((2, tb, d), h.dtype), pltpu.SemaphoreType.DMA((2,))],
    )
    return pl.pallas_call(
        kern,
        out_shape=jax.ShapeDtypeStruct((n_rows, d), h.dtype),
        grid_spec=grid_spec,
        input_output_aliases={2: 0},
        compiler_params=_params("arbitrary"),
        name="moe_scatter",
    )(dest_flat, h, jnp.zeros((n_rows, d), h.dtype))


def _row_copy(src_hbm, src_row, dst_buf, slot, dst_row, sem):
    return pltpu.make_async_copy(src_hbm.at[pl.ds(src_row, 1), :],
                                 dst_buf.at[slot, pl.ds(dst_row, 1), :], sem.at[slot])


def _gather_start(idx_ref, base, n, src_hbm, dst_buf, slot, sem):
    for r in range(n):
        _row_copy(src_hbm, idx_ref[base + r], dst_buf, slot, r, sem).start()


def _gather_wait(n, src_hbm, dst_buf, slot, sem):
    def body(r, _):
        _row_copy(src_hbm, 0, dst_buf, slot, r, sem).wait()
        return 0
    lax.fori_loop(0, n, body, 0, unroll=8)


def _expert_kernel(be_ref, nu_ref, x_ref, wg_ref, wu_ref, wd_ref, y_ref, wg_s, wu_s, wd_s):
    b = pl.program_id(0)
    changed = jnp.logical_or(b == 0, be_ref[b] != be_ref[jnp.maximum(b - 1, 0)])

    @pl.when(changed)
    def _():
        wg_s[...] = wg_ref[...].astype(BF16)
        wu_s[...] = wu_ref[...].astype(BF16)
        wd_s[...] = wd_ref[...].astype(BF16)

    @pl.when(b >= nu_ref[0])
    def _():
        y_ref[...] = jnp.zeros_like(y_ref)

    @pl.when(b < nu_ref[0])
    def _():
        x = x_ref[...].astype(BF16)
        hg = jnp.dot(x, wg_s[...], preferred_element_type=F32)
        hu = jnp.dot(x, wu_s[...], preferred_element_type=F32)
        hdn = (hg * _sigmoid(hg)) * hu
        y_ref[...] = jnp.dot(hdn.astype(BF16), wd_s[...], preferred_element_type=F32)


def _expert_mlp(xs, blk_e, n_used, layer, w_gate, w_up, w_down):
    n_rows, d = xs.shape
    de = w_gate.shape[3]
    grid_spec = pltpu.PrefetchScalarGridSpec(
        num_scalar_prefetch=2,
        grid=(n_rows // EXPERT_BLOCK,),
        in_specs=[
            pl.BlockSpec((EXPERT_BLOCK, d), lambda b, be, nu: (jnp.minimum(b, nu[0] - 1), 0)),
            pl.BlockSpec((None, None, d, de), lambda b, be, nu: (layer, be[b], 0, 0)),
            pl.BlockSpec((None, None, d, de), lambda b, be, nu: (layer, be[b], 0, 0)),
            pl.BlockSpec((None, None, de, d), lambda b, be, nu: (layer, be[b], 0, 0)),
        ],
        out_specs=pl.BlockSpec((EXPERT_BLOCK, d), lambda b, be, nu: (b, 0)),
        scratch_shapes=[pltpu.VMEM((d, de), BF16), pltpu.VMEM((d, de), BF16),
                        pltpu.VMEM((de, d), BF16)],
    )
    return pl.pallas_call(
        _expert_kernel,
        out_shape=jax.ShapeDtypeStruct((n_rows, d), F32),
        grid_spec=grid_spec,
        compiler_params=_params("arbitrary"),
        name="moe_experts",
    )(blk_e, n_used, xs, w_gate, w_up, w_down)


def _combine_kernel(dest_ref, x_ref, wt_ref, ys_hbm, g_ref, o_ref, ybuf, sem, *, tb, n_b, final):
    b = pl.program_id(0)
    slot = b % 2
    n = tb * TOP_K

    @pl.when(b == 0)
    def _():
        _gather_start(dest_ref, 0, n, ys_hbm, ybuf, 0, sem)

    _gather_wait(n, ys_hbm, ybuf, slot, sem)
    _gather_start(dest_ref, (b + 1) * n, n, ys_hbm, ybuf, 1 - slot, sem)
    y0 = ybuf[slot, 0:tb, :]
    y1 = ybuf[slot, tb:2 * tb, :]
    wt = wt_ref[...]
    out = x_ref[...] + (y0 * wt[:, 0:1] + y1 * wt[:, 1:2])
    if final:
        out = _rms(out, g_ref[...])
    o_ref[...] = out

    @pl.when(b == n_b - 1)
    def _():
        _gather_wait(n, ys_hbm, ybuf, 1 - slot, sem)


def _moe_combine(x, wt, ys, dest, g_final, *, final, tb=128):
    s, d = x.shape
    tb = min(tb, s)
    n_b = s // tb
    dest = dest.reshape(n_b, tb, TOP_K).transpose(0, 2, 1).reshape(-1)
    dest = jnp.concatenate([dest, jnp.zeros((tb * TOP_K,), jnp.int32)])
    kern = functools.partial(_combine_kernel, tb=tb, n_b=n_b, final=final)
    grid_spec = pltpu.PrefetchScalarGridSpec(
        num_scalar_prefetch=1,
        grid=(n_b,),
        in_specs=[
            pl.BlockSpec((tb, d), lambda b, dr: (b, 0)),
            pl.BlockSpec((tb, LANES), lambda b, dr: (b, 0)),
            pl.BlockSpec(memory_space=pl.ANY),
            pl.BlockSpec((1, d), lambda b, dr: (0, 0)),
        ],
        out_specs=pl.BlockSpec((tb, d), lambda b, dr: (b, 0)),
        scratch_shapes=[pltpu.VMEM((2, tb * TOP_K, d), F32), pltpu.SemaphoreType.DMA((2,))],
    )
    return pl.pallas_call(
        kern,
        out_shape=jax.ShapeDtypeStruct((s, d), F32),
        grid_spec=grid_spec,
        compiler_params=_params("arbitrary"),
        name="moe_combine",
    )(dest, x, wt, ys, g_final.reshape(1, d))


def _hier_moe(x, ln_g, w_group, w_expert, layer, w_gate, w_up, w_down, g_final, *, final):
    s = x.shape[0]
    h, idx, wt = _router(x, ln_g, w_group, w_expert)
    dest, counts = _dispatch_rank(idx)
    dest = dest[:, :TOP_K].reshape(-1)
    n_rows = -(-(s * TOP_K + N_EXPERTS * (EXPERT_BLOCK - 1)) // EXPERT_BLOCK) * EXPERT_BLOCK
    blk_e, n_used = _block_experts(counts, n_rows // EXPERT_BLOCK)
    xs = _dispatch_scatter(h, dest, n_rows)
    ys = _expert_mlp(xs, blk_e, n_used, layer, w_gate, w_up, w_down)
    return _moe_combine(x, wt, ys, dest, g_final, final=final)


def _rglru_layer(x, ln_g, w_in, conv_w, conv_b, w_a, w_x, b_a, b_x, lam, w_out):
    d = x.shape[1]
    c = conv_w.shape[1]
    proj = _norm_matmul(x, ln_g, w_in, name="lru_in_proj")
    yf, yb = _rglru_scan(proj, conv_w, conv_b, w_a, w_x, b_a, b_x, lam)
    return _matmul_residual([(yf, c, 0), (yb, c, 0), (proj, c, 0)], [], w_out, x, _pro_rglru_out,
                            k_dim=c, tm=512, name="lru_out_proj")


def _diff_layer(x, ln_g, rope, w_qkv, lq1, lk1, lq2, lk2, subln_g, w_out, lambda_init):
    qk_cols = 2 * DIFF_HEADS * 2 * HEAD_DIM
    qkv = _norm_matmul(x, ln_g, w_qkv, rope=rope, n_rope_cols=qk_cols, n_q_cols=qk_cols // 2,
                       q_scale=HEAD_DIM ** -0.5 * math.log2(math.e), out_dtype=BF16,
                       name="diff_qkv_proj")
    o = _diff_attention(qkv, lq1, lk1, lq2, lk2, subln_g, lambda_init)
    return _matmul_residual([(o, o.shape[1], 0)], [], w_out, x, _pro_identity,
                            k_dim=o.shape[1], name="diff_out_proj")


def _window_layer(x, ln_g, rope, w_qkv, sink, w_out):
    qk_cols = (WIN_Q_HEADS + WIN_KV_HEADS) * HEAD_DIM
    qkv = _norm_matmul(x, ln_g, w_qkv, rope=rope, n_rope_cols=qk_cols, out_dtype=BF16,
                       name="win_qkv_proj")
    o = _window_attention(qkv, sink)
    return _matmul_residual([(o, o.shape[1], 0)], [], w_out, x, _pro_identity,
                            k_dim=o.shape[1], name="win_out_proj")


def _s5_layer(x, ln_g, w_in, a_re, a_im, log_dt, b_re, b_im, c_re, c_im, d_skip, w_glu, w_out,
              *, gpc=16):
    u = _norm_matmul(x, ln_g, w_in, name="s5_in_proj")
    w = u.shape[1]
    wins, wouts, lams = [], [], []
    for dd in range(2):
        lr, li, bbr, bbi = _s5_discretize(a_re[dd], a_im[dd], log_dt[dd], b_re[dd], b_im[dd])
        wi, wo, lm = _s5_block_weights(lr, li, bbr, bbi, c_re[dd], c_im[dd], gpc)
        wins.append(wi)
        wouts.append(wo)
        lams.append(lm)
    yf, yb = _s5_scan(u, jnp.stack(wins), jnp.stack(wouts), jnp.stack(lams))
    z = _fused_mm([(yf, w, 0), (yb, w, 0), (u, w, 0)], [d_skip.reshape(1, w)], w_glu, [],
                  _pro_s5_glu, _epi_glu, out_dtype=BF16, tm=512, tn=w, k_dim=w, keep_f32=True,
                  name="s5_glu")
    return _matmul_residual([(z, w, 0)], [], w_out, x, _pro_identity, k_dim=w, name="s5_out_proj")


def kernel(x, positions, ln_mix, ln_ffn, ln_final, lru_w_in, lru_conv_w, lru_conv_b, lru_w_a, lru_w_x, lru_b_a, lru_b_x, lru_lambda, lru_w_out, diff_w_qkv, diff_lq1, diff_lk1, diff_lq2, diff_lk2, diff_subln, diff_w_out, win_w_qkv, win_sink, win_w_out, s5_w_in, s5_a_re, s5_a_im, s5_log_dt, s5_b_re, s5_b_im, s5_c_re, s5_c_im, s5_d, s5_w_glu, s5_w_out, moe_w_group, moe_w_expert, moe_w_gate, moe_w_up, moe_w_down):
    batch, s, d = x.shape
    depth = ln_mix.shape[0]
    outs = []
    for b in range(batch):
        xb = x[b]
        rope = _rope_tables(positions[b])
        for i in range(depth):
            kind, j = i % 4, i // 4
            if kind == 0:
                xb = _rglru_layer(xb, ln_mix[i], lru_w_in[j], lru_conv_w[j], lru_conv_b[j],
                                  lru_w_a[j], lru_w_x[j], lru_b_a[j], lru_b_x[j], lru_lambda[j],
                                  lru_w_out[j])
            elif kind == 1:
                xb = _diff_layer(xb, ln_mix[i], rope, diff_w_qkv[j], diff_lq1[j], diff_lk1[j],
                                 diff_lq2[j], diff_lk2[j], diff_subln[j], diff_w_out[j],
                                 0.8 - 0.6 * math.exp(-0.3 * i))
            elif kind == 2:
                xb = _window_layer(xb, ln_mix[i], rope, win_w_qkv[j], win_sink[j], win_w_out[j])
            else:
                xb = _s5_layer(xb, ln_mix[i], s5_w_in[j], s5_a_re[j], s5_a_im[j], s5_log_dt[j],
                               s5_b_re[j], s5_b_im[j], s5_c_re[j], s5_c_im[j], s5_d[j],
                               s5_w_glu[j], s5_w_out[j])
            xb = _hier_moe(xb, ln_ffn[i], moe_w_group[i], moe_w_expert[i], i, moe_w_gate,
                           moe_w_up, moe_w_down, ln_final, final=(i == depth - 1))
        outs.append(xb)
    return jnp.stack(outs)
```

```python
import functools
import math

import jax
import jax.numpy as jnp
from jax import lax
from jax.experimental import pallas as pl
from jax.experimental.pallas import tpu as pltpu

F32 = jnp.float32
BF16 = jnp.bfloat16

NORM_EPS = 1e-6
NEG_INF = -1e30
LANES = 128
SUBLANES = 8
VMEM_LIMIT = 56 * 1024 * 1024

HEAD_DIM = 128
ROT_DIM = HEAD_DIM // 4
ROPE_THETA = 500000.0
RGLRU_C = 8.0
CONV_W = 4
LRU_BLOCK_W = 128
WINDOW = 128
DIFF_HEADS = 8
WIN_Q_HEADS = 16
WIN_KV_HEADS = 4
SSM_GROUP_CH = 16
SSM_STATE = 64
MOE_GROUPS = 4
EXPERTS_PER_GROUP = 8
N_EXPERTS = MOE_GROUPS * EXPERTS_PER_GROUP
TOP_K = 2
EXPERT_BLOCK = 256


def _params(*sem):
    return pltpu.CompilerParams(dimension_semantics=sem, vmem_limit_bytes=VMEM_LIMIT)


def _rms(x, g):
    ms = jnp.mean(x * x, axis=-1, keepdims=True)
    return x * lax.rsqrt(ms + NORM_EPS) * g


def _gelu_tanh(x):
    return 0.5 * x * (1.0 + jnp.tanh(math.sqrt(2.0 / math.pi) * (x + 0.044715 * (x * x * x))))


def _sigmoid(x):
    return 1.0 / (1.0 + jnp.exp(-x))


def _fused_mm_kernel(*refs, n_row, n_vec, n_epi, prologue, epilogue, keep_f32):
    row_refs = refs[:n_row]
    vec_refs = refs[n_row:n_row + n_vec]
    w_ref = refs[n_row + n_vec]
    epi_refs = refs[n_row + n_vec + 1:n_row + n_vec + 1 + n_epi]
    o_ref = refs[n_row + n_vec + 1 + n_epi]
    a_s = refs[n_row + n_vec + 2 + n_epi]
    a32_s = refs[n_row + n_vec + 3 + n_epi] if keep_f32 else None
    j = pl.program_id(1)

    @pl.when(j == 0)
    def _():
        a = prologue([r[...] for r in row_refs], [v[...] for v in vec_refs])
        a_s[...] = a.astype(BF16)
        if keep_f32:
            a32_s[...] = a

    acc = jnp.dot(a_s[...], w_ref[...].astype(BF16), preferred_element_type=F32)
    epilogue(acc, epi_refs, o_ref, j, a32_s)


def _fused_mm(row_inputs, vec_inputs, w, epi_inputs, prologue, epilogue, *, out_dtype, tm, tn,
              k_dim, keep_f32=False, name):
    s = row_inputs[0][0].shape[0]
    n = w.shape[1]
    tm = min(tm, s)
    tn = min(tn, n)
    in_specs = []
    args = []
    for arr, width, cb in row_inputs:
        in_specs.append(pl.BlockSpec((tm, width), lambda i, j, cb=cb: (i, cb)))
        args.append(arr)
    for arr in vec_inputs:
        in_specs.append(pl.BlockSpec(arr.shape, lambda i, j: (0, 0)))
        args.append(arr)
    in_specs.append(pl.BlockSpec((k_dim, tn), lambda i, j: (0, j)))
    args.append(w)
    for arr, width, per_tile in epi_inputs:
        if per_tile:
            in_specs.append(pl.BlockSpec((tm, width), lambda i, j: (i, j)))
        else:
            in_specs.append(pl.BlockSpec((tm, width), lambda i, j: (i, 0)))
        args.append(arr)
    scratch = [pltpu.VMEM((tm, k_dim), BF16)]
    if keep_f32:
        scratch.append(pltpu.VMEM((tm, k_dim), F32))
    kern = functools.partial(_fused_mm_kernel, n_row=len(row_inputs), n_vec=len(vec_inputs),
                             n_epi=len(epi_inputs), prologue=prologue, epilogue=epilogue,
                             keep_f32=keep_f32)
    return pl.pallas_call(
        kern,
        out_shape=jax.ShapeDtypeStruct((s, n), out_dtype),
        grid=(s // tm, n // tn),
        in_specs=in_specs,
        out_specs=pl.BlockSpec((tm, tn), lambda i, j: (i, j)),
        scratch_shapes=scratch,
        compiler_params=_params("parallel", "arbitrary"),
        name=name,
    )(*args)


def _pro_rms(rows, vecs):
    return _rms(rows[0], vecs[0])


def _epi_store(acc, epi_refs, o_ref, j, a32_s):
    o_ref[...] = acc.astype(o_ref.dtype)


def _epi_residual(acc, epi_refs, o_ref, j, a32_s):
    o_ref[...] = (epi_refs[0][...] + acc).astype(o_ref.dtype)


def _make_epi_rope(n_rope_tiles, n_q_tiles, q_scale, tn):
    def epi(acc, epi_refs, o_ref, j, a32_s):
        c_ref, s1_ref, s2_ref = epi_refs

        @pl.when(j < n_rope_tiles)
        def _():
            c = c_ref[...]
            s1 = s1_ref[...]
            s2 = s2_ref[...]
            sc = jnp.where(j < n_q_tiles, q_scale, 1.0).astype(F32)
            for hh in range(tn // HEAD_DIM):
                xs = acc[:, hh * HEAD_DIM:(hh + 1) * HEAD_DIM]
                rot = (xs * c + pltpu.roll(xs, HEAD_DIM - ROT_DIM // 2, 1) * s1
                       + pltpu.roll(xs, ROT_DIM // 2, 1) * s2) * sc
                o_ref[:, hh * HEAD_DIM:(hh + 1) * HEAD_DIM] = rot.astype(o_ref.dtype)

        @pl.when(j >= n_rope_tiles)
        def _():
            o_ref[...] = acc.astype(o_ref.dtype)

    return epi


def _rope_tables(positions):
    half = ROT_DIM // 2
    inv = ROPE_THETA ** (-jnp.arange(0, ROT_DIM, 2, dtype=F32) / ROT_DIM)
    ang = positions.astype(F32)[:, None] * inv
    cos, sin = jnp.cos(ang), jnp.sin(ang)
    s = positions.shape[0]
    ones = jnp.ones((s, HEAD_DIM - ROT_DIM), F32)
    zeros = jnp.zeros((s, HEAD_DIM - ROT_DIM), F32)
    zh = jnp.zeros((s, half), F32)
    c_tab = jnp.concatenate([cos, cos, ones], axis=1)
    s1_tab = jnp.concatenate([-sin, zh, zeros], axis=1)
    s2_tab = jnp.concatenate([zh, sin, zeros], axis=1)
    return c_tab, s1_tab, s2_tab


def _norm_matmul(x, g, w, *, rope=None, n_rope_cols=0, n_q_cols=0, q_scale=1.0, out_dtype=F32,
                 tm=1024, tn=512, name):
    d = x.shape[1]
    if rope is None:
        epi, epi_inputs = _epi_store, []
    else:
        tn = min(tn, w.shape[1])
        assert n_rope_cols % tn == 0 and n_q_cols % tn == 0
        epi = _make_epi_rope(n_rope_cols // tn, n_q_cols // tn, q_scale, tn)
        epi_inputs = [(t, HEAD_DIM, False) for t in rope]
    return _fused_mm([(x, d, 0)], [g.reshape(1, d)], w, epi_inputs, _pro_rms, epi,
                     out_dtype=out_dtype, tm=tm, tn=tn, k_dim=d, name=name)


def _matmul_residual(row_inputs, vec_inputs, w, res, prologue, *, k_dim, tm=1024, tn=512, name):
    return _fused_mm(row_inputs, vec_inputs, w, [(res, min(tn, w.shape[1]), True)], prologue,
                     _epi_residual, out_dtype=F32, tm=tm, tn=tn, k_dim=k_dim, name=name)


def _rglru_kernel(xf_ref, xfp_ref, xfn_ref, xb_ref, xbp_ref, xbn_ref, cw_ref, cb_ref, wa_ref,
                  wx_ref, ba_ref, bx_ref, lam_ref, yf_ref, yb_ref,
                  ext_s, af_s, bf_s, ab_s, bb_s, hf_s, hb_s, *, tc, cw, n_t):
    i = pl.program_id(1)
    halo = SUBLANES

    @pl.when(i == 0)
    def _():
        hf_s[...] = jnp.zeros_like(hf_s)
        hb_s[...] = jnp.zeros_like(hb_s)

    def gates(x_ref, xp_ref, xn_ref, chunk, d, a_s, b_s):
        prev = jnp.where(chunk == 0, 0.0, xp_ref[...])
        nxt = jnp.where(chunk == n_t - 1, 0.0, xn_ref[...])
        ext_s[0:halo, :] = prev
        ext_s[halo:halo + tc, :] = x_ref[...]
        ext_s[halo + tc:halo + tc + halo, :] = nxt
        xc = cb_ref[...] + sum(
            cw_ref[k:k + 1, :] * ext_s[halo - 2 + k:halo - 2 + k + tc, :] for k in range(CONV_W))
        lam = lam_ref[d:d + 1, :]
        z = -lam
        sp = jnp.maximum(z, 0.0) + jnp.log1p(jnp.exp(-jnp.abs(z)))
        for blk in range(cw // LRU_BLOCK_W):
            sl = slice(blk * LRU_BLOCK_W, (blk + 1) * LRU_BLOCK_W)
            xb = xc[:, sl]
            xbh = xb.astype(BF16)
            r = _sigmoid(jnp.dot(xbh, wa_ref[d, blk].astype(BF16), preferred_element_type=F32)
                         + ba_ref[d:d + 1, sl])
            ig = _sigmoid(jnp.dot(xbh, wx_ref[d, blk].astype(BF16), preferred_element_type=F32)
                          + bx_ref[d:d + 1, sl])
            log_a = (-RGLRU_C) * r * sp[:, sl]
            a_s[:, sl] = jnp.exp(log_a)
            th = jnp.tanh(log_a)
            b_s[:, sl] = jnp.sqrt(-2.0 * th / (1.0 - th)) * (ig * xb)

    gates(xf_ref, xfp_ref, xfn_ref, i, 0, af_s, bf_s)
    gates(xb_ref, xbp_ref, xbn_ref, n_t - 1 - i, 1, ab_s, bb_s)

    def body(r, carry):
        hf, hb = carry
        hf = af_s[pl.ds(r, 1), :] * hf + bf_s[pl.ds(r, 1), :]
        yf_ref[pl.ds(r, 1), :] = hf
        rb = tc - 1 - r
        hb = ab_s[pl.ds(rb, 1), :] * hb + bb_s[pl.ds(rb, 1), :]
        yb_ref[pl.ds(rb, 1), :] = hb
        return hf, hb

    hf, hb = lax.fori_loop(0, tc, body, (hf_s[...], hb_s[...]), unroll=8)
    hf_s[...] = hf
    hb_s[...] = hb


def _rglru_scan(proj, conv_w, conv_b, w_a, w_x, b_a, b_x, lam, *, tc=256, cw=1024):
    s = proj.shape[0]
    c = conv_w.shape[1]
    tc = min(tc, s)
    n_t = s // tc
    n_c = c // cw
    xoff = c // cw
    hb = tc // SUBLANES
    last_h = s // SUBLANES - 1

    specs = [
        pl.BlockSpec((tc, cw), lambda ci, i: (i, xoff + ci)),
        pl.BlockSpec((SUBLANES, cw), lambda ci, i: (jnp.maximum(i * hb - 1, 0), xoff + ci)),
        pl.BlockSpec((SUBLANES, cw), lambda ci, i: (jnp.minimum((i + 1) * hb, last_h), xoff + ci)),
        pl.BlockSpec((tc, cw), lambda ci, i: (n_t - 1 - i, xoff + ci)),
        pl.BlockSpec((SUBLANES, cw),
                     lambda ci, i: (jnp.maximum((n_t - 1 - i) * hb - 1, 0), xoff + ci)),
        pl.BlockSpec((SUBLANES, cw),
                     lambda ci, i: (jnp.minimum((n_t - i) * hb, last_h), xoff + ci)),
        pl.BlockSpec((CONV_W, cw), lambda ci, i: (0, ci)),
        pl.BlockSpec((1, cw), lambda ci, i: (0, ci)),
        pl.BlockSpec((2, cw // LRU_BLOCK_W, LRU_BLOCK_W, LRU_BLOCK_W), lambda ci, i: (0, ci, 0, 0)),
        pl.BlockSpec((2, cw // LRU_BLOCK_W, LRU_BLOCK_W, LRU_BLOCK_W), lambda ci, i: (0, ci, 0, 0)),
        pl.BlockSpec((2, cw), lambda ci, i: (0, ci)),
        pl.BlockSpec((2, cw), lambda ci, i: (0, ci)),
        pl.BlockSpec((2, cw), lambda ci, i: (0, ci)),
    ]
    kern = functools.partial(_rglru_kernel, tc=tc, cw=cw, n_t=n_t)
    return pl.pallas_call(
        kern,
        out_shape=(jax.ShapeDtypeStruct((s, c), F32), jax.ShapeDtypeStruct((s, c), F32)),
        grid=(n_c, n_t),
        in_specs=specs,
        out_specs=(pl.BlockSpec((tc, cw), lambda ci, i: (i, ci)),
                   pl.BlockSpec((tc, cw), lambda ci, i: (n_t - 1 - i, ci))),
        scratch_shapes=[pltpu.VMEM((tc + 2 * SUBLANES, cw), F32)]
        + [pltpu.VMEM((tc, cw), F32) for _ in range(4)]
        + [pltpu.VMEM((1, cw), F32) for _ in range(2)],
        compiler_params=_params("parallel", "arbitrary"),
        name="rglru_scan",
    )(proj, proj, proj, proj, proj, proj, conv_w, conv_b.reshape(1, c), w_a, w_x, b_a, b_x, lam)


def _pro_rglru_out(rows, vecs):
    yf, yb, gate = rows
    return (yf + yb) * _gelu_tanh(gate)


def _diff_attn_kernel(q_ref, k_ref, v_ref, lq1_ref, lk1_ref, lq2_ref, lk2_ref, g_ref, o_ref,
                      s_buf, p_buf, m_s, a_s, l_s, acc_s, *, tq, tk, rc, n_kv, lambda_init):
    m_s[...] = jnp.full_like(m_s, -jnp.inf)
    l_s[...] = jnp.zeros_like(l_s)
    acc_s[...] = jnp.zeros_like(acc_s)
    q = q_ref[...]
    qs = (q[:, :HEAD_DIM], q[:, HEAD_DIM:])

    def scores(j, slot):
        off = pl.multiple_of(j * tk, tk)
        kb = k_ref[pl.ds(off, tk), :]
        for c in range(2):
            kc = kb[:, c * HEAD_DIM:(c + 1) * HEAD_DIM]
            s_buf[slot, c * tq:(c + 1) * tq, :] = lax.dot_general(
                qs[c], kc, (((1,), (1,)), ((), ())), preferred_element_type=F32)

    def update(j, slot):
        off = pl.multiple_of(j * tk, tk)
        vb = v_ref[pl.ds(off, tk), :]
        for r0 in range(0, 2 * tq, rc):
            rows = slice(r0, r0 + rc)
            sc = s_buf[slot, rows, :]
            m_old = m_s[rows, :]
            m_new = jnp.maximum(m_old, jnp.max(sc, axis=-1, keepdims=True))
            alpha = jnp.exp2(m_old - m_new)
            p = jnp.exp2(sc - m_new)
            l_s[rows, :] = alpha * l_s[rows, :] + sum(
                p[:, t * LANES:(t + 1) * LANES] for t in range(tk // LANES))
            p_buf[rows, :] = p.astype(BF16)
            m_s[rows, :] = m_new
            a_s[rows, :] = alpha
        acc_s[...] = a_s[...] * acc_s[...] + jnp.dot(p_buf[...], vb, preferred_element_type=F32)

    scores(0, 0)

    def kv_pair(jj, _):
        j = 2 * jj
        scores(j + 1, 1)
        update(j, 0)
        scores(jnp.minimum(j + 2, n_kv - 1), 0)
        update(j + 1, 1)
        return 0

    lax.fori_loop(0, n_kv // 2, kv_pair, 0)
    lam = (jnp.exp(jnp.sum(lq1_ref[...] * lk1_ref[...], axis=-1, keepdims=True))
           - jnp.exp(jnp.sum(lq2_ref[...] * lk2_ref[...], axis=-1, keepdims=True)) + lambda_init)
    l = jnp.sum(l_s[...], axis=-1, keepdims=True)
    o = acc_s[0:tq] / l[0:tq] - lam * (acc_s[tq:2 * tq] / l[tq:2 * tq])
    o_ref[...] = (_rms(o, g_ref[...]) * (1.0 - lambda_init)).astype(o_ref.dtype)


def _diff_attention(qkv, lq1, lk1, lq2, lk2, subln_g, lambda_init, *, tq=512, tk=1024, rc=32):
    s = qkv.shape[0]
    vd = 2 * HEAD_DIM
    tq = min(tq, s)
    tk = min(tk, s // 2)
    assert (s // tk) % 2 == 0
    kern = functools.partial(_diff_attn_kernel, tq=tq, tk=tk, rc=rc, n_kv=s // tk,
                             lambda_init=lambda_init)
    vec = lambda a: a.reshape(1, -1)
    vspec = lambda w: pl.BlockSpec((1, w), lambda h, i: (0, 0))
    return pl.pallas_call(
        kern,
        out_shape=jax.ShapeDtypeStruct((s, DIFF_HEADS * vd), BF16),
        grid=(DIFF_HEADS, s // tq),
        in_specs=[
            pl.BlockSpec((tq, vd), lambda h, i: (i, h)),
            pl.BlockSpec((s, vd), lambda h, i: (0, DIFF_HEADS + h)),
            pl.BlockSpec((s, vd), lambda h, i: (0, 2 * DIFF_HEADS + h)),
            vspec(HEAD_DIM), vspec(HEAD_DIM), vspec(HEAD_DIM), vspec(HEAD_DIM), vspec(vd),
        ],
        out_specs=pl.BlockSpec((tq, vd), lambda h, i: (i, h)),
        scratch_shapes=[pltpu.VMEM((2, 2 * tq, tk), F32), pltpu.VMEM((2 * tq, tk), BF16),
                        pltpu.VMEM((2 * tq, 1), F32), pltpu.VMEM((2 * tq, 1), F32),
                        pltpu.VMEM((2 * tq, LANES), F32), pltpu.VMEM((2 * tq, vd), F32)],
        compiler_params=_params("parallel", "arbitrary"),
        name="diff_attention",
    )(qkv, qkv, qkv, vec(lq1), vec(lk1), vec(lq2), vec(lk2), vec(subln_g))


def _pro_identity(rows, vecs):
    return rows[0]


def _win_attn_kernel(sink_ref, q_ref, k_ref, v_ref, o_ref, *, tq, win, s_len, group):
    kvh = pl.program_id(0)
    i = pl.program_id(1)
    start = jnp.clip(i * tq - WINDOW, 0, s_len - win)
    start = pl.multiple_of(start, WINDOW)
    kw = k_ref[pl.ds(start, win), :]
    vw = v_ref[pl.ds(start, win), :]
    qpos = i * tq + lax.broadcasted_iota(jnp.int32, (tq, win), 0)
    kpos = start + lax.broadcasted_iota(jnp.int32, (tq, win), 1)
    valid = jnp.abs(kpos - qpos) <= WINDOW
    for g in range(group):
        qg = q_ref[:, g * HEAD_DIM:(g + 1) * HEAD_DIM]
        sc = lax.dot_general(qg, kw, (((1,), (1,)), ((), ())), preferred_element_type=F32)
        sc = jnp.where(valid, sc, NEG_INF)
        sink = sink_ref[kvh * group + g] * math.log2(math.e)
        m = jnp.maximum(jnp.max(sc, axis=-1, keepdims=True), sink)
        e = jnp.exp2(sc - m)
        den = jnp.sum(e, axis=-1, keepdims=True) + jnp.exp2(sink - m)
        o = jnp.dot(e.astype(BF16), vw, preferred_element_type=F32) / den
        o_ref[:, g * HEAD_DIM:(g + 1) * HEAD_DIM] = o.astype(o_ref.dtype)


def _window_attention(qkv, sink, *, tq=256):
    s = qkv.shape[0]
    group = WIN_Q_HEADS // WIN_KV_HEADS
    tq = min(tq, s)
    win = min(tq + 2 * WINDOW, s)
    qw = group * HEAD_DIM
    k0 = WIN_Q_HEADS
    v0 = WIN_Q_HEADS + WIN_KV_HEADS
    kern = functools.partial(_win_attn_kernel, tq=tq, win=win, s_len=s, group=group)
    return pl.pallas_call(
        kern,
        out_shape=jax.ShapeDtypeStruct((s, WIN_Q_HEADS * HEAD_DIM), BF16),
        grid=(WIN_KV_HEADS, s // tq),
        in_specs=[
            pl.BlockSpec(memory_space=pltpu.SMEM),
            pl.BlockSpec((tq, qw), lambda h, i: (i, h)),
            pl.BlockSpec((s, HEAD_DIM), lambda h, i: (0, k0 + h)),
            pl.BlockSpec((s, HEAD_DIM), lambda h, i: (0, v0 + h)),
        ],
        out_specs=pl.BlockSpec((tq, qw), lambda h, i: (i, h)),
        compiler_params=_params("parallel", "arbitrary"),
        name="window_attention",
    )(sink.astype(F32), qkv, qkv, qkv)


def _s5_discretize(a_re, a_im, log_dt, b_re, b_im):
    dt = jnp.exp(log_dt)[:, None]
    mag = jnp.exp(dt * a_re)
    lr, li = mag * jnp.cos(dt * a_im), mag * jnp.sin(dt * a_im)
    den = a_re * a_re + a_im * a_im
    nr, ni = lr - 1.0, li
    fr = (nr * a_re + ni * a_im) / den
    fi = (ni * a_re - nr * a_im) / den
    bbr = fr[..., None] * b_re - fi[..., None] * b_im
    bbi = fr[..., None] * b_im + fi[..., None] * b_re
    return lr, li, bbr, bbi


def _s5_block_weights(lr, li, bbr, bbi, c_re, c_im, gpc):
    g, n, c = bbr.shape
    n_k = g // gpc
    eye = jnp.eye(gpc, dtype=F32)

    def w_in(bb):
        t = bb.reshape(n_k, gpc, n, c)
        return jnp.einsum('kgnc,gh->kgchn', t, eye).reshape(n_k, gpc * c, gpc * n)

    def w_out(cc):
        t = cc.reshape(n_k, gpc, c, n)
        return jnp.einsum('kgcn,gh->khngc', t, eye).reshape(n_k, gpc * n, gpc * c)

    win = jnp.concatenate([w_in(bbr), w_in(bbi)], axis=2)
    wout = jnp.concatenate([w_out(c_re), -w_out(c_im)], axis=1)
    lam = jnp.stack([lr.reshape(n_k, gpc * n), li.reshape(n_k, gpc * n)], axis=1)
    return win, wout, lam


def _s5_kernel(uf_ref, ub_ref, win_ref, wout_ref, lam_ref, yf_ref, yb_ref,
               xf_s, xb_s, st_s, *, tc, ns):
    i = pl.program_id(1)

    @pl.when(i == 0)
    def _():
        st_s[...] = jnp.zeros_like(st_s)

    xf_s[...] = jnp.dot(uf_ref[...].astype(BF16), win_ref[0, 0].astype(BF16),
                        preferred_element_type=F32)
    xb_s[...] = jnp.dot(ub_ref[...].astype(BF16), win_ref[1, 0].astype(BF16),
                        preferred_element_type=F32)
    lrf, lif = lam_ref[0, 0, 0:1, :], lam_ref[0, 0, 1:2, :]
    lrb, lib = lam_ref[1, 0, 0:1, :], lam_ref[1, 0, 1:2, :]

    def body(r, carry):
        srf, sif, srb, sib = carry
        nrf = lrf * srf - lif * sif + xf_s[pl.ds(r, 1), 0:ns]
        nif = lrf * sif + lif * srf + xf_s[pl.ds(r, 1), ns:2 * ns]
        xf_s[pl.ds(r, 1), 0:ns] = nrf
        xf_s[pl.ds(r, 1), ns:2 * ns] = nif
        rb = tc - 1 - r
        nrb = lrb * srb - lib * sib + xb_s[pl.ds(rb, 1), 0:ns]
        nib = lrb * sib + lib * srb + xb_s[pl.ds(rb, 1), ns:2 * ns]
        xb_s[pl.ds(rb, 1), 0:ns] = nrb
        xb_s[pl.ds(rb, 1), ns:2 * ns] = nib
        return nrf, nif, nrb, nib

    init = (st_s[0:1, :], st_s[1:2, :], st_s[2:3, :], st_s[3:4, :])
    srf, sif, srb, sib = lax.fori_loop(0, tc, body, init, unroll=8)
    st_s[0:1, :] = srf
    st_s[1:2, :] = sif
    st_s[2:3, :] = srb
    st_s[3:4, :] = sib
    yf_ref[...] = jnp.dot(xf_s[...].astype(BF16), wout_ref[0, 0].astype(BF16),
                          preferred_element_type=F32)
    yb_ref[...] = jnp.dot(xb_s[...].astype(BF16), wout_ref[1, 0].astype(BF16),
                          preferred_element_type=F32)


def _s5_scan(u, win, wout, lam, *, tc=256):
    s, w = u.shape
    _, n_k, cw, ns2 = win.shape
    ns = ns2 // 2
    tc = min(tc, s)
    n_t = s // tc
    kern = functools.partial(_s5_kernel, tc=tc, ns=ns)
    return pl.pallas_call(
        kern,
        out_shape=(jax.ShapeDtypeStruct((s, w), F32), jax.ShapeDtypeStruct((s, w), F32)),
        grid=(n_k, n_t),
        in_specs=[
            pl.BlockSpec((tc, cw), lambda k, i: (i, k)),
            pl.BlockSpec((tc, cw), lambda k, i: (n_t - 1 - i, k)),
            pl.BlockSpec((2, 1, cw, ns2), lambda k, i: (0, k, 0, 0)),
            pl.BlockSpec((2, 1, ns2, cw), lambda k, i: (0, k, 0, 0)),
            pl.BlockSpec((2, 1, 2, ns), lambda k, i: (0, k, 0, 0)),
        ],
        out_specs=(pl.BlockSpec((tc, cw), lambda k, i: (i, k)),
                   pl.BlockSpec((tc, cw), lambda k, i: (n_t - 1 - i, k))),
        scratch_shapes=[pltpu.VMEM((tc, ns2), F32), pltpu.VMEM((tc, ns2), F32),
                        pltpu.VMEM((4, ns), F32)],
        compiler_params=_params("parallel", "arbitrary"),
        name="s5_scan",
    )(u, u, win, wout, lam)


def _pro_s5_glu(rows, vecs):
    yf, yb, u = rows
    return _gelu_tanh(yf + yb + vecs[0] * u)


def _epi_glu(acc, epi_refs, o_ref, j, a32_s):
    o_ref[...] = (a32_s[...] * _sigmoid(acc)).astype(o_ref.dtype)


def _router_kernel(x_ref, g_ref, w_ref, h_ref, idx_ref, wt_ref):
    h = _rms(x_ref[...], g_ref[...])
    h_ref[...] = h
    logits = jnp.dot(h, w_ref[...], preferred_element_type=F32, precision=lax.Precision.HIGHEST)
    tm = logits.shape[0]
    lane = lax.broadcasted_iota(jnp.int32, (tm, LANES), 1)
    big = jnp.int32(LANES)
    ninf = -jnp.inf
    gl = jnp.where(lane < MOE_GROUPS, logits, ninf)
    gm = jnp.max(gl, axis=-1, keepdims=True)
    ge = jnp.exp(gl - gm)
    g_prob = ge / jnp.sum(ge, axis=-1, keepdims=True)
    g_p = jnp.max(g_prob, axis=-1, keepdims=True)
    g_idx = jnp.min(jnp.where(g_prob == g_p, lane, big), axis=-1, keepdims=True)
    lo = MOE_GROUPS + g_idx * EXPERTS_PER_GROUP
    in_grp = (lane >= lo) & (lane < lo + EXPERTS_PER_GROUP)
    el = jnp.where(in_grp, logits, ninf)
    em = jnp.max(el, axis=-1, keepdims=True)
    ee = jnp.exp(el - em)
    e_prob = jnp.where(in_grp, ee / jnp.sum(ee, axis=-1, keepdims=True), -1.0)
    p1 = jnp.max(e_prob, axis=-1, keepdims=True)
    i1 = jnp.min(jnp.where(e_prob == p1, lane, big), axis=-1, keepdims=True)
    rest = jnp.where(lane == i1, -1.0, e_prob)
    p2 = jnp.max(rest, axis=-1, keepdims=True)
    i2 = jnp.min(jnp.where(rest == p2, lane, big), axis=-1, keepdims=True)
    denom = p1 + p2
    w1 = g_p * (p1 / denom)
    w2 = g_p * (p2 / denom)
    idx_ref[...] = jnp.where(lane == 0, i1 - MOE_GROUPS, jnp.where(lane == 1, i2 - MOE_GROUPS, 0))
    wt_ref[...] = jnp.where(lane == 0, w1, jnp.where(lane == 1, w2, 0.0))


def _router(x, g, w_group, w_expert, *, tm=512):
    s, d = x.shape
    tm = min(tm, s)
    w_r = jnp.concatenate(
        [w_group, w_expert, jnp.zeros((d, LANES - MOE_GROUPS - N_EXPERTS), F32)], axis=1)
    return pl.pallas_call(
        _router_kernel,
        out_shape=(jax.ShapeDtypeStruct((s, d), F32), jax.ShapeDtypeStruct((s, LANES), jnp.int32),
                   jax.ShapeDtypeStruct((s, LANES), F32)),
        grid=(s // tm,),
        in_specs=[pl.BlockSpec((tm, d), lambda i: (i, 0)), pl.BlockSpec((1, d), lambda i: (0, 0)),
                  pl.BlockSpec((d, LANES), lambda i: (0, 0))],
        out_specs=(pl.BlockSpec((tm, d), lambda i: (i, 0)), pl.BlockSpec((tm, LANES), lambda i: (i, 0)),
                   pl.BlockSpec((tm, LANES), lambda i: (i, 0))),
        compiler_params=_params("parallel"),
        name="moe_router",
    )(x, g.reshape(1, d), w_r)


def _rank_kernel(idx_ref, dest_ref, cnt_ref, tot_s, pst_s, run_s, *, tm):
    ph = pl.program_id(0)
    i = pl.program_id(1)
    lane = lax.broadcasted_iota(jnp.int32, (tm, LANES), 1)
    idx = idx_ref[...]
    oh0 = (lane == idx[:, 0:1]).astype(F32)
    oh1 = (lane == idx[:, 1:2]).astype(F32)
    c = oh0 + oh1
    csum = jnp.sum(c, axis=0, keepdims=True)

    @pl.when(jnp.logical_and(ph == 0, i == 0))
    def _():
        tot_s[...] = jnp.zeros_like(tot_s)

    @pl.when(ph == 0)
    def _():
        tot_s[...] += csum

    @pl.when(jnp.logical_and(ph == 1, i == 0))
    def _():
        counts = tot_s[...]
        nblk = jnp.floor((counts + (EXPERT_BLOCK - 1)) * (1.0 / EXPERT_BLOCK))
        r = lax.broadcasted_iota(jnp.int32, (LANES, LANES), 0)
        cc = lax.broadcasted_iota(jnp.int32, (LANES, LANES), 1)
        upper = (r < cc).astype(F32)
        excl = jnp.dot(jnp.broadcast_to(nblk, (SUBLANES, LANES)), upper,
                       preferred_element_type=F32, precision=lax.Precision.HIGHEST)
        pst_s[...] = excl[0:1, :] * EXPERT_BLOCK
        run_s[...] = jnp.zeros_like(run_s)
        cnt_ref[...] = jnp.broadcast_to(counts, cnt_ref.shape)

    @pl.when(ph == 1)
    def _():
        rr = lax.broadcasted_iota(jnp.int32, (tm, tm), 0)
        cr = lax.broadcasted_iota(jnp.int32, (tm, tm), 1)
        lower = (rr > cr).astype(BF16)
        before = jnp.dot(lower, c.astype(BF16), preferred_element_type=F32) + run_s[...]
        base = pst_s[...] + before
        d0 = jnp.sum(oh0 * base, axis=-1, keepdims=True)
        d1 = jnp.sum(oh1 * (base + oh0), axis=-1, keepdims=True)
        dest_ref[...] = jnp.where(lane == 0, d0, jnp.where(lane == 1, d1, 0.0)).astype(jnp.int32)
        run_s[...] += csum


def _dispatch_rank(idx, *, tm=512):
    s = idx.shape[0]
    tm = min(tm, s)
    kern = functools.partial(_rank_kernel, tm=tm)
    return pl.pallas_call(
        kern,
        out_shape=(jax.ShapeDtypeStruct((s, LANES), jnp.int32),
                   jax.ShapeDtypeStruct((SUBLANES, LANES), F32)),
        grid=(2, s // tm),
        in_specs=[pl.BlockSpec((tm, LANES), lambda ph, i: (i, 0))],
        out_specs=(pl.BlockSpec((tm, LANES), lambda ph, i: (i * ph, 0)),
                   pl.BlockSpec((SUBLANES, LANES), lambda ph, i: (0, 0))),
        scratch_shapes=[pltpu.VMEM((1, LANES), F32) for _ in range(3)],
        compiler_params=_params("arbitrary", "arbitrary"),
        name="moe_rank",
    )(idx)


def _block_experts(counts, n_blk):
    cnt = counts[0, :N_EXPERTS].astype(jnp.int32)
    padded = ((cnt + EXPERT_BLOCK - 1) // EXPERT_BLOCK) * EXPERT_BLOCK
    pends = jnp.cumsum(padded)
    blk_start = jnp.arange(n_blk, dtype=jnp.int32) * EXPERT_BLOCK
    owner = jnp.sum((pends[None, :] <= blk_start[:, None]).astype(jnp.int32), axis=1)
    ids = jnp.arange(N_EXPERTS, dtype=jnp.int32)
    last_e = jnp.max(jnp.where(cnt > 0, ids, 0))
    n_used = (pends[-1] // EXPERT_BLOCK).reshape(1)
    blk_e = jnp.minimum(owner, last_e).astype(jnp.int32)
    later = jnp.logical_and(ids[None, :] > ids[:, None], cnt[None, :] > 0)
    nxt = jnp.min(jnp.where(later, ids[None, :], N_EXPERTS), axis=1)
    nxt = jnp.where(nxt == N_EXPERTS, ids, nxt).astype(jnp.int32)
    return blk_e, nxt[blk_e], n_used.astype(jnp.int32)


def _scatter_kernel(dest_ref, h_ref, xs_in_hbm, xs_hbm, h_s, sem, *, tb, n_b):
    del xs_in_hbm
    b = pl.program_id(0)
    slot = b % 2
    n = tb * TOP_K

    def copy(r, row, sl):
        return pltpu.make_async_copy(h_s.at[sl, pl.ds(r, 1), :], xs_hbm.at[pl.ds(row, 1), :],
                                     sem.at[sl])

    def wait_all(sl):
        def body(r, _):
            copy(0, 0, sl).wait()
            return 0
        lax.fori_loop(0, n, body, 0, unroll=8)

    h_s[slot] = h_ref[...]
    for r in range(tb):
        tok = b * tb + r
        for k in range(TOP_K):
            copy(r, dest_ref[tok * TOP_K + k], slot).start()

    @pl.when(b > 0)
    def _():
        wait_all(1 - slot)

    @pl.when(b == n_b - 1)
    def _():
        wait_all(slot)


def _dispatch_scatter(h, dest_flat, n_rows, *, tb=128):
    s, d = h.shape
    tb = min(tb, s)
    n_b = s // tb
    kern = functools.partial(_scatter_kernel, tb=tb, n_b=n_b)
    grid_spec = pltpu.PrefetchScalarGridSpec(
        num_scalar_prefetch=1,
        grid=(n_b,),
        in_specs=[pl.BlockSpec((tb, d), lambda b, dr: (b, 0)), pl.BlockSpec(memory_space=pl.ANY)],
        out_specs=pl.BlockSpec(memory_space=pl.ANY),
        scratch_shapes=[pltpu.VMEM((2, tb, d), h.dtype), pltpu.SemaphoreType.DMA((2,))],
    )
    return pl.pallas_call(
        kern,
        out_shape=jax.ShapeDtypeStruct((n_rows, d), h.dtype),
        grid_spec=grid_spec,
        input_output_aliases={2: 0},
        compiler_params=_params("arbitrary"),
        name="moe_scatter",
    )(dest_flat, h, jnp.zeros((n_rows, d), h.dtype))


def _row_copy(src_hbm, src_row, dst_buf, slot, dst_row, sem):
    return pltpu.make_async_copy(src_hbm.at[pl.ds(src_row, 1), :],
                                 dst_buf.at[slot, pl.ds(dst_row, 1), :], sem.at[slot])


def _gather_start(idx_ref, base, n, src_hbm, dst_buf, slot, sem):
    for r in range(n):
        _row_copy(src_hbm, idx_ref[base + r], dst_buf, slot, r, sem).start()


def _gather_wait(n, src_hbm, dst_buf, slot, sem):
    def body(r, _):
        _row_copy(src_hbm, 0, dst_buf, slot, r, sem).wait()
        return 0
    lax.fori_loop(0, n, body, 0, unroll=8)


def _expert_kernel(be_ref, nx_ref, nu_ref, x_ref, wg_hbm, wu_hbm, wd_hbm, y_ref,
                   wg_f, wu_f, wd_f, wg_s, wu_s, wd_s, sem, *, layer):
    b = pl.program_id(0)
    e = be_ref[b]

    def fetch(expert):
        return (pltpu.make_async_copy(wg_hbm.at[layer, expert], wg_f, sem.at[0]),
                pltpu.make_async_copy(wu_hbm.at[layer, expert], wu_f, sem.at[1]),
                pltpu.make_async_copy(wd_hbm.at[layer, expert], wd_f, sem.at[2]))

    @pl.when(b == 0)
    def _():
        for c in fetch(e):
            c.start()

    changed = jnp.logical_or(b == 0, e != be_ref[jnp.maximum(b - 1, 0)])

    @pl.when(changed)
    def _():
        for c in fetch(e):
            c.wait()
        wg_s[...] = wg_f[...].astype(BF16)
        wu_s[...] = wu_f[...].astype(BF16)
        wd_s[...] = wd_f[...].astype(BF16)

        @pl.when(nx_ref[b] != e)
        def _():
            for c in fetch(nx_ref[b]):
                c.start()

    @pl.when(b >= nu_ref[0])
    def _():
        y_ref[...] = jnp.zeros_like(y_ref)

    @pl.when(b < nu_ref[0])
    def _():
        x = x_ref[...].astype(BF16)
        hg = jnp.dot(x, wg_s[...], preferred_element_type=F32)
        hu = jnp.dot(x, wu_s[...], preferred_element_type=F32)
        hdn = (hg * _sigmoid(hg)) * hu
        y_ref[...] = jnp.dot(hdn.astype(BF16), wd_s[...], preferred_element_type=F32)


def _expert_mlp(xs, blk_e, blk_next, n_used, layer, w_gate, w_up, w_down):
    n_rows, d = xs.shape
    de = w_gate.shape[3]
    grid_spec = pltpu.PrefetchScalarGridSpec(
        num_scalar_prefetch=3,
        grid=(n_rows // EXPERT_BLOCK,),
        in_specs=[
            pl.BlockSpec((EXPERT_BLOCK, d),
                         lambda b, be, nx, nu: (jnp.minimum(b, nu[0] - 1), 0)),
            pl.BlockSpec(memory_space=pl.ANY),
            pl.BlockSpec(memory_space=pl.ANY),
            pl.BlockSpec(memory_space=pl.ANY),
        ],
        out_specs=pl.BlockSpec((EXPERT_BLOCK, d), lambda b, be, nx, nu: (b, 0)),
        scratch_shapes=[pltpu.VMEM((d, de), F32), pltpu.VMEM((d, de), F32),
                        pltpu.VMEM((de, d), F32), pltpu.VMEM((d, de), BF16),
                        pltpu.VMEM((d, de), BF16), pltpu.VMEM((de, d), BF16),
                        pltpu.SemaphoreType.DMA((3,))],
    )
    return pl.pallas_call(
        functools.partial(_expert_kernel, layer=layer),
        out_shape=jax.ShapeDtypeStruct((n_rows, d), F32),
        grid_spec=grid_spec,
        compiler_params=_params("arbitrary"),
        name="moe_experts",
    )(blk_e, blk_next, n_used, xs, w_gate, w_up, w_down)


def _combine_kernel(dest_ref, x_ref, wt_ref, ys_hbm, g_ref, o_ref, ybuf, sem, *, tb, n_b, final):
    b = pl.program_id(0)
    slot = b % 2
    n = tb * TOP_K

    @pl.when(b == 0)
    def _():
        _gather_start(dest_ref, 0, n, ys_hbm, ybuf, 0, sem)

    _gather_wait(n, ys_hbm, ybuf, slot, sem)
    _gather_start(dest_ref, (b + 1) * n, n, ys_hbm, ybuf, 1 - slot, sem)
    y0 = ybuf[slot, 0:tb, :]
    y1 = ybuf[slot, tb:2 * tb, :]
    wt = wt_ref[...]
    out = x_ref[...] + (y0 * wt[:, 0:1] + y1 * wt[:, 1:2])
    if final:
        out = _rms(out, g_ref[...])
    o_ref[...] = out

    @pl.when(b == n_b - 1)
    def _():
        _gather_wait(n, ys_hbm, ybuf, 1 - slot, sem)


def _moe_combine(x, wt, ys, dest, g_final, *, final, tb=128):
    s, d = x.shape
    tb = min(tb, s)
    n_b = s // tb
    dest = dest.reshape(n_b, tb, TOP_K).transpose(0, 2, 1).reshape(-1)
    dest = jnp.concatenate([dest, jnp.zeros((tb * TOP_K,), jnp.int32)])
    kern = functools.partial(_combine_kernel, tb=tb, n_b=n_b, final=final)
    grid_spec = pltpu.PrefetchScalarGridSpec(
        num_scalar_prefetch=1,
        grid=(n_b,),
        in_specs=[
            pl.BlockSpec((tb, d), lambda b, dr: (b, 0)),
            pl.BlockSpec((tb, LANES), lambda b, dr: (b, 0)),
            pl.BlockSpec(memory_space=pl.ANY),
            pl.BlockSpec((1, d), lambda b, dr: (0, 0)),
        ],
        out_specs=pl.BlockSpec((tb, d), lambda b, dr: (b, 0)),
        scratch_shapes=[pltpu.VMEM((2, tb * TOP_K, d), F32), pltpu.SemaphoreType.DMA((2,))],
    )
    return pl.pallas_call(
        kern,
        out_shape=jax.ShapeDtypeStruct((s, d), F32),
        grid_spec=grid_spec,
        compiler_params=_params("arbitrary"),
        name="moe_combine",
    )(dest, x, wt, ys, g_final.reshape(1, d))


def _hier_moe(x, ln_g, w_group, w_expert, layer, w_gate, w_up, w_down, g_final, *, final):
    s = x.shape[0]
    h, idx, wt = _router(x, ln_g, w_group, w_expert)
    dest, counts = _dispatch_rank(idx)
    dest = dest[:, :TOP_K].reshape(-1)
    n_rows = -(-(s * TOP_K + N_EXPERTS * (EXPERT_BLOCK - 1)) // EXPERT_BLOCK) * EXPERT_BLOCK
    blk_e, blk_next, n_used = _block_experts(counts, n_rows // EXPERT_BLOCK)
    xs = _dispatch_scatter(h, dest, n_rows)
    ys = _expert_mlp(xs, blk_e, blk_next, n_used, layer, w_gate, w_up, w_down)
    return _moe_combine(x, wt, ys, dest, g_final, final=final)


def _rglru_layer(x, ln_g, w_in, conv_w, conv_b, w_a, w_x, b_a, b_x, lam, w_out):
    d = x.shape[1]
    c = conv_w.shape[1]
    proj = _norm_matmul(x, ln_g, w_in, name="lru_in_proj")
    yf, yb = _rglru_scan(proj, conv_w, conv_b, w_a, w_x, b_a, b_x, lam)
    return _matmul_residual([(yf, c, 0), (yb, c, 0), (proj, c, 0)], [], w_out, x, _pro_rglru_out,
                            k_dim=c, tm=512, name="lru_out_proj")


def _diff_layer(x, ln_g, rope, w_qkv, lq1, lk1, lq2, lk2, subln_g, w_out, lambda_init):
    qk_cols = 2 * DIFF_HEADS * 2 * HEAD_DIM
    qkv = _norm_matmul(x, ln_g, w_qkv, rope=rope, n_rope_cols=qk_cols, n_q_cols=qk_cols // 2,
                       q_scale=HEAD_DIM ** -0.5 * math.log2(math.e), out_dtype=BF16,
                       name="diff_qkv_proj")
    o = _diff_attention(qkv, lq1, lk1, lq2, lk2, subln_g, lambda_init)
    return _matmul_residual([(o, o.shape[1], 0)], [], w_out, x, _pro_identity,
                            k_dim=o.shape[1], name="diff_out_proj")


def _window_layer(x, ln_g, rope, w_qkv, sink, w_out):
    qk_cols = (WIN_Q_HEADS + WIN_KV_HEADS) * HEAD_DIM
    qkv = _norm_matmul(x, ln_g, w_qkv, rope=rope, n_rope_cols=qk_cols,
                       n_q_cols=WIN_Q_HEADS * HEAD_DIM,
                       q_scale=HEAD_DIM ** -0.5 * math.log2(math.e), out_dtype=BF16,
                       name="win_qkv_proj")
    o = _window_attention(qkv, sink)
    return _matmul_residual([(o, o.shape[1], 0)], [], w_out, x, _pro_identity,
                            k_dim=o.shape[1], name="win_out_proj")


def _s5_layer(x, ln_g, w_in, a_re, a_im, log_dt, b_re, b_im, c_re, c_im, d_skip, w_glu, w_out,
              *, gpc=16):
    u = _norm_matmul(x, ln_g, w_in, name="s5_in_proj")
    w = u.shape[1]
    wins, wouts, lams = [], [], []
    for dd in range(2):
        lr, li, bbr, bbi = _s5_discretize(a_re[dd], a_im[dd], log_dt[dd], b_re[dd], b_im[dd])
        wi, wo, lm = _s5_block_weights(lr, li, bbr, bbi, c_re[dd], c_im[dd], gpc)
        wins.append(wi)
        wouts.append(wo)
        lams.append(lm)
    yf, yb = _s5_scan(u, jnp.stack(wins), jnp.stack(wouts), jnp.stack(lams))
    z = _fused_mm([(yf, w, 0), (yb, w, 0), (u, w, 0)], [d_skip.reshape(1, w)], w_glu, [],
                  _pro_s5_glu, _epi_glu, out_dtype=BF16, tm=512, tn=w, k_dim=w, keep_f32=True,
                  name="s5_glu")
    return _matmul_residual([(z, w, 0)], [], w_out, x, _pro_identity, k_dim=w, name="s5_out_proj")


def kernel(x, positions, ln_mix, ln_ffn, ln_final, lru_w_in, lru_conv_w, lru_conv_b, lru_w_a, lru_w_x, lru_b_a, lru_b_x, lru_lambda, lru_w_out, diff_w_qkv, diff_lq1, diff_lk1, diff_lq2, diff_lk2, diff_subln, diff_w_out, win_w_qkv, win_sink, win_w_out, s5_w_in, s5_a_re, s5_a_im, s5_log_dt, s5_b_re, s5_b_im, s5_c_re, s5_c_im, s5_d, s5_w_glu, s5_w_out, moe_w_group, moe_w_expert, moe_w_gate, moe_w_up, moe_w_down):
    batch, s, d = x.shape
    depth = ln_mix.shape[0]
    outs = []
    for b in range(batch):
        xb = x[b]
        rope = _rope_tables(positions[b])
        for i in range(depth):
            kind, j = i % 4, i // 4
            if kind == 0:
                xb = _rglru_layer(xb, ln_mix[i], lru_w_in[j], lru_conv_w[j], lru_conv_b[j],
                                  lru_w_a[j], lru_w_x[j], lru_b_a[j], lru_b_x[j], lru_lambda[j],
                                  lru_w_out[j])
            elif kind == 1:
                xb = _diff_layer(xb, ln_mix[i], rope, diff_w_qkv[j], diff_lq1[j], diff_lk1[j],
                                 diff_lq2[j], diff_lk2[j], diff_subln[j], diff_w_out[j],
                                 0.8 - 0.6 * math.exp(-0.3 * i))
            elif kind == 2:
                xb = _window_layer(xb, ln_mix[i], rope, win_w_qkv[j], win_sink[j], win_w_out[j])
            else:
                xb = _s5_layer(xb, ln_mix[i], s5_w_in[j], s5_a_re[j], s5_a_im[j], s5_log_dt[j],
                               s5_b_re[j], s5_b_im[j], s5_c_re[j], s5_c_im[j], s5_d[j],
                               s5_w_glu[j], s5_w_out[j])
            xb = _hier_moe(xb, ln_ffn[i], moe_w_group[i], moe_w_expert[i], i, moe_w_gate,
                           moe_w_up, moe_w_down, ln_final, final=(i == depth - 1))
        outs.append(xb)
    return jnp.stack(outs)
```

```python
import functools
import math

import jax
import jax.numpy as jnp
from jax import lax
from jax.experimental import pallas as pl
from jax.experimental.pallas import tpu as pltpu

F32 = jnp.float32
BF16 = jnp.bfloat16

NORM_EPS = 1e-6
NEG_INF = -1e30
LANES = 128
SUBLANES = 8
VMEM_LIMIT = 56 * 1024 * 1024

HEAD_DIM = 128
ROT_DIM = HEAD_DIM // 4
ROPE_THETA = 500000.0
RGLRU_C = 8.0
CONV_W = 4
LRU_BLOCK_W = 128
WINDOW = 128
DIFF_HEADS = 8
WIN_Q_HEADS = 16
WIN_KV_HEADS = 4
SSM_GROUP_CH = 16
SSM_STATE = 64
MOE_GROUPS = 4
EXPERTS_PER_GROUP = 8
N_EXPERTS = MOE_GROUPS * EXPERTS_PER_GROUP
TOP_K = 2
EXPERT_BLOCK = 256


def _params(*sem):
    return pltpu.CompilerParams(dimension_semantics=sem, vmem_limit_bytes=VMEM_LIMIT)


def _rms(x, g):
    ms = jnp.mean(x * x, axis=-1, keepdims=True)
    return x * lax.rsqrt(ms + NORM_EPS) * g


def _gelu_tanh(x):
    return 0.5 * x * (1.0 + jnp.tanh(math.sqrt(2.0 / math.pi) * (x + 0.044715 * (x * x * x))))


def _sigmoid(x):
    return 1.0 / (1.0 + jnp.exp(-x))


def _fused_mm_kernel(*refs, n_row, n_vec, n_epi, prologue, epilogue, keep_f32):
    row_refs = refs[:n_row]
    vec_refs = refs[n_row:n_row + n_vec]
    w_ref = refs[n_row + n_vec]
    epi_refs = refs[n_row + n_vec + 1:n_row + n_vec + 1 + n_epi]
    o_ref = refs[n_row + n_vec + 1 + n_epi]
    a_s = refs[n_row + n_vec + 2 + n_epi]
    a32_s = refs[n_row + n_vec + 3 + n_epi] if keep_f32 else None
    j = pl.program_id(1)

    @pl.when(j == 0)
    def _():
        a = prologue([r[...] for r in row_refs], [v[...] for v in vec_refs])
        a_s[...] = a.astype(BF16)
        if keep_f32:
            a32_s[...] = a

    acc = jnp.dot(a_s[...], w_ref[...].astype(BF16), preferred_element_type=F32)
    epilogue(acc, epi_refs, o_ref, j, a32_s)


def _fused_mm(row_inputs, vec_inputs, w, epi_inputs, prologue, epilogue, *, out_dtype, tm, tn,
              k_dim, keep_f32=False, name):
    s = row_inputs[0][0].shape[0]
    n = w.shape[1]
    tm = min(tm, s)
    tn = min(tn, n)
    w = w.astype(BF16)
    in_specs = []
    args = []
    for arr, width, cb in row_inputs:
        in_specs.append(pl.BlockSpec((tm, width), lambda i, j, cb=cb: (i, cb)))
        args.append(arr)
    for arr in vec_inputs:
        in_specs.append(pl.BlockSpec(arr.shape, lambda i, j: (0, 0)))
        args.append(arr)
    in_specs.append(pl.BlockSpec((k_dim, tn), lambda i, j: (0, j)))
    args.append(w)
    for arr, width, per_tile in epi_inputs:
        if per_tile:
            in_specs.append(pl.BlockSpec((tm, width), lambda i, j: (i, j)))
        else:
            in_specs.append(pl.BlockSpec((tm, width), lambda i, j: (i, 0)))
        args.append(arr)
    scratch = [pltpu.VMEM((tm, k_dim), BF16)]
    if keep_f32:
        scratch.append(pltpu.VMEM((tm, k_dim), F32))
    kern = functools.partial(_fused_mm_kernel, n_row=len(row_inputs), n_vec=len(vec_inputs),
                             n_epi=len(epi_inputs), prologue=prologue, epilogue=epilogue,
                             keep_f32=keep_f32)
    return pl.pallas_call(
        kern,
        out_shape=jax.ShapeDtypeStruct((s, n), out_dtype),
        grid=(s // tm, n // tn),
        in_specs=in_specs,
        out_specs=pl.BlockSpec((tm, tn), lambda i, j: (i, j)),
        scratch_shapes=scratch,
        compiler_params=_params("parallel", "arbitrary"),
        name=name,
    )(*args)


def _pro_rms(rows, vecs):
    return _rms(rows[0], vecs[0])


def _epi_store(acc, epi_refs, o_ref, j, a32_s):
    o_ref[...] = acc.astype(o_ref.dtype)


def _epi_residual(acc, epi_refs, o_ref, j, a32_s):
    o_ref[...] = (epi_refs[0][...] + acc).astype(o_ref.dtype)


def _make_epi_rope(n_rope_tiles, n_q_tiles, q_scale, tn):
    def epi(acc, epi_refs, o_ref, j, a32_s):
        c_ref, s1_ref, s2_ref = epi_refs

        @pl.when(j < n_rope_tiles)
        def _():
            c = c_ref[...]
            s1 = s1_ref[...]
            s2 = s2_ref[...]
            sc = jnp.where(j < n_q_tiles, q_scale, 1.0).astype(F32)
            for hh in range(tn // HEAD_DIM):
                xs = acc[:, hh * HEAD_DIM:(hh + 1) * HEAD_DIM]
                rot = (xs * c + pltpu.roll(xs, HEAD_DIM - ROT_DIM // 2, 1) * s1
                       + pltpu.roll(xs, ROT_DIM // 2, 1) * s2) * sc
                o_ref[:, hh * HEAD_DIM:(hh + 1) * HEAD_DIM] = rot.astype(o_ref.dtype)

        @pl.when(j >= n_rope_tiles)
        def _():
            o_ref[...] = acc.astype(o_ref.dtype)

    return epi


def _rope_tables(positions):
    half = ROT_DIM // 2
    inv = ROPE_THETA ** (-jnp.arange(0, ROT_DIM, 2, dtype=F32) / ROT_DIM)
    ang = positions.astype(F32)[:, None] * inv
    cos, sin = jnp.cos(ang), jnp.sin(ang)
    s = positions.shape[0]
    ones = jnp.ones((s, HEAD_DIM - ROT_DIM), F32)
    zeros = jnp.zeros((s, HEAD_DIM - ROT_DIM), F32)
    zh = jnp.zeros((s, half), F32)
    c_tab = jnp.concatenate([cos, cos, ones], axis=1)
    s1_tab = jnp.concatenate([-sin, zh, zeros], axis=1)
    s2_tab = jnp.concatenate([zh, sin, zeros], axis=1)
    return c_tab, s1_tab, s2_tab


def _norm_matmul(x, g, w, *, rope=None, n_rope_cols=0, n_q_cols=0, q_scale=1.0, out_dtype=F32,
                 tm=1024, tn=512, name):
    d = x.shape[1]
    if rope is None:
        epi, epi_inputs = _epi_store, []
    else:
        tn = min(tn, w.shape[1])
        assert n_rope_cols % tn == 0 and n_q_cols % tn == 0
        epi = _make_epi_rope(n_rope_cols // tn, n_q_cols // tn, q_scale, tn)
        epi_inputs = [(t, HEAD_DIM, False) for t in rope]
    return _fused_mm([(x, d, 0)], [g.reshape(1, d)], w, epi_inputs, _pro_rms, epi,
                     out_dtype=out_dtype, tm=tm, tn=tn, k_dim=d, name=name)


def _matmul_residual(row_inputs, vec_inputs, w, res, prologue, *, k_dim, tm=1024, tn=512, name):
    return _fused_mm(row_inputs, vec_inputs, w, [(res, min(tn, w.shape[1]), True)], prologue,
                     _epi_residual, out_dtype=F32, tm=tm, tn=tn, k_dim=k_dim, name=name)


def _rglru_kernel(xf_ref, xfp_ref, xfn_ref, xb_ref, xbp_ref, xbn_ref, cw_ref, cb_ref, wa_ref,
                  wx_ref, ba_ref, bx_ref, lam_ref, yf_ref, yb_ref,
                  ext_s, af_s, bf_s, ab_s, bb_s, hf_s, hb_s, *, tc, cw, n_t):
    i = pl.program_id(1)
    halo = SUBLANES

    @pl.when(i == 0)
    def _():
        hf_s[...] = jnp.zeros_like(hf_s)
        hb_s[...] = jnp.zeros_like(hb_s)

    def gates(x_ref, xp_ref, xn_ref, chunk, d, a_s, b_s):
        prev = jnp.where(chunk == 0, 0.0, xp_ref[...])
        nxt = jnp.where(chunk == n_t - 1, 0.0, xn_ref[...])
        ext_s[0:halo, :] = prev
        ext_s[halo:halo + tc, :] = x_ref[...]
        ext_s[halo + tc:halo + tc + halo, :] = nxt
        xc = cb_ref[...] + sum(
            cw_ref[k:k + 1, :] * ext_s[halo - 2 + k:halo - 2 + k + tc, :] for k in range(CONV_W))
        lam = lam_ref[d:d + 1, :]
        z = -lam
        sp = jnp.maximum(z, 0.0) + jnp.log1p(jnp.exp(-jnp.abs(z)))
        for blk in range(cw // LRU_BLOCK_W):
            sl = slice(blk * LRU_BLOCK_W, (blk + 1) * LRU_BLOCK_W)
            xb = xc[:, sl]
            xbh = xb.astype(BF16)
            r = _sigmoid(jnp.dot(xbh, wa_ref[d, blk].astype(BF16), preferred_element_type=F32)
                         + ba_ref[d:d + 1, sl])
            ig = _sigmoid(jnp.dot(xbh, wx_ref[d, blk].astype(BF16), preferred_element_type=F32)
                          + bx_ref[d:d + 1, sl])
            log_a = (-RGLRU_C) * r * sp[:, sl]
            a_s[:, sl] = jnp.exp(log_a)
            th = jnp.tanh(log_a)
            b_s[:, sl] = jnp.sqrt(-2.0 * th / (1.0 - th)) * (ig * xb)

    gates(xf_ref, xfp_ref, xfn_ref, i, 0, af_s, bf_s)
    gates(xb_ref, xbp_ref, xbn_ref, n_t - 1 - i, 1, ab_s, bb_s)

    def body(r, carry):
        hf, hb = carry
        hf = af_s[pl.ds(r, 1), :] * hf + bf_s[pl.ds(r, 1), :]
        yf_ref[pl.ds(r, 1), :] = hf
        rb = tc - 1 - r
        hb = ab_s[pl.ds(rb, 1), :] * hb + bb_s[pl.ds(rb, 1), :]
        yb_ref[pl.ds(rb, 1), :] = hb
        return hf, hb

    hf, hb = lax.fori_loop(0, tc, body, (hf_s[...], hb_s[...]), unroll=8)
    hf_s[...] = hf
    hb_s[...] = hb


def _rglru_scan(proj, conv_w, conv_b, w_a, w_x, b_a, b_x, lam, *, tc=256, cw=2048):
    s = proj.shape[0]
    c = conv_w.shape[1]
    tc = min(tc, s)
    n_t = s // tc
    n_c = c // cw
    xoff = c // cw
    hb = tc // SUBLANES
    last_h = s // SUBLANES - 1

    specs = [
        pl.BlockSpec((tc, cw), lambda ci, i: (i, xoff + ci)),
        pl.BlockSpec((SUBLANES, cw), lambda ci, i: (jnp.maximum(i * hb - 1, 0), xoff + ci)),
        pl.BlockSpec((SUBLANES, cw), lambda ci, i: (jnp.minimum((i + 1) * hb, last_h), xoff + ci)),
        pl.BlockSpec((tc, cw), lambda ci, i: (n_t - 1 - i, xoff + ci)),
        pl.BlockSpec((SUBLANES, cw),
                     lambda ci, i: (jnp.maximum((n_t - 1 - i) * hb - 1, 0), xoff + ci)),
        pl.BlockSpec((SUBLANES, cw),
                     lambda ci, i: (jnp.minimum((n_t - i) * hb, last_h), xoff + ci)),
        pl.BlockSpec((CONV_W, cw), lambda ci, i: (0, ci)),
        pl.BlockSpec((1, cw), lambda ci, i: (0, ci)),
        pl.BlockSpec((2, cw // LRU_BLOCK_W, LRU_BLOCK_W, LRU_BLOCK_W), lambda ci, i: (0, ci, 0, 0)),
        pl.BlockSpec((2, cw // LRU_BLOCK_W, LRU_BLOCK_W, LRU_BLOCK_W), lambda ci, i: (0, ci, 0, 0)),
        pl.BlockSpec((2, cw), lambda ci, i: (0, ci)),
        pl.BlockSpec((2, cw), lambda ci, i: (0, ci)),
        pl.BlockSpec((2, cw), lambda ci, i: (0, ci)),
    ]
    kern = functools.partial(_rglru_kernel, tc=tc, cw=cw, n_t=n_t)
    return pl.pallas_call(
        kern,
        out_shape=(jax.ShapeDtypeStruct((s, c), F32), jax.ShapeDtypeStruct((s, c), F32)),
        grid=(n_c, n_t),
        in_specs=specs,
        out_specs=(pl.BlockSpec((tc, cw), lambda ci, i: (i, ci)),
                   pl.BlockSpec((tc, cw), lambda ci, i: (n_t - 1 - i, ci))),
        scratch_shapes=[pltpu.VMEM((tc + 2 * SUBLANES, cw), F32)]
        + [pltpu.VMEM((tc, cw), F32) for _ in range(4)]
        + [pltpu.VMEM((1, cw), F32) for _ in range(2)],
        compiler_params=_params("parallel", "arbitrary"),
        name="rglru_scan",
    )(proj, proj, proj, proj, proj, proj, conv_w, conv_b.reshape(1, c), w_a, w_x, b_a, b_x, lam)


def _pro_rglru_out(rows, vecs):
    yf, yb, gate = rows
    return (yf + yb) * _gelu_tanh(gate)


def _diff_attn_kernel(q_ref, k_ref, v_ref, lq1_ref, lk1_ref, lq2_ref, lk2_ref, g_ref, o_ref,
                      s_buf, p_buf, m_s, a_s, l_s, acc_s, *, tq, tk, rc, n_kv, lambda_init):
    m_s[...] = jnp.full_like(m_s, -jnp.inf)
    l_s[...] = jnp.zeros_like(l_s)
    acc_s[...] = jnp.zeros_like(acc_s)
    q = q_ref[...]
    qs = (q[:, :HEAD_DIM], q[:, HEAD_DIM:])

    def scores(j, slot):
        off = pl.multiple_of(j * tk, tk)
        kb = k_ref[pl.ds(off, tk), :]
        for c in range(2):
            kc = kb[:, c * HEAD_DIM:(c + 1) * HEAD_DIM]
            s_buf[slot, c * tq:(c + 1) * tq, :] = lax.dot_general(
                qs[c], kc, (((1,), (1,)), ((), ())), preferred_element_type=F32)

    def update(j, slot):
        off = pl.multiple_of(j * tk, tk)
        vb = v_ref[pl.ds(off, tk), :]
        for r0 in range(0, 2 * tq, rc):
            rows = slice(r0, r0 + rc)
            sc = s_buf[slot, rows, :]
            m_old = m_s[rows, :]
            m_new = jnp.maximum(m_old, jnp.max(sc, axis=-1, keepdims=True))
            alpha = jnp.exp2(m_old - m_new)
            p = jnp.exp2(sc - m_new)
            l_s[rows, :] = alpha * l_s[rows, :] + sum(
                p[:, t * LANES:(t + 1) * LANES] for t in range(tk // LANES))
            p_buf[rows, :] = p.astype(BF16)
            m_s[rows, :] = m_new
            a_s[rows, :] = alpha
        acc_s[...] = a_s[...] * acc_s[...] + jnp.dot(p_buf[...], vb, preferred_element_type=F32)

    scores(0, 0)

    def kv_pair(jj, _):
        j = 2 * jj
        scores(j + 1, 1)
        update(j, 0)
        scores(jnp.minimum(j + 2, n_kv - 1), 0)
        update(j + 1, 1)
        return 0

    lax.fori_loop(0, n_kv // 2, kv_pair, 0)
    lam = (jnp.exp(jnp.sum(lq1_ref[...] * lk1_ref[...], axis=-1, keepdims=True))
           - jnp.exp(jnp.sum(lq2_ref[...] * lk2_ref[...], axis=-1, keepdims=True)) + lambda_init)
    l = jnp.sum(l_s[...], axis=-1, keepdims=True)
    o = acc_s[0:tq] / l[0:tq] - lam * (acc_s[tq:2 * tq] / l[tq:2 * tq])
    o_ref[...] = (_rms(o, g_ref[...]) * (1.0 - lambda_init)).astype(o_ref.dtype)


def _diff_attention(qkv, lq1, lk1, lq2, lk2, subln_g, lambda_init, *, tq=512, tk=1024, rc=32):
    s = qkv.shape[0]
    vd = 2 * HEAD_DIM
    tq = min(tq, s)
    tk = min(tk, s // 2)
    assert (s // tk) % 2 == 0
    kern = functools.partial(_diff_attn_kernel, tq=tq, tk=tk, rc=rc, n_kv=s // tk,
                             lambda_init=lambda_init)
    vec = lambda a: a.reshape(1, -1)
    vspec = lambda w: pl.BlockSpec((1, w), lambda h, i: (0, 0))
    return pl.pallas_call(
        kern,
        out_shape=jax.ShapeDtypeStruct((s, DIFF_HEADS * vd), BF16),
        grid=(DIFF_HEADS, s // tq),
        in_specs=[
            pl.BlockSpec((tq, vd), lambda h, i: (i, h)),
            pl.BlockSpec((s, vd), lambda h, i: (0, DIFF_HEADS + h)),
            pl.BlockSpec((s, vd), lambda h, i: (0, 2 * DIFF_HEADS + h)),
            vspec(HEAD_DIM), vspec(HEAD_DIM), vspec(HEAD_DIM), vspec(HEAD_DIM), vspec(vd),
        ],
        out_specs=pl.BlockSpec((tq, vd), lambda h, i: (i, h)),
        scratch_shapes=[pltpu.VMEM((2, 2 * tq, tk), F32), pltpu.VMEM((2 * tq, tk), BF16),
                        pltpu.VMEM((2 * tq, 1), F32), pltpu.VMEM((2 * tq, 1), F32),
                        pltpu.VMEM((2 * tq, LANES), F32), pltpu.VMEM((2 * tq, vd), F32)],
        compiler_params=_params("parallel", "arbitrary"),
        name="diff_attention",
    )(qkv, qkv, qkv, vec(lq1), vec(lk1), vec(lq2), vec(lk2), vec(subln_g))


def _pro_identity(rows, vecs):
    return rows[0]


def _win_attn_kernel(sink_ref, q_ref, k_ref, v_ref, o_ref, *, tq, win, s_len, group):
    kvh = pl.program_id(0)
    i = pl.program_id(1)
    start = jnp.clip(i * tq - WINDOW, 0, s_len - win)
    start = pl.multiple_of(start, WINDOW)
    kw = k_ref[pl.ds(start, win), :]
    vw = v_ref[pl.ds(start, win), :]
    qpos = i * tq + lax.broadcasted_iota(jnp.int32, (tq, win), 0)
    kpos = start + lax.broadcasted_iota(jnp.int32, (tq, win), 1)
    valid = jnp.abs(kpos - qpos) <= WINDOW
    for g in range(group):
        qg = q_ref[:, g * HEAD_DIM:(g + 1) * HEAD_DIM]
        sc = lax.dot_general(qg, kw, (((1,), (1,)), ((), ())), preferred_element_type=F32)
        sc = jnp.where(valid, sc, NEG_INF)
        sink = sink_ref[kvh * group + g] * math.log2(math.e)
        m = jnp.maximum(jnp.max(sc, axis=-1, keepdims=True), sink)
        e = jnp.exp2(sc - m)
        den = jnp.sum(e, axis=-1, keepdims=True) + jnp.exp2(sink - m)
        o = jnp.dot(e.astype(BF16), vw, preferred_element_type=F32) / den
        o_ref[:, g * HEAD_DIM:(g + 1) * HEAD_DIM] = o.astype(o_ref.dtype)


def _window_attention(qkv, sink, *, tq=256):
    s = qkv.shape[0]
    group = WIN_Q_HEADS // WIN_KV_HEADS
    tq = min(tq, s)
    win = min(tq + 2 * WINDOW, s)
    qw = group * HEAD_DIM
    k0 = WIN_Q_HEADS
    v0 = WIN_Q_HEADS + WIN_KV_HEADS
    kern = functools.partial(_win_attn_kernel, tq=tq, win=win, s_len=s, group=group)
    return pl.pallas_call(
        kern,
        out_shape=jax.ShapeDtypeStruct((s, WIN_Q_HEADS * HEAD_DIM), BF16),
        grid=(WIN_KV_HEADS, s // tq),
        in_specs=[
            pl.BlockSpec(memory_space=pltpu.SMEM),
            pl.BlockSpec((tq, qw), lambda h, i: (i, h)),
            pl.BlockSpec((s, HEAD_DIM), lambda h, i: (0, k0 + h)),
            pl.BlockSpec((s, HEAD_DIM), lambda h, i: (0, v0 + h)),
        ],
        out_specs=pl.BlockSpec((tq, qw), lambda h, i: (i, h)),
        compiler_params=_params("parallel", "arbitrary"),
        name="window_attention",
    )(sink.astype(F32), qkv, qkv, qkv)


def _s5_discretize(a_re, a_im, log_dt, b_re, b_im):
    dt = jnp.exp(log_dt)[:, None]
    mag = jnp.exp(dt * a_re)
    lr, li = mag * jnp.cos(dt * a_im), mag * jnp.sin(dt * a_im)
    den = a_re * a_re + a_im * a_im
    nr, ni = lr - 1.0, li
    fr = (nr * a_re + ni * a_im) / den
    fi = (ni * a_re - nr * a_im) / den
    bbr = fr[..., None] * b_re - fi[..., None] * b_im
    bbi = fr[..., None] * b_im + fi[..., None] * b_re
    return lr, li, bbr, bbi


def _s5_block_weights(lr, li, bbr, bbi, c_re, c_im, gpc):
    g, n, c = bbr.shape
    n_k = g // gpc
    eye = jnp.eye(gpc, dtype=F32)

    def w_in(bb):
        t = bb.reshape(n_k, gpc, n, c)
        return jnp.einsum('kgnc,gh->kgchn', t, eye).reshape(n_k, gpc * c, gpc * n)

    def w_out(cc):
        t = cc.reshape(n_k, gpc, c, n)
        return jnp.einsum('kgcn,gh->khngc', t, eye).reshape(n_k, gpc * n, gpc * c)

    win = jnp.concatenate([w_in(bbr), w_in(bbi)], axis=2)
    wout = jnp.concatenate([w_out(c_re), -w_out(c_im)], axis=1)
    lam = jnp.stack([lr.reshape(n_k, gpc * n), li.reshape(n_k, gpc * n)], axis=1)
    return win, wout, lam


def _s5_kernel(uf_ref, ub_ref, win_ref, wout_ref, lam_ref, yf_ref, yb_ref,
               xf_s, xb_s, st_s, *, tc, ns, cw, nsub):
    i = pl.program_id(1)

    @pl.when(i == 0)
    def _():
        st_s[...] = jnp.zeros_like(st_s)

    for c in range(nsub):
        cols = slice(c * cw, (c + 1) * cw)
        xf_s[c] = jnp.dot(uf_ref[:, cols].astype(BF16), win_ref[0, c].astype(BF16),
                          preferred_element_type=F32)
        xb_s[c] = jnp.dot(ub_ref[:, cols].astype(BF16), win_ref[1, c].astype(BF16),
                          preferred_element_type=F32)
    lam = [[(lam_ref[d, c, 0:1, :], lam_ref[d, c, 1:2, :]) for c in range(nsub)]
           for d in range(2)]

    def step(x_s, c, row, lr, li, sr, si):
        nr = lr * sr - li * si + x_s[c, pl.ds(row, 1), 0:ns]
        ni = lr * si + li * sr + x_s[c, pl.ds(row, 1), ns:2 * ns]
        x_s[c, pl.ds(row, 1), 0:ns] = nr
        x_s[c, pl.ds(row, 1), ns:2 * ns] = ni
        return nr, ni

    def body(r, carry):
        out = []
        for c in range(nsub):
            srf, sif, srb, sib = carry[4 * c:4 * c + 4]
            srf, sif = step(xf_s, c, r, *lam[0][c], srf, sif)
            srb, sib = step(xb_s, c, tc - 1 - r, *lam[1][c], srb, sib)
            out += [srf, sif, srb, sib]
        return tuple(out)

    init = tuple(st_s[j:j + 1, :] for j in range(4 * nsub))
    final = lax.fori_loop(0, tc, body, init, unroll=8)
    for j in range(4 * nsub):
        st_s[j:j + 1, :] = final[j]
    for c in range(nsub):
        cols = slice(c * cw, (c + 1) * cw)
        yf_ref[:, cols] = jnp.dot(xf_s[c].astype(BF16), wout_ref[0, c].astype(BF16),
                                  preferred_element_type=F32)
        yb_ref[:, cols] = jnp.dot(xb_s[c].astype(BF16), wout_ref[1, c].astype(BF16),
                                  preferred_element_type=F32)


def _s5_scan(u, win, wout, lam, *, tc=256, nsub=2):
    s, w = u.shape
    _, n_k, cw, ns2 = win.shape
    ns = ns2 // 2
    tc = min(tc, s)
    n_t = s // tc
    nsub = min(nsub, n_k)
    bw = nsub * cw
    kern = functools.partial(_s5_kernel, tc=tc, ns=ns, cw=cw, nsub=nsub)
    return pl.pallas_call(
        kern,
        out_shape=(jax.ShapeDtypeStruct((s, w), F32), jax.ShapeDtypeStruct((s, w), F32)),
        grid=(n_k // nsub, n_t),
        in_specs=[
            pl.BlockSpec((tc, bw), lambda k, i: (i, k)),
            pl.BlockSpec((tc, bw), lambda k, i: (n_t - 1 - i, k)),
            pl.BlockSpec((2, nsub, cw, ns2), lambda k, i: (0, k, 0, 0)),
            pl.BlockSpec((2, nsub, ns2, cw), lambda k, i: (0, k, 0, 0)),
            pl.BlockSpec((2, nsub, 2, ns), lambda k, i: (0, k, 0, 0)),
        ],
        out_specs=(pl.BlockSpec((tc, bw), lambda k, i: (i, k)),
                   pl.BlockSpec((tc, bw), lambda k, i: (n_t - 1 - i, k))),
        scratch_shapes=[pltpu.VMEM((nsub, tc, ns2), F32), pltpu.VMEM((nsub, tc, ns2), F32),
                        pltpu.VMEM((4 * nsub, ns), F32)],
        compiler_params=_params("parallel", "arbitrary"),
        name="s5_scan",
    )(u, u, win, wout, lam)


def _pro_s5_glu(rows, vecs):
    yf, yb, u = rows
    return _gelu_tanh(yf + yb + vecs[0] * u)


def _epi_glu(acc, epi_refs, o_ref, j, a32_s):
    o_ref[...] = (a32_s[...] * _sigmoid(acc)).astype(o_ref.dtype)


def _router_kernel(x_ref, g_ref, w_ref, h_ref, idx_ref, wt_ref):
    h = _rms(x_ref[...], g_ref[...])
    h_ref[...] = h
    logits = jnp.dot(h, w_ref[...], preferred_element_type=F32, precision=lax.Precision.HIGHEST)
    tm = logits.shape[0]
    lane = lax.broadcasted_iota(jnp.int32, (tm, LANES), 1)
    big = jnp.int32(LANES)
    ninf = -jnp.inf
    gl = jnp.where(lane < MOE_GROUPS, logits, ninf)
    gm = jnp.max(gl, axis=-1, keepdims=True)
    ge = jnp.exp(gl - gm)
    g_prob = ge / jnp.sum(ge, axis=-1, keepdims=True)
    g_p = jnp.max(g_prob, axis=-1, keepdims=True)
    g_idx = jnp.min(jnp.where(g_prob == g_p, lane, big), axis=-1, keepdims=True)
    lo = MOE_GROUPS + g_idx * EXPERTS_PER_GROUP
    in_grp = (lane >= lo) & (lane < lo + EXPERTS_PER_GROUP)
    el = jnp.where(in_grp, logits, ninf)
    em = jnp.max(el, axis=-1, keepdims=True)
    ee = jnp.exp(el - em)
    e_prob = jnp.where(in_grp, ee / jnp.sum(ee, axis=-1, keepdims=True), -1.0)
    p1 = jnp.max(e_prob, axis=-1, keepdims=True)
    i1 = jnp.min(jnp.where(e_prob == p1, lane, big), axis=-1, keepdims=True)
    rest = jnp.where(lane == i1, -1.0, e_prob)
    p2 = jnp.max(rest, axis=-1, keepdims=True)
    i2 = jnp.min(jnp.where(rest == p2, lane, big), axis=-1, keepdims=True)
    denom = p1 + p2
    w1 = g_p * (p1 / denom)
    w2 = g_p * (p2 / denom)
    idx_ref[...] = jnp.where(lane == 0, i1 - MOE_GROUPS, jnp.where(lane == 1, i2 - MOE_GROUPS, 0))
    wt_ref[...] = jnp.where(lane == 0, w1, jnp.where(lane == 1, w2, 0.0))


def _router(x, g, w_group, w_expert, *, tm=512):
    s, d = x.shape
    tm = min(tm, s)
    w_r = jnp.concatenate(
        [w_group, w_expert, jnp.zeros((d, LANES - MOE_GROUPS - N_EXPERTS), F32)], axis=1)
    return pl.pallas_call(
        _router_kernel,
        out_shape=(jax.ShapeDtypeStruct((s, d), F32), jax.ShapeDtypeStruct((s, LANES), jnp.int32),
                   jax.ShapeDtypeStruct((s, LANES), F32)),
        grid=(s // tm,),
        in_specs=[pl.BlockSpec((tm, d), lambda i: (i, 0)), pl.BlockSpec((1, d), lambda i: (0, 0)),
                  pl.BlockSpec((d, LANES), lambda i: (0, 0))],
        out_specs=(pl.BlockSpec((tm, d), lambda i: (i, 0)), pl.BlockSpec((tm, LANES), lambda i: (i, 0)),
                   pl.BlockSpec((tm, LANES), lambda i: (i, 0))),
        compiler_params=_params("parallel"),
        name="moe_router",
    )(x, g.reshape(1, d), w_r)


def _rank_kernel(idx_ref, dest_ref, cnt_ref, tot_s, pst_s, run_s, *, tm):
    ph = pl.program_id(0)
    i = pl.program_id(1)
    lane = lax.broadcasted_iota(jnp.int32, (tm, LANES), 1)
    idx = idx_ref[...]
    oh0 = (lane == idx[:, 0:1]).astype(F32)
    oh1 = (lane == idx[:, 1:2]).astype(F32)
    c = oh0 + oh1
    csum = jnp.sum(c, axis=0, keepdims=True)

    @pl.when(jnp.logical_and(ph == 0, i == 0))
    def _():
        tot_s[...] = jnp.zeros_like(tot_s)

    @pl.when(ph == 0)
    def _():
        tot_s[...] += csum

    @pl.when(jnp.logical_and(ph == 1, i == 0))
    def _():
        counts = tot_s[...]
        nblk = jnp.floor((counts + (EXPERT_BLOCK - 1)) * (1.0 / EXPERT_BLOCK))
        r = lax.broadcasted_iota(jnp.int32, (LANES, LANES), 0)
        cc = lax.broadcasted_iota(jnp.int32, (LANES, LANES), 1)
        upper = (r < cc).astype(F32)
        excl = jnp.dot(jnp.broadcast_to(nblk, (SUBLANES, LANES)), upper,
                       preferred_element_type=F32, precision=lax.Precision.HIGHEST)
        pst_s[...] = excl[0:1, :] * EXPERT_BLOCK
        run_s[...] = jnp.zeros_like(run_s)
        cnt_ref[...] = jnp.broadcast_to(counts, cnt_ref.shape)

    @pl.when(ph == 1)
    def _():
        rr = lax.broadcasted_iota(jnp.int32, (tm, tm), 0)
        cr = lax.broadcasted_iota(jnp.int32, (tm, tm), 1)
        lower = (rr > cr).astype(BF16)
        before = jnp.dot(lower, c.astype(BF16), preferred_element_type=F32) + run_s[...]
        base = pst_s[...] + before
        d0 = jnp.sum(oh0 * base, axis=-1, keepdims=True)
        d1 = jnp.sum(oh1 * (base + oh0), axis=-1, keepdims=True)
        dest_ref[...] = jnp.where(lane == 0, d0, jnp.where(lane == 1, d1, 0.0)).astype(jnp.int32)
        run_s[...] += csum


def _dispatch_rank(idx, *, tm=512):
    s = idx.shape[0]
    tm = min(tm, s)
    kern = functools.partial(_rank_kernel, tm=tm)
    return pl.pallas_call(
        kern,
        out_shape=(jax.ShapeDtypeStruct((s, LANES), jnp.int32),
                   jax.ShapeDtypeStruct((SUBLANES, LANES), F32)),
        grid=(2, s // tm),
        in_specs=[pl.BlockSpec((tm, LANES), lambda ph, i: (i, 0))],
        out_specs=(pl.BlockSpec((tm, LANES), lambda ph, i: (i * ph, 0)),
                   pl.BlockSpec((SUBLANES, LANES), lambda ph, i: (0, 0))),
        scratch_shapes=[pltpu.VMEM((1, LANES), F32) for _ in range(3)],
        compiler_params=_params("arbitrary", "arbitrary"),
        name="moe_rank",
    )(idx)


def _block_experts(counts, n_blk):
    cnt = counts[0, :N_EXPERTS].astype(jnp.int32)
    padded = ((cnt + EXPERT_BLOCK - 1) // EXPERT_BLOCK) * EXPERT_BLOCK
    pends = jnp.cumsum(padded)
    blk_start = jnp.arange(n_blk, dtype=jnp.int32) * EXPERT_BLOCK
    owner = jnp.sum((pends[None, :] <= blk_start[:, None]).astype(jnp.int32), axis=1)
    ids = jnp.arange(N_EXPERTS, dtype=jnp.int32)
    last_e = jnp.max(jnp.where(cnt > 0, ids, 0))
    n_used = (pends[-1] // EXPERT_BLOCK).reshape(1)
    blk_e = jnp.minimum(owner, last_e).astype(jnp.int32)
    later = jnp.logical_and(ids[None, :] > ids[:, None], cnt[None, :] > 0)
    nxt = jnp.min(jnp.where(later, ids[None, :], N_EXPERTS), axis=1)
    nxt = jnp.where(nxt == N_EXPERTS, ids, nxt).astype(jnp.int32)
    return blk_e, nxt[blk_e], n_used.astype(jnp.int32)


def _scatter_kernel(dest_ref, h_ref, xs_in_hbm, xs_hbm, h_s, sem, *, tb, n_b):
    del xs_in_hbm
    b = pl.program_id(0)
    slot = b % 2
    n = tb * TOP_K

    def copy(r, row, sl):
        return pltpu.make_async_copy(h_s.at[sl, pl.ds(r, 1), :], xs_hbm.at[pl.ds(row, 1), :],
                                     sem.at[sl])

    def wait_all(sl):
        def body(r, _):
            copy(0, 0, sl).wait()
            return 0
        lax.fori_loop(0, n, body, 0, unroll=8)

    h_s[slot] = h_ref[...]
    for r in range(tb):
        tok = b * tb + r
        for k in range(TOP_K):
            copy(r, dest_ref[tok * TOP_K + k], slot).start()

    @pl.when(b > 0)
    def _():
        wait_all(1 - slot)

    @pl.when(b == n_b - 1)
    def _():
        wait_all(slot)


def _dispatch_scatter(h, dest_flat, n_rows, *, tb=128):
    s, d = h.shape
    tb = min(tb, s)
    n_b = s // tb
    kern = functools.partial(_scatter_kernel, tb=tb, n_b=n_b)
    grid_spec = pltpu.PrefetchScalarGridSpec(
        num_scalar_prefetch=1,
        grid=(n_b,),
        in_specs=[pl.BlockSpec((tb, d), lambda b, dr: (b, 0)), pl.BlockSpec(memory_space=pl.ANY)],
        out_specs=pl.BlockSpec(memory_space=pl.ANY),
        scratch_shapes=[pltpu.VMEM((2, tb, d), h.dtype), pltpu.SemaphoreType.DMA((2,))],
    )
    return pl.pallas_call(
        kern,
        out_shape=jax.ShapeDtypeStruct((n_rows, d), h.dtype),
        grid_spec=grid_spec,
        input_output_aliases={2: 0},
        compiler_params=_params("arbitrary"),
        name="moe_scatter",
    )(dest_flat, h, jnp.zeros((n_rows, d), h.dtype))


def _row_copy(src_hbm, src_row, dst_buf, slot, dst_row, sem):
    return pltpu.make_async_copy(src_hbm.at[pl.ds(src_row, 1), :],
                                 dst_buf.at[slot, pl.ds(dst_row, 1), :], sem.at[slot])


def _gather_start(idx_ref, base, n, src_hbm, dst_buf, slot, sem):
    for r in range(n):
        _row_copy(src_hbm, idx_ref[base + r], dst_buf, slot, r, sem).start()


def _gather_wait(n, src_hbm, dst_buf, slot, sem):
    def body(r, _):
        _row_copy(src_hbm, 0, dst_buf, slot, r, sem).wait()
        return 0
    lax.fori_loop(0, n, body, 0, unroll=8)


def _expert_kernel(be_ref, nx_ref, nu_ref, x_ref, wg_hbm, wu_hbm, wd_hbm, y_ref,
                   wg_f, wu_f, wd_f, wg_s, wu_s, wd_s, sem, *, layer):
    b = pl.program_id(0)
    e = be_ref[b]

    def fetch(expert):
        return (pltpu.make_async_copy(wg_hbm.at[layer, expert], wg_f, sem.at[0]),
                pltpu.make_async_copy(wu_hbm.at[layer, expert], wu_f, sem.at[1]),
                pltpu.make_async_copy(wd_hbm.at[layer, expert], wd_f, sem.at[2]))

    @pl.when(b == 0)
    def _():
        for c in fetch(e):
            c.start()

    changed = jnp.logical_or(b == 0, e != be_ref[jnp.maximum(b - 1, 0)])

    @pl.when(changed)
    def _():
        for c in fetch(e):
            c.wait()
        wg_s[...] = wg_f[...].astype(BF16)
        wu_s[...] = wu_f[...].astype(BF16)
        wd_s[...] = wd_f[...].astype(BF16)

        @pl.when(nx_ref[b] != e)
        def _():
            for c in fetch(nx_ref[b]):
                c.start()

    @pl.when(b >= nu_ref[0])
    def _():
        y_ref[...] = jnp.zeros_like(y_ref)

    @pl.when(b < nu_ref[0])
    def _():
        x = x_ref[...].astype(BF16)
        hg = jnp.dot(x, wg_s[...], preferred_element_type=F32)
        hu = jnp.dot(x, wu_s[...], preferred_element_type=F32)
        hdn = (hg * _sigmoid(hg)) * hu
        y_ref[...] = jnp.dot(hdn.astype(BF16), wd_s[...], preferred_element_type=F32)


def _expert_mlp(xs, blk_e, blk_next, n_used, layer, w_gate, w_up, w_down):
    n_rows, d = xs.shape
    de = w_gate.shape[3]
    grid_spec = pltpu.PrefetchScalarGridSpec(
        num_scalar_prefetch=3,
        grid=(n_rows // EXPERT_BLOCK,),
        in_specs=[
            pl.BlockSpec((EXPERT_BLOCK, d),
                         lambda b, be, nx, nu: (jnp.minimum(b, nu[0] - 1), 0)),
            pl.BlockSpec(memory_space=pl.ANY),
            pl.BlockSpec(memory_space=pl.ANY),
            pl.BlockSpec(memory_space=pl.ANY),
        ],
        out_specs=pl.BlockSpec((EXPERT_BLOCK, d), lambda b, be, nx, nu: (b, 0)),
        scratch_shapes=[pltpu.VMEM((d, de), F32), pltpu.VMEM((d, de), F32),
                        pltpu.VMEM((de, d), F32), pltpu.VMEM((d, de), BF16),
                        pltpu.VMEM((d, de), BF16), pltpu.VMEM((de, d), BF16),
                        pltpu.SemaphoreType.DMA((3,))],
    )
    return pl.pallas_call(
        functools.partial(_expert_kernel, layer=layer),
        out_shape=jax.ShapeDtypeStruct((n_rows, d), F32),
        grid_spec=grid_spec,
        compiler_params=_params("arbitrary"),
        name="moe_experts",
    )(blk_e, blk_next, n_used, xs, w_gate, w_up, w_down)


def _combine_kernel(dest_ref, x_ref, wt_ref, ys_hbm, g_ref, o_ref, ybuf, sem, *, tb, n_b, final):
    b = pl.program_id(0)
    slot = b % 2
    n = tb * TOP_K

    @pl.when(b == 0)
    def _():
        _gather_start(dest_ref, 0, n, ys_hbm, ybuf, 0, sem)

    _gather_wait(n, ys_hbm, ybuf, slot, sem)
    _gather_start(dest_ref, (b + 1) * n, n, ys_hbm, ybuf, 1 - slot, sem)
    y0 = ybuf[slot, 0:tb, :]
    y1 = ybuf[slot, tb:2 * tb, :]
    wt = wt_ref[...]
    out = x_ref[...] + (y0 * wt[:, 0:1] + y1 * wt[:, 1:2])
    if final:
        out = _rms(out, g_ref[...])
    o_ref[...] = out

    @pl.when(b == n_b - 1)
    def _():
        _gather_wait(n, ys_hbm, ybuf, 1 - slot, sem)


def _moe_combine(x, wt, ys, dest, g_final, *, final, tb=128):
    s, d = x.shape
    tb = min(tb, s)
    n_b = s // tb
    dest = dest.reshape(n_b, tb, TOP_K).transpose(0, 2, 1).reshape(-1)
    dest = jnp.concatenate([dest, jnp.zeros((tb * TOP_K,), jnp.int32)])
    kern = functools.partial(_combine_kernel, tb=tb, n_b=n_b, final=final)
    grid_spec = pltpu.PrefetchScalarGridSpec(
        num_scalar_prefetch=1,
        grid=(n_b,),
        in_specs=[
            pl.BlockSpec((tb, d), lambda b, dr: (b, 0)),
            pl.BlockSpec((tb, LANES), lambda b, dr: (b, 0)),
            pl.BlockSpec(memory_space=pl.ANY),
            pl.BlockSpec((1, d), lambda b, dr: (0, 0)),
        ],
        out_specs=pl.BlockSpec((tb, d), lambda b, dr: (b, 0)),
        scratch_shapes=[pltpu.VMEM((2, tb * TOP_K, d), F32), pltpu.SemaphoreType.DMA((2,))],
    )
    return pl.pallas_call(
        kern,
        out_shape=jax.ShapeDtypeStruct((s, d), F32),
        grid_spec=grid_spec,
        compiler_params=_params("arbitrary"),
        name="moe_combine",
    )(dest, x, wt, ys, g_final.reshape(1, d))


def _hier_moe(x, ln_g, w_group, w_expert, layer, w_gate, w_up, w_down, g_final, *, final):
    s = x.shape[0]
    h, idx, wt = _router(x, ln_g, w_group, w_expert)
    dest, counts = _dispatch_rank(idx)
    dest = dest[:, :TOP_K].reshape(-1)
    n_rows = -(-(s * TOP_K + N_EXPERTS * (EXPERT_BLOCK - 1)) // EXPERT_BLOCK) * EXPERT_BLOCK
    blk_e, blk_next, n_used = _block_experts(counts, n_rows // EXPERT_BLOCK)
    xs = _dispatch_scatter(h, dest, n_rows)
    ys = _expert_mlp(xs, blk_e, blk_next, n_used, layer, w_gate, w_up, w_down)
    return _moe_combine(x, wt, ys, dest, g_final, final=final)


def _rglru_layer(x, ln_g, w_in, conv_w, conv_b, w_a, w_x, b_a, b_x, lam, w_out):
    d = x.shape[1]
    c = conv_w.shape[1]
    proj = _norm_matmul(x, ln_g, w_in, name="lru_in_proj")
    yf, yb = _rglru_scan(proj, conv_w, conv_b, w_a, w_x, b_a, b_x, lam)
    return _matmul_residual([(yf, c, 0), (yb, c, 0), (proj, c, 0)], [], w_out, x, _pro_rglru_out,
                            k_dim=c, tm=512, name="lru_out_proj")


def _diff_layer(x, ln_g, rope, w_qkv, lq1, lk1, lq2, lk2, subln_g, w_out, lambda_init):
    qk_cols = 2 * DIFF_HEADS * 2 * HEAD_DIM
    qkv = _norm_matmul(x, ln_g, w_qkv, rope=rope, n_rope_cols=qk_cols, n_q_cols=qk_cols // 2,
                       q_scale=HEAD_DIM ** -0.5 * math.log2(math.e), out_dtype=BF16,
                       name="diff_qkv_proj")
    o = _diff_attention(qkv, lq1, lk1, lq2, lk2, subln_g, lambda_init)
    return _matmul_residual([(o, o.shape[1], 0)], [], w_out, x, _pro_identity,
                            k_dim=o.shape[1], name="diff_out_proj")


def _window_layer(x, ln_g, rope, w_qkv, sink, w_out):
    qk_cols = (WIN_Q_HEADS + WIN_KV_HEADS) * HEAD_DIM
    qkv = _norm_matmul(x, ln_g, w_qkv, rope=rope, n_rope_cols=qk_cols,
                       n_q_cols=WIN_Q_HEADS * HEAD_DIM,
                       q_scale=HEAD_DIM ** -0.5 * math.log2(math.e), out_dtype=BF16,
                       name="win_qkv_proj")
    o = _window_attention(qkv, sink)
    return _matmul_residual([(o, o.shape[1], 0)], [], w_out, x, _pro_identity,
                            k_dim=o.shape[1], name="win_out_proj")


def _s5_layer(x, ln_g, w_in, a_re, a_im, log_dt, b_re, b_im, c_re, c_im, d_skip, w_glu, w_out,
              *, gpc=16):
    u = _norm_matmul(x, ln_g, w_in, name="s5_in_proj")
    w = u.shape[1]
    wins, wouts, lams = [], [], []
    for dd in range(2):
        lr, li, bbr, bbi = _s5_discretize(a_re[dd], a_im[dd], log_dt[dd], b_re[dd], b_im[dd])
        wi, wo, lm = _s5_block_weights(lr, li, bbr, bbi, c_re[dd], c_im[dd], gpc)
        wins.append(wi)
        wouts.append(wo)
        lams.append(lm)
    yf, yb = _s5_scan(u, jnp.stack(wins).astype(BF16), jnp.stack(wouts).astype(BF16),
                      jnp.stack(lams))
    z = _fused_mm([(yf, w, 0), (yb, w, 0), (u, w, 0)], [d_skip.reshape(1, w)], w_glu, [],
                  _pro_s5_glu, _epi_glu, out_dtype=BF16, tm=512, tn=w, k_dim=w, keep_f32=True,
                  name="s5_glu")
    return _matmul_residual([(z, w, 0)], [], w_out, x, _pro_identity, k_dim=w, name="s5_out_proj")


def kernel(x, positions, ln_mix, ln_ffn, ln_final, lru_w_in, lru_conv_w, lru_conv_b, lru_w_a, lru_w_x, lru_b_a, lru_b_x, lru_lambda, lru_w_out, diff_w_qkv, diff_lq1, diff_lk1, diff_lq2, diff_lk2, diff_subln, diff_w_out, win_w_qkv, win_sink, win_w_out, s5_w_in, s5_a_re, s5_a_im, s5_log_dt, s5_b_re, s5_b_im, s5_c_re, s5_c_im, s5_d, s5_w_glu, s5_w_out, moe_w_group, moe_w_expert, moe_w_gate, moe_w_up, moe_w_down):
    batch, s, d = x.shape
    depth = ln_mix.shape[0]
    outs = []
    for b in range(batch):
        xb = x[b]
        rope = _rope_tables(positions[b])
        for i in range(depth):
            kind, j = i % 4, i // 4
            if kind == 0:
                xb = _rglru_layer(xb, ln_mix[i], lru_w_in[j], lru_conv_w[j], lru_conv_b[j],
                                  lru_w_a[j], lru_w_x[j], lru_b_a[j], lru_b_x[j], lru_lambda[j],
                                  lru_w_out[j])
            elif kind == 1:
                xb = _diff_layer(xb, ln_mix[i], rope, diff_w_qkv[j], diff_lq1[j], diff_lk1[j],
                                 diff_lq2[j], diff_lk2[j], diff_subln[j], diff_w_out[j],
                                 0.8 - 0.6 * math.exp(-0.3 * i))
            elif kind == 2:
                xb = _window_layer(xb, ln_mix[i], rope, win_w_qkv[j], win_sink[j], win_w_out[j])
            else:
                xb = _s5_layer(xb, ln_mix[i], s5_w_in[j], s5_a_re[j], s5_a_im[j], s5_log_dt[j],
                               s5_b_re[j], s5_b_im[j], s5_c_re[j], s5_c_im[j], s5_d[j],
                               s5_w_glu[j], s5_w_out[j])
            xb = _hier_moe(xb, ln_ffn[i], moe_w_group[i], moe_w_expert[i], i, moe_w_gate,
                           moe_w_up, moe_w_down, ln_final, final=(i == depth - 1))
        outs.append(xb)
    return jnp.stack(outs)
```

```python
import functools
import math

import jax
import jax.numpy as jnp
from jax import lax
from jax.experimental import pallas as pl
from jax.experimental.pallas import tpu as pltpu

F32 = jnp.float32
BF16 = jnp.bfloat16

NORM_EPS = 1e-6
NEG_INF = -1e30
LANES = 128
SUBLANES = 8
VMEM_LIMIT = 56 * 1024 * 1024

HEAD_DIM = 128
ROT_DIM = HEAD_DIM // 4
ROPE_THETA = 500000.0
RGLRU_C = 8.0
CONV_W = 4
LRU_BLOCK_W = 128
WINDOW = 128
DIFF_HEADS = 8
WIN_Q_HEADS = 16
WIN_KV_HEADS = 4
SSM_GROUP_CH = 16
SSM_STATE = 64
MOE_GROUPS = 4
EXPERTS_PER_GROUP = 8
N_EXPERTS = MOE_GROUPS * EXPERTS_PER_GROUP
TOP_K = 2
EXPERT_BLOCK = 256


def _params(*sem):
    return pltpu.CompilerParams(dimension_semantics=sem, vmem_limit_bytes=VMEM_LIMIT)


def _rms(x, g):
    ms = jnp.mean(x * x, axis=-1, keepdims=True)
    return x * lax.rsqrt(ms + NORM_EPS) * g


def _gelu_tanh(x):
    return 0.5 * x * (1.0 + jnp.tanh(math.sqrt(2.0 / math.pi) * (x + 0.044715 * (x * x * x))))


def _sigmoid(x):
    return 1.0 / (1.0 + jnp.exp(-x))


def _fused_mm_kernel(*refs, n_row, n_vec, n_epi, prologue, epilogue, keep_f32):
    row_refs = refs[:n_row]
    vec_refs = refs[n_row:n_row + n_vec]
    w_ref = refs[n_row + n_vec]
    epi_refs = refs[n_row + n_vec + 1:n_row + n_vec + 1 + n_epi]
    o_ref = refs[n_row + n_vec + 1 + n_epi]
    a_s = refs[n_row + n_vec + 2 + n_epi]
    a32_s = refs[n_row + n_vec + 3 + n_epi] if keep_f32 else None
    j = pl.program_id(1)

    @pl.when(j == 0)
    def _():
        a = prologue([r[...] for r in row_refs], [v[...] for v in vec_refs])
        a_s[...] = a.astype(BF16)
        if keep_f32:
            a32_s[...] = a

    acc = jnp.dot(a_s[...], w_ref[...].astype(BF16), preferred_element_type=F32)
    epilogue(acc, epi_refs, o_ref, j, a32_s)


def _fused_mm(row_inputs, vec_inputs, w, epi_inputs, prologue, epilogue, *, out_dtype, tm, tn,
              k_dim, keep_f32=False, name):
    s = row_inputs[0][0].shape[0]
    n = w.shape[1]
    tm = min(tm, s)
    tn = min(tn, n)
    w = w.astype(BF16)
    in_specs = []
    args = []
    for arr, width, cb in row_inputs:
        in_specs.append(pl.BlockSpec((tm, width), lambda i, j, cb=cb: (i, cb)))
        args.append(arr)
    for arr in vec_inputs:
        in_specs.append(pl.BlockSpec(arr.shape, lambda i, j: (0, 0)))
        args.append(arr)
    in_specs.append(pl.BlockSpec((k_dim, tn), lambda i, j: (0, j)))
    args.append(w)
    for arr, width, per_tile in epi_inputs:
        if per_tile:
            in_specs.append(pl.BlockSpec((tm, width), lambda i, j: (i, j)))
        else:
            in_specs.append(pl.BlockSpec((tm, width), lambda i, j: (i, 0)))
        args.append(arr)
    scratch = [pltpu.VMEM((tm, k_dim), BF16)]
    if keep_f32:
        scratch.append(pltpu.VMEM((tm, k_dim), F32))
    kern = functools.partial(_fused_mm_kernel, n_row=len(row_inputs), n_vec=len(vec_inputs),
                             n_epi=len(epi_inputs), prologue=prologue, epilogue=epilogue,
                             keep_f32=keep_f32)
    return pl.pallas_call(
        kern,
        out_shape=jax.ShapeDtypeStruct((s, n), out_dtype),
        grid=(s // tm, n // tn),
        in_specs=in_specs,
        out_specs=pl.BlockSpec((tm, tn), lambda i, j: (i, j)),
        scratch_shapes=scratch,
        compiler_params=_params("parallel", "arbitrary"),
        name=name,
    )(*args)


def _pro_rms(rows, vecs):
    return _rms(rows[0], vecs[0])


def _epi_store(acc, epi_refs, o_ref, j, a32_s):
    o_ref[...] = acc.astype(o_ref.dtype)


def _epi_residual(acc, epi_refs, o_ref, j, a32_s):
    o_ref[...] = (epi_refs[0][...] + acc).astype(o_ref.dtype)


def _make_epi_rope(n_rope_tiles, n_q_tiles, q_scale, tn):
    def epi(acc, epi_refs, o_ref, j, a32_s):
        c_ref, s1_ref, s2_ref = epi_refs

        @pl.when(j < n_rope_tiles)
        def _():
            c = c_ref[...]
            s1 = s1_ref[...]
            s2 = s2_ref[...]
            sc = jnp.where(j < n_q_tiles, q_scale, 1.0).astype(F32)
            for hh in range(tn // HEAD_DIM):
                xs = acc[:, hh * HEAD_DIM:(hh + 1) * HEAD_DIM]
                rot = (xs * c + pltpu.roll(xs, HEAD_DIM - ROT_DIM // 2, 1) * s1
                       + pltpu.roll(xs, ROT_DIM // 2, 1) * s2) * sc
                o_ref[:, hh * HEAD_DIM:(hh + 1) * HEAD_DIM] = rot.astype(o_ref.dtype)

        @pl.when(j >= n_rope_tiles)
        def _():
            o_ref[...] = acc.astype(o_ref.dtype)

    return epi


def _rope_tables(positions):
    half = ROT_DIM // 2
    inv = ROPE_THETA ** (-jnp.arange(0, ROT_DIM, 2, dtype=F32) / ROT_DIM)
    ang = positions.astype(F32)[:, None] * inv
    cos, sin = jnp.cos(ang), jnp.sin(ang)
    s = positions.shape[0]
    ones = jnp.ones((s, HEAD_DIM - ROT_DIM), F32)
    zeros = jnp.zeros((s, HEAD_DIM - ROT_DIM), F32)
    zh = jnp.zeros((s, half), F32)
    c_tab = jnp.concatenate([cos, cos, ones], axis=1)
    s1_tab = jnp.concatenate([-sin, zh, zeros], axis=1)
    s2_tab = jnp.concatenate([zh, sin, zeros], axis=1)
    return c_tab, s1_tab, s2_tab


def _norm_matmul(x, g, w, *, rope=None, n_rope_cols=0, n_q_cols=0, q_scale=1.0, out_dtype=F32,
                 tm=1024, tn=512, name):
    d = x.shape[1]
    if rope is None:
        epi, epi_inputs = _epi_store, []
    else:
        tn = min(tn, w.shape[1])
        assert n_rope_cols % tn == 0 and n_q_cols % tn == 0
        epi = _make_epi_rope(n_rope_cols // tn, n_q_cols // tn, q_scale, tn)
        epi_inputs = [(t, HEAD_DIM, False) for t in rope]
    return _fused_mm([(x, d, 0)], [g.reshape(1, d)], w, epi_inputs, _pro_rms, epi,
                     out_dtype=out_dtype, tm=tm, tn=tn, k_dim=d, name=name)


def _matmul_residual(row_inputs, vec_inputs, w, res, prologue, *, k_dim, tm=1024, tn=512, name):
    return _fused_mm(row_inputs, vec_inputs, w, [(res, min(tn, w.shape[1]), True)], prologue,
                     _epi_residual, out_dtype=F32, tm=tm, tn=tn, k_dim=k_dim, name=name)


def _rglru_kernel(xf_ref, xfp_ref, xfn_ref, xb_ref, xbp_ref, xbn_ref, cw_ref, cb_ref, wa_ref,
                  wx_ref, ba_ref, bx_ref, lam_ref, yf_ref, yb_ref,
                  ext_s, af_s, bf_s, ab_s, bb_s, hf_s, hb_s, *, tc, cw, n_t):
    i = pl.program_id(1)
    halo = SUBLANES

    @pl.when(i == 0)
    def _():
        hf_s[...] = jnp.zeros_like(hf_s)
        hb_s[...] = jnp.zeros_like(hb_s)

    def gates(x_ref, xp_ref, xn_ref, chunk, d, a_s, b_s):
        prev = jnp.where(chunk == 0, 0.0, xp_ref[...])
        nxt = jnp.where(chunk == n_t - 1, 0.0, xn_ref[...])
        ext_s[0:halo, :] = prev
        ext_s[halo:halo + tc, :] = x_ref[...]
        ext_s[halo + tc:halo + tc + halo, :] = nxt
        xc = cb_ref[...] + sum(
            cw_ref[k:k + 1, :] * ext_s[halo - 2 + k:halo - 2 + k + tc, :] for k in range(CONV_W))
        lam = lam_ref[d:d + 1, :]
        z = -lam
        sp = jnp.maximum(z, 0.0) + jnp.log1p(jnp.exp(-jnp.abs(z)))
        for blk in range(cw // LRU_BLOCK_W):
            sl = slice(blk * LRU_BLOCK_W, (blk + 1) * LRU_BLOCK_W)
            xb = xc[:, sl]
            xbh = xb.astype(BF16)
            r = _sigmoid(jnp.dot(xbh, wa_ref[d, blk].astype(BF16), preferred_element_type=F32)
                         + ba_ref[d:d + 1, sl])
            ig = _sigmoid(jnp.dot(xbh, wx_ref[d, blk].astype(BF16), preferred_element_type=F32)
                          + bx_ref[d:d + 1, sl])
            log_a = (-RGLRU_C) * r * sp[:, sl]
            a_s[:, sl] = jnp.exp(log_a)
            th = jnp.tanh(log_a)
            b_s[:, sl] = jnp.sqrt(-2.0 * th / (1.0 - th)) * (ig * xb)

    gates(xf_ref, xfp_ref, xfn_ref, i, 0, af_s, bf_s)
    gates(xb_ref, xbp_ref, xbn_ref, n_t - 1 - i, 1, ab_s, bb_s)

    def body(r, carry):
        hf, hb = carry
        hf = af_s[pl.ds(r, 1), :] * hf + bf_s[pl.ds(r, 1), :]
        yf_ref[pl.ds(r, 1), :] = hf
        rb = tc - 1 - r
        hb = ab_s[pl.ds(rb, 1), :] * hb + bb_s[pl.ds(rb, 1), :]
        yb_ref[pl.ds(rb, 1), :] = hb
        return hf, hb

    hf, hb = lax.fori_loop(0, tc, body, (hf_s[...], hb_s[...]), unroll=8)
    hf_s[...] = hf
    hb_s[...] = hb


def _rglru_scan(proj, conv_w, conv_b, w_a, w_x, b_a, b_x, lam, *, tc=256, cw=2048):
    s = proj.shape[0]
    c = conv_w.shape[1]
    tc = min(tc, s)
    n_t = s // tc
    n_c = c // cw
    xoff = c // cw
    hb = tc // SUBLANES
    last_h = s // SUBLANES - 1

    specs = [
        pl.BlockSpec((tc, cw), lambda ci, i: (i, xoff + ci)),
        pl.BlockSpec((SUBLANES, cw), lambda ci, i: (jnp.maximum(i * hb - 1, 0), xoff + ci)),
        pl.BlockSpec((SUBLANES, cw), lambda ci, i: (jnp.minimum((i + 1) * hb, last_h), xoff + ci)),
        pl.BlockSpec((tc, cw), lambda ci, i: (n_t - 1 - i, xoff + ci)),
        pl.BlockSpec((SUBLANES, cw),
                     lambda ci, i: (jnp.maximum((n_t - 1 - i) * hb - 1, 0), xoff + ci)),
        pl.BlockSpec((SUBLANES, cw),
                     lambda ci, i: (jnp.minimum((n_t - i) * hb, last_h), xoff + ci)),
        pl.BlockSpec((CONV_W, cw), lambda ci, i: (0, ci)),
        pl.BlockSpec((1, cw), lambda ci, i: (0, ci)),
        pl.BlockSpec((2, cw // LRU_BLOCK_W, LRU_BLOCK_W, LRU_BLOCK_W), lambda ci, i: (0, ci, 0, 0)),
        pl.BlockSpec((2, cw // LRU_BLOCK_W, LRU_BLOCK_W, LRU_BLOCK_W), lambda ci, i: (0, ci, 0, 0)),
        pl.BlockSpec((2, cw), lambda ci, i: (0, ci)),
        pl.BlockSpec((2, cw), lambda ci, i: (0, ci)),
        pl.BlockSpec((2, cw), lambda ci, i: (0, ci)),
    ]
    kern = functools.partial(_rglru_kernel, tc=tc, cw=cw, n_t=n_t)
    return pl.pallas_call(
        kern,
        out_shape=(jax.ShapeDtypeStruct((s, c), F32), jax.ShapeDtypeStruct((s, c), F32)),
        grid=(n_c, n_t),
        in_specs=specs,
        out_specs=(pl.BlockSpec((tc, cw), lambda ci, i: (i, ci)),
                   pl.BlockSpec((tc, cw), lambda ci, i: (n_t - 1 - i, ci))),
        scratch_shapes=[pltpu.VMEM((tc + 2 * SUBLANES, cw), F32)]
        + [pltpu.VMEM((tc, cw), F32) for _ in range(4)]
        + [pltpu.VMEM((1, cw), F32) for _ in range(2)],
        compiler_params=_params("parallel", "arbitrary"),
        name="rglru_scan",
    )(proj, proj, proj, proj, proj, proj, conv_w, conv_b.reshape(1, c), w_a, w_x, b_a, b_x, lam)


def _pro_rglru_out(rows, vecs):
    yf, yb, gate = rows
    return (yf + yb) * _gelu_tanh(gate)


def _diff_attn_kernel(q_ref, k_ref, v_ref, lq1_ref, lk1_ref, lq2_ref, lk2_ref, g_ref, o_ref,
                      s_buf, p_buf, m_s, a_s, l_s, acc_s, *, tq, tk, rc, n_kv, lambda_init):
    m_s[...] = jnp.full_like(m_s, -jnp.inf)
    l_s[...] = jnp.zeros_like(l_s)
    acc_s[...] = jnp.zeros_like(acc_s)
    q = q_ref[...]
    qs = (q[:, :HEAD_DIM], q[:, HEAD_DIM:])

    def scores(j, slot):
        off = pl.multiple_of(j * tk, tk)
        kb = k_ref[pl.ds(off, tk), :]
        for c in range(2):
            kc = kb[:, c * HEAD_DIM:(c + 1) * HEAD_DIM]
            s_buf[slot, c * tq:(c + 1) * tq, :] = lax.dot_general(
                qs[c], kc, (((1,), (1,)), ((), ())), preferred_element_type=F32)

    def update(j, slot):
        off = pl.multiple_of(j * tk, tk)
        vb = v_ref[pl.ds(off, tk), :]
        for r0 in range(0, 2 * tq, rc):
            rows = slice(r0, r0 + rc)
            sc = s_buf[slot, rows, :]
            m_old = m_s[rows, :]
            m_new = jnp.maximum(m_old, jnp.max(sc, axis=-1, keepdims=True))
            alpha = jnp.exp2(m_old - m_new)
            p = jnp.exp2(sc - m_new)
            l_s[rows, :] = alpha * l_s[rows, :] + sum(
                p[:, t * LANES:(t + 1) * LANES] for t in range(tk // LANES))
            p_buf[rows, :] = p.astype(BF16)
            m_s[rows, :] = m_new
            a_s[rows, :] = alpha
        acc_s[...] = a_s[...] * acc_s[...] + jnp.dot(p_buf[...], vb, preferred_element_type=F32)

    scores(0, 0)

    def kv_pair(jj, _):
        j = 2 * jj
        scores(j + 1, 1)
        update(j, 0)
        scores(jnp.minimum(j + 2, n_kv - 1), 0)
        update(j + 1, 1)
        return 0

    lax.fori_loop(0, n_kv // 2, kv_pair, 0)
    lam = (jnp.exp(jnp.sum(lq1_ref[...] * lk1_ref[...], axis=-1, keepdims=True))
           - jnp.exp(jnp.sum(lq2_ref[...] * lk2_ref[...], axis=-1, keepdims=True)) + lambda_init)
    l = jnp.sum(l_s[...], axis=-1, keepdims=True)
    o = acc_s[0:tq] / l[0:tq] - lam * (acc_s[tq:2 * tq] / l[tq:2 * tq])
    o_ref[...] = (_rms(o, g_ref[...]) * (1.0 - lambda_init)).astype(o_ref.dtype)


def _diff_attention(qkv, lq1, lk1, lq2, lk2, subln_g, lambda_init, *, tq=512, tk=1024, rc=32):
    s = qkv.shape[0]
    vd = 2 * HEAD_DIM
    tq = min(tq, s)
    tk = min(tk, s // 2)
    assert (s // tk) % 2 == 0
    kern = functools.partial(_diff_attn_kernel, tq=tq, tk=tk, rc=rc, n_kv=s // tk,
                             lambda_init=lambda_init)
    vec = lambda a: a.reshape(1, -1)
    vspec = lambda w: pl.BlockSpec((1, w), lambda h, i: (0, 0))
    return pl.pallas_call(
        kern,
        out_shape=jax.ShapeDtypeStruct((s, DIFF_HEADS * vd), BF16),
        grid=(DIFF_HEADS, s // tq),
        in_specs=[
            pl.BlockSpec((tq, vd), lambda h, i: (i, h)),
            pl.BlockSpec((s, vd), lambda h, i: (0, DIFF_HEADS + h)),
            pl.BlockSpec((s, vd), lambda h, i: (0, 2 * DIFF_HEADS + h)),
            vspec(HEAD_DIM), vspec(HEAD_DIM), vspec(HEAD_DIM), vspec(HEAD_DIM), vspec(vd),
        ],
        out_specs=pl.BlockSpec((tq, vd), lambda h, i: (i, h)),
        scratch_shapes=[pltpu.VMEM((2, 2 * tq, tk), F32), pltpu.VMEM((2 * tq, tk), BF16),
                        pltpu.VMEM((2 * tq, 1), F32), pltpu.VMEM((2 * tq, 1), F32),
                        pltpu.VMEM((2 * tq, LANES), F32), pltpu.VMEM((2 * tq, vd), F32)],
        compiler_params=_params("parallel", "arbitrary"),
        name="diff_attention",
    )(qkv, qkv, qkv, vec(lq1), vec(lk1), vec(lq2), vec(lk2), vec(subln_g))


def _pro_identity(rows, vecs):
    return rows[0]


def _win_attn_kernel(sink_ref, q_ref, k_ref, v_ref, o_ref, *, tq, win, s_len, group):
    kvh = pl.program_id(0)
    i = pl.program_id(1)
    start = jnp.clip(i * tq - WINDOW, 0, s_len - win)
    start = pl.multiple_of(start, WINDOW)
    kw = k_ref[pl.ds(start, win), :]
    vw = v_ref[pl.ds(start, win), :]
    qpos = i * tq + lax.broadcasted_iota(jnp.int32, (tq, win), 0)
    kpos = start + lax.broadcasted_iota(jnp.int32, (tq, win), 1)
    valid = jnp.abs(kpos - qpos) <= WINDOW
    for g in range(group):
        qg = q_ref[:, g * HEAD_DIM:(g + 1) * HEAD_DIM]
        sc = lax.dot_general(qg, kw, (((1,), (1,)), ((), ())), preferred_element_type=F32)
        sc = jnp.where(valid, sc, NEG_INF)
        sink = sink_ref[kvh * group + g] * math.log2(math.e)
        m = jnp.maximum(jnp.max(sc, axis=-1, keepdims=True), sink)
        e = jnp.exp2(sc - m)
        den = jnp.sum(e, axis=-1, keepdims=True) + jnp.exp2(sink - m)
        o = jnp.dot(e.astype(BF16), vw, preferred_element_type=F32) / den
        o_ref[:, g * HEAD_DIM:(g + 1) * HEAD_DIM] = o.astype(o_ref.dtype)


def _window_attention(qkv, sink, *, tq=256):
    s = qkv.shape[0]
    group = WIN_Q_HEADS // WIN_KV_HEADS
    tq = min(tq, s)
    win = min(tq + 2 * WINDOW, s)
    qw = group * HEAD_DIM
    k0 = WIN_Q_HEADS
    v0 = WIN_Q_HEADS + WIN_KV_HEADS
    kern = functools.partial(_win_attn_kernel, tq=tq, win=win, s_len=s, group=group)
    return pl.pallas_call(
        kern,
        out_shape=jax.ShapeDtypeStruct((s, WIN_Q_HEADS * HEAD_DIM), BF16),
        grid=(WIN_KV_HEADS, s // tq),
        in_specs=[
            pl.BlockSpec(memory_space=pltpu.SMEM),
            pl.BlockSpec((tq, qw), lambda h, i: (i, h)),
            pl.BlockSpec((s, HEAD_DIM), lambda h, i: (0, k0 + h)),
            pl.BlockSpec((s, HEAD_DIM), lambda h, i: (0, v0 + h)),
        ],
        out_specs=pl.BlockSpec((tq, qw), lambda h, i: (i, h)),
        compiler_params=_params("parallel", "arbitrary"),
        name="window_attention",
    )(sink.astype(F32), qkv, qkv, qkv)


def _s5_discretize(a_re, a_im, log_dt, b_re, b_im):
    dt = jnp.exp(log_dt)[:, None]
    mag = jnp.exp(dt * a_re)
    lr, li = mag * jnp.cos(dt * a_im), mag * jnp.sin(dt * a_im)
    den = a_re * a_re + a_im * a_im
    nr, ni = lr - 1.0, li
    fr = (nr * a_re + ni * a_im) / den
    fi = (ni * a_re - nr * a_im) / den
    bbr = fr[..., None] * b_re - fi[..., None] * b_im
    bbi = fr[..., None] * b_im + fi[..., None] * b_re
    return lr, li, bbr, bbi


def _s5_block_weights(lr, li, bbr, bbi, c_re, c_im, gpc):
    g, n, c = bbr.shape
    n_k = g // gpc
    eye = jnp.eye(gpc, dtype=F32)

    def w_in(bb):
        t = bb.reshape(n_k, gpc, n, c)
        return jnp.einsum('kgnc,gh->kgchn', t, eye).reshape(n_k, gpc * c, gpc * n)

    def w_out(cc):
        t = cc.reshape(n_k, gpc, c, n)
        return jnp.einsum('kgcn,gh->khngc', t, eye).reshape(n_k, gpc * n, gpc * c)

    win = jnp.concatenate([w_in(bbr), w_in(bbi)], axis=2)
    wout = jnp.concatenate([w_out(c_re), -w_out(c_im)], axis=1)
    lam = jnp.stack([lr.reshape(n_k, gpc * n), li.reshape(n_k, gpc * n)], axis=1)
    return win, wout, lam


def _s5_kernel(uf_ref, ub_ref, win_ref, wout_ref, lam_ref, yf_ref, yb_ref,
               xf_s, xb_s, st_s, *, tc, ns, cw, nsub):
    i = pl.program_id(1)

    @pl.when(i == 0)
    def _():
        st_s[...] = jnp.zeros_like(st_s)

    for c in range(nsub):
        cols = slice(c * cw, (c + 1) * cw)
        xf_s[c] = jnp.dot(uf_ref[:, cols].astype(BF16), win_ref[0, c].astype(BF16),
                          preferred_element_type=F32)
        xb_s[c] = jnp.dot(ub_ref[:, cols].astype(BF16), win_ref[1, c].astype(BF16),
                          preferred_element_type=F32)
    lam = [[(lam_ref[d, c, 0:1, :], lam_ref[d, c, 1:2, :]) for c in range(nsub)]
           for d in range(2)]

    def step(x_s, c, row, lr, li, sr, si):
        nr = lr * sr - li * si + x_s[c, pl.ds(row, 1), 0:ns]
        ni = lr * si + li * sr + x_s[c, pl.ds(row, 1), ns:2 * ns]
        x_s[c, pl.ds(row, 1), 0:ns] = nr
        x_s[c, pl.ds(row, 1), ns:2 * ns] = ni
        return nr, ni

    def body(r, carry):
        out = []
        for c in range(nsub):
            srf, sif, srb, sib = carry[4 * c:4 * c + 4]
            srf, sif = step(xf_s, c, r, *lam[0][c], srf, sif)
            srb, sib = step(xb_s, c, tc - 1 - r, *lam[1][c], srb, sib)
            out += [srf, sif, srb, sib]
        return tuple(out)

    init = tuple(st_s[j:j + 1, :] for j in range(4 * nsub))
    final = lax.fori_loop(0, tc, body, init, unroll=8)
    for j in range(4 * nsub):
        st_s[j:j + 1, :] = final[j]
    for c in range(nsub):
        cols = slice(c * cw, (c + 1) * cw)
        yf_ref[:, cols] = jnp.dot(xf_s[c].astype(BF16), wout_ref[0, c].astype(BF16),
                                  preferred_element_type=F32)
        yb_ref[:, cols] = jnp.dot(xb_s[c].astype(BF16), wout_ref[1, c].astype(BF16),
                                  preferred_element_type=F32)


def _s5_scan(u, win, wout, lam, *, tc=256, nsub=2):
    s, w = u.shape
    _, n_k, cw, ns2 = win.shape
    ns = ns2 // 2
    tc = min(tc, s)
    n_t = s // tc
    nsub = min(nsub, n_k)
    bw = nsub * cw
    kern = functools.partial(_s5_kernel, tc=tc, ns=ns, cw=cw, nsub=nsub)
    return pl.pallas_call(
        kern,
        out_shape=(jax.ShapeDtypeStruct((s, w), F32), jax.ShapeDtypeStruct((s, w), F32)),
        grid=(n_k // nsub, n_t),
        in_specs=[
            pl.BlockSpec((tc, bw), lambda k, i: (i, k)),
            pl.BlockSpec((tc, bw), lambda k, i: (n_t - 1 - i, k)),
            pl.BlockSpec((2, nsub, cw, ns2), lambda k, i: (0, k, 0, 0)),
            pl.BlockSpec((2, nsub, ns2, cw), lambda k, i: (0, k, 0, 0)),
            pl.BlockSpec((2, nsub, 2, ns), lambda k, i: (0, k, 0, 0)),
        ],
        out_specs=(pl.BlockSpec((tc, bw), lambda k, i: (i, k)),
                   pl.BlockSpec((tc, bw), lambda k, i: (n_t - 1 - i, k))),
        scratch_shapes=[pltpu.VMEM((nsub, tc, ns2), F32), pltpu.VMEM((nsub, tc, ns2), F32),
                        pltpu.VMEM((4 * nsub, ns), F32)],
        compiler_params=_params("parallel", "arbitrary"),
        name="s5_scan",
    )(u, u, win, wout, lam)


def _pro_s5_glu(rows, vecs):
    yf, yb, u = rows
    return _gelu_tanh(yf + yb + vecs[0] * u)


def _epi_glu(acc, epi_refs, o_ref, j, a32_s):
    o_ref[...] = (a32_s[...] * _sigmoid(acc)).astype(o_ref.dtype)


def _route(x, g, w_r):
    h = _rms(x, g)
    logits = jnp.dot(h.astype(BF16), w_r, preferred_element_type=F32)
    tm = logits.shape[0]
    lane = lax.broadcasted_iota(jnp.int32, (tm, LANES), 1)
    big = jnp.int32(LANES)
    ninf = -jnp.inf
    gl = jnp.where(lane < MOE_GROUPS, logits, ninf)
    gm = jnp.max(gl, axis=-1, keepdims=True)
    ge = jnp.exp(gl - gm)
    g_prob = ge / jnp.sum(ge, axis=-1, keepdims=True)
    g_p = jnp.max(g_prob, axis=-1, keepdims=True)
    g_idx = jnp.min(jnp.where(g_prob == g_p, lane, big), axis=-1, keepdims=True)
    lo = MOE_GROUPS + g_idx * EXPERTS_PER_GROUP
    in_grp = (lane >= lo) & (lane < lo + EXPERTS_PER_GROUP)
    el = jnp.where(in_grp, logits, ninf)
    em = jnp.max(el, axis=-1, keepdims=True)
    ee = jnp.exp(el - em)
    e_prob = jnp.where(in_grp, ee / jnp.sum(ee, axis=-1, keepdims=True), -1.0)
    p1 = jnp.max(e_prob, axis=-1, keepdims=True)
    i1 = jnp.min(jnp.where(e_prob == p1, lane, big), axis=-1, keepdims=True)
    rest = jnp.where(lane == i1, -1.0, e_prob)
    p2 = jnp.max(rest, axis=-1, keepdims=True)
    i2 = jnp.min(jnp.where(rest == p2, lane, big), axis=-1, keepdims=True)
    denom = p1 + p2
    w1 = g_p * (p1 / denom)
    w2 = g_p * (p2 / denom)
    idx = jnp.where(lane == 0, i1 - MOE_GROUPS, jnp.where(lane == 1, i2 - MOE_GROUPS, 0))
    wt = jnp.where(lane == 0, w1, jnp.where(lane == 1, w2, 0.0))
    return h, idx, wt


def _router_kernel(x_ref, g_ref, w_ref, h_ref, idx_ref, wt_ref):
    h, idx, wt = _route(x_ref[...], g_ref[...], w_ref[...])
    h_ref[...] = h
    idx_ref[...] = idx
    wt_ref[...] = wt


def _router(x, g, w_group, w_expert, *, tm=512):
    s, d = x.shape
    tm = min(tm, s)
    w_r = jnp.concatenate(
        [w_group, w_expert, jnp.zeros((d, LANES - MOE_GROUPS - N_EXPERTS), F32)], axis=1)
    return pl.pallas_call(
        _router_kernel,
        out_shape=(jax.ShapeDtypeStruct((s, d), F32), jax.ShapeDtypeStruct((s, LANES), jnp.int32),
                   jax.ShapeDtypeStruct((s, LANES), F32)),
        grid=(s // tm,),
        in_specs=[pl.BlockSpec((tm, d), lambda i: (i, 0)), pl.BlockSpec((1, d), lambda i: (0, 0)),
                  pl.BlockSpec((d, LANES), lambda i: (0, 0))],
        out_specs=(pl.BlockSpec((tm, d), lambda i: (i, 0)), pl.BlockSpec((tm, LANES), lambda i: (i, 0)),
                   pl.BlockSpec((tm, LANES), lambda i: (i, 0))),
        compiler_params=_params("parallel"),
        name="moe_router",
    )(x, g.reshape(1, d), w_r.astype(BF16))


def _rank_kernel(idx_ref, dest_ref, cnt_ref, tot_s, pst_s, run_s, *, tm):
    ph = pl.program_id(0)
    i = pl.program_id(1)
    lane = lax.broadcasted_iota(jnp.int32, (tm, LANES), 1)
    idx = idx_ref[...]
    oh0 = (lane == idx[:, 0:1]).astype(F32)
    oh1 = (lane == idx[:, 1:2]).astype(F32)
    c = oh0 + oh1
    csum = jnp.sum(c, axis=0, keepdims=True)

    @pl.when(jnp.logical_and(ph == 0, i == 0))
    def _():
        tot_s[...] = jnp.zeros_like(tot_s)

    @pl.when(ph == 0)
    def _():
        tot_s[...] += csum

    @pl.when(jnp.logical_and(ph == 1, i == 0))
    def _():
        counts = tot_s[...]
        nblk = jnp.floor((counts + (EXPERT_BLOCK - 1)) * (1.0 / EXPERT_BLOCK))
        r = lax.broadcasted_iota(jnp.int32, (LANES, LANES), 0)
        cc = lax.broadcasted_iota(jnp.int32, (LANES, LANES), 1)
        upper = (r < cc).astype(F32)
        excl = jnp.dot(jnp.broadcast_to(nblk, (SUBLANES, LANES)), upper,
                       preferred_element_type=F32, precision=lax.Precision.HIGHEST)
        pst_s[...] = excl[0:1, :] * EXPERT_BLOCK
        run_s[...] = jnp.zeros_like(run_s)
        cnt_ref[...] = jnp.broadcast_to(counts, cnt_ref.shape)

    @pl.when(ph == 1)
    def _():
        rr = lax.broadcasted_iota(jnp.int32, (tm, tm), 0)
        cr = lax.broadcasted_iota(jnp.int32, (tm, tm), 1)
        lower = (rr > cr).astype(BF16)
        before = jnp.dot(lower, c.astype(BF16), preferred_element_type=F32) + run_s[...]
        base = pst_s[...] + before
        d0 = jnp.sum(oh0 * base, axis=-1, keepdims=True)
        d1 = jnp.sum(oh1 * (base + oh0), axis=-1, keepdims=True)
        dest_ref[...] = jnp.where(lane == 0, d0, jnp.where(lane == 1, d1, 0.0)).astype(jnp.int32)
        run_s[...] += csum


def _dispatch_rank(idx, *, tm=512):
    s = idx.shape[0]
    tm = min(tm, s)
    kern = functools.partial(_rank_kernel, tm=tm)
    return pl.pallas_call(
        kern,
        out_shape=(jax.ShapeDtypeStruct((s, LANES), jnp.int32),
                   jax.ShapeDtypeStruct((SUBLANES, LANES), F32)),
        grid=(2, s // tm),
        in_specs=[pl.BlockSpec((tm, LANES), lambda ph, i: (i, 0))],
        out_specs=(pl.BlockSpec((tm, LANES), lambda ph, i: (i * ph, 0)),
                   pl.BlockSpec((SUBLANES, LANES), lambda ph, i: (0, 0))),
        scratch_shapes=[pltpu.VMEM((1, LANES), F32) for _ in range(3)],
        compiler_params=_params("arbitrary", "arbitrary"),
        name="moe_rank",
    )(idx)


def _block_experts(counts, n_blk):
    cnt = counts[0, :N_EXPERTS].astype(jnp.int32)
    padded = ((cnt + EXPERT_BLOCK - 1) // EXPERT_BLOCK) * EXPERT_BLOCK
    pends = jnp.cumsum(padded)
    blk_start = jnp.arange(n_blk, dtype=jnp.int32) * EXPERT_BLOCK
    owner = jnp.sum((pends[None, :] <= blk_start[:, None]).astype(jnp.int32), axis=1)
    ids = jnp.arange(N_EXPERTS, dtype=jnp.int32)
    last_e = jnp.max(jnp.where(cnt > 0, ids, 0))
    n_used = (pends[-1] // EXPERT_BLOCK).reshape(1)
    blk_e = jnp.minimum(owner, last_e).astype(jnp.int32)
    later = jnp.logical_and(ids[None, :] > ids[:, None], cnt[None, :] > 0)
    nxt = jnp.min(jnp.where(later, ids[None, :], N_EXPERTS), axis=1)
    nxt = jnp.where(nxt == N_EXPERTS, ids, nxt).astype(jnp.int32)
    return blk_e, nxt[blk_e], n_used.astype(jnp.int32)


def _scatter_kernel(dest_ref, h_ref, xs_in_hbm, xs_hbm, h_s, sem, *, tb, n_b):
    del xs_in_hbm
    b = pl.program_id(0)
    slot = b % 2
    n = tb * TOP_K

    def copy(r, row, sl):
        return pltpu.make_async_copy(h_s.at[sl, pl.ds(r, 1), :], xs_hbm.at[pl.ds(row, 1), :],
                                     sem.at[sl])

    def wait_all(sl):
        def body(r, _):
            copy(0, 0, sl).wait()
            return 0
        lax.fori_loop(0, n, body, 0, unroll=8)

    h_s[slot] = h_ref[...]
    for r in range(tb):
        tok = b * tb + r
        for k in range(TOP_K):
            copy(r, dest_ref[tok * TOP_K + k], slot).start()

    @pl.when(b > 0)
    def _():
        wait_all(1 - slot)

    @pl.when(b == n_b - 1)
    def _():
        wait_all(slot)


def _dispatch_scatter(h, dest_flat, n_rows, *, tb=128):
    s, d = h.shape
    tb = min(tb, s)
    n_b = s // tb
    kern = functools.partial(_scatter_kernel, tb=tb, n_b=n_b)
    grid_spec = pltpu.PrefetchScalarGridSpec(
        num_scalar_prefetch=1,
        grid=(n_b,),
        in_specs=[pl.BlockSpec((tb, d), lambda b, dr: (b, 0)), pl.BlockSpec(memory_space=pl.ANY)],
        out_specs=pl.BlockSpec(memory_space=pl.ANY),
        scratch_shapes=[pltpu.VMEM((2, tb, d), h.dtype), pltpu.SemaphoreType.DMA((2,))],
    )
    return pl.pallas_call(
        kern,
        out_shape=jax.ShapeDtypeStruct((n_rows, d), h.dtype),
        grid_spec=grid_spec,
        input_output_aliases={2: 0},
        compiler_params=_params("arbitrary"),
        name="moe_scatter",
    )(dest_flat, h, jnp.zeros((n_rows, d), h.dtype))


def _row_copy(src_hbm, src_row, dst_buf, slot, dst_row, sem):
    return pltpu.make_async_copy(src_hbm.at[pl.ds(src_row, 1), :],
                                 dst_buf.at[slot, pl.ds(dst_row, 1), :], sem.at[slot])


def _gather_start(idx_ref, base, n, src_hbm, dst_buf, slot, sem):
    for r in range(n):
        _row_copy(src_hbm, idx_ref[base + r], dst_buf, slot, r, sem).start()


def _gather_wait(n, src_hbm, dst_buf, slot, sem):
    def body(r, _):
        _row_copy(src_hbm, 0, dst_buf, slot, r, sem).wait()
        return 0
    lax.fori_loop(0, n, body, 0, unroll=8)


def _expert_kernel(be_ref, nx_ref, nu_ref, x_ref, wg_hbm, wu_hbm, wd_hbm, y_ref,
                   wg_f, wu_f, wd_f, wg_s, wu_s, wd_s, sem, *, layer):
    b = pl.program_id(0)
    e = be_ref[b]

    def fetch(expert):
        return (pltpu.make_async_copy(wg_hbm.at[layer, expert], wg_f, sem.at[0]),
                pltpu.make_async_copy(wu_hbm.at[layer, expert], wu_f, sem.at[1]),
                pltpu.make_async_copy(wd_hbm.at[layer, expert], wd_f, sem.at[2]))

    @pl.when(b == 0)
    def _():
        for c in fetch(e):
            c.start()

    changed = jnp.logical_or(b == 0, e != be_ref[jnp.maximum(b - 1, 0)])

    @pl.when(changed)
    def _():
        for c in fetch(e):
            c.wait()
        wg_s[...] = wg_f[...].astype(BF16)
        wu_s[...] = wu_f[...].astype(BF16)
        wd_s[...] = wd_f[...].astype(BF16)

        @pl.when(nx_ref[b] != e)
        def _():
            for c in fetch(nx_ref[b]):
                c.start()

    @pl.when(b >= nu_ref[0])
    def _():
        y_ref[...] = jnp.zeros_like(y_ref)

    @pl.when(b < nu_ref[0])
    def _():
        x = x_ref[...].astype(BF16)
        hg = jnp.dot(x, wg_s[...], preferred_element_type=F32)
        hu = jnp.dot(x, wu_s[...], preferred_element_type=F32)
        hdn = (hg * _sigmoid(hg)) * hu
        y_ref[...] = jnp.dot(hdn.astype(BF16), wd_s[...], preferred_element_type=F32)


def _expert_mlp(xs, blk_e, blk_next, n_used, layer, w_gate, w_up, w_down):
    n_rows, d = xs.shape
    de = w_gate.shape[3]
    grid_spec = pltpu.PrefetchScalarGridSpec(
        num_scalar_prefetch=3,
        grid=(n_rows // EXPERT_BLOCK,),
        in_specs=[
            pl.BlockSpec((EXPERT_BLOCK, d),
                         lambda b, be, nx, nu: (jnp.minimum(b, nu[0] - 1), 0)),
            pl.BlockSpec(memory_space=pl.ANY),
            pl.BlockSpec(memory_space=pl.ANY),
            pl.BlockSpec(memory_space=pl.ANY),
        ],
        out_specs=pl.BlockSpec((EXPERT_BLOCK, d), lambda b, be, nx, nu: (b, 0)),
        scratch_shapes=[pltpu.VMEM((d, de), F32), pltpu.VMEM((d, de), F32),
                        pltpu.VMEM((de, d), F32), pltpu.VMEM((d, de), BF16),
                        pltpu.VMEM((d, de), BF16), pltpu.VMEM((de, d), BF16),
                        pltpu.SemaphoreType.DMA((3,))],
    )
    return pl.pallas_call(
        functools.partial(_expert_kernel, layer=layer),
        out_shape=jax.ShapeDtypeStruct((n_rows, d), F32),
        grid_spec=grid_spec,
        compiler_params=_params("arbitrary"),
        name="moe_experts",
    )(blk_e, blk_next, n_used, xs, w_gate, w_up, w_down)


def _combine_kernel(dest_ref, x_ref, wt_ref, ys_hbm, g_ref, o_ref, ybuf, sem, *, tb, n_b, final):
    b = pl.program_id(0)
    slot = b % 2
    n = tb * TOP_K

    @pl.when(b == 0)
    def _():
        _gather_start(dest_ref, 0, n, ys_hbm, ybuf, 0, sem)

    _gather_wait(n, ys_hbm, ybuf, slot, sem)
    _gather_start(dest_ref, (b + 1) * n, n, ys_hbm, ybuf, 1 - slot, sem)
    y0 = ybuf[slot, 0:tb, :]
    y1 = ybuf[slot, tb:2 * tb, :]
    wt = wt_ref[...]
    out = x_ref[...] + (y0 * wt[:, 0:1] + y1 * wt[:, 1:2])
    if final:
        out = _rms(out, g_ref[...])
    o_ref[...] = out

    @pl.when(b == n_b - 1)
    def _():
        _gather_wait(n, ys_hbm, ybuf, 1 - slot, sem)


def _moe_combine(x, wt, ys, dest, g_final, *, final, tb=128):
    s, d = x.shape
    tb = min(tb, s)
    n_b = s // tb
    dest = dest.reshape(n_b, tb, TOP_K).transpose(0, 2, 1).reshape(-1)
    dest = jnp.concatenate([dest, jnp.zeros((tb * TOP_K,), jnp.int32)])
    kern = functools.partial(_combine_kernel, tb=tb, n_b=n_b, final=final)
    grid_spec = pltpu.PrefetchScalarGridSpec(
        num_scalar_prefetch=1,
        grid=(n_b,),
        in_specs=[
            pl.BlockSpec((tb, d), lambda b, dr: (b, 0)),
            pl.BlockSpec((tb, LANES), lambda b, dr: (b, 0)),
            pl.BlockSpec(memory_space=pl.ANY),
            pl.BlockSpec((1, d), lambda b, dr: (0, 0)),
        ],
        out_specs=pl.BlockSpec((tb, d), lambda b, dr: (b, 0)),
        scratch_shapes=[pltpu.VMEM((2, tb * TOP_K, d), F32), pltpu.SemaphoreType.DMA((2,))],
    )
    return pl.pallas_call(
        kern,
        out_shape=jax.ShapeDtypeStruct((s, d), F32),
        grid_spec=grid_spec,
        compiler_params=_params("arbitrary"),
        name="moe_combine",
    )(dest, x, wt, ys, g_final.reshape(1, d))


def _hier_moe(x, ln_g, w_group, w_expert, layer, w_gate, w_up, w_down, g_final, *, final):
    s = x.shape[0]
    h, idx, wt = _router(x, ln_g, w_group, w_expert)
    dest, counts = _dispatch_rank(idx)
    dest = dest[:, :TOP_K].reshape(-1)
    n_rows = -(-(s * TOP_K + N_EXPERTS * (EXPERT_BLOCK - 1)) // EXPERT_BLOCK) * EXPERT_BLOCK
    blk_e, blk_next, n_used = _block_experts(counts, n_rows // EXPERT_BLOCK)
    xs = _dispatch_scatter(h, dest, n_rows)
    ys = _expert_mlp(xs, blk_e, blk_next, n_used, layer, w_gate, w_up, w_down)
    return _moe_combine(x, wt, ys, dest, g_final, final=final)


def _rglru_layer(x, ln_g, w_in, conv_w, conv_b, w_a, w_x, b_a, b_x, lam, w_out):
    d = x.shape[1]
    c = conv_w.shape[1]
    proj = _norm_matmul(x, ln_g, w_in, name="lru_in_proj")
    yf, yb = _rglru_scan(proj, conv_w, conv_b, w_a, w_x, b_a, b_x, lam)
    return _matmul_residual([(yf, c, 0), (yb, c, 0), (proj, c, 0)], [], w_out, x, _pro_rglru_out,
                            k_dim=c, tm=512, name="lru_out_proj")


def _diff_layer(x, ln_g, rope, w_qkv, lq1, lk1, lq2, lk2, subln_g, w_out, lambda_init):
    qk_cols = 2 * DIFF_HEADS * 2 * HEAD_DIM
    qkv = _norm_matmul(x, ln_g, w_qkv, rope=rope, n_rope_cols=qk_cols, n_q_cols=qk_cols // 2,
                       q_scale=HEAD_DIM ** -0.5 * math.log2(math.e), out_dtype=BF16,
                       name="diff_qkv_proj")
    o = _diff_attention(qkv, lq1, lk1, lq2, lk2, subln_g, lambda_init)
    return _matmul_residual([(o, o.shape[1], 0)], [], w_out, x, _pro_identity,
                            k_dim=o.shape[1], name="diff_out_proj")


def _window_layer(x, ln_g, rope, w_qkv, sink, w_out):
    qk_cols = (WIN_Q_HEADS + WIN_KV_HEADS) * HEAD_DIM
    qkv = _norm_matmul(x, ln_g, w_qkv, rope=rope, n_rope_cols=qk_cols,
                       n_q_cols=WIN_Q_HEADS * HEAD_DIM,
                       q_scale=HEAD_DIM ** -0.5 * math.log2(math.e), out_dtype=BF16,
                       name="win_qkv_proj")
    o = _window_attention(qkv, sink)
    return _matmul_residual([(o, o.shape[1], 0)], [], w_out, x, _pro_identity,
                            k_dim=o.shape[1], name="win_out_proj")


def _s5_layer(x, ln_g, w_in, a_re, a_im, log_dt, b_re, b_im, c_re, c_im, d_skip, w_glu, w_out,
              *, gpc=16):
    u = _norm_matmul(x, ln_g, w_in, name="s5_in_proj")
    w = u.shape[1]
    wins, wouts, lams = [], [], []
    for dd in range(2):
        lr, li, bbr, bbi = _s5_discretize(a_re[dd], a_im[dd], log_dt[dd], b_re[dd], b_im[dd])
        wi, wo, lm = _s5_block_weights(lr, li, bbr, bbi, c_re[dd], c_im[dd], gpc)
        wins.append(wi)
        wouts.append(wo)
        lams.append(lm)
    yf, yb = _s5_scan(u, jnp.stack(wins).astype(BF16), jnp.stack(wouts).astype(BF16),
                      jnp.stack(lams))
    z = _fused_mm([(yf, w, 0), (yb, w, 0), (u, w, 0)], [d_skip.reshape(1, w)], w_glu, [],
                  _pro_s5_glu, _epi_glu, out_dtype=BF16, tm=512, tn=w, k_dim=w, keep_f32=True,
                  name="s5_glu")
    return _matmul_residual([(z, w, 0)], [], w_out, x, _pro_identity, k_dim=w, name="s5_out_proj")


def kernel(x, positions, ln_mix, ln_ffn, ln_final, lru_w_in, lru_conv_w, lru_conv_b, lru_w_a, lru_w_x, lru_b_a, lru_b_x, lru_lambda, lru_w_out, diff_w_qkv, diff_lq1, diff_lk1, diff_lq2, diff_lk2, diff_subln, diff_w_out, win_w_qkv, win_sink, win_w_out, s5_w_in, s5_a_re, s5_a_im, s5_log_dt, s5_b_re, s5_b_im, s5_c_re, s5_c_im, s5_d, s5_w_glu, s5_w_out, moe_w_group, moe_w_expert, moe_w_gate, moe_w_up, moe_w_down):
    batch, s, d = x.shape
    depth = ln_mix.shape[0]
    outs = []
    for b in range(batch):
        xb = x[b]
        rope = _rope_tables(positions[b])
        for i in range(depth):
            kind, j = i % 4, i // 4
            if kind == 0:
                xb = _rglru_layer(xb, ln_mix[i], lru_w_in[j], lru_conv_w[j], lru_conv_b[j],
                                  lru_w_a[j], lru_w_x[j], lru_b_a[j], lru_b_x[j], lru_lambda[j],
                                  lru_w_out[j])
            elif kind == 1:
                xb = _diff_layer(xb, ln_mix[i], rope, diff_w_qkv[j], diff_lq1[j], diff_lk1[j],
                                 diff_lq2[j], diff_lk2[j], diff_subln[j], diff_w_out[j],
                                 0.8 - 0.6 * math.exp(-0.3 * i))
            elif kind == 2:
                xb = _window_layer(xb, ln_mix[i], rope, win_w_qkv[j], win_sink[j], win_w_out[j])
            else:
                xb = _s5_layer(xb, ln_mix[i], s5_w_in[j], s5_a_re[j], s5_a_im[j], s5_log_dt[j],
                               s5_b_re[j], s5_b_im[j], s5_c_re[j], s5_c_im[j], s5_d[j],
                               s5_w_glu[j], s5_w_out[j])
            xb = _hier_moe(xb, ln_ffn[i], moe_w_group[i], moe_w_expert[i], i, moe_w_gate,
                           moe_w_up, moe_w_down, ln_final, final=(i == depth - 1))
        outs.append(xb)
    return jnp.stack(outs)
```

```python
import functools
import math

import jax
import jax.numpy as jnp
from jax import lax
from jax.experimental import pallas as pl
from jax.experimental.pallas import tpu as pltpu

F32 = jnp.float32
BF16 = jnp.bfloat16

NORM_EPS = 1e-6
NEG_INF = -1e30
LANES = 128
SUBLANES = 8
VMEM_LIMIT = 56 * 1024 * 1024

HEAD_DIM = 128
ROT_DIM = HEAD_DIM // 4
ROPE_THETA = 500000.0
RGLRU_C = 8.0
CONV_W = 4
LRU_BLOCK_W = 128
WINDOW = 128
DIFF_HEADS = 8
WIN_Q_HEADS = 16
WIN_KV_HEADS = 4
SSM_GROUP_CH = 16
SSM_STATE = 64
MOE_GROUPS = 4
EXPERTS_PER_GROUP = 8
N_EXPERTS = MOE_GROUPS * EXPERTS_PER_GROUP
TOP_K = 2
EXPERT_BLOCK = 256


def _params(*sem):
    return pltpu.CompilerParams(dimension_semantics=sem, vmem_limit_bytes=VMEM_LIMIT)


def _rms(x, g):
    ms = jnp.mean(x * x, axis=-1, keepdims=True)
    return x * lax.rsqrt(ms + NORM_EPS) * g


def _gelu_tanh(x):
    return 0.5 * x * (1.0 + jnp.tanh(math.sqrt(2.0 / math.pi) * (x + 0.044715 * (x * x * x))))


def _sigmoid(x):
    return 1.0 / (1.0 + jnp.exp(-x))


def _fused_mm_kernel(*refs, n_row, n_vec, n_epi, prologue, epilogue, keep_f32):
    row_refs = refs[:n_row]
    vec_refs = refs[n_row:n_row + n_vec]
    w_ref = refs[n_row + n_vec]
    epi_refs = refs[n_row + n_vec + 1:n_row + n_vec + 1 + n_epi]
    o_ref = refs[n_row + n_vec + 1 + n_epi]
    a_s = refs[n_row + n_vec + 2 + n_epi]
    a32_s = refs[n_row + n_vec + 3 + n_epi] if keep_f32 else None
    j = pl.program_id(1)

    @pl.when(j == 0)
    def _():
        a = prologue([r[...] for r in row_refs], [v[...] for v in vec_refs])
        a_s[...] = a.astype(BF16)
        if keep_f32:
            a32_s[...] = a

    acc = jnp.dot(a_s[...], w_ref[...].astype(BF16), preferred_element_type=F32)
    epilogue(acc, epi_refs, o_ref, j, a32_s)


def _fused_mm(row_inputs, vec_inputs, w, epi_inputs, prologue, epilogue, *, out_dtype, tm, tn,
              k_dim, keep_f32=False, name):
    s = row_inputs[0][0].shape[0]
    n = w.shape[1]
    tm = min(tm, s)
    tn = min(tn, n)
    w = w.astype(BF16)
    in_specs = []
    args = []
    for arr, width, cb in row_inputs:
        in_specs.append(pl.BlockSpec((tm, width), lambda i, j, cb=cb: (i, cb)))
        args.append(arr)
    for arr in vec_inputs:
        in_specs.append(pl.BlockSpec(arr.shape, lambda i, j: (0, 0)))
        args.append(arr)
    in_specs.append(pl.BlockSpec((k_dim, tn), lambda i, j: (0, j)))
    args.append(w)
    for arr, width, per_tile in epi_inputs:
        if per_tile:
            in_specs.append(pl.BlockSpec((tm, width), lambda i, j: (i, j)))
        else:
            in_specs.append(pl.BlockSpec((tm, width), lambda i, j: (i, 0)))
        args.append(arr)
    scratch = [pltpu.VMEM((tm, k_dim), BF16)]
    if keep_f32:
        scratch.append(pltpu.VMEM((tm, k_dim), F32))
    kern = functools.partial(_fused_mm_kernel, n_row=len(row_inputs), n_vec=len(vec_inputs),
                             n_epi=len(epi_inputs), prologue=prologue, epilogue=epilogue,
                             keep_f32=keep_f32)
    return pl.pallas_call(
        kern,
        out_shape=jax.ShapeDtypeStruct((s, n), out_dtype),
        grid=(s // tm, n // tn),
        in_specs=in_specs,
        out_specs=pl.BlockSpec((tm, tn), lambda i, j: (i, j)),
        scratch_shapes=scratch,
        compiler_params=_params("parallel", "arbitrary"),
        name=name,
    )(*args)


def _pro_rms(rows, vecs):
    return _rms(rows[0], vecs[0])


def _epi_store(acc, epi_refs, o_ref, j, a32_s):
    o_ref[...] = acc.astype(o_ref.dtype)


def _epi_residual(acc, epi_refs, o_ref, j, a32_s):
    o_ref[...] = (epi_refs[0][...] + acc).astype(o_ref.dtype)


def _make_epi_rope(n_rope_tiles, n_q_tiles, q_scale, tn):
    def epi(acc, epi_refs, o_ref, j, a32_s):
        c_ref, s1_ref, s2_ref = epi_refs

        @pl.when(j < n_rope_tiles)
        def _():
            c = c_ref[...]
            s1 = s1_ref[...]
            s2 = s2_ref[...]
            sc = jnp.where(j < n_q_tiles, q_scale, 1.0).astype(F32)
            for hh in range(tn // HEAD_DIM):
                xs = acc[:, hh * HEAD_DIM:(hh + 1) * HEAD_DIM]
                rot = (xs * c + pltpu.roll(xs, HEAD_DIM - ROT_DIM // 2, 1) * s1
                       + pltpu.roll(xs, ROT_DIM // 2, 1) * s2) * sc
                o_ref[:, hh * HEAD_DIM:(hh + 1) * HEAD_DIM] = rot.astype(o_ref.dtype)

        @pl.when(j >= n_rope_tiles)
        def _():
            o_ref[...] = acc.astype(o_ref.dtype)

    return epi


def _rope_tables(positions):
    half = ROT_DIM // 2
    inv = ROPE_THETA ** (-jnp.arange(0, ROT_DIM, 2, dtype=F32) / ROT_DIM)
    ang = positions.astype(F32)[:, None] * inv
    cos, sin = jnp.cos(ang), jnp.sin(ang)
    s = positions.shape[0]
    ones = jnp.ones((s, HEAD_DIM - ROT_DIM), F32)
    zeros = jnp.zeros((s, HEAD_DIM - ROT_DIM), F32)
    zh = jnp.zeros((s, half), F32)
    c_tab = jnp.concatenate([cos, cos, ones], axis=1)
    s1_tab = jnp.concatenate([-sin, zh, zeros], axis=1)
    s2_tab = jnp.concatenate([zh, sin, zeros], axis=1)
    return c_tab, s1_tab, s2_tab


def _norm_matmul(x, g, w, *, rope=None, n_rope_cols=0, n_q_cols=0, q_scale=1.0, out_dtype=F32,
                 tm=1024, tn=1024, name):
    d = x.shape[1]
    if rope is None:
        epi, epi_inputs = _epi_store, []
    else:
        tn = min(tn, w.shape[1])
        assert n_rope_cols % tn == 0 and n_q_cols % tn == 0
        epi = _make_epi_rope(n_rope_cols // tn, n_q_cols // tn, q_scale, tn)
        epi_inputs = [(t, HEAD_DIM, False) for t in rope]
    return _fused_mm([(x, d, 0)], [g.reshape(1, d)], w, epi_inputs, _pro_rms, epi,
                     out_dtype=out_dtype, tm=tm, tn=tn, k_dim=d, name=name)


def _matmul_residual(row_inputs, vec_inputs, w, res, prologue, *, k_dim, tm=1024, tn=512, name):
    return _fused_mm(row_inputs, vec_inputs, w, [(res, min(tn, w.shape[1]), True)], prologue,
                     _epi_residual, out_dtype=F32, tm=tm, tn=tn, k_dim=k_dim, name=name)


def _rglru_kernel(xf_ref, xfp_ref, xfn_ref, xb_ref, xbp_ref, xbn_ref, cw_ref, cb_ref, wa_ref,
                  wx_ref, ba_ref, bx_ref, lam_ref, yf_ref, yb_ref,
                  ext_s, af_s, bf_s, ab_s, bb_s, hf_s, hb_s, *, tc, cw, n_t):
    i = pl.program_id(1)
    halo = SUBLANES

    @pl.when(i == 0)
    def _():
        hf_s[...] = jnp.zeros_like(hf_s)
        hb_s[...] = jnp.zeros_like(hb_s)

    def gates(x_ref, xp_ref, xn_ref, chunk, d, a_s, b_s):
        prev = jnp.where(chunk == 0, 0.0, xp_ref[...])
        nxt = jnp.where(chunk == n_t - 1, 0.0, xn_ref[...])
        ext_s[0:halo, :] = prev
        ext_s[halo:halo + tc, :] = x_ref[...]
        ext_s[halo + tc:halo + tc + halo, :] = nxt
        xc = cb_ref[...] + sum(
            cw_ref[k:k + 1, :] * ext_s[halo - 2 + k:halo - 2 + k + tc, :] for k in range(CONV_W))
        lam = lam_ref[d:d + 1, :]
        z = -lam
        sp = jnp.maximum(z, 0.0) + jnp.log1p(jnp.exp(-jnp.abs(z)))
        for blk in range(cw // LRU_BLOCK_W):
            sl = slice(blk * LRU_BLOCK_W, (blk + 1) * LRU_BLOCK_W)
            xb = xc[:, sl]
            xbh = xb.astype(BF16)
            r = _sigmoid(jnp.dot(xbh, wa_ref[d, blk].astype(BF16), preferred_element_type=F32)
                         + ba_ref[d:d + 1, sl])
            ig = _sigmoid(jnp.dot(xbh, wx_ref[d, blk].astype(BF16), preferred_element_type=F32)
                          + bx_ref[d:d + 1, sl])
            log_a = (-RGLRU_C) * r * sp[:, sl]
            a_s[:, sl] = jnp.exp(log_a)
            th = jnp.tanh(log_a)
            b_s[:, sl] = jnp.sqrt(-2.0 * th / (1.0 - th)) * (ig * xb)

    gates(xf_ref, xfp_ref, xfn_ref, i, 0, af_s, bf_s)
    gates(xb_ref, xbp_ref, xbn_ref, n_t - 1 - i, 1, ab_s, bb_s)

    def body(r, carry):
        hf, hb = carry
        hf = af_s[pl.ds(r, 1), :] * hf + bf_s[pl.ds(r, 1), :]
        yf_ref[pl.ds(r, 1), :] = hf
        rb = tc - 1 - r
        hb = ab_s[pl.ds(rb, 1), :] * hb + bb_s[pl.ds(rb, 1), :]
        yb_ref[pl.ds(rb, 1), :] = hb
        return hf, hb

    hf, hb = lax.fori_loop(0, tc, body, (hf_s[...], hb_s[...]), unroll=8)
    hf_s[...] = hf
    hb_s[...] = hb


def _rglru_scan(proj, conv_w, conv_b, w_a, w_x, b_a, b_x, lam, *, tc=256, cw=2048):
    s = proj.shape[0]
    c = conv_w.shape[1]
    tc = min(tc, s)
    n_t = s // tc
    n_c = c // cw
    xoff = c // cw
    hb = tc // SUBLANES
    last_h = s // SUBLANES - 1

    specs = [
        pl.BlockSpec((tc, cw), lambda ci, i: (i, xoff + ci)),
        pl.BlockSpec((SUBLANES, cw), lambda ci, i: (jnp.maximum(i * hb - 1, 0), xoff + ci)),
        pl.BlockSpec((SUBLANES, cw), lambda ci, i: (jnp.minimum((i + 1) * hb, last_h), xoff + ci)),
        pl.BlockSpec((tc, cw), lambda ci, i: (n_t - 1 - i, xoff + ci)),
        pl.BlockSpec((SUBLANES, cw),
                     lambda ci, i: (jnp.maximum((n_t - 1 - i) * hb - 1, 0), xoff + ci)),
        pl.BlockSpec((SUBLANES, cw),
                     lambda ci, i: (jnp.minimum((n_t - i) * hb, last_h), xoff + ci)),
        pl.BlockSpec((CONV_W, cw), lambda ci, i: (0, ci)),
        pl.BlockSpec((1, cw), lambda ci, i: (0, ci)),
        pl.BlockSpec((2, cw // LRU_BLOCK_W, LRU_BLOCK_W, LRU_BLOCK_W), lambda ci, i: (0, ci, 0, 0)),
        pl.BlockSpec((2, cw // LRU_BLOCK_W, LRU_BLOCK_W, LRU_BLOCK_W), lambda ci, i: (0, ci, 0, 0)),
        pl.BlockSpec((2, cw), lambda ci, i: (0, ci)),
        pl.BlockSpec((2, cw), lambda ci, i: (0, ci)),
        pl.BlockSpec((2, cw), lambda ci, i: (0, ci)),
    ]
    kern = functools.partial(_rglru_kernel, tc=tc, cw=cw, n_t=n_t)
    return pl.pallas_call(
        kern,
        out_shape=(jax.ShapeDtypeStruct((s, c), F32), jax.ShapeDtypeStruct((s, c), F32)),
        grid=(n_c, n_t),
        in_specs=specs,
        out_specs=(pl.BlockSpec((tc, cw), lambda ci, i: (i, ci)),
                   pl.BlockSpec((tc, cw), lambda ci, i: (n_t - 1 - i, ci))),
        scratch_shapes=[pltpu.VMEM((tc + 2 * SUBLANES, cw), F32)]
        + [pltpu.VMEM((tc, cw), F32) for _ in range(4)]
        + [pltpu.VMEM((1, cw), F32) for _ in range(2)],
        compiler_params=_params("parallel", "arbitrary"),
        name="rglru_scan",
    )(proj, proj, proj, proj, proj, proj, conv_w, conv_b.reshape(1, c), w_a, w_x, b_a, b_x, lam)


def _pro_rglru_out(rows, vecs):
    yf, yb, gate = rows
    return (yf + yb) * _gelu_tanh(gate)


def _diff_attn_kernel(q_ref, k_ref, v_ref, lq1_ref, lk1_ref, lq2_ref, lk2_ref, g_ref, o_ref,
                      s_buf, p_buf, m_s, a_s, l_s, acc_s, *, tq, tk, rc, pv_rows, n_kv,
                      lambda_init):
    m_s[...] = jnp.full_like(m_s, -jnp.inf)
    l_s[...] = jnp.zeros_like(l_s)
    acc_s[...] = jnp.zeros_like(acc_s)
    q = q_ref[...]
    qs = (q[:, :HEAD_DIM], q[:, HEAD_DIM:])

    def scores(j, slot):
        off = pl.multiple_of(j * tk, tk)
        kb = k_ref[pl.ds(off, tk), :]
        for c in range(2):
            kc = kb[:, c * HEAD_DIM:(c + 1) * HEAD_DIM]
            s_buf[slot, c * tq:(c + 1) * tq, :] = lax.dot_general(
                qs[c], kc, (((1,), (1,)), ((), ())), preferred_element_type=F32)

    def update(j, slot):
        off = pl.multiple_of(j * tk, tk)
        vb = v_ref[pl.ds(off, tk), :]
        for g0 in range(0, 2 * tq, pv_rows):
            for r0 in range(g0, g0 + pv_rows, rc):
                rows = slice(r0, r0 + rc)
                sc = s_buf[slot, rows, :]
                m_old = m_s[rows, :]
                m_new = jnp.maximum(m_old, jnp.max(sc, axis=-1, keepdims=True))
                alpha = jnp.exp2(m_old - m_new)
                p = jnp.exp2(sc - m_new)
                l_s[rows, :] = alpha * l_s[rows, :] + sum(
                    p[:, t * LANES:(t + 1) * LANES] for t in range(tk // LANES))
                p_buf[rows, :] = p.astype(BF16)
                m_s[rows, :] = m_new
                a_s[rows, :] = alpha
            grp = slice(g0, g0 + pv_rows)
            acc_s[grp, :] = a_s[grp, :] * acc_s[grp, :] + jnp.dot(
                p_buf[grp, :], vb, preferred_element_type=F32)

    scores(0, 0)

    def kv_pair(jj, _):
        j = 2 * jj
        scores(j + 1, 1)
        update(j, 0)
        scores(jnp.minimum(j + 2, n_kv - 1), 0)
        update(j + 1, 1)
        return 0

    lax.fori_loop(0, n_kv // 2, kv_pair, 0)
    lam = (jnp.exp(jnp.sum(lq1_ref[...] * lk1_ref[...], axis=-1, keepdims=True))
           - jnp.exp(jnp.sum(lq2_ref[...] * lk2_ref[...], axis=-1, keepdims=True)) + lambda_init)
    l = jnp.sum(l_s[...], axis=-1, keepdims=True)
    o = acc_s[0:tq] / l[0:tq] - lam * (acc_s[tq:2 * tq] / l[tq:2 * tq])
    o_ref[...] = (_rms(o, g_ref[...]) * (1.0 - lambda_init)).astype(o_ref.dtype)


def _diff_attention(qkv, lq1, lk1, lq2, lk2, subln_g, lambda_init, *, tq=512, tk=1024, rc=32):
    s = qkv.shape[0]
    vd = 2 * HEAD_DIM
    tq = min(tq, s)
    tk = min(tk, s // 2)
    assert (s // tk) % 2 == 0
    kern = functools.partial(_diff_attn_kernel, tq=tq, tk=tk, rc=rc, pv_rows=tq, n_kv=s // tk,
                             lambda_init=lambda_init)
    vec = lambda a: a.reshape(1, -1)
    vspec = lambda w: pl.BlockSpec((1, w), lambda h, i: (0, 0))
    return pl.pallas_call(
        kern,
        out_shape=jax.ShapeDtypeStruct((s, DIFF_HEADS * vd), BF16),
        grid=(DIFF_HEADS, s // tq),
        in_specs=[
            pl.BlockSpec((tq, vd), lambda h, i: (i, h)),
            pl.BlockSpec((s, vd), lambda h, i: (0, DIFF_HEADS + h)),
            pl.BlockSpec((s, vd), lambda h, i: (0, 2 * DIFF_HEADS + h)),
            vspec(HEAD_DIM), vspec(HEAD_DIM), vspec(HEAD_DIM), vspec(HEAD_DIM), vspec(vd),
        ],
        out_specs=pl.BlockSpec((tq, vd), lambda h, i: (i, h)),
        scratch_shapes=[pltpu.VMEM((2, 2 * tq, tk), F32), pltpu.VMEM((2 * tq, tk), BF16),
                        pltpu.VMEM((2 * tq, 1), F32), pltpu.VMEM((2 * tq, 1), F32),
                        pltpu.VMEM((2 * tq, LANES), F32), pltpu.VMEM((2 * tq, vd), F32)],
        compiler_params=_params("parallel", "arbitrary"),
        name="diff_attention",
    )(qkv, qkv, qkv, vec(lq1), vec(lk1), vec(lq2), vec(lk2), vec(subln_g))


def _pro_identity(rows, vecs):
    return rows[0]


def _win_attn_kernel(sink_ref, q_ref, k_ref, v_ref, o_ref, *, tq, win, s_len, group):
    kvh = pl.program_id(0)
    i = pl.program_id(1)
    start = jnp.clip(i * tq - WINDOW, 0, s_len - win)
    start = pl.multiple_of(start, WINDOW)
    kw = k_ref[pl.ds(start, win), :]
    vw = v_ref[pl.ds(start, win), :]
    qpos = i * tq + lax.broadcasted_iota(jnp.int32, (tq, win), 0)
    kpos = start + lax.broadcasted_iota(jnp.int32, (tq, win), 1)
    valid = jnp.abs(kpos - qpos) <= WINDOW
    for g in range(group):
        qg = q_ref[:, g * HEAD_DIM:(g + 1) * HEAD_DIM]
        sc = lax.dot_general(qg, kw, (((1,), (1,)), ((), ())), preferred_element_type=F32)
        sc = jnp.where(valid, sc, NEG_INF)
        sink = sink_ref[kvh * group + g] * math.log2(math.e)
        m = jnp.maximum(jnp.max(sc, axis=-1, keepdims=True), sink)
        e = jnp.exp2(sc - m)
        den = jnp.sum(e, axis=-1, keepdims=True) + jnp.exp2(sink - m)
        o = jnp.dot(e.astype(BF16), vw, preferred_element_type=F32) / den
        o_ref[:, g * HEAD_DIM:(g + 1) * HEAD_DIM] = o.astype(o_ref.dtype)


def _window_attention(qkv, sink, *, tq=256):
    s = qkv.shape[0]
    group = WIN_Q_HEADS // WIN_KV_HEADS
    tq = min(tq, s)
    win = min(tq + 2 * WINDOW, s)
    qw = group * HEAD_DIM
    k0 = WIN_Q_HEADS
    v0 = WIN_Q_HEADS + WIN_KV_HEADS
    kern = functools.partial(_win_attn_kernel, tq=tq, win=win, s_len=s, group=group)
    return pl.pallas_call(
        kern,
        out_shape=jax.ShapeDtypeStruct((s, WIN_Q_HEADS * HEAD_DIM), BF16),
        grid=(WIN_KV_HEADS, s // tq),
        in_specs=[
            pl.BlockSpec(memory_space=pltpu.SMEM),
            pl.BlockSpec((tq, qw), lambda h, i: (i, h)),
            pl.BlockSpec((s, HEAD_DIM), lambda h, i: (0, k0 + h)),
            pl.BlockSpec((s, HEAD_DIM), lambda h, i: (0, v0 + h)),
        ],
        out_specs=pl.BlockSpec((tq, qw), lambda h, i: (i, h)),
        compiler_params=_params("parallel", "arbitrary"),
        name="window_attention",
    )(sink.astype(F32), qkv, qkv, qkv)


def _s5_discretize(a_re, a_im, log_dt, b_re, b_im):
    dt = jnp.exp(log_dt)[:, None]
    mag = jnp.exp(dt * a_re)
    lr, li = mag * jnp.cos(dt * a_im), mag * jnp.sin(dt * a_im)
    den = a_re * a_re + a_im * a_im
    nr, ni = lr - 1.0, li
    fr = (nr * a_re + ni * a_im) / den
    fi = (ni * a_re - nr * a_im) / den
    bbr = fr[..., None] * b_re - fi[..., None] * b_im
    bbi = fr[..., None] * b_im + fi[..., None] * b_re
    return lr, li, bbr, bbi


def _s5_block_weights(lr, li, bbr, bbi, c_re, c_im, gpc):
    g, n, c = bbr.shape
    n_k = g // gpc
    eye = jnp.eye(gpc, dtype=F32)

    def w_in(bb):
        t = bb.reshape(n_k, gpc, n, c)
        return jnp.einsum('kgnc,gh->kgchn', t, eye).reshape(n_k, gpc * c, gpc * n)

    def w_out(cc):
        t = cc.reshape(n_k, gpc, c, n)
        return jnp.einsum('kgcn,gh->khngc', t, eye).reshape(n_k, gpc * n, gpc * c)

    win = jnp.concatenate([w_in(bbr), w_in(bbi)], axis=2)
    wout = jnp.concatenate([w_out(c_re), -w_out(c_im)], axis=1)
    lam = jnp.stack([lr.reshape(n_k, gpc * n), li.reshape(n_k, gpc * n)], axis=1)
    return win, wout, lam


def _s5_kernel(uf_ref, ub_ref, win_ref, wout_ref, lam_ref, yf_ref, yb_ref,
               xf_s, xb_s, st_s, *, tc, ns, cw, nsub):
    i = pl.program_id(1)

    @pl.when(i == 0)
    def _():
        st_s[...] = jnp.zeros_like(st_s)

    for c in range(nsub):
        cols = slice(c * cw, (c + 1) * cw)
        xf_s[c] = jnp.dot(uf_ref[:, cols].astype(BF16), win_ref[0, c].astype(BF16),
                          preferred_element_type=F32)
        xb_s[c] = jnp.dot(ub_ref[:, cols].astype(BF16), win_ref[1, c].astype(BF16),
                          preferred_element_type=F32)
    lam = [[(lam_ref[d, c, 0:1, :], lam_ref[d, c, 1:2, :]) for c in range(nsub)]
           for d in range(2)]

    def step(x_s, c, row, lr, li, sr, si):
        nr = lr * sr - li * si + x_s[c, pl.ds(row, 1), 0:ns]
        ni = lr * si + li * sr + x_s[c, pl.ds(row, 1), ns:2 * ns]
        x_s[c, pl.ds(row, 1), 0:ns] = nr
        x_s[c, pl.ds(row, 1), ns:2 * ns] = ni
        return nr, ni

    def body(r, carry):
        out = []
        for c in range(nsub):
            srf, sif, srb, sib = carry[4 * c:4 * c + 4]
            srf, sif = step(xf_s, c, r, *lam[0][c], srf, sif)
            srb, sib = step(xb_s, c, tc - 1 - r, *lam[1][c], srb, sib)
            out += [srf, sif, srb, sib]
        return tuple(out)

    init = tuple(st_s[j:j + 1, :] for j in range(4 * nsub))
    final = lax.fori_loop(0, tc, body, init, unroll=8)
    for j in range(4 * nsub):
        st_s[j:j + 1, :] = final[j]
    for c in range(nsub):
        cols = slice(c * cw, (c + 1) * cw)
        yf_ref[:, cols] = jnp.dot(xf_s[c].astype(BF16), wout_ref[0, c].astype(BF16),
                                  preferred_element_type=F32)
        yb_ref[:, cols] = jnp.dot(xb_s[c].astype(BF16), wout_ref[1, c].astype(BF16),
                                  preferred_element_type=F32)


def _s5_scan(u, win, wout, lam, *, tc=256, nsub=2):
    s, w = u.shape
    _, n_k, cw, ns2 = win.shape
    ns = ns2 // 2
    tc = min(tc, s)
    n_t = s // tc
    nsub = min(nsub, n_k)
    bw = nsub * cw
    kern = functools.partial(_s5_kernel, tc=tc, ns=ns, cw=cw, nsub=nsub)
    return pl.pallas_call(
        kern,
        out_shape=(jax.ShapeDtypeStruct((s, w), F32), jax.ShapeDtypeStruct((s, w), F32)),
        grid=(n_k // nsub, n_t),
        in_specs=[
            pl.BlockSpec((tc, bw), lambda k, i: (i, k)),
            pl.BlockSpec((tc, bw), lambda k, i: (n_t - 1 - i, k)),
            pl.BlockSpec((2, nsub, cw, ns2), lambda k, i: (0, k, 0, 0)),
            pl.BlockSpec((2, nsub, ns2, cw), lambda k, i: (0, k, 0, 0)),
            pl.BlockSpec((2, nsub, 2, ns), lambda k, i: (0, k, 0, 0)),
        ],
        out_specs=(pl.BlockSpec((tc, bw), lambda k, i: (i, k)),
                   pl.BlockSpec((tc, bw), lambda k, i: (n_t - 1 - i, k))),
        scratch_shapes=[pltpu.VMEM((nsub, tc, ns2), F32), pltpu.VMEM((nsub, tc, ns2), F32),
                        pltpu.VMEM((4 * nsub, ns), F32)],
        compiler_params=_params("parallel", "arbitrary"),
        name="s5_scan",
    )(u, u, win, wout, lam)


def _pro_s5_glu(rows, vecs):
    yf, yb, u = rows
    return _gelu_tanh(yf + yb + vecs[0] * u)


def _epi_glu(acc, epi_refs, o_ref, j, a32_s):
    o_ref[...] = (a32_s[...] * _sigmoid(acc)).astype(o_ref.dtype)


def _route(x, g, w_r):
    h = _rms(x, g)
    logits = jnp.dot(h.astype(BF16), w_r, preferred_element_type=F32)
    tm = logits.shape[0]
    lane = lax.broadcasted_iota(jnp.int32, (tm, LANES), 1)
    big = jnp.int32(LANES)
    ninf = -jnp.inf
    gl = jnp.where(lane < MOE_GROUPS, logits, ninf)
    gm = jnp.max(gl, axis=-1, keepdims=True)
    ge = jnp.exp(gl - gm)
    g_prob = ge / jnp.sum(ge, axis=-1, keepdims=True)
    g_p = jnp.max(g_prob, axis=-1, keepdims=True)
    g_idx = jnp.min(jnp.where(g_prob == g_p, lane, big), axis=-1, keepdims=True)
    lo = MOE_GROUPS + g_idx * EXPERTS_PER_GROUP
    in_grp = (lane >= lo) & (lane < lo + EXPERTS_PER_GROUP)
    el = jnp.where(in_grp, logits, ninf)
    em = jnp.max(el, axis=-1, keepdims=True)
    ee = jnp.exp(el - em)
    e_prob = jnp.where(in_grp, ee / jnp.sum(ee, axis=-1, keepdims=True), -1.0)
    p1 = jnp.max(e_prob, axis=-1, keepdims=True)
    i1 = jnp.min(jnp.where(e_prob == p1, lane, big), axis=-1, keepdims=True)
    rest = jnp.where(lane == i1, -1.0, e_prob)
    p2 = jnp.max(rest, axis=-1, keepdims=True)
    i2 = jnp.min(jnp.where(rest == p2, lane, big), axis=-1, keepdims=True)
    denom = p1 + p2
    w1 = g_p * (p1 / denom)
    w2 = g_p * (p2 / denom)
    idx = jnp.where(lane == 0, i1 - MOE_GROUPS, jnp.where(lane == 1, i2 - MOE_GROUPS, 0))
    wt = jnp.where(lane == 0, w1, jnp.where(lane == 1, w2, 0.0))
    return h, idx, wt


def _router_kernel(x_ref, g_ref, w_ref, h_ref, idx_ref, wt_ref):
    h, idx, wt = _route(x_ref[...], g_ref[...], w_ref[...])
    h_ref[...] = h
    idx_ref[...] = idx
    wt_ref[...] = wt


def _router(x, g, w_group, w_expert, *, tm=512):
    s, d = x.shape
    tm = min(tm, s)
    w_r = jnp.concatenate(
        [w_group, w_expert, jnp.zeros((d, LANES - MOE_GROUPS - N_EXPERTS), F32)], axis=1)
    return pl.pallas_call(
        _router_kernel,
        out_shape=(jax.ShapeDtypeStruct((s, d), F32), jax.ShapeDtypeStruct((s, LANES), jnp.int32),
                   jax.ShapeDtypeStruct((s, LANES), F32)),
        grid=(s // tm,),
        in_specs=[pl.BlockSpec((tm, d), lambda i: (i, 0)), pl.BlockSpec((1, d), lambda i: (0, 0)),
                  pl.BlockSpec((d, LANES), lambda i: (0, 0))],
        out_specs=(pl.BlockSpec((tm, d), lambda i: (i, 0)), pl.BlockSpec((tm, LANES), lambda i: (i, 0)),
                   pl.BlockSpec((tm, LANES), lambda i: (i, 0))),
        compiler_params=_params("parallel"),
        name="moe_router",
    )(x, g.reshape(1, d), w_r.astype(BF16))


def _rank_kernel(idx_ref, dest_ref, cnt_ref, tot_s, pst_s, run_s, *, tm):
    ph = pl.program_id(0)
    i = pl.program_id(1)
    lane = lax.broadcasted_iota(jnp.int32, (tm, LANES), 1)
    idx = idx_ref[...]
    oh0 = (lane == idx[:, 0:1]).astype(F32)
    oh1 = (lane == idx[:, 1:2]).astype(F32)
    c = oh0 + oh1
    csum = jnp.sum(c, axis=0, keepdims=True)

    @pl.when(jnp.logical_and(ph == 0, i == 0))
    def _():
        tot_s[...] = jnp.zeros_like(tot_s)

    @pl.when(ph == 0)
    def _():
        tot_s[...] += csum

    @pl.when(jnp.logical_and(ph == 1, i == 0))
    def _():
        counts = tot_s[...]
        nblk = jnp.floor((counts + (EXPERT_BLOCK - 1)) * (1.0 / EXPERT_BLOCK))
        r = lax.broadcasted_iota(jnp.int32, (LANES, LANES), 0)
        cc = lax.broadcasted_iota(jnp.int32, (LANES, LANES), 1)
        upper = (r < cc).astype(F32)
        excl = jnp.dot(jnp.broadcast_to(nblk, (SUBLANES, LANES)), upper,
                       preferred_element_type=F32, precision=lax.Precision.HIGHEST)
        pst_s[...] = excl[0:1, :] * EXPERT_BLOCK
        run_s[...] = jnp.zeros_like(run_s)
        cnt_ref[...] = jnp.broadcast_to(counts, cnt_ref.shape)

    @pl.when(ph == 1)
    def _():
        rr = lax.broadcasted_iota(jnp.int32, (tm, tm), 0)
        cr = lax.broadcasted_iota(jnp.int32, (tm, tm), 1)
        lower = (rr > cr).astype(BF16)
        before = jnp.dot(lower, c.astype(BF16), preferred_element_type=F32) + run_s[...]
        base = pst_s[...] + before
        d0 = jnp.sum(oh0 * base, axis=-1, keepdims=True)
        d1 = jnp.sum(oh1 * (base + oh0), axis=-1, keepdims=True)
        dest_ref[...] = jnp.where(lane == 0, d0, jnp.where(lane == 1, d1, 0.0)).astype(jnp.int32)
        run_s[...] += csum


def _dispatch_rank(idx, *, tm=512):
    s = idx.shape[0]
    tm = min(tm, s)
    kern = functools.partial(_rank_kernel, tm=tm)
    return pl.pallas_call(
        kern,
        out_shape=(jax.ShapeDtypeStruct((s, LANES), jnp.int32),
                   jax.ShapeDtypeStruct((SUBLANES, LANES), F32)),
        grid=(2, s // tm),
        in_specs=[pl.BlockSpec((tm, LANES), lambda ph, i: (i, 0))],
        out_specs=(pl.BlockSpec((tm, LANES), lambda ph, i: (i * ph, 0)),
                   pl.BlockSpec((SUBLANES, LANES), lambda ph, i: (0, 0))),
        scratch_shapes=[pltpu.VMEM((1, LANES), F32) for _ in range(3)],
        compiler_params=_params("arbitrary", "arbitrary"),
        name="moe_rank",
    )(idx)


def _block_experts(counts, n_blk):
    cnt = counts[0, :N_EXPERTS].astype(jnp.int32)
    padded = ((cnt + EXPERT_BLOCK - 1) // EXPERT_BLOCK) * EXPERT_BLOCK
    pends = jnp.cumsum(padded)
    blk_start = jnp.arange(n_blk, dtype=jnp.int32) * EXPERT_BLOCK
    owner = jnp.sum((pends[None, :] <= blk_start[:, None]).astype(jnp.int32), axis=1)
    ids = jnp.arange(N_EXPERTS, dtype=jnp.int32)
    last_e = jnp.max(jnp.where(cnt > 0, ids, 0))
    n_used = (pends[-1] // EXPERT_BLOCK).reshape(1)
    blk_e = jnp.minimum(owner, last_e).astype(jnp.int32)
    later = jnp.logical_and(ids[None, :] > ids[:, None], cnt[None, :] > 0)
    nxt = jnp.min(jnp.where(later, ids[None, :], N_EXPERTS), axis=1)
    nxt = jnp.where(nxt == N_EXPERTS, ids, nxt).astype(jnp.int32)
    return blk_e, nxt[blk_e], n_used.astype(jnp.int32)


def _scatter_kernel(dest_ref, h_ref, xs_in_hbm, xs_hbm, h_s, sem, *, tb, n_b):
    del xs_in_hbm
    b = pl.program_id(0)
    slot = b % 2
    n = tb * TOP_K

    def copy(r, row, sl):
        return pltpu.make_async_copy(h_s.at[sl, pl.ds(r, 1), :], xs_hbm.at[pl.ds(row, 1), :],
                                     sem.at[sl])

    def wait_all(sl):
        def body(r, _):
            copy(0, 0, sl).wait()
            return 0
        lax.fori_loop(0, n, body, 0, unroll=8)

    h_s[slot] = h_ref[...]
    for r in range(tb):
        tok = b * tb + r
        for k in range(TOP_K):
            copy(r, dest_ref[tok * TOP_K + k], slot).start()

    @pl.when(b > 0)
    def _():
        wait_all(1 - slot)

    @pl.when(b == n_b - 1)
    def _():
        wait_all(slot)


def _dispatch_scatter(h, dest_flat, n_rows, *, tb=128):
    s, d = h.shape
    tb = min(tb, s)
    n_b = s // tb
    kern = functools.partial(_scatter_kernel, tb=tb, n_b=n_b)
    grid_spec = pltpu.PrefetchScalarGridSpec(
        num_scalar_prefetch=1,
        grid=(n_b,),
        in_specs=[pl.BlockSpec((tb, d), lambda b, dr: (b, 0)), pl.BlockSpec(memory_space=pl.ANY)],
        out_specs=pl.BlockSpec(memory_space=pl.ANY),
        scratch_shapes=[pltpu.VMEM((2, tb, d), h.dtype), pltpu.SemaphoreType.DMA((2,))],
    )
    return pl.pallas_call(
        kern,
        out_shape=jax.ShapeDtypeStruct((n_rows, d), h.dtype),
        grid_spec=grid_spec,
        input_output_aliases={2: 0},
        compiler_params=_params("arbitrary"),
        name="moe_scatter",
    )(dest_flat, h, jnp.zeros((n_rows, d), h.dtype))


def _row_copy(src_hbm, src_row, dst_buf, slot, dst_row, sem):
    return pltpu.make_async_copy(src_hbm.at[pl.ds(src_row, 1), :],
                                 dst_buf.at[slot, pl.ds(dst_row, 1), :], sem.at[slot])


def _gather_start(idx_ref, base, n, src_hbm, dst_buf, slot, sem):
    for r in range(n):
        _row_copy(src_hbm, idx_ref[base + r], dst_buf, slot, r, sem).start()


def _gather_wait(n, src_hbm, dst_buf, slot, sem):
    def body(r, _):
        _row_copy(src_hbm, 0, dst_buf, slot, r, sem).wait()
        return 0
    lax.fori_loop(0, n, body, 0, unroll=8)


def _expert_kernel(be_ref, nx_ref, nu_ref, x_ref, wg_hbm, wu_hbm, wd_hbm, y_ref,
                   wg_f, wu_f, wd_f, wg_s, wu_s, wd_s, sem, *, layer):
    b = pl.program_id(0)
    e = be_ref[b]

    def fetch(expert):
        return (pltpu.make_async_copy(wg_hbm.at[layer, expert], wg_f, sem.at[0]),
                pltpu.make_async_copy(wu_hbm.at[layer, expert], wu_f, sem.at[1]),
                pltpu.make_async_copy(wd_hbm.at[layer, expert], wd_f, sem.at[2]))

    @pl.when(b == 0)
    def _():
        for c in fetch(e):
            c.start()

    changed = jnp.logical_or(b == 0, e != be_ref[jnp.maximum(b - 1, 0)])

    @pl.when(changed)
    def _():
        for c in fetch(e):
            c.wait()
        wg_s[...] = wg_f[...].astype(BF16)
        wu_s[...] = wu_f[...].astype(BF16)
        wd_s[...] = wd_f[...].astype(BF16)

        @pl.when(nx_ref[b] != e)
        def _():
            for c in fetch(nx_ref[b]):
                c.start()

    @pl.when(b >= nu_ref[0])
    def _():
        y_ref[...] = jnp.zeros_like(y_ref)

    @pl.when(b < nu_ref[0])
    def _():
        x = x_ref[...].astype(BF16)
        hg = jnp.dot(x, wg_s[...], preferred_element_type=F32)
        hu = jnp.dot(x, wu_s[...], preferred_element_type=F32)
        hdn = (hg * _sigmoid(hg)) * hu
        y_ref[...] = jnp.dot(hdn.astype(BF16), wd_s[...], preferred_element_type=F32)


def _expert_mlp(xs, blk_e, blk_next, n_used, layer, w_gate, w_up, w_down):
    n_rows, d = xs.shape
    de = w_gate.shape[3]
    grid_spec = pltpu.PrefetchScalarGridSpec(
        num_scalar_prefetch=3,
        grid=(n_rows // EXPERT_BLOCK,),
        in_specs=[
            pl.BlockSpec((EXPERT_BLOCK, d),
                         lambda b, be, nx, nu: (jnp.minimum(b, nu[0] - 1), 0)),
            pl.BlockSpec(memory_space=pl.ANY),
            pl.BlockSpec(memory_space=pl.ANY),
            pl.BlockSpec(memory_space=pl.ANY),
        ],
        out_specs=pl.BlockSpec((EXPERT_BLOCK, d), lambda b, be, nx, nu: (b, 0)),
        scratch_shapes=[pltpu.VMEM((d, de), F32), pltpu.VMEM((d, de), F32),
                        pltpu.VMEM((de, d), F32), pltpu.VMEM((d, de), BF16),
                        pltpu.VMEM((d, de), BF16), pltpu.VMEM((de, d), BF16),
                        pltpu.SemaphoreType.DMA((3,))],
    )
    return pl.pallas_call(
        functools.partial(_expert_kernel, layer=layer),
        out_shape=jax.ShapeDtypeStruct((n_rows, d), F32),
        grid_spec=grid_spec,
        compiler_params=_params("arbitrary"),
        name="moe_experts",
    )(blk_e, blk_next, n_used, xs, w_gate, w_up, w_down)


def _combine_kernel(dest_ref, x_ref, wt_ref, ys_hbm, g_ref, o_ref, ybuf, sem, *, tb, n_b, final):
    b = pl.program_id(0)
    slot = b % 2
    n = tb * TOP_K

    @pl.when(b == 0)
    def _():
        _gather_start(dest_ref, 0, n, ys_hbm, ybuf, 0, sem)

    _gather_wait(n, ys_hbm, ybuf, slot, sem)
    _gather_start(dest_ref, (b + 1) * n, n, ys_hbm, ybuf, 1 - slot, sem)
    y0 = ybuf[slot, 0:tb, :]
    y1 = ybuf[slot, tb:2 * tb, :]
    wt = wt_ref[...]
    out = x_ref[...] + (y0 * wt[:, 0:1] + y1 * wt[:, 1:2])
    if final:
        out = _rms(out, g_ref[...])
    o_ref[...] = out

    @pl.when(b == n_b - 1)
    def _():
        _gather_wait(n, ys_hbm, ybuf, 1 - slot, sem)


def _moe_combine(x, wt, ys, dest, g_final, *, final, tb=128):
    s, d = x.shape
    tb = min(tb, s)
    n_b = s // tb
    dest = dest.reshape(n_b, tb, TOP_K).transpose(0, 2, 1).reshape(-1)
    dest = jnp.concatenate([dest, jnp.zeros((tb * TOP_K,), jnp.int32)])
    kern = functools.partial(_combine_kernel, tb=tb, n_b=n_b, final=final)
    grid_spec = pltpu.PrefetchScalarGridSpec(
        num_scalar_prefetch=1,
        grid=(n_b,),
        in_specs=[
            pl.BlockSpec((tb, d), lambda b, dr: (b, 0)),
            pl.BlockSpec((tb, LANES), lambda b, dr: (b, 0)),
            pl.BlockSpec(memory_space=pl.ANY),
            pl.BlockSpec((1, d), lambda b, dr: (0, 0)),
        ],
        out_specs=pl.BlockSpec((tb, d), lambda b, dr: (b, 0)),
        scratch_shapes=[pltpu.VMEM((2, tb * TOP_K, d), F32), pltpu.SemaphoreType.DMA((2,))],
    )
    return pl.pallas_call(
        kern,
        out_shape=jax.ShapeDtypeStruct((s, d), F32),
        grid_spec=grid_spec,
        compiler_params=_params("arbitrary"),
        name="moe_combine",
    )(dest, x, wt, ys, g_final.reshape(1, d))


def _hier_moe(x, ln_g, w_group, w_expert, layer, w_gate, w_up, w_down, g_final, *, final):
    s = x.shape[0]
    h, idx, wt = _router(x, ln_g, w_group, w_expert)
    dest, counts = _dispatch_rank(idx)
    dest = dest[:, :TOP_K].reshape(-1)
    n_rows = -(-(s * TOP_K + N_EXPERTS * (EXPERT_BLOCK - 1)) // EXPERT_BLOCK) * EXPERT_BLOCK
    blk_e, blk_next, n_used = _block_experts(counts, n_rows // EXPERT_BLOCK)
    xs = _dispatch_scatter(h, dest, n_rows)
    ys = _expert_mlp(xs, blk_e, blk_next, n_used, layer, w_gate, w_up, w_down)
    return _moe_combine(x, wt, ys, dest, g_final, final=final)


def _rglru_layer(x, ln_g, w_in, conv_w, conv_b, w_a, w_x, b_a, b_x, lam, w_out):
    d = x.shape[1]
    c = conv_w.shape[1]
    proj = _norm_matmul(x, ln_g, w_in, name="lru_in_proj")
    yf, yb = _rglru_scan(proj, conv_w, conv_b, w_a, w_x, b_a, b_x, lam)
    return _matmul_residual([(yf, c, 0), (yb, c, 0), (proj, c, 0)], [], w_out, x, _pro_rglru_out,
                            k_dim=c, tm=512, name="lru_out_proj")


def _diff_layer(x, ln_g, rope, w_qkv, lq1, lk1, lq2, lk2, subln_g, w_out, lambda_init):
    qk_cols = 2 * DIFF_HEADS * 2 * HEAD_DIM
    qkv = _norm_matmul(x, ln_g, w_qkv, rope=rope, n_rope_cols=qk_cols, n_q_cols=qk_cols // 2,
                       q_scale=HEAD_DIM ** -0.5 * math.log2(math.e), out_dtype=BF16,
                       name="diff_qkv_proj")
    o = _diff_attention(qkv, lq1, lk1, lq2, lk2, subln_g, lambda_init)
    return _matmul_residual([(o, o.shape[1], 0)], [], w_out, x, _pro_identity,
                            k_dim=o.shape[1], name="diff_out_proj")


def _window_layer(x, ln_g, rope, w_qkv, sink, w_out):
    qk_cols = (WIN_Q_HEADS + WIN_KV_HEADS) * HEAD_DIM
    qkv = _norm_matmul(x, ln_g, w_qkv, rope=rope, n_rope_cols=qk_cols,
                       n_q_cols=WIN_Q_HEADS * HEAD_DIM,
                       q_scale=HEAD_DIM ** -0.5 * math.log2(math.e), out_dtype=BF16, tn=512,
                       name="win_qkv_proj")
    o = _window_attention(qkv, sink)
    return _matmul_residual([(o, o.shape[1], 0)], [], w_out, x, _pro_identity,
                            k_dim=o.shape[1], name="win_out_proj")


def _s5_layer(x, ln_g, w_in, a_re, a_im, log_dt, b_re, b_im, c_re, c_im, d_skip, w_glu, w_out,
              *, gpc=16):
    u = _norm_matmul(x, ln_g, w_in, name="s5_in_proj")
    w = u.shape[1]
    wins, wouts, lams = [], [], []
    for dd in range(2):
        lr, li, bbr, bbi = _s5_discretize(a_re[dd], a_im[dd], log_dt[dd], b_re[dd], b_im[dd])
        wi, wo, lm = _s5_block_weights(lr, li, bbr, bbi, c_re[dd], c_im[dd], gpc)
        wins.append(wi)
        wouts.append(wo)
        lams.append(lm)
    yf, yb = _s5_scan(u, jnp.stack(wins).astype(BF16), jnp.stack(wouts).astype(BF16),
                      jnp.stack(lams))
    z = _fused_mm([(yf, w, 0), (yb, w, 0), (u, w, 0)], [d_skip.reshape(1, w)], w_glu, [],
                  _pro_s5_glu, _epi_glu, out_dtype=BF16, tm=512, tn=w, k_dim=w, keep_f32=True,
                  name="s5_glu")
    return _matmul_residual([(z, w, 0)], [], w_out, x, _pro_identity, k_dim=w, name="s5_out_proj")


def kernel(x, positions, ln_mix, ln_ffn, ln_final, lru_w_in, lru_conv_w, lru_conv_b, lru_w_a, lru_w_x, lru_b_a, lru_b_x, lru_lambda, lru_w_out, diff_w_qkv, diff_lq1, diff_lk1, diff_lq2, diff_lk2, diff_subln, diff_w_out, win_w_qkv, win_sink, win_w_out, s5_w_in, s5_a_re, s5_a_im, s5_log_dt, s5_b_re, s5_b_im, s5_c_re, s5_c_im, s5_d, s5_w_glu, s5_w_out, moe_w_group, moe_w_expert, moe_w_gate, moe_w_up, moe_w_down):
    batch, s, d = x.shape
    depth = ln_mix.shape[0]
    outs = []
    for b in range(batch):
        xb = x[b]
        rope = _rope_tables(positions[b])
        for i in range(depth):
            kind, j = i % 4, i // 4
            if kind == 0:
                xb = _rglru_layer(xb, ln_mix[i], lru_w_in[j], lru_conv_w[j], lru_conv_b[j],
                                  lru_w_a[j], lru_w_x[j], lru_b_a[j], lru_b_x[j], lru_lambda[j],
                                  lru_w_out[j])
            elif kind == 1:
                xb = _diff_layer(xb, ln_mix[i], rope, diff_w_qkv[j], diff_lq1[j], diff_lk1[j],
                                 diff_lq2[j], diff_lk2[j], diff_subln[j], diff_w_out[j],
                                 0.8 - 0.6 * math.exp(-0.3 * i))
            elif kind == 2:
                xb = _window_layer(xb, ln_mix[i], rope, win_w_qkv[j], win_sink[j], win_w_out[j])
            else:
                xb = _s5_layer(xb, ln_mix[i], s5_w_in[j], s5_a_re[j], s5_a_im[j], s5_log_dt[j],
                               s5_b_re[j], s5_b_im[j], s5_c_re[j], s5_c_im[j], s5_d[j],
                               s5_w_glu[j], s5_w_out[j])
            xb = _hier_moe(xb, ln_ffn[i], moe_w_group[i], moe_w_expert[i], i, moe_w_gate,
                           moe_w_up, moe_w_down, ln_final, final=(i == depth - 1))
        outs.append(xb)
    return jnp.stack(outs)
```

```python
import functools
import math

import jax
import jax.numpy as jnp
from jax import lax
from jax.experimental import pallas as pl
from jax.experimental.pallas import tpu as pltpu

F32 = jnp.float32
BF16 = jnp.bfloat16

NORM_EPS = 1e-6
NEG_INF = -1e30
LANES = 128
SUBLANES = 8
VMEM_LIMIT = 56 * 1024 * 1024

HEAD_DIM = 128
ROT_DIM = HEAD_DIM // 4
ROPE_THETA = 500000.0
RGLRU_C = 8.0
CONV_W = 4
LRU_BLOCK_W = 128
WINDOW = 128
DIFF_HEADS = 8
WIN_Q_HEADS = 16
WIN_KV_HEADS = 4
SSM_GROUP_CH = 16
SSM_STATE = 64
MOE_GROUPS = 4
EXPERTS_PER_GROUP = 8
N_EXPERTS = MOE_GROUPS * EXPERTS_PER_GROUP
TOP_K = 2
EXPERT_BLOCK = 256


def _params(*sem):
    return pltpu.CompilerParams(dimension_semantics=sem, vmem_limit_bytes=VMEM_LIMIT)


def _rms(x, g):
    ms = jnp.mean(x * x, axis=-1, keepdims=True)
    return x * lax.rsqrt(ms + NORM_EPS) * g


def _gelu_tanh(x):
    return 0.5 * x * (1.0 + jnp.tanh(math.sqrt(2.0 / math.pi) * (x + 0.044715 * (x * x * x))))


def _sigmoid(x):
    return 1.0 / (1.0 + jnp.exp(-x))


def _fused_mm_kernel(*refs, n_row, n_vec, n_epi, prologue, epilogue, keep_f32):
    row_refs = refs[:n_row]
    vec_refs = refs[n_row:n_row + n_vec]
    w_ref = refs[n_row + n_vec]
    epi_refs = refs[n_row + n_vec + 1:n_row + n_vec + 1 + n_epi]
    o_ref = refs[n_row + n_vec + 1 + n_epi]
    a_s = refs[n_row + n_vec + 2 + n_epi]
    a32_s = refs[n_row + n_vec + 3 + n_epi] if keep_f32 else None
    j = pl.program_id(1)

    @pl.when(j == 0)
    def _():
        a = prologue([r[...] for r in row_refs], [v[...] for v in vec_refs])
        a_s[...] = a.astype(BF16)
        if keep_f32:
            a32_s[...] = a

    acc = jnp.dot(a_s[...], w_ref[...].astype(BF16), preferred_element_type=F32)
    epilogue(acc, epi_refs, o_ref, j, a32_s)


def _fused_mm(row_inputs, vec_inputs, w, epi_inputs, prologue, epilogue, *, out_dtype, tm, tn,
              k_dim, keep_f32=False, name):
    s = row_inputs[0][0].shape[0]
    n = w.shape[1]
    tm = min(tm, s)
    tn = min(tn, n)
    w = w.astype(BF16)
    in_specs = []
    args = []
    for arr, width, cb in row_inputs:
        in_specs.append(pl.BlockSpec((tm, width), lambda i, j, cb=cb: (i, cb)))
        args.append(arr)
    for arr in vec_inputs:
        in_specs.append(pl.BlockSpec(arr.shape, lambda i, j: (0, 0)))
        args.append(arr)
    in_specs.append(pl.BlockSpec((k_dim, tn), lambda i, j: (0, j)))
    args.append(w)
    for arr, width, per_tile in epi_inputs:
        if per_tile:
            in_specs.append(pl.BlockSpec((tm, width), lambda i, j: (i, j)))
        else:
            in_specs.append(pl.BlockSpec((tm, width), lambda i, j: (i, 0)))
        args.append(arr)
    scratch = [pltpu.VMEM((tm, k_dim), BF16)]
    if keep_f32:
        scratch.append(pltpu.VMEM((tm, k_dim), F32))
    kern = functools.partial(_fused_mm_kernel, n_row=len(row_inputs), n_vec=len(vec_inputs),
                             n_epi=len(epi_inputs), prologue=prologue, epilogue=epilogue,
                             keep_f32=keep_f32)
    return pl.pallas_call(
        kern,
        out_shape=jax.ShapeDtypeStruct((s, n), out_dtype),
        grid=(s // tm, n // tn),
        in_specs=in_specs,
        out_specs=pl.BlockSpec((tm, tn), lambda i, j: (i, j)),
        scratch_shapes=scratch,
        compiler_params=_params("parallel", "arbitrary"),
        name=name,
    )(*args)


def _pro_rms(rows, vecs):
    return _rms(rows[0], vecs[0])


def _epi_store(acc, epi_refs, o_ref, j, a32_s):
    o_ref[...] = acc.astype(o_ref.dtype)


def _epi_residual(acc, epi_refs, o_ref, j, a32_s):
    o_ref[...] = (epi_refs[0][...] + acc).astype(o_ref.dtype)


def _make_epi_rope(n_rope_tiles, n_q_tiles, q_scale, tn):
    def epi(acc, epi_refs, o_ref, j, a32_s):
        c_ref, s1_ref, s2_ref = epi_refs

        @pl.when(j < n_rope_tiles)
        def _():
            c = c_ref[...]
            s1 = s1_ref[...]
            s2 = s2_ref[...]
            sc = jnp.where(j < n_q_tiles, q_scale, 1.0).astype(F32)
            for hh in range(tn // HEAD_DIM):
                xs = acc[:, hh * HEAD_DIM:(hh + 1) * HEAD_DIM]
                rot = (xs * c + pltpu.roll(xs, HEAD_DIM - ROT_DIM // 2, 1) * s1
                       + pltpu.roll(xs, ROT_DIM // 2, 1) * s2) * sc
                o_ref[:, hh * HEAD_DIM:(hh + 1) * HEAD_DIM] = rot.astype(o_ref.dtype)

        @pl.when(j >= n_rope_tiles)
        def _():
            o_ref[...] = acc.astype(o_ref.dtype)

    return epi


def _rope_tables(positions):
    half = ROT_DIM // 2
    inv = ROPE_THETA ** (-jnp.arange(0, ROT_DIM, 2, dtype=F32) / ROT_DIM)
    ang = positions.astype(F32)[:, None] * inv
    cos, sin = jnp.cos(ang), jnp.sin(ang)
    s = positions.shape[0]
    ones = jnp.ones((s, HEAD_DIM - ROT_DIM), F32)
    zeros = jnp.zeros((s, HEAD_DIM - ROT_DIM), F32)
    zh = jnp.zeros((s, half), F32)
    c_tab = jnp.concatenate([cos, cos, ones], axis=1)
    s1_tab = jnp.concatenate([-sin, zh, zeros], axis=1)
    s2_tab = jnp.concatenate([zh, sin, zeros], axis=1)
    return c_tab, s1_tab, s2_tab


def _norm_matmul(x, g, w, *, rope=None, n_rope_cols=0, n_q_cols=0, q_scale=1.0, out_dtype=F32,
                 tm=1024, tn=1024, name):
    d = x.shape[1]
    if rope is None:
        epi, epi_inputs = _epi_store, []
    else:
        tn = min(tn, w.shape[1])
        assert n_rope_cols % tn == 0 and n_q_cols % tn == 0
        epi = _make_epi_rope(n_rope_cols // tn, n_q_cols // tn, q_scale, tn)
        epi_inputs = [(t, HEAD_DIM, False) for t in rope]
    return _fused_mm([(x, d, 0)], [g.reshape(1, d)], w, epi_inputs, _pro_rms, epi,
                     out_dtype=out_dtype, tm=tm, tn=tn, k_dim=d, name=name)


def _matmul_residual(row_inputs, vec_inputs, w, res, prologue, *, k_dim, tm=1024, tn=512, name):
    return _fused_mm(row_inputs, vec_inputs, w, [(res, min(tn, w.shape[1]), True)], prologue,
                     _epi_residual, out_dtype=F32, tm=tm, tn=tn, k_dim=k_dim, name=name)


def _rglru_kernel(xf_ref, xfp_ref, xfn_ref, xb_ref, xbp_ref, xbn_ref, cw_ref, cb_ref, wa_ref,
                  wx_ref, ba_ref, bx_ref, lam_ref, yf_ref, yb_ref,
                  ext_s, af_s, bf_s, ab_s, bb_s, hf_s, hb_s, *, tc, cw, n_t):
    i = pl.program_id(1)
    halo = SUBLANES

    @pl.when(i == 0)
    def _():
        hf_s[...] = jnp.zeros_like(hf_s)
        hb_s[...] = jnp.zeros_like(hb_s)

    def gates(x_ref, xp_ref, xn_ref, chunk, d, a_s, b_s):
        prev = jnp.where(chunk == 0, 0.0, xp_ref[...])
        nxt = jnp.where(chunk == n_t - 1, 0.0, xn_ref[...])
        ext_s[0:halo, :] = prev
        ext_s[halo:halo + tc, :] = x_ref[...]
        ext_s[halo + tc:halo + tc + halo, :] = nxt
        xc = cb_ref[...] + sum(
            cw_ref[k:k + 1, :] * ext_s[halo - 2 + k:halo - 2 + k + tc, :] for k in range(CONV_W))
        lam = lam_ref[d:d + 1, :]
        z = -lam
        sp = jnp.maximum(z, 0.0) + jnp.log1p(jnp.exp(-jnp.abs(z)))
        for blk in range(cw // LRU_BLOCK_W):
            sl = slice(blk * LRU_BLOCK_W, (blk + 1) * LRU_BLOCK_W)
            xb = xc[:, sl]
            xbh = xb.astype(BF16)
            r = _sigmoid(jnp.dot(xbh, wa_ref[d, blk].astype(BF16), preferred_element_type=F32)
                         + ba_ref[d:d + 1, sl])
            ig = _sigmoid(jnp.dot(xbh, wx_ref[d, blk].astype(BF16), preferred_element_type=F32)
                          + bx_ref[d:d + 1, sl])
            log_a = (-RGLRU_C) * r * sp[:, sl]
            a_s[:, sl] = jnp.exp(log_a)
            th = jnp.tanh(log_a)
            b_s[:, sl] = jnp.sqrt(-2.0 * th / (1.0 - th)) * (ig * xb)

    gates(xf_ref, xfp_ref, xfn_ref, i, 0, af_s, bf_s)
    gates(xb_ref, xbp_ref, xbn_ref, n_t - 1 - i, 1, ab_s, bb_s)

    def body(r, carry):
        hf, hb = carry
        hf = af_s[pl.ds(r, 1), :] * hf + bf_s[pl.ds(r, 1), :]
        yf_ref[pl.ds(r, 1), :] = hf
        rb = tc - 1 - r
        hb = ab_s[pl.ds(rb, 1), :] * hb + bb_s[pl.ds(rb, 1), :]
        yb_ref[pl.ds(rb, 1), :] = hb
        return hf, hb

    hf, hb = lax.fori_loop(0, tc, body, (hf_s[...], hb_s[...]), unroll=8)
    hf_s[...] = hf
    hb_s[...] = hb


def _rglru_scan(proj, conv_w, conv_b, w_a, w_x, b_a, b_x, lam, *, tc=256, cw=2048):
    s = proj.shape[0]
    c = conv_w.shape[1]
    tc = min(tc, s)
    n_t = s // tc
    n_c = c // cw
    xoff = c // cw
    hb = tc // SUBLANES
    last_h = s // SUBLANES - 1

    specs = [
        pl.BlockSpec((tc, cw), lambda ci, i: (i, xoff + ci)),
        pl.BlockSpec((SUBLANES, cw), lambda ci, i: (jnp.maximum(i * hb - 1, 0), xoff + ci)),
        pl.BlockSpec((SUBLANES, cw), lambda ci, i: (jnp.minimum((i + 1) * hb, last_h), xoff + ci)),
        pl.BlockSpec((tc, cw), lambda ci, i: (n_t - 1 - i, xoff + ci)),
        pl.BlockSpec((SUBLANES, cw),
                     lambda ci, i: (jnp.maximum((n_t - 1 - i) * hb - 1, 0), xoff + ci)),
        pl.BlockSpec((SUBLANES, cw),
                     lambda ci, i: (jnp.minimum((n_t - i) * hb, last_h), xoff + ci)),
        pl.BlockSpec((CONV_W, cw), lambda ci, i: (0, ci)),
        pl.BlockSpec((1, cw), lambda ci, i: (0, ci)),
        pl.BlockSpec((2, cw // LRU_BLOCK_W, LRU_BLOCK_W, LRU_BLOCK_W), lambda ci, i: (0, ci, 0, 0)),
        pl.BlockSpec((2, cw // LRU_BLOCK_W, LRU_BLOCK_W, LRU_BLOCK_W), lambda ci, i: (0, ci, 0, 0)),
        pl.BlockSpec((2, cw), lambda ci, i: (0, ci)),
        pl.BlockSpec((2, cw), lambda ci, i: (0, ci)),
        pl.BlockSpec((2, cw), lambda ci, i: (0, ci)),
    ]
    kern = functools.partial(_rglru_kernel, tc=tc, cw=cw, n_t=n_t)
    return pl.pallas_call(
        kern,
        out_shape=(jax.ShapeDtypeStruct((s, c), F32), jax.ShapeDtypeStruct((s, c), F32)),
        grid=(n_c, n_t),
        in_specs=specs,
        out_specs=(pl.BlockSpec((tc, cw), lambda ci, i: (i, ci)),
                   pl.BlockSpec((tc, cw), lambda ci, i: (n_t - 1 - i, ci))),
        scratch_shapes=[pltpu.VMEM((tc + 2 * SUBLANES, cw), F32)]
        + [pltpu.VMEM((tc, cw), F32) for _ in range(4)]
        + [pltpu.VMEM((1, cw), F32) for _ in range(2)],
        compiler_params=_params("parallel", "arbitrary"),
        name="rglru_scan",
    )(proj, proj, proj, proj, proj, proj, conv_w, conv_b.reshape(1, c), w_a, w_x, b_a, b_x, lam)


def _pro_rglru_out(rows, vecs):
    yf, yb, gate = rows
    return (yf + yb) * _gelu_tanh(gate)


def _diff_attn_kernel(q_ref, k_ref, v_ref, lq1_ref, lk1_ref, lq2_ref, lk2_ref, g_ref, o_ref,
                      s_buf, p_buf, m_s, a_s, l_s, acc_s, *, tq, tk, rc, pv_rows, n_kv,
                      lambda_init):
    m_s[...] = jnp.full_like(m_s, -jnp.inf)
    l_s[...] = jnp.zeros_like(l_s)
    acc_s[...] = jnp.zeros_like(acc_s)
    q = q_ref[...]
    qs = (q[:, :HEAD_DIM], q[:, HEAD_DIM:])

    def scores(j, slot):
        off = pl.multiple_of(j * tk, tk)
        kb = k_ref[pl.ds(off, tk), :]
        for c in range(2):
            kc = kb[:, c * HEAD_DIM:(c + 1) * HEAD_DIM]
            s_buf[slot, c * tq:(c + 1) * tq, :] = lax.dot_general(
                qs[c], kc, (((1,), (1,)), ((), ())), preferred_element_type=F32)

    def update(j, slot):
        off = pl.multiple_of(j * tk, tk)
        vb = v_ref[pl.ds(off, tk), :]
        for g0 in range(0, 2 * tq, pv_rows):
            for r0 in range(g0, g0 + pv_rows, rc):
                rows = slice(r0, r0 + rc)
                sc = s_buf[slot, rows, :]
                m_old = m_s[rows, :]
                m_new = jnp.maximum(m_old, jnp.max(sc, axis=-1, keepdims=True))
                alpha = jnp.exp2(m_old - m_new)
                p = jnp.exp2(sc - m_new)
                l_s[rows, :] = alpha * l_s[rows, :] + sum(
                    p[:, t * LANES:(t + 1) * LANES] for t in range(tk // LANES))
                p_buf[rows, :] = p.astype(BF16)
                m_s[rows, :] = m_new
                a_s[rows, :] = alpha
            grp = slice(g0, g0 + pv_rows)
            acc_s[grp, :] = a_s[grp, :] * acc_s[grp, :] + jnp.dot(
                p_buf[grp, :], vb, preferred_element_type=F32)

    scores(0, 0)

    def kv_pair(jj, _):
        j = 2 * jj
        scores(j + 1, 1)
        update(j, 0)
        scores(jnp.minimum(j + 2, n_kv - 1), 0)
        update(j + 1, 1)
        return 0

    lax.fori_loop(0, n_kv // 2, kv_pair, 0)
    lam = (jnp.exp(jnp.sum(lq1_ref[...] * lk1_ref[...], axis=-1, keepdims=True))
           - jnp.exp(jnp.sum(lq2_ref[...] * lk2_ref[...], axis=-1, keepdims=True)) + lambda_init)
    l = jnp.sum(l_s[...], axis=-1, keepdims=True)
    o = acc_s[0:tq] / l[0:tq] - lam * (acc_s[tq:2 * tq] / l[tq:2 * tq])
    o_ref[...] = (_rms(o, g_ref[...]) * (1.0 - lambda_init)).astype(o_ref.dtype)


def _diff_attention(qkv, lq1, lk1, lq2, lk2, subln_g, lambda_init, *, tq=512, tk=1024, rc=32):
    s = qkv.shape[0]
    vd = 2 * HEAD_DIM
    tq = min(tq, s)
    tk = min(tk, s // 2)
    assert (s // tk) % 2 == 0
    kern = functools.partial(_diff_attn_kernel, tq=tq, tk=tk, rc=rc, pv_rows=tq, n_kv=s // tk,
                             lambda_init=lambda_init)
    vec = lambda a: a.reshape(1, -1)
    vspec = lambda w: pl.BlockSpec((1, w), lambda h, i: (0, 0))
    return pl.pallas_call(
        kern,
        out_shape=jax.ShapeDtypeStruct((s, DIFF_HEADS * vd), BF16),
        grid=(DIFF_HEADS, s // tq),
        in_specs=[
            pl.BlockSpec((tq, vd), lambda h, i: (i, h)),
            pl.BlockSpec((s, vd), lambda h, i: (0, DIFF_HEADS + h)),
            pl.BlockSpec((s, vd), lambda h, i: (0, 2 * DIFF_HEADS + h)),
            vspec(HEAD_DIM), vspec(HEAD_DIM), vspec(HEAD_DIM), vspec(HEAD_DIM), vspec(vd),
        ],
        out_specs=pl.BlockSpec((tq, vd), lambda h, i: (i, h)),
        scratch_shapes=[pltpu.VMEM((2, 2 * tq, tk), F32), pltpu.VMEM((2 * tq, tk), BF16),
                        pltpu.VMEM((2 * tq, 1), F32), pltpu.VMEM((2 * tq, 1), F32),
                        pltpu.VMEM((2 * tq, LANES), F32), pltpu.VMEM((2 * tq, vd), F32)],
        compiler_params=_params("parallel", "arbitrary"),
        name="diff_attention",
    )(qkv, qkv, qkv, vec(lq1), vec(lk1), vec(lq2), vec(lk2), vec(subln_g))


def _pro_identity(rows, vecs):
    return rows[0]


def _win_attn_kernel(sink_ref, q_ref, k_ref, v_ref, o_ref, *, tq, win, s_len, group):
    kvh = pl.program_id(0)
    i = pl.program_id(1)
    start = jnp.clip(i * tq - WINDOW, 0, s_len - win)
    start = pl.multiple_of(start, WINDOW)
    kw = k_ref[pl.ds(start, win), :]
    vw = v_ref[pl.ds(start, win), :]
    qpos = i * tq + lax.broadcasted_iota(jnp.int32, (tq, win), 0)
    kpos = start + lax.broadcasted_iota(jnp.int32, (tq, win), 1)
    valid = jnp.abs(kpos - qpos) <= WINDOW
    for g in range(group):
        qg = q_ref[:, g * HEAD_DIM:(g + 1) * HEAD_DIM]
        sc = lax.dot_general(qg, kw, (((1,), (1,)), ((), ())), preferred_element_type=F32)
        sc = jnp.where(valid, sc, NEG_INF)
        sink = sink_ref[kvh * group + g] * math.log2(math.e)
        m = jnp.maximum(jnp.max(sc, axis=-1, keepdims=True), sink)
        e = jnp.exp2(sc - m)
        den = jnp.sum(e, axis=-1, keepdims=True) + jnp.exp2(sink - m)
        o = jnp.dot(e.astype(BF16), vw, preferred_element_type=F32) / den
        o_ref[:, g * HEAD_DIM:(g + 1) * HEAD_DIM] = o.astype(o_ref.dtype)


def _window_attention(qkv, sink, *, tq=256):
    s = qkv.shape[0]
    group = WIN_Q_HEADS // WIN_KV_HEADS
    tq = min(tq, s)
    win = min(tq + 2 * WINDOW, s)
    qw = group * HEAD_DIM
    k0 = WIN_Q_HEADS
    v0 = WIN_Q_HEADS + WIN_KV_HEADS
    kern = functools.partial(_win_attn_kernel, tq=tq, win=win, s_len=s, group=group)
    return pl.pallas_call(
        kern,
        out_shape=jax.ShapeDtypeStruct((s, WIN_Q_HEADS * HEAD_DIM), BF16),
        grid=(WIN_KV_HEADS, s // tq),
        in_specs=[
            pl.BlockSpec(memory_space=pltpu.SMEM),
            pl.BlockSpec((tq, qw), lambda h, i: (i, h)),
            pl.BlockSpec((s, HEAD_DIM), lambda h, i: (0, k0 + h)),
            pl.BlockSpec((s, HEAD_DIM), lambda h, i: (0, v0 + h)),
        ],
        out_specs=pl.BlockSpec((tq, qw), lambda h, i: (i, h)),
        compiler_params=_params("parallel", "arbitrary"),
        name="window_attention",
    )(sink.astype(F32), qkv, qkv, qkv)


def _s5_discretize(a_re, a_im, log_dt, b_re, b_im):
    dt = jnp.exp(log_dt)[:, None]
    mag = jnp.exp(dt * a_re)
    lr, li = mag * jnp.cos(dt * a_im), mag * jnp.sin(dt * a_im)
    den = a_re * a_re + a_im * a_im
    nr, ni = lr - 1.0, li
    fr = (nr * a_re + ni * a_im) / den
    fi = (ni * a_re - nr * a_im) / den
    bbr = fr[..., None] * b_re - fi[..., None] * b_im
    bbi = fr[..., None] * b_im + fi[..., None] * b_re
    return lr, li, bbr, bbi


def _s5_block_weights(lr, li, bbr, bbi, c_re, c_im, gpc):
    g, n, c = bbr.shape
    n_k = g // gpc
    eye = jnp.eye(gpc, dtype=F32)

    def w_in(bb):
        t = bb.reshape(n_k, gpc, n, c)
        return jnp.einsum('kgnc,gh->kgchn', t, eye).reshape(n_k, gpc * c, gpc * n)

    def w_out(cc):
        t = cc.reshape(n_k, gpc, c, n)
        return jnp.einsum('kgcn,gh->khngc', t, eye).reshape(n_k, gpc * n, gpc * c)

    win = jnp.concatenate([w_in(bbr), w_in(bbi)], axis=2)
    wout = jnp.concatenate([w_out(c_re), -w_out(c_im)], axis=1)
    lam = jnp.stack([lr.reshape(n_k, gpc * n), li.reshape(n_k, gpc * n)], axis=1)
    return win, wout, lam


def _s5_kernel(uf_ref, ub_ref, win_ref, wout_ref, lam_ref, yf_ref, yb_ref,
               xf_s, xb_s, st_s, *, tc, ns, cw, nsub):
    i = pl.program_id(1)

    @pl.when(i == 0)
    def _():
        st_s[...] = jnp.zeros_like(st_s)

    for c in range(nsub):
        cols = slice(c * cw, (c + 1) * cw)
        xf_s[c] = jnp.dot(uf_ref[:, cols].astype(BF16), win_ref[0, c].astype(BF16),
                          preferred_element_type=F32)
        xb_s[c] = jnp.dot(ub_ref[:, cols].astype(BF16), win_ref[1, c].astype(BF16),
                          preferred_element_type=F32)
    lam = [[(lam_ref[d, c, 0:1, :], lam_ref[d, c, 1:2, :]) for c in range(nsub)]
           for d in range(2)]

    def step(x_s, c, row, lr, li, sr, si):
        nr = lr * sr - li * si + x_s[c, pl.ds(row, 1), 0:ns]
        ni = lr * si + li * sr + x_s[c, pl.ds(row, 1), ns:2 * ns]
        x_s[c, pl.ds(row, 1), 0:ns] = nr
        x_s[c, pl.ds(row, 1), ns:2 * ns] = ni
        return nr, ni

    def body(r, carry):
        out = []
        for c in range(nsub):
            srf, sif, srb, sib = carry[4 * c:4 * c + 4]
            srf, sif = step(xf_s, c, r, *lam[0][c], srf, sif)
            srb, sib = step(xb_s, c, tc - 1 - r, *lam[1][c], srb, sib)
            out += [srf, sif, srb, sib]
        return tuple(out)

    init = tuple(st_s[j:j + 1, :] for j in range(4 * nsub))
    final = lax.fori_loop(0, tc, body, init, unroll=8)
    for j in range(4 * nsub):
        st_s[j:j + 1, :] = final[j]
    for c in range(nsub):
        cols = slice(c * cw, (c + 1) * cw)
        yf_ref[:, cols] = jnp.dot(xf_s[c].astype(BF16), wout_ref[0, c].astype(BF16),
                                  preferred_element_type=F32)
        yb_ref[:, cols] = jnp.dot(xb_s[c].astype(BF16), wout_ref[1, c].astype(BF16),
                                  preferred_element_type=F32)


def _s5_scan(u, win, wout, lam, *, tc=256, nsub=2):
    s, w = u.shape
    _, n_k, cw, ns2 = win.shape
    ns = ns2 // 2
    tc = min(tc, s)
    n_t = s // tc
    nsub = min(nsub, n_k)
    bw = nsub * cw
    kern = functools.partial(_s5_kernel, tc=tc, ns=ns, cw=cw, nsub=nsub)
    return pl.pallas_call(
        kern,
        out_shape=(jax.ShapeDtypeStruct((s, w), F32), jax.ShapeDtypeStruct((s, w), F32)),
        grid=(n_k // nsub, n_t),
        in_specs=[
            pl.BlockSpec((tc, bw), lambda k, i: (i, k)),
            pl.BlockSpec((tc, bw), lambda k, i: (n_t - 1 - i, k)),
            pl.BlockSpec((2, nsub, cw, ns2), lambda k, i: (0, k, 0, 0)),
            pl.BlockSpec((2, nsub, ns2, cw), lambda k, i: (0, k, 0, 0)),
            pl.BlockSpec((2, nsub, 2, ns), lambda k, i: (0, k, 0, 0)),
        ],
        out_specs=(pl.BlockSpec((tc, bw), lambda k, i: (i, k)),
                   pl.BlockSpec((tc, bw), lambda k, i: (n_t - 1 - i, k))),
        scratch_shapes=[pltpu.VMEM((nsub, tc, ns2), F32), pltpu.VMEM((nsub, tc, ns2), F32),
                        pltpu.VMEM((4 * nsub, ns), F32)],
        compiler_params=_params("parallel", "arbitrary"),
        name="s5_scan",
    )(u, u, win, wout, lam)


def _pro_s5_glu(rows, vecs):
    yf, yb, u = rows
    return _gelu_tanh(yf + yb + vecs[0] * u)


def _epi_glu(acc, epi_refs, o_ref, j, a32_s):
    o_ref[...] = (a32_s[...] * _sigmoid(acc)).astype(o_ref.dtype)


def _route(x, g, w_r):
    h = _rms(x, g)
    logits = jnp.dot(h.astype(BF16), w_r, preferred_element_type=F32)
    tm = logits.shape[0]
    lane = lax.broadcasted_iota(jnp.int32, (tm, LANES), 1)
    big = jnp.int32(LANES)
    ninf = -jnp.inf
    gl = jnp.where(lane < MOE_GROUPS, logits, ninf)
    gm = jnp.max(gl, axis=-1, keepdims=True)
    ge = jnp.exp(gl - gm)
    g_prob = ge / jnp.sum(ge, axis=-1, keepdims=True)
    g_p = jnp.max(g_prob, axis=-1, keepdims=True)
    g_idx = jnp.min(jnp.where(g_prob == g_p, lane, big), axis=-1, keepdims=True)
    lo = MOE_GROUPS + g_idx * EXPERTS_PER_GROUP
    in_grp = (lane >= lo) & (lane < lo + EXPERTS_PER_GROUP)
    el = jnp.where(in_grp, logits, ninf)
    em = jnp.max(el, axis=-1, keepdims=True)
    ee = jnp.exp(el - em)
    e_prob = jnp.where(in_grp, ee / jnp.sum(ee, axis=-1, keepdims=True), -1.0)
    p1 = jnp.max(e_prob, axis=-1, keepdims=True)
    i1 = jnp.min(jnp.where(e_prob == p1, lane, big), axis=-1, keepdims=True)
    rest = jnp.where(lane == i1, -1.0, e_prob)
    p2 = jnp.max(rest, axis=-1, keepdims=True)
    i2 = jnp.min(jnp.where(rest == p2, lane, big), axis=-1, keepdims=True)
    denom = p1 + p2
    w1 = g_p * (p1 / denom)
    w2 = g_p * (p2 / denom)
    idx = jnp.where(lane == 0, i1 - MOE_GROUPS, jnp.where(lane == 1, i2 - MOE_GROUPS, 0))
    wt = jnp.where(lane == 0, w1, jnp.where(lane == 1, w2, 0.0))
    return h, idx, wt


def _router_kernel(x_ref, g_ref, w_ref, h_ref, idx_ref, wt_ref):
    h, idx, wt = _route(x_ref[...], g_ref[...], w_ref[...])
    h_ref[...] = h
    idx_ref[...] = idx
    wt_ref[...] = wt


def _router(x, g, w_group, w_expert, *, tm=512):
    s, d = x.shape
    tm = min(tm, s)
    w_r = jnp.concatenate(
        [w_group, w_expert, jnp.zeros((d, LANES - MOE_GROUPS - N_EXPERTS), F32)], axis=1)
    return pl.pallas_call(
        _router_kernel,
        out_shape=(jax.ShapeDtypeStruct((s, d), F32), jax.ShapeDtypeStruct((s, LANES), jnp.int32),
                   jax.ShapeDtypeStruct((s, LANES), F32)),
        grid=(s // tm,),
        in_specs=[pl.BlockSpec((tm, d), lambda i: (i, 0)), pl.BlockSpec((1, d), lambda i: (0, 0)),
                  pl.BlockSpec((d, LANES), lambda i: (0, 0))],
        out_specs=(pl.BlockSpec((tm, d), lambda i: (i, 0)), pl.BlockSpec((tm, LANES), lambda i: (i, 0)),
                   pl.BlockSpec((tm, LANES), lambda i: (i, 0))),
        compiler_params=_params("parallel"),
        name="moe_router",
    )(x, g.reshape(1, d), w_r.astype(BF16))


def _rank_kernel(idx_ref, dest_ref, cnt_ref, tot_s, pst_s, run_s, *, tm):
    ph = pl.program_id(0)
    i = pl.program_id(1)
    lane = lax.broadcasted_iota(jnp.int32, (tm, LANES), 1)
    idx = idx_ref[...]
    oh0 = (lane == idx[:, 0:1]).astype(F32)
    oh1 = (lane == idx[:, 1:2]).astype(F32)
    c = oh0 + oh1
    csum = jnp.sum(c, axis=0, keepdims=True)

    @pl.when(jnp.logical_and(ph == 0, i == 0))
    def _():
        tot_s[...] = jnp.zeros_like(tot_s)

    @pl.when(ph == 0)
    def _():
        tot_s[...] += csum

    @pl.when(jnp.logical_and(ph == 1, i == 0))
    def _():
        counts = tot_s[...]
        nblk = jnp.floor((counts + (EXPERT_BLOCK - 1)) * (1.0 / EXPERT_BLOCK))
        r = lax.broadcasted_iota(jnp.int32, (LANES, LANES), 0)
        cc = lax.broadcasted_iota(jnp.int32, (LANES, LANES), 1)
        upper = (r < cc).astype(F32)
        excl = jnp.dot(jnp.broadcast_to(nblk, (SUBLANES, LANES)), upper,
                       preferred_element_type=F32, precision=lax.Precision.HIGHEST)
        pst_s[...] = excl[0:1, :] * EXPERT_BLOCK
        run_s[...] = jnp.zeros_like(run_s)
        cnt_ref[...] = jnp.broadcast_to(counts, cnt_ref.shape)

    @pl.when(ph == 1)
    def _():
        rr = lax.broadcasted_iota(jnp.int32, (tm, tm), 0)
        cr = lax.broadcasted_iota(jnp.int32, (tm, tm), 1)
        lower = (rr > cr).astype(BF16)
        before = jnp.dot(lower, c.astype(BF16), preferred_element_type=F32) + run_s[...]
        base = pst_s[...] + before
        d0 = jnp.sum(oh0 * base, axis=-1, keepdims=True)
        d1 = jnp.sum(oh1 * (base + oh0), axis=-1, keepdims=True)
        dest_ref[...] = jnp.where(lane == 0, d0, jnp.where(lane == 1, d1, 0.0)).astype(jnp.int32)
        run_s[...] += csum


def _dispatch_rank(idx, *, tm=512):
    s = idx.shape[0]
    tm = min(tm, s)
    kern = functools.partial(_rank_kernel, tm=tm)
    return pl.pallas_call(
        kern,
        out_shape=(jax.ShapeDtypeStruct((s, LANES), jnp.int32),
                   jax.ShapeDtypeStruct((SUBLANES, LANES), F32)),
        grid=(2, s // tm),
        in_specs=[pl.BlockSpec((tm, LANES), lambda ph, i: (i, 0))],
        out_specs=(pl.BlockSpec((tm, LANES), lambda ph, i: (i * ph, 0)),
                   pl.BlockSpec((SUBLANES, LANES), lambda ph, i: (0, 0))),
        scratch_shapes=[pltpu.VMEM((1, LANES), F32) for _ in range(3)],
        compiler_params=_params("arbitrary", "arbitrary"),
        name="moe_rank",
    )(idx)


def _block_experts(counts, n_blk):
    cnt = counts[0, :N_EXPERTS].astype(jnp.int32)
    padded = ((cnt + EXPERT_BLOCK - 1) // EXPERT_BLOCK) * EXPERT_BLOCK
    pends = jnp.cumsum(padded)
    blk_start = jnp.arange(n_blk, dtype=jnp.int32) * EXPERT_BLOCK
    owner = jnp.sum((pends[None, :] <= blk_start[:, None]).astype(jnp.int32), axis=1)
    ids = jnp.arange(N_EXPERTS, dtype=jnp.int32)
    last_e = jnp.max(jnp.where(cnt > 0, ids, 0))
    n_used = (pends[-1] // EXPERT_BLOCK).reshape(1)
    blk_e = jnp.minimum(owner, last_e).astype(jnp.int32)
    later = jnp.logical_and(ids[None, :] > ids[:, None], cnt[None, :] > 0)
    nxt = jnp.min(jnp.where(later, ids[None, :], N_EXPERTS), axis=1)
    nxt = jnp.where(nxt == N_EXPERTS, ids, nxt).astype(jnp.int32)
    pad_start = (pends - padded + cnt).astype(jnp.int32)
    pad_len = (padded - cnt).astype(jnp.int32)
    return blk_e, nxt[blk_e], n_used.astype(jnp.int32), pad_start, pad_len


def _scatter_kernel(dest_ref, ps_ref, pl_ref, nu_ref, h_ref, xs_hbm, h_s, z_s, sem, zsem,
                    *, tb, n_b, n_blk):
    b = pl.program_id(0)
    slot = b % 2
    n = tb * TOP_K
    bits = [1 << k for k in reversed(range(3, (EXPERT_BLOCK - 1).bit_length()))]

    def zero_fill(action):
        def pad_run(e, _):
            start = ps_ref[e]
            head = (-start) & (SUBLANES - 1)
            for r in range(SUBLANES - 1):
                @pl.when(r < head)
                def _():
                    action(pltpu.make_async_copy(z_s.at[pl.ds(0, 1), :],
                                                 xs_hbm.at[pl.ds(start + r, 1), :], zsem.at[0]))
            astart = start + head
            length = pl_ref[e] - head
            for bit in bits:
                @pl.when((length & bit) != 0)
                def _():
                    off = pl.multiple_of(astart + (length & ~(2 * bit - 1)), SUBLANES)
                    action(pltpu.make_async_copy(z_s.at[pl.ds(0, bit), :],
                                                 xs_hbm.at[pl.ds(off, bit), :], zsem.at[0]))
            return 0

        def unused_block(bb, _):
            row0 = pl.multiple_of(bb * EXPERT_BLOCK, EXPERT_BLOCK)
            action(pltpu.make_async_copy(
                z_s, xs_hbm.at[pl.ds(row0, EXPERT_BLOCK), :], zsem.at[0]))
            return 0

        lax.fori_loop(0, N_EXPERTS, pad_run, 0)
        lax.fori_loop(nu_ref[0], n_blk, unused_block, 0)

    @pl.when(b == 0)
    def _():
        z_s[...] = jnp.zeros_like(z_s)
        zero_fill(lambda c: c.start())

    def copy(r, row, sl):
        return pltpu.make_async_copy(h_s.at[sl, pl.ds(r, 1), :], xs_hbm.at[pl.ds(row, 1), :],
                                     sem.at[sl])

    def wait_all(sl):
        def body(r, _):
            copy(0, 0, sl).wait()
            return 0
        lax.fori_loop(0, n, body, 0, unroll=8)

    h_s[slot] = h_ref[...]
    for r in range(tb):
        tok = b * tb + r
        for k in range(TOP_K):
            copy(r, dest_ref[tok * TOP_K + k], slot).start()

    @pl.when(b > 0)
    def _():
        wait_all(1 - slot)

    @pl.when(b == n_b - 1)
    def _():
        wait_all(slot)
        zero_fill(lambda c: c.wait())


def _dispatch_scatter(h, dest_flat, pad_start, pad_len, n_used, n_rows, *, tb=128):
    s, d = h.shape
    tb = min(tb, s)
    n_b = s // tb
    kern = functools.partial(_scatter_kernel, tb=tb, n_b=n_b, n_blk=n_rows // EXPERT_BLOCK)
    grid_spec = pltpu.PrefetchScalarGridSpec(
        num_scalar_prefetch=4,
        grid=(n_b,),
        in_specs=[pl.BlockSpec((tb, d), lambda b, *_: (b, 0))],
        out_specs=pl.BlockSpec(memory_space=pl.ANY),
        scratch_shapes=[pltpu.VMEM((2, tb, d), h.dtype), pltpu.VMEM((EXPERT_BLOCK, d), h.dtype),
                        pltpu.SemaphoreType.DMA((2,)), pltpu.SemaphoreType.DMA((1,))],
    )
    return pl.pallas_call(
        kern,
        out_shape=jax.ShapeDtypeStruct((n_rows, d), h.dtype),
        grid_spec=grid_spec,
        compiler_params=_params("arbitrary"),
        name="moe_scatter",
    )(dest_flat, pad_start, pad_len, n_used, h)


def _row_copy(src_hbm, src_row, dst_buf, slot, dst_row, sem):
    return pltpu.make_async_copy(src_hbm.at[pl.ds(src_row, 1), :],
                                 dst_buf.at[slot, pl.ds(dst_row, 1), :], sem.at[slot])


def _gather_start(idx_ref, base, n, src_hbm, dst_buf, slot, sem):
    for r in range(n):
        _row_copy(src_hbm, idx_ref[base + r], dst_buf, slot, r, sem).start()


def _gather_wait(n, src_hbm, dst_buf, slot, sem):
    def body(r, _):
        _row_copy(src_hbm, 0, dst_buf, slot, r, sem).wait()
        return 0
    lax.fori_loop(0, n, body, 0, unroll=8)


def _expert_kernel(be_ref, nx_ref, nu_ref, x_ref, wg_hbm, wu_hbm, wd_hbm, y_ref,
                   wg_f, wu_f, wd_f, wg_s, wu_s, wd_s, sem, *, layer):
    b = pl.program_id(0)
    e = be_ref[b]

    def fetch(expert):
        return (pltpu.make_async_copy(wg_hbm.at[layer, expert], wg_f, sem.at[0]),
                pltpu.make_async_copy(wu_hbm.at[layer, expert], wu_f, sem.at[1]),
                pltpu.make_async_copy(wd_hbm.at[layer, expert], wd_f, sem.at[2]))

    @pl.when(b == 0)
    def _():
        for c in fetch(e):
            c.start()

    changed = jnp.logical_or(b == 0, e != be_ref[jnp.maximum(b - 1, 0)])

    @pl.when(changed)
    def _():
        for c in fetch(e):
            c.wait()
        wg_s[...] = wg_f[...].astype(BF16)
        wu_s[...] = wu_f[...].astype(BF16)
        wd_s[...] = wd_f[...].astype(BF16)

        @pl.when(nx_ref[b] != e)
        def _():
            for c in fetch(nx_ref[b]):
                c.start()

    @pl.when(b >= nu_ref[0])
    def _():
        y_ref[...] = jnp.zeros_like(y_ref)

    @pl.when(b < nu_ref[0])
    def _():
        x = x_ref[...].astype(BF16)
        hg = jnp.dot(x, wg_s[...], preferred_element_type=F32)
        hu = jnp.dot(x, wu_s[...], preferred_element_type=F32)
        hdn = (hg * _sigmoid(hg)) * hu
        y_ref[...] = jnp.dot(hdn.astype(BF16), wd_s[...], preferred_element_type=F32)


def _expert_mlp(xs, blk_e, blk_next, n_used, layer, w_gate, w_up, w_down):
    n_rows, d = xs.shape
    de = w_gate.shape[3]
    grid_spec = pltpu.PrefetchScalarGridSpec(
        num_scalar_prefetch=3,
        grid=(n_rows // EXPERT_BLOCK,),
        in_specs=[
            pl.BlockSpec((EXPERT_BLOCK, d),
                         lambda b, be, nx, nu: (jnp.minimum(b, nu[0] - 1), 0)),
            pl.BlockSpec(memory_space=pl.ANY),
            pl.BlockSpec(memory_space=pl.ANY),
            pl.BlockSpec(memory_space=pl.ANY),
        ],
        out_specs=pl.BlockSpec((EXPERT_BLOCK, d), lambda b, be, nx, nu: (b, 0)),
        scratch_shapes=[pltpu.VMEM((d, de), F32), pltpu.VMEM((d, de), F32),
                        pltpu.VMEM((de, d), F32), pltpu.VMEM((d, de), BF16),
                        pltpu.VMEM((d, de), BF16), pltpu.VMEM((de, d), BF16),
                        pltpu.SemaphoreType.DMA((3,))],
    )
    return pl.pallas_call(
        functools.partial(_expert_kernel, layer=layer),
        out_shape=jax.ShapeDtypeStruct((n_rows, d), F32),
        grid_spec=grid_spec,
        compiler_params=_params("arbitrary"),
        name="moe_experts",
    )(blk_e, blk_next, n_used, xs, w_gate, w_up, w_down)


def _combine_kernel(dest_ref, x_ref, wt_ref, ys_hbm, g_ref, o_ref, ybuf, sem, *, tb, n_b, final):
    b = pl.program_id(0)
    slot = b % 2
    n = tb * TOP_K

    @pl.when(b == 0)
    def _():
        _gather_start(dest_ref, 0, n, ys_hbm, ybuf, 0, sem)

    _gather_wait(n, ys_hbm, ybuf, slot, sem)
    _gather_start(dest_ref, (b + 1) * n, n, ys_hbm, ybuf, 1 - slot, sem)
    y0 = ybuf[slot, 0:tb, :]
    y1 = ybuf[slot, tb:2 * tb, :]
    wt = wt_ref[...]
    out = x_ref[...] + (y0 * wt[:, 0:1] + y1 * wt[:, 1:2])
    if final:
        out = _rms(out, g_ref[...])
    o_ref[...] = out

    @pl.when(b == n_b - 1)
    def _():
        _gather_wait(n, ys_hbm, ybuf, 1 - slot, sem)


def _moe_combine(x, wt, ys, dest, g_final, *, final, tb=128):
    s, d = x.shape
    tb = min(tb, s)
    n_b = s // tb
    dest = dest.reshape(n_b, tb, TOP_K).transpose(0, 2, 1).reshape(-1)
    dest = jnp.concatenate([dest, jnp.zeros((tb * TOP_K,), jnp.int32)])
    kern = functools.partial(_combine_kernel, tb=tb, n_b=n_b, final=final)
    grid_spec = pltpu.PrefetchScalarGridSpec(
        num_scalar_prefetch=1,
        grid=(n_b,),
        in_specs=[
            pl.BlockSpec((tb, d), lambda b, dr: (b, 0)),
            pl.BlockSpec((tb, LANES), lambda b, dr: (b, 0)),
            pl.BlockSpec(memory_space=pl.ANY),
            pl.BlockSpec((1, d), lambda b, dr: (0, 0)),
        ],
        out_specs=pl.BlockSpec((tb, d), lambda b, dr: (b, 0)),
        scratch_shapes=[pltpu.VMEM((2, tb * TOP_K, d), F32), pltpu.SemaphoreType.DMA((2,))],
    )
    return pl.pallas_call(
        kern,
        out_shape=jax.ShapeDtypeStruct((s, d), F32),
        grid_spec=grid_spec,
        compiler_params=_params("arbitrary"),
        name="moe_combine",
    )(dest, x, wt, ys, g_final.reshape(1, d))


def _hier_moe(x, ln_g, w_group, w_expert, layer, w_gate, w_up, w_down, g_final, *, final):
    s = x.shape[0]
    h, idx, wt = _router(x, ln_g, w_group, w_expert)
    dest, counts = _dispatch_rank(idx)
    dest = dest[:, :TOP_K].reshape(-1)
    n_rows = -(-(s * TOP_K + N_EXPERTS * (EXPERT_BLOCK - 1)) // EXPERT_BLOCK) * EXPERT_BLOCK
    blk_e, blk_next, n_used, pad_start, pad_len = _block_experts(counts, n_rows // EXPERT_BLOCK)
    xs = _dispatch_scatter(h, dest, pad_start, pad_len, n_used, n_rows)
    ys = _expert_mlp(xs, blk_e, blk_next, n_used, layer, w_gate, w_up, w_down)
    return _moe_combine(x, wt, ys, dest, g_final, final=final)


def _rglru_layer(x, ln_g, w_in, conv_w, conv_b, w_a, w_x, b_a, b_x, lam, w_out):
    d = x.shape[1]
    c = conv_w.shape[1]
    proj = _norm_matmul(x, ln_g, w_in, name="lru_in_proj")
    yf, yb = _rglru_scan(proj, conv_w, conv_b, w_a, w_x, b_a, b_x, lam)
    return _matmul_residual([(yf, c, 0), (yb, c, 0), (proj, c, 0)], [], w_out, x, _pro_rglru_out,
                            k_dim=c, tm=512, name="lru_out_proj")


def _diff_layer(x, ln_g, rope, w_qkv, lq1, lk1, lq2, lk2, subln_g, w_out, lambda_init):
    qk_cols = 2 * DIFF_HEADS * 2 * HEAD_DIM
    qkv = _norm_matmul(x, ln_g, w_qkv, rope=rope, n_rope_cols=qk_cols, n_q_cols=qk_cols // 2,
                       q_scale=HEAD_DIM ** -0.5 * math.log2(math.e), out_dtype=BF16,
                       name="diff_qkv_proj")
    o = _diff_attention(qkv, lq1, lk1, lq2, lk2, subln_g, lambda_init)
    return _matmul_residual([(o, o.shape[1], 0)], [], w_out, x, _pro_identity,
                            k_dim=o.shape[1], name="diff_out_proj")


def _window_layer(x, ln_g, rope, w_qkv, sink, w_out):
    qk_cols = (WIN_Q_HEADS + WIN_KV_HEADS) * HEAD_DIM
    qkv = _norm_matmul(x, ln_g, w_qkv, rope=rope, n_rope_cols=qk_cols,
                       n_q_cols=WIN_Q_HEADS * HEAD_DIM,
                       q_scale=HEAD_DIM ** -0.5 * math.log2(math.e), out_dtype=BF16, tn=512,
                       name="win_qkv_proj")
    o = _window_attention(qkv, sink)
    return _matmul_residual([(o, o.shape[1], 0)], [], w_out, x, _pro_identity,
                            k_dim=o.shape[1], name="win_out_proj")


def _s5_layer(x, ln_g, w_in, a_re, a_im, log_dt, b_re, b_im, c_re, c_im, d_skip, w_glu, w_out,
              *, gpc=16):
    u = _norm_matmul(x, ln_g, w_in, name="s5_in_proj")
    w = u.shape[1]
    wins, wouts, lams = [], [], []
    for dd in range(2):
        lr, li, bbr, bbi = _s5_discretize(a_re[dd], a_im[dd], log_dt[dd], b_re[dd], b_im[dd])
        wi, wo, lm = _s5_block_weights(lr, li, bbr, bbi, c_re[dd], c_im[dd], gpc)
        wins.append(wi)
        wouts.append(wo)
        lams.append(lm)
    yf, yb = _s5_scan(u, jnp.stack(wins).astype(BF16), jnp.stack(wouts).astype(BF16),
                      jnp.stack(lams))
    z = _fused_mm([(yf, w, 0), (yb, w, 0), (u, w, 0)], [d_skip.reshape(1, w)], w_glu, [],
                  _pro_s5_glu, _epi_glu, out_dtype=BF16, tm=512, tn=w, k_dim=w, keep_f32=True,
                  name="s5_glu")
    return _matmul_residual([(z, w, 0)], [], w_out, x, _pro_identity, k_dim=w, name="s5_out_proj")


def kernel(x, positions, ln_mix, ln_ffn, ln_final, lru_w_in, lru_conv_w, lru_conv_b, lru_w_a, lru_w_x, lru_b_a, lru_b_x, lru_lambda, lru_w_out, diff_w_qkv, diff_lq1, diff_lk1, diff_lq2, diff_lk2, diff_subln, diff_w_out, win_w_qkv, win_sink, win_w_out, s5_w_in, s5_a_re, s5_a_im, s5_log_dt, s5_b_re, s5_b_im, s5_c_re, s5_c_im, s5_d, s5_w_glu, s5_w_out, moe_w_group, moe_w_expert, moe_w_gate, moe_w_up, moe_w_down):
    batch, s, d = x.shape
    depth = ln_mix.shape[0]
    outs = []
    for b in range(batch):
        xb = x[b]
        rope = _rope_tables(positions[b])
        for i in range(depth):
            kind, j = i % 4, i // 4
            if kind == 0:
                xb = _rglru_layer(xb, ln_mix[i], lru_w_in[j], lru_conv_w[j], lru_conv_b[j],
                                  lru_w_a[j], lru_w_x[j], lru_b_a[j], lru_b_x[j], lru_lambda[j],
                                  lru_w_out[j])
            elif kind == 1:
                xb = _diff_layer(xb, ln_mix[i], rope, diff_w_qkv[j], diff_lq1[j], diff_lk1[j],
                                 diff_lq2[j], diff_lk2[j], diff_subln[j], diff_w_out[j],
                                 0.8 - 0.6 * math.exp(-0.3 * i))
            elif kind == 2:
                xb = _window_layer(xb, ln_mix[i], rope, win_w_qkv[j], win_sink[j], win_w_out[j])
            else:
                xb = _s5_layer(xb, ln_mix[i], s5_w_in[j], s5_a_re[j], s5_a_im[j], s5_log_dt[j],
                               s5_b_re[j], s5_b_im[j], s5_c_re[j], s5_c_im[j], s5_d[j],
                               s5_w_glu[j], s5_w_out[j])
            xb = _hier_moe(xb, ln_ffn[i], moe_w_group[i], moe_w_expert[i], i, moe_w_gate,
                           moe_w_up, moe_w_down, ln_final, final=(i == depth - 1))
        outs.append(xb)
    return jnp.stack(outs)
```

```python
import functools
import math

import jax
import jax.numpy as jnp
from jax import lax
from jax.experimental import pallas as pl
from jax.experimental.pallas import tpu as pltpu

F32 = jnp.float32
BF16 = jnp.bfloat16

NORM_EPS = 1e-6
NEG_INF = -1e30
LANES = 128
SUBLANES = 8
VMEM_LIMIT = 56 * 1024 * 1024

HEAD_DIM = 128
ROT_DIM = HEAD_DIM // 4
ROPE_THETA = 500000.0
RGLRU_C = 8.0
CONV_W = 4
LRU_BLOCK_W = 128
WINDOW = 128
DIFF_HEADS = 8
WIN_Q_HEADS = 16
WIN_KV_HEADS = 4
SSM_GROUP_CH = 16
SSM_STATE = 64
MOE_GROUPS = 4
EXPERTS_PER_GROUP = 8
N_EXPERTS = MOE_GROUPS * EXPERTS_PER_GROUP
TOP_K = 2
EXPERT_BLOCK = 256


def _params(*sem):
    return pltpu.CompilerParams(dimension_semantics=sem, vmem_limit_bytes=VMEM_LIMIT)


def _rms(x, g):
    ms = jnp.mean(x * x, axis=-1, keepdims=True)
    return x * lax.rsqrt(ms + NORM_EPS) * g


def _gelu_tanh(x):
    return 0.5 * x * (1.0 + jnp.tanh(math.sqrt(2.0 / math.pi) * (x + 0.044715 * (x * x * x))))


def _sigmoid(x):
    return 1.0 / (1.0 + jnp.exp(-x))


def _fused_mm_kernel(*refs, n_row, n_vec, n_epi, prologue, epilogue, keep_f32):
    row_refs = refs[:n_row]
    vec_refs = refs[n_row:n_row + n_vec]
    w_ref = refs[n_row + n_vec]
    epi_refs = refs[n_row + n_vec + 1:n_row + n_vec + 1 + n_epi]
    o_ref = refs[n_row + n_vec + 1 + n_epi]
    a_s = refs[n_row + n_vec + 2 + n_epi]
    a32_s = refs[n_row + n_vec + 3 + n_epi] if keep_f32 else None
    j = pl.program_id(1)

    @pl.when(j == 0)
    def _():
        a = prologue([r[...] for r in row_refs], [v[...] for v in vec_refs])
        a_s[...] = a.astype(BF16)
        if keep_f32:
            a32_s[...] = a

    acc = jnp.dot(a_s[...], w_ref[...].astype(BF16), preferred_element_type=F32)
    epilogue(acc, epi_refs, o_ref, j, a32_s)


def _fused_mm(row_inputs, vec_inputs, w, epi_inputs, prologue, epilogue, *, out_dtype, tm, tn,
              k_dim, keep_f32=False, name):
    s = row_inputs[0][0].shape[0]
    n = w.shape[1]
    tm = min(tm, s)
    tn = min(tn, n)
    w = w.astype(BF16)
    in_specs = []
    args = []
    for arr, width, cb in row_inputs:
        in_specs.append(pl.BlockSpec((tm, width), lambda i, j, cb=cb: (i, cb)))
        args.append(arr)
    for arr in vec_inputs:
        in_specs.append(pl.BlockSpec(arr.shape, lambda i, j: (0, 0)))
        args.append(arr)
    in_specs.append(pl.BlockSpec((k_dim, tn), lambda i, j: (0, j)))
    args.append(w)
    for arr, width, per_tile in epi_inputs:
        if per_tile:
            in_specs.append(pl.BlockSpec((tm, width), lambda i, j: (i, j)))
        else:
            in_specs.append(pl.BlockSpec((tm, width), lambda i, j: (i, 0)))
        args.append(arr)
    scratch = [pltpu.VMEM((tm, k_dim), BF16)]
    if keep_f32:
        scratch.append(pltpu.VMEM((tm, k_dim), F32))
    kern = functools.partial(_fused_mm_kernel, n_row=len(row_inputs), n_vec=len(vec_inputs),
                             n_epi=len(epi_inputs), prologue=prologue, epilogue=epilogue,
                             keep_f32=keep_f32)
    return pl.pallas_call(
        kern,
        out_shape=jax.ShapeDtypeStruct((s, n), out_dtype),
        grid=(s // tm, n // tn),
        in_specs=in_specs,
        out_specs=pl.BlockSpec((tm, tn), lambda i, j: (i, j)),
        scratch_shapes=scratch,
        compiler_params=_params("parallel", "arbitrary"),
        name=name,
    )(*args)


def _pro_rms(rows, vecs):
    return _rms(rows[0], vecs[0])


def _epi_store(acc, epi_refs, o_ref, j, a32_s):
    o_ref[...] = acc.astype(o_ref.dtype)


def _epi_residual(acc, epi_refs, o_ref, j, a32_s):
    o_ref[...] = (epi_refs[0][...] + acc).astype(o_ref.dtype)


def _make_epi_rope(n_rope_tiles, n_q_tiles, q_scale, tn):
    def epi(acc, epi_refs, o_ref, j, a32_s):
        c_ref, s1_ref, s2_ref = epi_refs

        @pl.when(j < n_rope_tiles)
        def _():
            c = c_ref[...]
            s1 = s1_ref[...]
            s2 = s2_ref[...]
            sc = jnp.where(j < n_q_tiles, q_scale, 1.0).astype(F32)
            for hh in range(tn // HEAD_DIM):
                xs = acc[:, hh * HEAD_DIM:(hh + 1) * HEAD_DIM]
                rot = (xs * c + pltpu.roll(xs, HEAD_DIM - ROT_DIM // 2, 1) * s1
                       + pltpu.roll(xs, ROT_DIM // 2, 1) * s2) * sc
                o_ref[:, hh * HEAD_DIM:(hh + 1) * HEAD_DIM] = rot.astype(o_ref.dtype)

        @pl.when(j >= n_rope_tiles)
        def _():
            o_ref[...] = acc.astype(o_ref.dtype)

    return epi


def _rope_tables(positions):
    half = ROT_DIM // 2
    inv = ROPE_THETA ** (-jnp.arange(0, ROT_DIM, 2, dtype=F32) / ROT_DIM)
    ang = positions.astype(F32)[:, None] * inv
    cos, sin = jnp.cos(ang), jnp.sin(ang)
    s = positions.shape[0]
    ones = jnp.ones((s, HEAD_DIM - ROT_DIM), F32)
    zeros = jnp.zeros((s, HEAD_DIM - ROT_DIM), F32)
    zh = jnp.zeros((s, half), F32)
    c_tab = jnp.concatenate([cos, cos, ones], axis=1)
    s1_tab = jnp.concatenate([-sin, zh, zeros], axis=1)
    s2_tab = jnp.concatenate([zh, sin, zeros], axis=1)
    return c_tab, s1_tab, s2_tab


def _norm_matmul(x, g, w, *, rope=None, n_rope_cols=0, n_q_cols=0, q_scale=1.0, out_dtype=F32,
                 tm=1024, tn=1024, name):
    d = x.shape[1]
    if rope is None:
        epi, epi_inputs = _epi_store, []
    else:
        tn = min(tn, w.shape[1])
        assert n_rope_cols % tn == 0 and n_q_cols % tn == 0
        epi = _make_epi_rope(n_rope_cols // tn, n_q_cols // tn, q_scale, tn)
        epi_inputs = [(t, HEAD_DIM, False) for t in rope]
    return _fused_mm([(x, d, 0)], [g.reshape(1, d)], w, epi_inputs, _pro_rms, epi,
                     out_dtype=out_dtype, tm=tm, tn=tn, k_dim=d, name=name)


def _matmul_residual(row_inputs, vec_inputs, w, res, prologue, *, k_dim, tm=1024, tn=512, name):
    return _fused_mm(row_inputs, vec_inputs, w, [(res, min(tn, w.shape[1]), True)], prologue,
                     _epi_residual, out_dtype=F32, tm=tm, tn=tn, k_dim=k_dim, name=name)


def _rglru_kernel(xf_ref, xfp_ref, xfn_ref, xb_ref, xbp_ref, xbn_ref, cw_ref, cb_ref, wa_ref,
                  wx_ref, ba_ref, bx_ref, lam_ref, yf_ref, yb_ref,
                  ext_s, af_s, bf_s, ab_s, bb_s, hf_s, hb_s, *, tc, cw, n_t):
    i = pl.program_id(1)
    halo = SUBLANES

    @pl.when(i == 0)
    def _():
        hf_s[...] = jnp.zeros_like(hf_s)
        hb_s[...] = jnp.zeros_like(hb_s)

    def gates(x_ref, xp_ref, xn_ref, chunk, d, a_s, b_s):
        prev = jnp.where(chunk == 0, 0.0, xp_ref[...])
        nxt = jnp.where(chunk == n_t - 1, 0.0, xn_ref[...])
        ext_s[0:halo, :] = prev
        ext_s[halo:halo + tc, :] = x_ref[...]
        ext_s[halo + tc:halo + tc + halo, :] = nxt
        xc = cb_ref[...] + sum(
            cw_ref[k:k + 1, :] * ext_s[halo - 2 + k:halo - 2 + k + tc, :] for k in range(CONV_W))
        lam = lam_ref[d:d + 1, :]
        z = -lam
        sp = jnp.maximum(z, 0.0) + jnp.log1p(jnp.exp(-jnp.abs(z)))
        for blk in range(cw // LRU_BLOCK_W):
            sl = slice(blk * LRU_BLOCK_W, (blk + 1) * LRU_BLOCK_W)
            xb = xc[:, sl]
            xbh = xb.astype(BF16)
            r = _sigmoid(jnp.dot(xbh, wa_ref[d, blk].astype(BF16), preferred_element_type=F32)
                         + ba_ref[d:d + 1, sl])
            ig = _sigmoid(jnp.dot(xbh, wx_ref[d, blk].astype(BF16), preferred_element_type=F32)
                          + bx_ref[d:d + 1, sl])
            log_a = (-RGLRU_C) * r * sp[:, sl]
            a_s[:, sl] = jnp.exp(log_a)
            th = jnp.tanh(log_a)
            b_s[:, sl] = jnp.sqrt(-2.0 * th / (1.0 - th)) * (ig * xb)

    gates(xf_ref, xfp_ref, xfn_ref, i, 0, af_s, bf_s)
    gates(xb_ref, xbp_ref, xbn_ref, n_t - 1 - i, 1, ab_s, bb_s)

    def body(r, carry):
        hf, hb = carry
        hf = af_s[pl.ds(r, 1), :] * hf + bf_s[pl.ds(r, 1), :]
        yf_ref[pl.ds(r, 1), :] = hf
        rb = tc - 1 - r
        hb = ab_s[pl.ds(rb, 1), :] * hb + bb_s[pl.ds(rb, 1), :]
        yb_ref[pl.ds(rb, 1), :] = hb
        return hf, hb

    hf, hb = lax.fori_loop(0, tc, body, (hf_s[...], hb_s[...]), unroll=8)
    hf_s[...] = hf
    hb_s[...] = hb


def _rglru_scan(proj, conv_w, conv_b, w_a, w_x, b_a, b_x, lam, *, tc=256, cw=2048):
    s = proj.shape[0]
    c = conv_w.shape[1]
    tc = min(tc, s)
    n_t = s // tc
    n_c = c // cw
    xoff = c // cw
    hb = tc // SUBLANES
    last_h = s // SUBLANES - 1

    specs = [
        pl.BlockSpec((tc, cw), lambda ci, i: (i, xoff + ci)),
        pl.BlockSpec((SUBLANES, cw), lambda ci, i: (jnp.maximum(i * hb - 1, 0), xoff + ci)),
        pl.BlockSpec((SUBLANES, cw), lambda ci, i: (jnp.minimum((i + 1) * hb, last_h), xoff + ci)),
        pl.BlockSpec((tc, cw), lambda ci, i: (n_t - 1 - i, xoff + ci)),
        pl.BlockSpec((SUBLANES, cw),
                     lambda ci, i: (jnp.maximum((n_t - 1 - i) * hb - 1, 0), xoff + ci)),
        pl.BlockSpec((SUBLANES, cw),
                     lambda ci, i: (jnp.minimum((n_t - i) * hb, last_h), xoff + ci)),
        pl.BlockSpec((CONV_W, cw), lambda ci, i: (0, ci)),
        pl.BlockSpec((1, cw), lambda ci, i: (0, ci)),
        pl.BlockSpec((2, cw // LRU_BLOCK_W, LRU_BLOCK_W, LRU_BLOCK_W), lambda ci, i: (0, ci, 0, 0)),
        pl.BlockSpec((2, cw // LRU_BLOCK_W, LRU_BLOCK_W, LRU_BLOCK_W), lambda ci, i: (0, ci, 0, 0)),
        pl.BlockSpec((2, cw), lambda ci, i: (0, ci)),
        pl.BlockSpec((2, cw), lambda ci, i: (0, ci)),
        pl.BlockSpec((2, cw), lambda ci, i: (0, ci)),
    ]
    kern = functools.partial(_rglru_kernel, tc=tc, cw=cw, n_t=n_t)
    return pl.pallas_call(
        kern,
        out_shape=(jax.ShapeDtypeStruct((s, c), F32), jax.ShapeDtypeStruct((s, c), F32)),
        grid=(n_c, n_t),
        in_specs=specs,
        out_specs=(pl.BlockSpec((tc, cw), lambda ci, i: (i, ci)),
                   pl.BlockSpec((tc, cw), lambda ci, i: (n_t - 1 - i, ci))),
        scratch_shapes=[pltpu.VMEM((tc + 2 * SUBLANES, cw), F32)]
        + [pltpu.VMEM((tc, cw), F32) for _ in range(4)]
        + [pltpu.VMEM((1, cw), F32) for _ in range(2)],
        compiler_params=_params("parallel", "arbitrary"),
        name="rglru_scan",
    )(proj, proj, proj, proj, proj, proj, conv_w, conv_b.reshape(1, c), w_a, w_x, b_a, b_x, lam)


def _pro_rglru_out(rows, vecs):
    yf, yb, gate = rows
    return (yf + yb) * _gelu_tanh(gate)


def _diff_attn_kernel(q_ref, k_ref, v_ref, lq1_ref, lk1_ref, lq2_ref, lk2_ref, g_ref, o_ref,
                      s_buf, p_buf, m_s, a_s, l_s, acc_s, *, tq, tk, rc, pv_rows, n_kv,
                      lambda_init):
    m_s[...] = jnp.full_like(m_s, -jnp.inf)
    l_s[...] = jnp.zeros_like(l_s)
    acc_s[...] = jnp.zeros_like(acc_s)
    q = q_ref[...]
    qs = (q[:, :HEAD_DIM], q[:, HEAD_DIM:])

    def scores(j, slot):
        off = pl.multiple_of(j * tk, tk)
        kb = k_ref[pl.ds(off, tk), :]
        for c in range(2):
            kc = kb[:, c * HEAD_DIM:(c + 1) * HEAD_DIM]
            s_buf[slot, c * tq:(c + 1) * tq, :] = lax.dot_general(
                qs[c], kc, (((1,), (1,)), ((), ())), preferred_element_type=F32)

    def update(j, slot):
        off = pl.multiple_of(j * tk, tk)
        vb = v_ref[pl.ds(off, tk), :]
        for g0 in range(0, 2 * tq, pv_rows):
            for r0 in range(g0, g0 + pv_rows, rc):
                rows = slice(r0, r0 + rc)
                sc = s_buf[slot, rows, :]
                m_old = m_s[rows, :]
                m_new = jnp.maximum(m_old, jnp.max(sc, axis=-1, keepdims=True))
                alpha = jnp.exp2(m_old - m_new)
                p = jnp.exp2(sc - m_new)
                l_s[rows, :] = alpha * l_s[rows, :] + sum(
                    p[:, t * LANES:(t + 1) * LANES] for t in range(tk // LANES))
                p_buf[rows, :] = p.astype(BF16)
                m_s[rows, :] = m_new
                a_s[rows, :] = alpha
            grp = slice(g0, g0 + pv_rows)
            acc_s[grp, :] = a_s[grp, :] * acc_s[grp, :] + jnp.dot(
                p_buf[grp, :], vb, preferred_element_type=F32)

    scores(0, 0)

    def kv_pair(jj, _):
        j = 2 * jj
        scores(j + 1, 1)
        update(j, 0)
        scores(jnp.minimum(j + 2, n_kv - 1), 0)
        update(j + 1, 1)
        return 0

    lax.fori_loop(0, n_kv // 2, kv_pair, 0)
    lam = (jnp.exp(jnp.sum(lq1_ref[...] * lk1_ref[...], axis=-1, keepdims=True))
           - jnp.exp(jnp.sum(lq2_ref[...] * lk2_ref[...], axis=-1, keepdims=True)) + lambda_init)
    l = jnp.sum(l_s[...], axis=-1, keepdims=True)
    o = acc_s[0:tq] / l[0:tq] - lam * (acc_s[tq:2 * tq] / l[tq:2 * tq])
    o_ref[...] = (_rms(o, g_ref[...]) * (1.0 - lambda_init)).astype(o_ref.dtype)


def _diff_attention(qkv, lq1, lk1, lq2, lk2, subln_g, lambda_init, *, tq=512, tk=1024, rc=32):
    s = qkv.shape[0]
    vd = 2 * HEAD_DIM
    tq = min(tq, s)
    tk = min(tk, s // 2)
    assert (s // tk) % 2 == 0
    kern = functools.partial(_diff_attn_kernel, tq=tq, tk=tk, rc=rc, pv_rows=tq, n_kv=s // tk,
                             lambda_init=lambda_init)
    vec = lambda a: a.reshape(1, -1)
    vspec = lambda w: pl.BlockSpec((1, w), lambda h, i: (0, 0))
    return pl.pallas_call(
        kern,
        out_shape=jax.ShapeDtypeStruct((s, DIFF_HEADS * vd), BF16),
        grid=(DIFF_HEADS, s // tq),
        in_specs=[
            pl.BlockSpec((tq, vd), lambda h, i: (i, h)),
            pl.BlockSpec((s, vd), lambda h, i: (0, DIFF_HEADS + h)),
            pl.BlockSpec((s, vd), lambda h, i: (0, 2 * DIFF_HEADS + h)),
            vspec(HEAD_DIM), vspec(HEAD_DIM), vspec(HEAD_DIM), vspec(HEAD_DIM), vspec(vd),
        ],
        out_specs=pl.BlockSpec((tq, vd), lambda h, i: (i, h)),
        scratch_shapes=[pltpu.VMEM((2, 2 * tq, tk), F32), pltpu.VMEM((2 * tq, tk), BF16),
                        pltpu.VMEM((2 * tq, 1), F32), pltpu.VMEM((2 * tq, 1), F32),
                        pltpu.VMEM((2 * tq, LANES), F32), pltpu.VMEM((2 * tq, vd), F32)],
        compiler_params=_params("parallel", "arbitrary"),
        name="diff_attention",
    )(qkv, qkv, qkv, vec(lq1), vec(lk1), vec(lq2), vec(lk2), vec(subln_g))


def _pro_identity(rows, vecs):
    return rows[0]


def _win_attn_kernel(sink_ref, q_ref, k_ref, v_ref, o_ref, *, tq, win, s_len, group):
    kvh = pl.program_id(0)
    i = pl.program_id(1)
    start = jnp.clip(i * tq - WINDOW, 0, s_len - win)
    start = pl.multiple_of(start, WINDOW)
    kw = k_ref[pl.ds(start, win), :]
    vw = v_ref[pl.ds(start, win), :]
    qpos = i * tq + lax.broadcasted_iota(jnp.int32, (tq, win), 0)
    kpos = start + lax.broadcasted_iota(jnp.int32, (tq, win), 1)
    valid = jnp.abs(kpos - qpos) <= WINDOW
    for g in range(group):
        qg = q_ref[:, g * HEAD_DIM:(g + 1) * HEAD_DIM]
        sc = lax.dot_general(qg, kw, (((1,), (1,)), ((), ())), preferred_element_type=F32)
        sc = jnp.where(valid, sc, NEG_INF)
        sink = sink_ref[kvh * group + g] * math.log2(math.e)
        m = jnp.maximum(jnp.max(sc, axis=-1, keepdims=True), sink)
        e = jnp.exp2(sc - m)
        den = jnp.sum(e, axis=-1, keepdims=True) + jnp.exp2(sink - m)
        o = jnp.dot(e.astype(BF16), vw, preferred_element_type=F32) / den
        o_ref[:, g * HEAD_DIM:(g + 1) * HEAD_DIM] = o.astype(o_ref.dtype)


def _window_attention(qkv, sink, *, tq=256):
    s = qkv.shape[0]
    group = WIN_Q_HEADS // WIN_KV_HEADS
    tq = min(tq, s)
    win = min(tq + 2 * WINDOW, s)
    qw = group * HEAD_DIM
    k0 = WIN_Q_HEADS
    v0 = WIN_Q_HEADS + WIN_KV_HEADS
    kern = functools.partial(_win_attn_kernel, tq=tq, win=win, s_len=s, group=group)
    return pl.pallas_call(
        kern,
        out_shape=jax.ShapeDtypeStruct((s, WIN_Q_HEADS * HEAD_DIM), BF16),
        grid=(WIN_KV_HEADS, s // tq),
        in_specs=[
            pl.BlockSpec(memory_space=pltpu.SMEM),
            pl.BlockSpec((tq, qw), lambda h, i: (i, h)),
            pl.BlockSpec((s, HEAD_DIM), lambda h, i: (0, k0 + h)),
            pl.BlockSpec((s, HEAD_DIM), lambda h, i: (0, v0 + h)),
        ],
        out_specs=pl.BlockSpec((tq, qw), lambda h, i: (i, h)),
        compiler_params=_params("parallel", "arbitrary"),
        name="window_attention",
    )(sink.astype(F32), qkv, qkv, qkv)


def _s5_discretize(a_re, a_im, log_dt, b_re, b_im):
    dt = jnp.exp(log_dt)[:, None]
    mag = jnp.exp(dt * a_re)
    lr, li = mag * jnp.cos(dt * a_im), mag * jnp.sin(dt * a_im)
    den = a_re * a_re + a_im * a_im
    nr, ni = lr - 1.0, li
    fr = (nr * a_re + ni * a_im) / den
    fi = (ni * a_re - nr * a_im) / den
    bbr = fr[..., None] * b_re - fi[..., None] * b_im
    bbi = fr[..., None] * b_im + fi[..., None] * b_re
    return lr, li, bbr, bbi


def _s5_block_weights(lr, li, bbr, bbi, c_re, c_im, gpc):
    g, n, c = bbr.shape
    n_k = g // gpc
    eye = jnp.eye(gpc, dtype=F32)

    def w_in(bb):
        t = bb.reshape(n_k, gpc, n, c)
        return jnp.einsum('kgnc,gh->kgchn', t, eye).reshape(n_k, gpc * c, gpc * n)

    def w_out(cc):
        t = cc.reshape(n_k, gpc, c, n)
        return jnp.einsum('kgcn,gh->khngc', t, eye).reshape(n_k, gpc * n, gpc * c)

    win = jnp.concatenate([w_in(bbr), w_in(bbi)], axis=2)
    wout = jnp.concatenate([w_out(c_re), -w_out(c_im)], axis=1)
    lam = jnp.stack([lr.reshape(n_k, gpc * n), li.reshape(n_k, gpc * n)], axis=1)
    return win, wout, lam


def _s5_kernel(uf_ref, ub_ref, win_ref, wout_ref, lam_ref, yf_ref, yb_ref,
               xf_s, xb_s, st_s, *, tc, ns, cw, nsub):
    i = pl.program_id(1)

    @pl.when(i == 0)
    def _():
        st_s[...] = jnp.zeros_like(st_s)

    for c in range(nsub):
        cols = slice(c * cw, (c + 1) * cw)
        xf_s[c] = jnp.dot(uf_ref[:, cols].astype(BF16), win_ref[0, c].astype(BF16),
                          preferred_element_type=F32)
        xb_s[c] = jnp.dot(ub_ref[:, cols].astype(BF16), win_ref[1, c].astype(BF16),
                          preferred_element_type=F32)
    lam = [[(lam_ref[d, c, 0:1, :], lam_ref[d, c, 1:2, :]) for c in range(nsub)]
           for d in range(2)]

    def step(x_s, c, row, lr, li, sr, si):
        nr = lr * sr - li * si + x_s[c, pl.ds(row, 1), 0:ns]
        ni = lr * si + li * sr + x_s[c, pl.ds(row, 1), ns:2 * ns]
        x_s[c, pl.ds(row, 1), 0:ns] = nr
        x_s[c, pl.ds(row, 1), ns:2 * ns] = ni
        return nr, ni

    def body(r, carry):
        out = []
        for c in range(nsub):
            srf, sif, srb, sib = carry[4 * c:4 * c + 4]
            srf, sif = step(xf_s, c, r, *lam[0][c], srf, sif)
            srb, sib = step(xb_s, c, tc - 1 - r, *lam[1][c], srb, sib)
            out += [srf, sif, srb, sib]
        return tuple(out)

    init = tuple(st_s[j:j + 1, :] for j in range(4 * nsub))
    final = lax.fori_loop(0, tc, body, init, unroll=8)
    for j in range(4 * nsub):
        st_s[j:j + 1, :] = final[j]
    for c in range(nsub):
        cols = slice(c * cw, (c + 1) * cw)
        yf_ref[:, cols] = jnp.dot(xf_s[c].astype(BF16), wout_ref[0, c].astype(BF16),
                                  preferred_element_type=F32)
        yb_ref[:, cols] = jnp.dot(xb_s[c].astype(BF16), wout_ref[1, c].astype(BF16),
                                  preferred_element_type=F32)


def _s5_scan(u, win, wout, lam, *, tc=256, nsub=2):
    s, w = u.shape
    _, n_k, cw, ns2 = win.shape
    ns = ns2 // 2
    tc = min(tc, s)
    n_t = s // tc
    nsub = min(nsub, n_k)
    bw = nsub * cw
    kern = functools.partial(_s5_kernel, tc=tc, ns=ns, cw=cw, nsub=nsub)
    return pl.pallas_call(
        kern,
        out_shape=(jax.ShapeDtypeStruct((s, w), F32), jax.ShapeDtypeStruct((s, w), F32)),
        grid=(n_k // nsub, n_t),
        in_specs=[
            pl.BlockSpec((tc, bw), lambda k, i: (i, k)),
            pl.BlockSpec((tc, bw), lambda k, i: (n_t - 1 - i, k)),
            pl.BlockSpec((2, nsub, cw, ns2), lambda k, i: (0, k, 0, 0)),
            pl.BlockSpec((2, nsub, ns2, cw), lambda k, i: (0, k, 0, 0)),
            pl.BlockSpec((2, nsub, 2, ns), lambda k, i: (0, k, 0, 0)),
        ],
        out_specs=(pl.BlockSpec((tc, bw), lambda k, i: (i, k)),
                   pl.BlockSpec((tc, bw), lambda k, i: (n_t - 1 - i, k))),
        scratch_shapes=[pltpu.VMEM((nsub, tc, ns2), F32), pltpu.VMEM((nsub, tc, ns2), F32),
                        pltpu.VMEM((4 * nsub, ns), F32)],
        compiler_params=_params("parallel", "arbitrary"),
        name="s5_scan",
    )(u, u, win, wout, lam)


def _pro_s5_glu(rows, vecs):
    yf, yb, u = rows
    return _gelu_tanh(yf + yb + vecs[0] * u)


def _epi_glu(acc, epi_refs, o_ref, j, a32_s):
    o_ref[...] = (a32_s[...] * _sigmoid(acc)).astype(o_ref.dtype)


def _route(x, g, w_r):
    h = _rms(x, g)
    logits = jnp.dot(h.astype(BF16), w_r, preferred_element_type=F32)
    tm = logits.shape[0]
    lane = lax.broadcasted_iota(jnp.int32, (tm, LANES), 1)
    big = jnp.int32(LANES)
    ninf = -jnp.inf
    gl = jnp.where(lane < MOE_GROUPS, logits, ninf)
    gm = jnp.max(gl, axis=-1, keepdims=True)
    ge = jnp.exp(gl - gm)
    g_prob = ge / jnp.sum(ge, axis=-1, keepdims=True)
    g_p = jnp.max(g_prob, axis=-1, keepdims=True)
    g_idx = jnp.min(jnp.where(g_prob == g_p, lane, big), axis=-1, keepdims=True)
    lo = MOE_GROUPS + g_idx * EXPERTS_PER_GROUP
    in_grp = (lane >= lo) & (lane < lo + EXPERTS_PER_GROUP)
    el = jnp.where(in_grp, logits, ninf)
    em = jnp.max(el, axis=-1, keepdims=True)
    ee = jnp.exp(el - em)
    e_prob = jnp.where(in_grp, ee / jnp.sum(ee, axis=-1, keepdims=True), -1.0)
    p1 = jnp.max(e_prob, axis=-1, keepdims=True)
    i1 = jnp.min(jnp.where(e_prob == p1, lane, big), axis=-1, keepdims=True)
    rest = jnp.where(lane == i1, -1.0, e_prob)
    p2 = jnp.max(rest, axis=-1, keepdims=True)
    i2 = jnp.min(jnp.where(rest == p2, lane, big), axis=-1, keepdims=True)
    denom = p1 + p2
    w1 = g_p * (p1 / denom)
    w2 = g_p * (p2 / denom)
    idx = jnp.where(lane == 0, i1 - MOE_GROUPS, jnp.where(lane == 1, i2 - MOE_GROUPS, 0))
    wt = jnp.where(lane == 0, w1, jnp.where(lane == 1, w2, 0.0))
    return h, idx, wt


def _router_kernel(x_ref, g_ref, w_ref, h_ref, idx_ref, wt_ref):
    h, idx, wt = _route(x_ref[...], g_ref[...], w_ref[...])
    h_ref[...] = h
    idx_ref[...] = idx
    wt_ref[...] = wt


def _router(x, g, w_group, w_expert, *, tm=512):
    s, d = x.shape
    tm = min(tm, s)
    w_r = jnp.concatenate(
        [w_group, w_expert, jnp.zeros((d, LANES - MOE_GROUPS - N_EXPERTS), F32)], axis=1)
    return pl.pallas_call(
        _router_kernel,
        out_shape=(jax.ShapeDtypeStruct((s, d), F32), jax.ShapeDtypeStruct((s, LANES), jnp.int32),
                   jax.ShapeDtypeStruct((s, LANES), F32)),
        grid=(s // tm,),
        in_specs=[pl.BlockSpec((tm, d), lambda i: (i, 0)), pl.BlockSpec((1, d), lambda i: (0, 0)),
                  pl.BlockSpec((d, LANES), lambda i: (0, 0))],
        out_specs=(pl.BlockSpec((tm, d), lambda i: (i, 0)), pl.BlockSpec((tm, LANES), lambda i: (i, 0)),
                   pl.BlockSpec((tm, LANES), lambda i: (i, 0))),
        compiler_params=_params("parallel"),
        name="moe_router",
    )(x, g.reshape(1, d), w_r.astype(BF16))


def _rank_kernel(idx_ref, dest_ref, cnt_ref, tot_s, pst_s, run_s, *, tm):
    ph = pl.program_id(0)
    i = pl.program_id(1)
    lane = lax.broadcasted_iota(jnp.int32, (tm, LANES), 1)
    idx = idx_ref[...]
    oh0 = (lane == idx[:, 0:1]).astype(F32)
    oh1 = (lane == idx[:, 1:2]).astype(F32)
    c = oh0 + oh1
    csum = jnp.sum(c, axis=0, keepdims=True)

    @pl.when(jnp.logical_and(ph == 0, i == 0))
    def _():
        tot_s[...] = jnp.zeros_like(tot_s)

    @pl.when(ph == 0)
    def _():
        tot_s[...] += csum

    @pl.when(jnp.logical_and(ph == 1, i == 0))
    def _():
        counts = tot_s[...]
        nblk = jnp.floor((counts + (EXPERT_BLOCK - 1)) * (1.0 / EXPERT_BLOCK))
        r = lax.broadcasted_iota(jnp.int32, (LANES, LANES), 0)
        cc = lax.broadcasted_iota(jnp.int32, (LANES, LANES), 1)
        upper = (r < cc).astype(F32)
        excl = jnp.dot(jnp.broadcast_to(nblk, (SUBLANES, LANES)), upper,
                       preferred_element_type=F32, precision=lax.Precision.HIGHEST)
        pst_s[...] = excl[0:1, :] * EXPERT_BLOCK
        run_s[...] = jnp.zeros_like(run_s)
        cnt_ref[...] = jnp.broadcast_to(counts, cnt_ref.shape)

    @pl.when(ph == 1)
    def _():
        rr = lax.broadcasted_iota(jnp.int32, (tm, tm), 0)
        cr = lax.broadcasted_iota(jnp.int32, (tm, tm), 1)
        lower = (rr > cr).astype(BF16)
        before = jnp.dot(lower, c.astype(BF16), preferred_element_type=F32) + run_s[...]
        base = pst_s[...] + before
        d0 = jnp.sum(oh0 * base, axis=-1, keepdims=True)
        d1 = jnp.sum(oh1 * (base + oh0), axis=-1, keepdims=True)
        dest_ref[...] = jnp.where(lane == 0, d0, jnp.where(lane == 1, d1, 0.0)).astype(jnp.int32)
        run_s[...] += csum


def _dispatch_rank(idx, *, tm=512):
    s = idx.shape[0]
    tm = min(tm, s)
    kern = functools.partial(_rank_kernel, tm=tm)
    return pl.pallas_call(
        kern,
        out_shape=(jax.ShapeDtypeStruct((s, LANES), jnp.int32),
                   jax.ShapeDtypeStruct((SUBLANES, LANES), F32)),
        grid=(2, s // tm),
        in_specs=[pl.BlockSpec((tm, LANES), lambda ph, i: (i, 0))],
        out_specs=(pl.BlockSpec((tm, LANES), lambda ph, i: (i * ph, 0)),
                   pl.BlockSpec((SUBLANES, LANES), lambda ph, i: (0, 0))),
        scratch_shapes=[pltpu.VMEM((1, LANES), F32) for _ in range(3)],
        compiler_params=_params("arbitrary", "arbitrary"),
        name="moe_rank",
    )(idx)


def _block_experts(counts, n_blk):
    cnt = counts[0, :N_EXPERTS].astype(jnp.int32)
    padded = ((cnt + EXPERT_BLOCK - 1) // EXPERT_BLOCK) * EXPERT_BLOCK
    pends = jnp.cumsum(padded)
    blk_start = jnp.arange(n_blk, dtype=jnp.int32) * EXPERT_BLOCK
    owner = jnp.sum((pends[None, :] <= blk_start[:, None]).astype(jnp.int32), axis=1)
    ids = jnp.arange(N_EXPERTS, dtype=jnp.int32)
    last_e = jnp.max(jnp.where(cnt > 0, ids, 0))
    n_used = (pends[-1] // EXPERT_BLOCK).reshape(1)
    blk_e = jnp.minimum(owner, last_e).astype(jnp.int32)
    later = jnp.logical_and(ids[None, :] > ids[:, None], cnt[None, :] > 0)
    nxt = jnp.min(jnp.where(later, ids[None, :], N_EXPERTS), axis=1)
    nxt = jnp.where(nxt == N_EXPERTS, ids, nxt).astype(jnp.int32)
    nxt2 = nxt[nxt]
    parity = ((jnp.cumsum((cnt > 0).astype(jnp.int32)) - 1) % 2).astype(jnp.int32)
    pad_start = (pends - padded + cnt).astype(jnp.int32)
    pad_len = (padded - cnt).astype(jnp.int32)
    tables = (blk_e, nxt[blk_e], nxt2[blk_e], parity[blk_e], n_used.astype(jnp.int32))
    return tables, pad_start, pad_len


def _scatter_kernel(dest_ref, ps_ref, pl_ref, nu_ref, h_ref, xs_hbm, h_s, z_s, sem, zsem,
                    *, tb, n_b, n_blk):
    b = pl.program_id(0)
    slot = b % 2
    n = tb * TOP_K
    bits = [1 << k for k in reversed(range(3, (EXPERT_BLOCK - 1).bit_length()))]

    def zero_fill(action):
        def pad_run(e, _):
            start = ps_ref[e]
            head = (-start) & (SUBLANES - 1)
            for r in range(SUBLANES - 1):
                @pl.when(r < head)
                def _():
                    action(pltpu.make_async_copy(z_s.at[pl.ds(0, 1), :],
                                                 xs_hbm.at[pl.ds(start + r, 1), :], zsem.at[0]))
            astart = start + head
            length = pl_ref[e] - head
            for bit in bits:
                @pl.when((length & bit) != 0)
                def _():
                    off = pl.multiple_of(astart + (length & ~(2 * bit - 1)), SUBLANES)
                    action(pltpu.make_async_copy(z_s.at[pl.ds(0, bit), :],
                                                 xs_hbm.at[pl.ds(off, bit), :], zsem.at[0]))
            return 0

        def unused_block(bb, _):
            row0 = pl.multiple_of(bb * EXPERT_BLOCK, EXPERT_BLOCK)
            action(pltpu.make_async_copy(
                z_s, xs_hbm.at[pl.ds(row0, EXPERT_BLOCK), :], zsem.at[0]))
            return 0

        lax.fori_loop(0, N_EXPERTS, pad_run, 0)
        lax.fori_loop(nu_ref[0], n_blk, unused_block, 0)

    @pl.when(b == 0)
    def _():
        z_s[...] = jnp.zeros_like(z_s)
        zero_fill(lambda c: c.start())

    def copy(r, row, sl):
        return pltpu.make_async_copy(h_s.at[sl, pl.ds(r, 1), :], xs_hbm.at[pl.ds(row, 1), :],
                                     sem.at[sl])

    def wait_all(sl):
        def body(r, _):
            copy(0, 0, sl).wait()
            return 0
        lax.fori_loop(0, n, body, 0, unroll=8)

    h_s[slot] = h_ref[...]
    for r in range(tb):
        tok = b * tb + r
        for k in range(TOP_K):
            copy(r, dest_ref[tok * TOP_K + k], slot).start()

    @pl.when(b > 0)
    def _():
        wait_all(1 - slot)

    @pl.when(b == n_b - 1)
    def _():
        wait_all(slot)
        zero_fill(lambda c: c.wait())


def _dispatch_scatter(h, dest_flat, pad_start, pad_len, n_used, n_rows, *, tb=128):
    s, d = h.shape
    tb = min(tb, s)
    n_b = s // tb
    kern = functools.partial(_scatter_kernel, tb=tb, n_b=n_b, n_blk=n_rows // EXPERT_BLOCK)
    grid_spec = pltpu.PrefetchScalarGridSpec(
        num_scalar_prefetch=4,
        grid=(n_b,),
        in_specs=[pl.BlockSpec((tb, d), lambda b, *_: (b, 0))],
        out_specs=pl.BlockSpec(memory_space=pl.ANY),
        scratch_shapes=[pltpu.VMEM((2, tb, d), h.dtype), pltpu.VMEM((EXPERT_BLOCK, d), h.dtype),
                        pltpu.SemaphoreType.DMA((2,)), pltpu.SemaphoreType.DMA((1,))],
    )
    return pl.pallas_call(
        kern,
        out_shape=jax.ShapeDtypeStruct((n_rows, d), h.dtype),
        grid_spec=grid_spec,
        compiler_params=_params("arbitrary"),
        name="moe_scatter",
    )(dest_flat, pad_start, pad_len, n_used, h)


def _row_copy(src_hbm, src_row, dst_buf, slot, dst_row, sem):
    return pltpu.make_async_copy(src_hbm.at[pl.ds(src_row, 1), :],
                                 dst_buf.at[slot, pl.ds(dst_row, 1), :], sem.at[slot])


def _gather_start(idx_ref, base, n, src_hbm, dst_buf, slot, sem):
    for r in range(n):
        _row_copy(src_hbm, idx_ref[base + r], dst_buf, slot, r, sem).start()


def _gather_wait(n, src_hbm, dst_buf, slot, sem):
    def body(r, _):
        _row_copy(src_hbm, 0, dst_buf, slot, r, sem).wait()
        return 0
    lax.fori_loop(0, n, body, 0, unroll=8)


def _expert_kernel(be_ref, nx_ref, nx2_ref, par_ref, nu_ref, x_ref, wg_hbm, wu_hbm, wd_hbm, y_ref,
                   wg_f, wu_f, wd_f, wg_s, wu_s, wd_s, sem, *, layer):
    b = pl.program_id(0)
    e = be_ref[b]
    slot = par_ref[b]

    def fetch(expert, sl):
        return (pltpu.make_async_copy(wg_hbm.at[layer, expert], wg_f.at[sl], sem.at[sl, 0]),
                pltpu.make_async_copy(wu_hbm.at[layer, expert], wu_f.at[sl], sem.at[sl, 1]),
                pltpu.make_async_copy(wd_hbm.at[layer, expert], wd_f.at[sl], sem.at[sl, 2]))

    @pl.when(b == 0)
    def _():
        for c in fetch(e, slot):
            c.start()

        @pl.when(nx_ref[b] != e)
        def _():
            for c in fetch(nx_ref[b], 1 - slot):
                c.start()

    changed = jnp.logical_or(b == 0, e != be_ref[jnp.maximum(b - 1, 0)])

    @pl.when(changed)
    def _():
        for c in fetch(e, slot):
            c.wait()
        wg_s[...] = wg_f[slot].astype(BF16)
        wu_s[...] = wu_f[slot].astype(BF16)
        wd_s[...] = wd_f[slot].astype(BF16)

        @pl.when(nx2_ref[b] != nx_ref[b])
        def _():
            for c in fetch(nx2_ref[b], slot):
                c.start()

    @pl.when(b >= nu_ref[0])
    def _():
        y_ref[...] = jnp.zeros_like(y_ref)

    @pl.when(b < nu_ref[0])
    def _():
        x = x_ref[...].astype(BF16)
        hg = jnp.dot(x, wg_s[...], preferred_element_type=F32)
        hu = jnp.dot(x, wu_s[...], preferred_element_type=F32)
        hdn = (hg * _sigmoid(hg)) * hu
        y_ref[...] = jnp.dot(hdn.astype(BF16), wd_s[...], preferred_element_type=F32)


def _expert_mlp(xs, blk_e, blk_next, blk_next2, blk_par, n_used, layer, w_gate, w_up, w_down):
    n_rows, d = xs.shape
    de = w_gate.shape[3]
    grid_spec = pltpu.PrefetchScalarGridSpec(
        num_scalar_prefetch=5,
        grid=(n_rows // EXPERT_BLOCK,),
        in_specs=[
            pl.BlockSpec((EXPERT_BLOCK, d),
                         lambda b, be, nx, nx2, par, nu: (jnp.minimum(b, nu[0] - 1), 0)),
            pl.BlockSpec(memory_space=pl.ANY),
            pl.BlockSpec(memory_space=pl.ANY),
            pl.BlockSpec(memory_space=pl.ANY),
        ],
        out_specs=pl.BlockSpec((EXPERT_BLOCK, d), lambda b, be, nx, nx2, par, nu: (b, 0)),
        scratch_shapes=[pltpu.VMEM((2, d, de), F32), pltpu.VMEM((2, d, de), F32),
                        pltpu.VMEM((2, de, d), F32), pltpu.VMEM((d, de), BF16),
                        pltpu.VMEM((d, de), BF16), pltpu.VMEM((de, d), BF16),
                        pltpu.SemaphoreType.DMA((2, 3))],
    )
    return pl.pallas_call(
        functools.partial(_expert_kernel, layer=layer),
        out_shape=jax.ShapeDtypeStruct((n_rows, d), F32),
        grid_spec=grid_spec,
        compiler_params=_params("arbitrary"),
        name="moe_experts",
    )(blk_e, blk_next, blk_next2, blk_par, n_used, xs, w_gate, w_up, w_down)


def _combine_kernel(dest_ref, x_ref, wt_ref, ys_hbm, g_ref, o_ref, ybuf, sem, *, tb, n_b, final):
    b = pl.program_id(0)
    slot = b % 2
    n = tb * TOP_K

    @pl.when(b == 0)
    def _():
        _gather_start(dest_ref, 0, n, ys_hbm, ybuf, 0, sem)

    _gather_wait(n, ys_hbm, ybuf, slot, sem)
    _gather_start(dest_ref, (b + 1) * n, n, ys_hbm, ybuf, 1 - slot, sem)
    y0 = ybuf[slot, 0:tb, :]
    y1 = ybuf[slot, tb:2 * tb, :]
    wt = wt_ref[...]
    out = x_ref[...] + (y0 * wt[:, 0:1] + y1 * wt[:, 1:2])
    if final:
        out = _rms(out, g_ref[...])
    o_ref[...] = out

    @pl.when(b == n_b - 1)
    def _():
        _gather_wait(n, ys_hbm, ybuf, 1 - slot, sem)


def _moe_combine(x, wt, ys, dest, g_final, *, final, tb=128):
    s, d = x.shape
    tb = min(tb, s)
    n_b = s // tb
    dest = dest.reshape(n_b, tb, TOP_K).transpose(0, 2, 1).reshape(-1)
    dest = jnp.concatenate([dest, jnp.zeros((tb * TOP_K,), jnp.int32)])
    kern = functools.partial(_combine_kernel, tb=tb, n_b=n_b, final=final)
    grid_spec = pltpu.PrefetchScalarGridSpec(
        num_scalar_prefetch=1,
        grid=(n_b,),
        in_specs=[
            pl.BlockSpec((tb, d), lambda b, dr: (b, 0)),
            pl.BlockSpec((tb, LANES), lambda b, dr: (b, 0)),
            pl.BlockSpec(memory_space=pl.ANY),
            pl.BlockSpec((1, d), lambda b, dr: (0, 0)),
        ],
        out_specs=pl.BlockSpec((tb, d), lambda b, dr: (b, 0)),
        scratch_shapes=[pltpu.VMEM((2, tb * TOP_K, d), F32), pltpu.SemaphoreType.DMA((2,))],
    )
    return pl.pallas_call(
        kern,
        out_shape=jax.ShapeDtypeStruct((s, d), F32),
        grid_spec=grid_spec,
        compiler_params=_params("arbitrary"),
        name="moe_combine",
    )(dest, x, wt, ys, g_final.reshape(1, d))


def _hier_moe(x, ln_g, w_group, w_expert, layer, w_gate, w_up, w_down, g_final, *, final):
    s = x.shape[0]
    h, idx, wt = _router(x, ln_g, w_group, w_expert)
    dest, counts = _dispatch_rank(idx)
    dest = dest[:, :TOP_K].reshape(-1)
    n_rows = -(-(s * TOP_K + N_EXPERTS * (EXPERT_BLOCK - 1)) // EXPERT_BLOCK) * EXPERT_BLOCK
    tables, pad_start, pad_len = _block_experts(counts, n_rows // EXPERT_BLOCK)
    xs = _dispatch_scatter(h, dest, pad_start, pad_len, tables[-1], n_rows)
    ys = _expert_mlp(xs, *tables, layer, w_gate, w_up, w_down)
    return _moe_combine(x, wt, ys, dest, g_final, final=final)


def _rglru_layer(x, ln_g, w_in, conv_w, conv_b, w_a, w_x, b_a, b_x, lam, w_out):
    d = x.shape[1]
    c = conv_w.shape[1]
    proj = _norm_matmul(x, ln_g, w_in, name="lru_in_proj")
    yf, yb = _rglru_scan(proj, conv_w, conv_b, w_a, w_x, b_a, b_x, lam)
    return _matmul_residual([(yf, c, 0), (yb, c, 0), (proj, c, 0)], [], w_out, x, _pro_rglru_out,
                            k_dim=c, tm=512, name="lru_out_proj")


def _diff_layer(x, ln_g, rope, w_qkv, lq1, lk1, lq2, lk2, subln_g, w_out, lambda_init):
    qk_cols = 2 * DIFF_HEADS * 2 * HEAD_DIM
    qkv = _norm_matmul(x, ln_g, w_qkv, rope=rope, n_rope_cols=qk_cols, n_q_cols=qk_cols // 2,
                       q_scale=HEAD_DIM ** -0.5 * math.log2(math.e), out_dtype=BF16,
                       name="diff_qkv_proj")
    o = _diff_attention(qkv, lq1, lk1, lq2, lk2, subln_g, lambda_init)
    return _matmul_residual([(o, o.shape[1], 0)], [], w_out, x, _pro_identity,
                            k_dim=o.shape[1], name="diff_out_proj")


def _window_layer(x, ln_g, rope, w_qkv, sink, w_out):
    qk_cols = (WIN_Q_HEADS + WIN_KV_HEADS) * HEAD_DIM
    qkv = _norm_matmul(x, ln_g, w_qkv, rope=rope, n_rope_cols=qk_cols,
                       n_q_cols=WIN_Q_HEADS * HEAD_DIM,
                       q_scale=HEAD_DIM ** -0.5 * math.log2(math.e), out_dtype=BF16, tn=512,
                       name="win_qkv_proj")
    o = _window_attention(qkv, sink)
    return _matmul_residual([(o, o.shape[1], 0)], [], w_out, x, _pro_identity,
                            k_dim=o.shape[1], name="win_out_proj")


def _s5_layer(x, ln_g, w_in, a_re, a_im, log_dt, b_re, b_im, c_re, c_im, d_skip, w_glu, w_out,
              *, gpc=16):
    u = _norm_matmul(x, ln_g, w_in, name="s5_in_proj")
    w = u.shape[1]
    wins, wouts, lams = [], [], []
    for dd in range(2):
        lr, li, bbr, bbi = _s5_discretize(a_re[dd], a_im[dd], log_dt[dd], b_re[dd], b_im[dd])
        wi, wo, lm = _s5_block_weights(lr, li, bbr, bbi, c_re[dd], c_im[dd], gpc)
        wins.append(wi)
        wouts.append(wo)
        lams.append(lm)
    yf, yb = _s5_scan(u, jnp.stack(wins).astype(BF16), jnp.stack(wouts).astype(BF16),
                      jnp.stack(lams))
    z = _fused_mm([(yf, w, 0), (yb, w, 0), (u, w, 0)], [d_skip.reshape(1, w)], w_glu, [],
                  _pro_s5_glu, _epi_glu, out_dtype=BF16, tm=512, tn=w, k_dim=w, keep_f32=True,
                  name="s5_glu")
    return _matmul_residual([(z, w, 0)], [], w_out, x, _pro_identity, k_dim=w, name="s5_out_proj")


def kernel(x, positions, ln_mix, ln_ffn, ln_final, lru_w_in, lru_conv_w, lru_conv_b, lru_w_a, lru_w_x, lru_b_a, lru_b_x, lru_lambda, lru_w_out, diff_w_qkv, diff_lq1, diff_lk1, diff_lq2, diff_lk2, diff_subln, diff_w_out, win_w_qkv, win_sink, win_w_out, s5_w_in, s5_a_re, s5_a_im, s5_log_dt, s5_b_re, s5_b_im, s5_c_re, s5_c_im, s5_d, s5_w_glu, s5_w_out, moe_w_group, moe_w_expert, moe_w_gate, moe_w_up, moe_w_down):
    batch, s, d = x.shape
    depth = ln_mix.shape[0]
    outs = []
    for b in range(batch):
        xb = x[b]
        rope = _rope_tables(positions[b])
        for i in range(depth):
            kind, j = i % 4, i // 4
            if kind == 0:
                xb = _rglru_layer(xb, ln_mix[i], lru_w_in[j], lru_conv_w[j], lru_conv_b[j],
                                  lru_w_a[j], lru_w_x[j], lru_b_a[j], lru_b_x[j], lru_lambda[j],
                                  lru_w_out[j])
            elif kind == 1:
                xb = _diff_layer(xb, ln_mix[i], rope, diff_w_qkv[j], diff_lq1[j], diff_lk1[j],
                                 diff_lq2[j], diff_lk2[j], diff_subln[j], diff_w_out[j],
                                 0.8 - 0.6 * math.exp(-0.3 * i))
            elif kind == 2:
                xb = _window_layer(xb, ln_mix[i], rope, win_w_qkv[j], win_sink[j], win_w_out[j])
            else:
                xb = _s5_layer(xb, ln_mix[i], s5_w_in[j], s5_a_re[j], s5_a_im[j], s5_log_dt[j],
                               s5_b_re[j], s5_b_im[j], s5_c_re[j], s5_c_im[j], s5_d[j],
                               s5_w_glu[j], s5_w_out[j])
            xb = _hier_moe(xb, ln_ffn[i], moe_w_group[i], moe_w_expert[i], i, moe_w_gate,
                           moe_w_up, moe_w_down, ln_final, final=(i == depth - 1))
        outs.append(xb)
    return jnp.stack(outs)
```

```python
import functools
import math

import jax
import jax.numpy as jnp
from jax import lax
from jax.experimental import pallas as pl
from jax.experimental.pallas import tpu as pltpu

F32 = jnp.float32
BF16 = jnp.bfloat16

NORM_EPS = 1e-6
NEG_INF = -1e30
LANES = 128
SUBLANES = 8
VMEM_LIMIT = 56 * 1024 * 1024

HEAD_DIM = 128
ROT_DIM = HEAD_DIM // 4
ROPE_THETA = 500000.0
RGLRU_C = 8.0
CONV_W = 4
LRU_BLOCK_W = 128
WINDOW = 128
DIFF_HEADS = 8
WIN_Q_HEADS = 16
WIN_KV_HEADS = 4
SSM_GROUP_CH = 16
SSM_STATE = 64
MOE_GROUPS = 4
EXPERTS_PER_GROUP = 8
N_EXPERTS = MOE_GROUPS * EXPERTS_PER_GROUP
TOP_K = 2
EXPERT_BLOCK = 256


def _params(*sem):
    return pltpu.CompilerParams(dimension_semantics=sem, vmem_limit_bytes=VMEM_LIMIT)


def _rms(x, g):
    ms = jnp.mean(x * x, axis=-1, keepdims=True)
    return x * lax.rsqrt(ms + NORM_EPS) * g


def _gelu_tanh(x):
    return 0.5 * x * (1.0 + jnp.tanh(math.sqrt(2.0 / math.pi) * (x + 0.044715 * (x * x * x))))


def _sigmoid(x):
    return 1.0 / (1.0 + jnp.exp(-x))


def _fused_mm_kernel(*refs, n_row, n_vec, n_epi, prologue, epilogue, keep_f32):
    row_refs = refs[:n_row]
    vec_refs = refs[n_row:n_row + n_vec]
    w_ref = refs[n_row + n_vec]
    epi_refs = refs[n_row + n_vec + 1:n_row + n_vec + 1 + n_epi]
    o_ref = refs[n_row + n_vec + 1 + n_epi]
    a_s = refs[n_row + n_vec + 2 + n_epi]
    a32_s = refs[n_row + n_vec + 3 + n_epi] if keep_f32 else None
    j = pl.program_id(1)

    @pl.when(j == 0)
    def _():
        a = prologue([r[...] for r in row_refs], [v[...] for v in vec_refs])
        a_s[...] = a.astype(BF16)
        if keep_f32:
            a32_s[...] = a

    acc = jnp.dot(a_s[...], w_ref[...].astype(BF16), preferred_element_type=F32)
    epilogue(acc, epi_refs, o_ref, j, a32_s)


def _fused_mm(row_inputs, vec_inputs, w, epi_inputs, prologue, epilogue, *, out_dtype, tm, tn,
              k_dim, keep_f32=False, name):
    s = row_inputs[0][0].shape[0]
    n = w.shape[1]
    tm = min(tm, s)
    tn = min(tn, n)
    w = w.astype(BF16)
    in_specs = []
    args = []
    for arr, width, cb in row_inputs:
        in_specs.append(pl.BlockSpec((tm, width), lambda i, j, cb=cb: (i, cb)))
        args.append(arr)
    for arr in vec_inputs:
        in_specs.append(pl.BlockSpec(arr.shape, lambda i, j: (0, 0)))
        args.append(arr)
    in_specs.append(pl.BlockSpec((k_dim, tn), lambda i, j: (0, j)))
    args.append(w)
    for arr, width, per_tile in epi_inputs:
        if per_tile:
            in_specs.append(pl.BlockSpec((tm, width), lambda i, j: (i, j)))
        else:
            in_specs.append(pl.BlockSpec((tm, width), lambda i, j: (i, 0)))
        args.append(arr)
    scratch = [pltpu.VMEM((tm, k_dim), BF16)]
    if keep_f32:
        scratch.append(pltpu.VMEM((tm, k_dim), F32))
    kern = functools.partial(_fused_mm_kernel, n_row=len(row_inputs), n_vec=len(vec_inputs),
                             n_epi=len(epi_inputs), prologue=prologue, epilogue=epilogue,
                             keep_f32=keep_f32)
    return pl.pallas_call(
        kern,
        out_shape=jax.ShapeDtypeStruct((s, n), out_dtype),
        grid=(s // tm, n // tn),
        in_specs=in_specs,
        out_specs=pl.BlockSpec((tm, tn), lambda i, j: (i, j)),
        scratch_shapes=scratch,
        compiler_params=_params("parallel", "arbitrary"),
        name=name,
    )(*args)


def _pro_rms(rows, vecs):
    return _rms(rows[0], vecs[0])


def _epi_store(acc, epi_refs, o_ref, j, a32_s):
    o_ref[...] = acc.astype(o_ref.dtype)


def _epi_residual(acc, epi_refs, o_ref, j, a32_s):
    o_ref[...] = (epi_refs[0][...] + acc).astype(o_ref.dtype)


def _make_epi_rope(n_rope_tiles, n_q_tiles, q_scale, tn):
    def epi(acc, epi_refs, o_ref, j, a32_s):
        c_ref, s1_ref, s2_ref = epi_refs

        @pl.when(j < n_rope_tiles)
        def _():
            c = c_ref[...]
            s1 = s1_ref[...]
            s2 = s2_ref[...]
            sc = jnp.where(j < n_q_tiles, q_scale, 1.0).astype(F32)
            for hh in range(tn // HEAD_DIM):
                xs = acc[:, hh * HEAD_DIM:(hh + 1) * HEAD_DIM]
                rot = (xs * c + pltpu.roll(xs, HEAD_DIM - ROT_DIM // 2, 1) * s1
                       + pltpu.roll(xs, ROT_DIM // 2, 1) * s2) * sc
                o_ref[:, hh * HEAD_DIM:(hh + 1) * HEAD_DIM] = rot.astype(o_ref.dtype)

        @pl.when(j >= n_rope_tiles)
        def _():
            o_ref[...] = acc.astype(o_ref.dtype)

    return epi


def _rope_tables(positions):
    half = ROT_DIM // 2
    inv = ROPE_THETA ** (-jnp.arange(0, ROT_DIM, 2, dtype=F32) / ROT_DIM)
    ang = positions.astype(F32)[:, None] * inv
    cos, sin = jnp.cos(ang), jnp.sin(ang)
    s = positions.shape[0]
    ones = jnp.ones((s, HEAD_DIM - ROT_DIM), F32)
    zeros = jnp.zeros((s, HEAD_DIM - ROT_DIM), F32)
    zh = jnp.zeros((s, half), F32)
    c_tab = jnp.concatenate([cos, cos, ones], axis=1)
    s1_tab = jnp.concatenate([-sin, zh, zeros], axis=1)
    s2_tab = jnp.concatenate([zh, sin, zeros], axis=1)
    return c_tab, s1_tab, s2_tab


def _norm_matmul(x, g, w, *, rope=None, n_rope_cols=0, n_q_cols=0, q_scale=1.0, out_dtype=F32,
                 tm=1024, tn=1024, name):
    d = x.shape[1]
    if rope is None:
        epi, epi_inputs = _epi_store, []
    else:
        tn = min(tn, w.shape[1])
        assert n_rope_cols % tn == 0 and n_q_cols % tn == 0
        epi = _make_epi_rope(n_rope_cols // tn, n_q_cols // tn, q_scale, tn)
        epi_inputs = [(t, HEAD_DIM, False) for t in rope]
    return _fused_mm([(x, d, 0)], [g.reshape(1, d)], w, epi_inputs, _pro_rms, epi,
                     out_dtype=out_dtype, tm=tm, tn=tn, k_dim=d, name=name)


def _matmul_residual(row_inputs, vec_inputs, w, res, prologue, *, k_dim, tm=1024, tn=512, name):
    return _fused_mm(row_inputs, vec_inputs, w, [(res, min(tn, w.shape[1]), True)], prologue,
                     _epi_residual, out_dtype=F32, tm=tm, tn=tn, k_dim=k_dim, name=name)


def _rglru_kernel(xf_ref, xfp_ref, xfn_ref, xb_ref, xbp_ref, xbn_ref, cw_ref, cb_ref, wa_ref,
                  wx_ref, ba_ref, bx_ref, lam_ref, yf_ref, yb_ref,
                  ext_s, af_s, bf_s, ab_s, bb_s, hf_s, hb_s, *, tc, cw, n_t):
    i = pl.program_id(1)
    halo = SUBLANES

    @pl.when(i == 0)
    def _():
        hf_s[...] = jnp.zeros_like(hf_s)
        hb_s[...] = jnp.zeros_like(hb_s)

    def gates(x_ref, xp_ref, xn_ref, chunk, d, a_s, b_s):
        prev = jnp.where(chunk == 0, 0.0, xp_ref[...])
        nxt = jnp.where(chunk == n_t - 1, 0.0, xn_ref[...])
        ext_s[0:halo, :] = prev
        ext_s[halo:halo + tc, :] = x_ref[...]
        ext_s[halo + tc:halo + tc + halo, :] = nxt
        xc = cb_ref[...] + sum(
            cw_ref[k:k + 1, :] * ext_s[halo - 2 + k:halo - 2 + k + tc, :] for k in range(CONV_W))
        lam = lam_ref[d:d + 1, :]
        z = -lam
        sp = jnp.maximum(z, 0.0) + jnp.log1p(jnp.exp(-jnp.abs(z)))
        for blk in range(cw // LRU_BLOCK_W):
            sl = slice(blk * LRU_BLOCK_W, (blk + 1) * LRU_BLOCK_W)
            xb = xc[:, sl]
            xbh = xb.astype(BF16)
            r = _sigmoid(jnp.dot(xbh, wa_ref[d, blk].astype(BF16), preferred_element_type=F32)
                         + ba_ref[d:d + 1, sl])
            ig = _sigmoid(jnp.dot(xbh, wx_ref[d, blk].astype(BF16), preferred_element_type=F32)
                          + bx_ref[d:d + 1, sl])
            log_a = (-RGLRU_C) * r * sp[:, sl]
            a_s[:, sl] = jnp.exp(log_a)
            th = jnp.tanh(log_a)
            b_s[:, sl] = jnp.sqrt(-2.0 * th / (1.0 - th)) * (ig * xb)

    gates(xf_ref, xfp_ref, xfn_ref, i, 0, af_s, bf_s)
    gates(xb_ref, xbp_ref, xbn_ref, n_t - 1 - i, 1, ab_s, bb_s)

    def body(r, carry):
        hf, hb = carry
        hf = af_s[pl.ds(r, 1), :] * hf + bf_s[pl.ds(r, 1), :]
        yf_ref[pl.ds(r, 1), :] = hf
        rb = tc - 1 - r
        hb = ab_s[pl.ds(rb, 1), :] * hb + bb_s[pl.ds(rb, 1), :]
        yb_ref[pl.ds(rb, 1), :] = hb
        return hf, hb

    hf, hb = lax.fori_loop(0, tc, body, (hf_s[...], hb_s[...]), unroll=8)
    hf_s[...] = hf
    hb_s[...] = hb


def _rglru_scan(proj, conv_w, conv_b, w_a, w_x, b_a, b_x, lam, *, tc=256, cw=2048):
    s = proj.shape[0]
    c = conv_w.shape[1]
    tc = min(tc, s)
    n_t = s // tc
    n_c = c // cw
    xoff = c // cw
    hb = tc // SUBLANES
    last_h = s // SUBLANES - 1

    specs = [
        pl.BlockSpec((tc, cw), lambda ci, i: (i, xoff + ci)),
        pl.BlockSpec((SUBLANES, cw), lambda ci, i: (jnp.maximum(i * hb - 1, 0), xoff + ci)),
        pl.BlockSpec((SUBLANES, cw), lambda ci, i: (jnp.minimum((i + 1) * hb, last_h), xoff + ci)),
        pl.BlockSpec((tc, cw), lambda ci, i: (n_t - 1 - i, xoff + ci)),
        pl.BlockSpec((SUBLANES, cw),
                     lambda ci, i: (jnp.maximum((n_t - 1 - i) * hb - 1, 0), xoff + ci)),
        pl.BlockSpec((SUBLANES, cw),
                     lambda ci, i: (jnp.minimum((n_t - i) * hb, last_h), xoff + ci)),
        pl.BlockSpec((CONV_W, cw), lambda ci, i: (0, ci)),
        pl.BlockSpec((1, cw), lambda ci, i: (0, ci)),
        pl.BlockSpec((2, cw // LRU_BLOCK_W, LRU_BLOCK_W, LRU_BLOCK_W), lambda ci, i: (0, ci, 0, 0)),
        pl.BlockSpec((2, cw // LRU_BLOCK_W, LRU_BLOCK_W, LRU_BLOCK_W), lambda ci, i: (0, ci, 0, 0)),
        pl.BlockSpec((2, cw), lambda ci, i: (0, ci)),
        pl.BlockSpec((2, cw), lambda ci, i: (0, ci)),
        pl.BlockSpec((2, cw), lambda ci, i: (0, ci)),
    ]
    kern = functools.partial(_rglru_kernel, tc=tc, cw=cw, n_t=n_t)
    return pl.pallas_call(
        kern,
        out_shape=(jax.ShapeDtypeStruct((s, c), F32), jax.ShapeDtypeStruct((s, c), F32)),
        grid=(n_c, n_t),
        in_specs=specs,
        out_specs=(pl.BlockSpec((tc, cw), lambda ci, i: (i, ci)),
                   pl.BlockSpec((tc, cw), lambda ci, i: (n_t - 1 - i, ci))),
        scratch_shapes=[pltpu.VMEM((tc + 2 * SUBLANES, cw), F32)]
        + [pltpu.VMEM((tc, cw), F32) for _ in range(4)]
        + [pltpu.VMEM((1, cw), F32) for _ in range(2)],
        compiler_params=_params("parallel", "arbitrary"),
        name="rglru_scan",
    )(proj, proj, proj, proj, proj, proj, conv_w, conv_b.reshape(1, c), w_a, w_x, b_a, b_x, lam)


def _pro_rglru_out(rows, vecs):
    yf, yb, gate = rows
    return (yf + yb) * _gelu_tanh(gate)


def _diff_attn_kernel(q_ref, k_ref, v_ref, lq1_ref, lk1_ref, lq2_ref, lk2_ref, g_ref, o_ref,
                      s_buf, p_buf, m_s, a_s, l_s, acc_s, *, tq, tk, rc, pv_rows, n_kv,
                      lambda_init):
    m_s[...] = jnp.full_like(m_s, -jnp.inf)
    l_s[...] = jnp.zeros_like(l_s)
    acc_s[...] = jnp.zeros_like(acc_s)
    q = q_ref[...]
    qs = (q[:, :HEAD_DIM], q[:, HEAD_DIM:])

    def scores(j, slot):
        off = pl.multiple_of(j * tk, tk)
        kb = k_ref[pl.ds(off, tk), :]
        for c in range(2):
            kc = kb[:, c * HEAD_DIM:(c + 1) * HEAD_DIM]
            s_buf[slot, c * tq:(c + 1) * tq, :] = lax.dot_general(
                qs[c], kc, (((1,), (1,)), ((), ())), preferred_element_type=F32)

    def update(j, slot):
        off = pl.multiple_of(j * tk, tk)
        vb = v_ref[pl.ds(off, tk), :]
        for g0 in range(0, 2 * tq, pv_rows):
            for r0 in range(g0, g0 + pv_rows, rc):
                rows = slice(r0, r0 + rc)
                sc = s_buf[slot, rows, :]
                m_old = m_s[rows, :]
                m_new = jnp.maximum(m_old, jnp.max(sc, axis=-1, keepdims=True))
                alpha = jnp.exp2(m_old - m_new)
                p = jnp.exp2(sc - m_new)
                l_s[rows, :] = alpha * l_s[rows, :] + sum(
                    p[:, t * LANES:(t + 1) * LANES] for t in range(tk // LANES))
                p_buf[rows, :] = p.astype(BF16)
                m_s[rows, :] = m_new
                a_s[rows, :] = alpha
            grp = slice(g0, g0 + pv_rows)
            acc_s[grp, :] = a_s[grp, :] * acc_s[grp, :] + jnp.dot(
                p_buf[grp, :], vb, preferred_element_type=F32)

    scores(0, 0)

    def kv_pair(jj, _):
        j = 2 * jj
        scores(j + 1, 1)
        update(j, 0)
        scores(jnp.minimum(j + 2, n_kv - 1), 0)
        update(j + 1, 1)
        return 0

    lax.fori_loop(0, n_kv // 2, kv_pair, 0)
    lam = (jnp.exp(jnp.sum(lq1_ref[...] * lk1_ref[...], axis=-1, keepdims=True))
           - jnp.exp(jnp.sum(lq2_ref[...] * lk2_ref[...], axis=-1, keepdims=True)) + lambda_init)
    l = jnp.sum(l_s[...], axis=-1, keepdims=True)
    o = acc_s[0:tq] / l[0:tq] - lam * (acc_s[tq:2 * tq] / l[tq:2 * tq])
    o_ref[...] = (_rms(o, g_ref[...]) * (1.0 - lambda_init)).astype(o_ref.dtype)


def _diff_attention(qkv, lq1, lk1, lq2, lk2, subln_g, lambda_init, *, tq=512, tk=1024, rc=32):
    s = qkv.shape[0]
    vd = 2 * HEAD_DIM
    tq = min(tq, s)
    tk = min(tk, s // 2)
    assert (s // tk) % 2 == 0
    kern = functools.partial(_diff_attn_kernel, tq=tq, tk=tk, rc=rc, pv_rows=tq, n_kv=s // tk,
                             lambda_init=lambda_init)
    vec = lambda a: a.reshape(1, -1)
    vspec = lambda w: pl.BlockSpec((1, w), lambda h, i: (0, 0))
    return pl.pallas_call(
        kern,
        out_shape=jax.ShapeDtypeStruct((s, DIFF_HEADS * vd), BF16),
        grid=(DIFF_HEADS, s // tq),
        in_specs=[
            pl.BlockSpec((tq, vd), lambda h, i: (i, h)),
            pl.BlockSpec((s, vd), lambda h, i: (0, DIFF_HEADS + h)),
            pl.BlockSpec((s, vd), lambda h, i: (0, 2 * DIFF_HEADS + h)),
            vspec(HEAD_DIM), vspec(HEAD_DIM), vspec(HEAD_DIM), vspec(HEAD_DIM), vspec(vd),
        ],
        out_specs=pl.BlockSpec((tq, vd), lambda h, i: (i, h)),
        scratch_shapes=[pltpu.VMEM((2, 2 * tq, tk), F32), pltpu.VMEM((2 * tq, tk), BF16),
                        pltpu.VMEM((2 * tq, 1), F32), pltpu.VMEM((2 * tq, 1), F32),
                        pltpu.VMEM((2 * tq, LANES), F32), pltpu.VMEM((2 * tq, vd), F32)],
        compiler_params=_params("parallel", "arbitrary"),
        name="diff_attention",
    )(qkv, qkv, qkv, vec(lq1), vec(lk1), vec(lq2), vec(lk2), vec(subln_g))


def _pro_identity(rows, vecs):
    return rows[0]


def _win_attn_kernel(sink_ref, q_ref, k_ref, v_ref, o_ref, *, tq, win, s_len, group):
    kvh = pl.program_id(0)
    i = pl.program_id(1)
    start = jnp.clip(i * tq - WINDOW, 0, s_len - win)
    start = pl.multiple_of(start, WINDOW)
    kw = k_ref[pl.ds(start, win), :]
    vw = v_ref[pl.ds(start, win), :]
    qpos = i * tq + lax.broadcasted_iota(jnp.int32, (tq, win), 0)
    kpos = start + lax.broadcasted_iota(jnp.int32, (tq, win), 1)
    valid = jnp.abs(kpos - qpos) <= WINDOW
    for g in range(group):
        qg = q_ref[:, g * HEAD_DIM:(g + 1) * HEAD_DIM]
        sc = lax.dot_general(qg, kw, (((1,), (1,)), ((), ())), preferred_element_type=F32)
        sc = jnp.where(valid, sc, NEG_INF)
        sink = sink_ref[kvh * group + g] * math.log2(math.e)
        m = jnp.maximum(jnp.max(sc, axis=-1, keepdims=True), sink)
        e = jnp.exp2(sc - m)
        den = jnp.sum(e, axis=-1, keepdims=True) + jnp.exp2(sink - m)
        o = jnp.dot(e.astype(BF16), vw, preferred_element_type=F32) / den
        o_ref[:, g * HEAD_DIM:(g + 1) * HEAD_DIM] = o.astype(o_ref.dtype)


def _window_attention(qkv, sink, *, tq=256):
    s = qkv.shape[0]
    group = WIN_Q_HEADS // WIN_KV_HEADS
    tq = min(tq, s)
    win = min(tq + 2 * WINDOW, s)
    qw = group * HEAD_DIM
    k0 = WIN_Q_HEADS
    v0 = WIN_Q_HEADS + WIN_KV_HEADS
    kern = functools.partial(_win_attn_kernel, tq=tq, win=win, s_len=s, group=group)
    return pl.pallas_call(
        kern,
        out_shape=jax.ShapeDtypeStruct((s, WIN_Q_HEADS * HEAD_DIM), BF16),
        grid=(WIN_KV_HEADS, s // tq),
        in_specs=[
            pl.BlockSpec(memory_space=pltpu.SMEM),
            pl.BlockSpec((tq, qw), lambda h, i: (i, h)),
            pl.BlockSpec((s, HEAD_DIM), lambda h, i: (0, k0 + h)),
            pl.BlockSpec((s, HEAD_DIM), lambda h, i: (0, v0 + h)),
        ],
        out_specs=pl.BlockSpec((tq, qw), lambda h, i: (i, h)),
        compiler_params=_params("parallel", "arbitrary"),
        name="window_attention",
    )(sink.astype(F32), qkv, qkv, qkv)


def _s5_discretize(a_re, a_im, log_dt, b_re, b_im):
    dt = jnp.exp(log_dt)[:, None]
    mag = jnp.exp(dt * a_re)
    lr, li = mag * jnp.cos(dt * a_im), mag * jnp.sin(dt * a_im)
    den = a_re * a_re + a_im * a_im
    nr, ni = lr - 1.0, li
    fr = (nr * a_re + ni * a_im) / den
    fi = (ni * a_re - nr * a_im) / den
    bbr = fr[..., None] * b_re - fi[..., None] * b_im
    bbi = fr[..., None] * b_im + fi[..., None] * b_re
    return lr, li, bbr, bbi


def _s5_block_weights(lr, li, bbr, bbi, c_re, c_im, gpc):
    g, n, c = bbr.shape
    n_k = g // gpc
    eye = jnp.eye(gpc, dtype=F32)

    def w_in(bb):
        t = bb.reshape(n_k, gpc, n, c)
        return jnp.einsum('kgnc,gh->kgchn', t, eye).reshape(n_k, gpc * c, gpc * n)

    def w_out(cc):
        t = cc.reshape(n_k, gpc, c, n)
        return jnp.einsum('kgcn,gh->khngc', t, eye).reshape(n_k, gpc * n, gpc * c)

    win = jnp.concatenate([w_in(bbr), w_in(bbi)], axis=2)
    wout = jnp.concatenate([w_out(c_re), -w_out(c_im)], axis=1)
    lam = jnp.stack([lr.reshape(n_k, gpc * n), li.reshape(n_k, gpc * n)], axis=1)
    return win, wout, lam


def _s5_kernel(uf_ref, ub_ref, win_ref, wout_ref, lam_ref, yf_ref, yb_ref,
               xf_s, xb_s, st_s, *, tc, ns, cw, nsub):
    i = pl.program_id(1)

    @pl.when(i == 0)
    def _():
        st_s[...] = jnp.zeros_like(st_s)

    for c in range(nsub):
        cols = slice(c * cw, (c + 1) * cw)
        xf_s[c] = jnp.dot(uf_ref[:, cols].astype(BF16), win_ref[0, c].astype(BF16),
                          preferred_element_type=F32)
        xb_s[c] = jnp.dot(ub_ref[:, cols].astype(BF16), win_ref[1, c].astype(BF16),
                          preferred_element_type=F32)
    lam = [[(lam_ref[d, c, 0:1, :], lam_ref[d, c, 1:2, :]) for c in range(nsub)]
           for d in range(2)]

    def step(x_s, c, row, lr, li, sr, si):
        nr = lr * sr - li * si + x_s[c, pl.ds(row, 1), 0:ns]
        ni = lr * si + li * sr + x_s[c, pl.ds(row, 1), ns:2 * ns]
        x_s[c, pl.ds(row, 1), 0:ns] = nr
        x_s[c, pl.ds(row, 1), ns:2 * ns] = ni
        return nr, ni

    def body(r, carry):
        out = []
        for c in range(nsub):
            srf, sif, srb, sib = carry[4 * c:4 * c + 4]
            srf, sif = step(xf_s, c, r, *lam[0][c], srf, sif)
            srb, sib = step(xb_s, c, tc - 1 - r, *lam[1][c], srb, sib)
            out += [srf, sif, srb, sib]
        return tuple(out)

    init = tuple(st_s[j:j + 1, :] for j in range(4 * nsub))
    final = lax.fori_loop(0, tc, body, init, unroll=8)
    for j in range(4 * nsub):
        st_s[j:j + 1, :] = final[j]
    for c in range(nsub):
        cols = slice(c * cw, (c + 1) * cw)
        yf_ref[:, cols] = jnp.dot(xf_s[c].astype(BF16), wout_ref[0, c].astype(BF16),
                                  preferred_element_type=F32)
        yb_ref[:, cols] = jnp.dot(xb_s[c].astype(BF16), wout_ref[1, c].astype(BF16),
                                  preferred_element_type=F32)


def _s5_scan(u, win, wout, lam, *, tc=256, nsub=2):
    s, w = u.shape
    _, n_k, cw, ns2 = win.shape
    ns = ns2 // 2
    tc = min(tc, s)
    n_t = s // tc
    nsub = min(nsub, n_k)
    bw = nsub * cw
    kern = functools.partial(_s5_kernel, tc=tc, ns=ns, cw=cw, nsub=nsub)
    return pl.pallas_call(
        kern,
        out_shape=(jax.ShapeDtypeStruct((s, w), F32), jax.ShapeDtypeStruct((s, w), F32)),
        grid=(n_k // nsub, n_t),
        in_specs=[
            pl.BlockSpec((tc, bw), lambda k, i: (i, k)),
            pl.BlockSpec((tc, bw), lambda k, i: (n_t - 1 - i, k)),
            pl.BlockSpec((2, nsub, cw, ns2), lambda k, i: (0, k, 0, 0)),
            pl.BlockSpec((2, nsub, ns2, cw), lambda k, i: (0, k, 0, 0)),
            pl.BlockSpec((2, nsub, 2, ns), lambda k, i: (0, k, 0, 0)),
        ],
        out_specs=(pl.BlockSpec((tc, bw), lambda k, i: (i, k)),
                   pl.BlockSpec((tc, bw), lambda k, i: (n_t - 1 - i, k))),
        scratch_shapes=[pltpu.VMEM((nsub, tc, ns2), F32), pltpu.VMEM((nsub, tc, ns2), F32),
                        pltpu.VMEM((4 * nsub, ns), F32)],
        compiler_params=_params("parallel", "arbitrary"),
        name="s5_scan",
    )(u, u, win, wout, lam)


def _pro_s5_glu(rows, vecs):
    yf, yb, u = rows
    return _gelu_tanh(yf + yb + vecs[0] * u)


def _epi_glu(acc, epi_refs, o_ref, j, a32_s):
    o_ref[...] = (a32_s[...] * _sigmoid(acc)).astype(o_ref.dtype)


def _route(x, g, w_r):
    h = _rms(x, g)
    logits = jnp.dot(h.astype(BF16), w_r, preferred_element_type=F32)
    tm = logits.shape[0]
    lane = lax.broadcasted_iota(jnp.int32, (tm, LANES), 1)
    big = jnp.int32(LANES)
    ninf = -jnp.inf
    gl = jnp.where(lane < MOE_GROUPS, logits, ninf)
    gm = jnp.max(gl, axis=-1, keepdims=True)
    ge = jnp.exp(gl - gm)
    g_prob = ge / jnp.sum(ge, axis=-1, keepdims=True)
    g_p = jnp.max(g_prob, axis=-1, keepdims=True)
    g_idx = jnp.min(jnp.where(g_prob == g_p, lane, big), axis=-1, keepdims=True)
    lo = MOE_GROUPS + g_idx * EXPERTS_PER_GROUP
    in_grp = (lane >= lo) & (lane < lo + EXPERTS_PER_GROUP)
    el = jnp.where(in_grp, logits, ninf)
    em = jnp.max(el, axis=-1, keepdims=True)
    ee = jnp.exp(el - em)
    e_prob = jnp.where(in_grp, ee / jnp.sum(ee, axis=-1, keepdims=True), -1.0)
    p1 = jnp.max(e_prob, axis=-1, keepdims=True)
    i1 = jnp.min(jnp.where(e_prob == p1, lane, big), axis=-1, keepdims=True)
    rest = jnp.where(lane == i1, -1.0, e_prob)
    p2 = jnp.max(rest, axis=-1, keepdims=True)
    i2 = jnp.min(jnp.where(rest == p2, lane, big), axis=-1, keepdims=True)
    denom = p1 + p2
    w1 = g_p * (p1 / denom)
    w2 = g_p * (p2 / denom)
    idx = jnp.where(lane == 0, i1 - MOE_GROUPS, jnp.where(lane == 1, i2 - MOE_GROUPS, 0))
    wt = jnp.where(lane == 0, w1, jnp.where(lane == 1, w2, 0.0))
    return h, idx, wt


def _pack_bf16_pairs(h):
    half = h.shape[1] // 2
    lo = pltpu.bitcast(h[:, :half].astype(BF16).astype(F32), jnp.uint32)
    hi = pltpu.bitcast(h[:, half:].astype(BF16).astype(F32), jnp.uint32)
    return (lo >> 16) | (hi & jnp.uint32(0xFFFF0000))


def _unpack_bf16_pairs(w):
    lo = pltpu.bitcast(w << 16, F32)
    hi = pltpu.bitcast(w & jnp.uint32(0xFFFF0000), F32)
    return jnp.concatenate([lo, hi], axis=1).astype(BF16)


def _router_kernel(x_ref, g_ref, w_ref, h_ref, idx_ref, wt_ref):
    h, idx, wt = _route(x_ref[...], g_ref[...], w_ref[...])
    h_ref[...] = _pack_bf16_pairs(h)
    idx_ref[...] = idx
    wt_ref[...] = wt


def _router(x, g, w_group, w_expert, *, tm=512):
    s, d = x.shape
    tm = min(tm, s)
    w_r = jnp.concatenate(
        [w_group, w_expert, jnp.zeros((d, LANES - MOE_GROUPS - N_EXPERTS), F32)], axis=1)
    return pl.pallas_call(
        _router_kernel,
        out_shape=(jax.ShapeDtypeStruct((s, d // 2), jnp.uint32),
                   jax.ShapeDtypeStruct((s, LANES), jnp.int32),
                   jax.ShapeDtypeStruct((s, LANES), F32)),
        grid=(s // tm,),
        in_specs=[pl.BlockSpec((tm, d), lambda i: (i, 0)), pl.BlockSpec((1, d), lambda i: (0, 0)),
                  pl.BlockSpec((d, LANES), lambda i: (0, 0))],
        out_specs=(pl.BlockSpec((tm, d // 2), lambda i: (i, 0)),
                   pl.BlockSpec((tm, LANES), lambda i: (i, 0)),
                   pl.BlockSpec((tm, LANES), lambda i: (i, 0))),
        compiler_params=_params("parallel"),
        name="moe_router",
    )(x, g.reshape(1, d), w_r.astype(BF16))


def _rank_kernel(idx_ref, dest_ref, cnt_ref, tot_s, pst_s, run_s, *, tm):
    ph = pl.program_id(0)
    i = pl.program_id(1)
    lane = lax.broadcasted_iota(jnp.int32, (tm, LANES), 1)
    idx = idx_ref[...]
    oh0 = (lane == idx[:, 0:1]).astype(F32)
    oh1 = (lane == idx[:, 1:2]).astype(F32)
    c = oh0 + oh1
    csum = jnp.sum(c, axis=0, keepdims=True)

    @pl.when(jnp.logical_and(ph == 0, i == 0))
    def _():
        tot_s[...] = jnp.zeros_like(tot_s)

    @pl.when(ph == 0)
    def _():
        tot_s[...] += csum

    @pl.when(jnp.logical_and(ph == 1, i == 0))
    def _():
        counts = tot_s[...]
        nblk = jnp.floor((counts + (EXPERT_BLOCK - 1)) * (1.0 / EXPERT_BLOCK))
        r = lax.broadcasted_iota(jnp.int32, (LANES, LANES), 0)
        cc = lax.broadcasted_iota(jnp.int32, (LANES, LANES), 1)
        upper = (r < cc).astype(F32)
        excl = jnp.dot(jnp.broadcast_to(nblk, (SUBLANES, LANES)), upper,
                       preferred_element_type=F32, precision=lax.Precision.HIGHEST)
        pst_s[...] = excl[0:1, :] * EXPERT_BLOCK
        run_s[...] = jnp.zeros_like(run_s)
        cnt_ref[...] = jnp.broadcast_to(counts, cnt_ref.shape)

    @pl.when(ph == 1)
    def _():
        rr = lax.broadcasted_iota(jnp.int32, (tm, tm), 0)
        cr = lax.broadcasted_iota(jnp.int32, (tm, tm), 1)
        lower = (rr > cr).astype(BF16)
        before = jnp.dot(lower, c.astype(BF16), preferred_element_type=F32) + run_s[...]
        base = pst_s[...] + before
        d0 = jnp.sum(oh0 * base, axis=-1, keepdims=True)
        d1 = jnp.sum(oh1 * (base + oh0), axis=-1, keepdims=True)
        dest_ref[...] = jnp.where(lane == 0, d0, jnp.where(lane == 1, d1, 0.0)).astype(jnp.int32)
        run_s[...] += csum


def _dispatch_rank(idx, *, tm=512):
    s = idx.shape[0]
    tm = min(tm, s)
    kern = functools.partial(_rank_kernel, tm=tm)
    return pl.pallas_call(
        kern,
        out_shape=(jax.ShapeDtypeStruct((s, LANES), jnp.int32),
                   jax.ShapeDtypeStruct((SUBLANES, LANES), F32)),
        grid=(2, s // tm),
        in_specs=[pl.BlockSpec((tm, LANES), lambda ph, i: (i, 0))],
        out_specs=(pl.BlockSpec((tm, LANES), lambda ph, i: (i * ph, 0)),
                   pl.BlockSpec((SUBLANES, LANES), lambda ph, i: (0, 0))),
        scratch_shapes=[pltpu.VMEM((1, LANES), F32) for _ in range(3)],
        compiler_params=_params("arbitrary", "arbitrary"),
        name="moe_rank",
    )(idx)


def _block_experts(counts, n_blk):
    cnt = counts[0, :N_EXPERTS].astype(jnp.int32)
    padded = ((cnt + EXPERT_BLOCK - 1) // EXPERT_BLOCK) * EXPERT_BLOCK
    pends = jnp.cumsum(padded)
    blk_start = jnp.arange(n_blk, dtype=jnp.int32) * EXPERT_BLOCK
    owner = jnp.sum((pends[None, :] <= blk_start[:, None]).astype(jnp.int32), axis=1)
    ids = jnp.arange(N_EXPERTS, dtype=jnp.int32)
    last_e = jnp.max(jnp.where(cnt > 0, ids, 0))
    n_used = (pends[-1] // EXPERT_BLOCK).reshape(1)
    blk_e = jnp.minimum(owner, last_e).astype(jnp.int32)
    later = jnp.logical_and(ids[None, :] > ids[:, None], cnt[None, :] > 0)
    nxt = jnp.min(jnp.where(later, ids[None, :], N_EXPERTS), axis=1)
    nxt = jnp.where(nxt == N_EXPERTS, ids, nxt).astype(jnp.int32)
    nxt2 = nxt[nxt]
    parity = ((jnp.cumsum((cnt > 0).astype(jnp.int32)) - 1) % 2).astype(jnp.int32)
    pad_start = (pends - padded + cnt).astype(jnp.int32)
    pad_len = (padded - cnt).astype(jnp.int32)
    tables = (blk_e, nxt[blk_e], nxt2[blk_e], parity[blk_e], n_used.astype(jnp.int32))
    return tables, pad_start, pad_len


def _scatter_kernel(dest_ref, ps_ref, pl_ref, nu_ref, h_ref, xs_hbm, h_s, z_s, sem, zsem,
                    *, tb, n_b, n_blk):
    b = pl.program_id(0)
    slot = b % 2
    n = tb * TOP_K
    bits = [1 << k for k in reversed(range(3, (EXPERT_BLOCK - 1).bit_length()))]

    def zero_fill(action):
        def pad_run(e, _):
            start = ps_ref[e]
            head = (-start) & (SUBLANES - 1)
            for r in range(SUBLANES - 1):
                @pl.when(r < head)
                def _():
                    action(pltpu.make_async_copy(z_s.at[pl.ds(0, 1), :],
                                                 xs_hbm.at[pl.ds(start + r, 1), :], zsem.at[0]))
            astart = start + head
            length = pl_ref[e] - head
            for bit in bits:
                @pl.when((length & bit) != 0)
                def _():
                    off = pl.multiple_of(astart + (length & ~(2 * bit - 1)), SUBLANES)
                    action(pltpu.make_async_copy(z_s.at[pl.ds(0, bit), :],
                                                 xs_hbm.at[pl.ds(off, bit), :], zsem.at[0]))
            return 0

        def unused_block(bb, _):
            row0 = pl.multiple_of(bb * EXPERT_BLOCK, EXPERT_BLOCK)
            action(pltpu.make_async_copy(
                z_s, xs_hbm.at[pl.ds(row0, EXPERT_BLOCK), :], zsem.at[0]))
            return 0

        lax.fori_loop(0, N_EXPERTS, pad_run, 0)
        lax.fori_loop(nu_ref[0], n_blk, unused_block, 0)

    @pl.when(b == 0)
    def _():
        z_s[...] = jnp.zeros_like(z_s)
        zero_fill(lambda c: c.start())

    def copy(r, row, sl):
        return pltpu.make_async_copy(h_s.at[sl, pl.ds(r, 1), :], xs_hbm.at[pl.ds(row, 1), :],
                                     sem.at[sl])

    def wait_all(sl):
        def body(r, _):
            copy(0, 0, sl).wait()
            return 0
        lax.fori_loop(0, n, body, 0, unroll=8)

    h_s[slot] = h_ref[...]
    for r in range(tb):
        tok = b * tb + r
        for k in range(TOP_K):
            copy(r, dest_ref[tok * TOP_K + k], slot).start()

    @pl.when(b > 0)
    def _():
        wait_all(1 - slot)

    @pl.when(b == n_b - 1)
    def _():
        wait_all(slot)
        zero_fill(lambda c: c.wait())


def _dispatch_scatter(h, dest_flat, pad_start, pad_len, n_used, n_rows, *, tb=128):
    s, d = h.shape
    tb = min(tb, s)
    n_b = s // tb
    kern = functools.partial(_scatter_kernel, tb=tb, n_b=n_b, n_blk=n_rows // EXPERT_BLOCK)
    grid_spec = pltpu.PrefetchScalarGridSpec(
        num_scalar_prefetch=4,
        grid=(n_b,),
        in_specs=[pl.BlockSpec((tb, d), lambda b, *_: (b, 0))],
        out_specs=pl.BlockSpec(memory_space=pl.ANY),
        scratch_shapes=[pltpu.VMEM((2, tb, d), h.dtype), pltpu.VMEM((EXPERT_BLOCK, d), h.dtype),
                        pltpu.SemaphoreType.DMA((2,)), pltpu.SemaphoreType.DMA((1,))],
    )
    return pl.pallas_call(
        kern,
        out_shape=jax.ShapeDtypeStruct((n_rows, d), h.dtype),
        grid_spec=grid_spec,
        compiler_params=_params("arbitrary"),
        name="moe_scatter",
    )(dest_flat, pad_start, pad_len, n_used, h)


def _row_copy(src_hbm, src_row, dst_buf, slot, dst_row, sem):
    return pltpu.make_async_copy(src_hbm.at[pl.ds(src_row, 1), :],
                                 dst_buf.at[slot, pl.ds(dst_row, 1), :], sem.at[slot])


def _gather_start(idx_ref, base, n, src_hbm, dst_buf, slot, sem):
    for r in range(n):
        _row_copy(src_hbm, idx_ref[base + r], dst_buf, slot, r, sem).start()


def _gather_wait(n, src_hbm, dst_buf, slot, sem):
    def body(r, _):
        _row_copy(src_hbm, 0, dst_buf, slot, r, sem).wait()
        return 0
    lax.fori_loop(0, n, body, 0, unroll=8)


def _expert_kernel(be_ref, nx_ref, nx2_ref, par_ref, nu_ref, x_ref, wg_hbm, wu_hbm, wd_hbm, y_ref,
                   wg_f, wu_f, wd_f, wg_s, wu_s, wd_s, sem, *, layer):
    b = pl.program_id(0)
    e = be_ref[b]
    slot = par_ref[b]

    def fetch(expert, sl):
        return (pltpu.make_async_copy(wg_hbm.at[layer, expert], wg_f.at[sl], sem.at[sl, 0]),
                pltpu.make_async_copy(wu_hbm.at[layer, expert], wu_f.at[sl], sem.at[sl, 1]),
                pltpu.make_async_copy(wd_hbm.at[layer, expert], wd_f.at[sl], sem.at[sl, 2]))

    @pl.when(b == 0)
    def _():
        for c in fetch(e, slot):
            c.start()

        @pl.when(nx_ref[b] != e)
        def _():
            for c in fetch(nx_ref[b], 1 - slot):
                c.start()

    changed = jnp.logical_or(b == 0, e != be_ref[jnp.maximum(b - 1, 0)])

    @pl.when(changed)
    def _():
        for c in fetch(e, slot):
            c.wait()
        wg_s[...] = wg_f[slot].astype(BF16)
        wu_s[...] = wu_f[slot].astype(BF16)
        wd_s[...] = wd_f[slot].astype(BF16)

        @pl.when(nx2_ref[b] != nx_ref[b])
        def _():
            for c in fetch(nx2_ref[b], slot):
                c.start()

    @pl.when(b >= nu_ref[0])
    def _():
        y_ref[...] = jnp.zeros_like(y_ref)

    @pl.when(b < nu_ref[0])
    def _():
        x = _unpack_bf16_pairs(x_ref[...])
        hg = jnp.dot(x, wg_s[...], preferred_element_type=F32)
        hu = jnp.dot(x, wu_s[...], preferred_element_type=F32)
        hdn = (hg * _sigmoid(hg)) * hu
        y_ref[...] = jnp.dot(hdn.astype(BF16), wd_s[...], preferred_element_type=F32)


def _expert_mlp(xs, blk_e, blk_next, blk_next2, blk_par, n_used, layer, w_gate, w_up, w_down):
    n_rows, dp = xs.shape
    d, de = w_gate.shape[2], w_gate.shape[3]
    grid_spec = pltpu.PrefetchScalarGridSpec(
        num_scalar_prefetch=5,
        grid=(n_rows // EXPERT_BLOCK,),
        in_specs=[
            pl.BlockSpec((EXPERT_BLOCK, dp),
                         lambda b, be, nx, nx2, par, nu: (jnp.minimum(b, nu[0] - 1), 0)),
            pl.BlockSpec(memory_space=pl.ANY),
            pl.BlockSpec(memory_space=pl.ANY),
            pl.BlockSpec(memory_space=pl.ANY),
        ],
        out_specs=pl.BlockSpec((EXPERT_BLOCK, d), lambda b, be, nx, nx2, par, nu: (b, 0)),
        scratch_shapes=[pltpu.VMEM((2, d, de), F32), pltpu.VMEM((2, d, de), F32),
                        pltpu.VMEM((2, de, d), F32), pltpu.VMEM((d, de), BF16),
                        pltpu.VMEM((d, de), BF16), pltpu.VMEM((de, d), BF16),
                        pltpu.SemaphoreType.DMA((2, 3))],
    )
    return pl.pallas_call(
        functools.partial(_expert_kernel, layer=layer),
        out_shape=jax.ShapeDtypeStruct((n_rows, d), F32),
        grid_spec=grid_spec,
        compiler_params=_params("arbitrary"),
        name="moe_experts",
    )(blk_e, blk_next, blk_next2, blk_par, n_used, xs, w_gate, w_up, w_down)


def _combine_kernel(dest_ref, x_ref, wt_ref, ys_hbm, g_ref, o_ref, ybuf, sem, *, tb, n_b, final):
    b = pl.program_id(0)
    slot = b % 2
    n = tb * TOP_K

    @pl.when(b == 0)
    def _():
        _gather_start(dest_ref, 0, n, ys_hbm, ybuf, 0, sem)

    _gather_wait(n, ys_hbm, ybuf, slot, sem)
    _gather_start(dest_ref, (b + 1) * n, n, ys_hbm, ybuf, 1 - slot, sem)
    y0 = ybuf[slot, 0:tb, :]
    y1 = ybuf[slot, tb:2 * tb, :]
    wt = wt_ref[...]
    out = x_ref[...] + (y0 * wt[:, 0:1] + y1 * wt[:, 1:2])
    if final:
        out = _rms(out, g_ref[...])
    o_ref[...] = out

    @pl.when(b == n_b - 1)
    def _():
        _gather_wait(n, ys_hbm, ybuf, 1 - slot, sem)


def _moe_combine(x, wt, ys, dest, g_final, *, final, tb=128):
    s, d = x.shape
    tb = min(tb, s)
    n_b = s // tb
    dest = dest.reshape(n_b, tb, TOP_K).transpose(0, 2, 1).reshape(-1)
    dest = jnp.concatenate([dest, jnp.zeros((tb * TOP_K,), jnp.int32)])
    kern = functools.partial(_combine_kernel, tb=tb, n_b=n_b, final=final)
    grid_spec = pltpu.PrefetchScalarGridSpec(
        num_scalar_prefetch=1,
        grid=(n_b,),
        in_specs=[
            pl.BlockSpec((tb, d), lambda b, dr: (b, 0)),
            pl.BlockSpec((tb, LANES), lambda b, dr: (b, 0)),
            pl.BlockSpec(memory_space=pl.ANY),
            pl.BlockSpec((1, d), lambda b, dr: (0, 0)),
        ],
        out_specs=pl.BlockSpec((tb, d), lambda b, dr: (b, 0)),
        scratch_shapes=[pltpu.VMEM((2, tb * TOP_K, d), F32), pltpu.SemaphoreType.DMA((2,))],
    )
    return pl.pallas_call(
        kern,
        out_shape=jax.ShapeDtypeStruct((s, d), F32),
        grid_spec=grid_spec,
        compiler_params=_params("arbitrary"),
        name="moe_combine",
    )(dest, x, wt, ys, g_final.reshape(1, d))


def _hier_moe(x, ln_g, w_group, w_expert, layer, w_gate, w_up, w_down, g_final, *, final):
    s = x.shape[0]
    h, idx, wt = _router(x, ln_g, w_group, w_expert)
    dest, counts = _dispatch_rank(idx)
    dest = dest[:, :TOP_K].reshape(-1)
    n_rows = -(-(s * TOP_K + N_EXPERTS * (EXPERT_BLOCK - 1)) // EXPERT_BLOCK) * EXPERT_BLOCK
    tables, pad_start, pad_len = _block_experts(counts, n_rows // EXPERT_BLOCK)
    xs = _dispatch_scatter(h, dest, pad_start, pad_len, tables[-1], n_rows)
    ys = _expert_mlp(xs, *tables, layer, w_gate, w_up, w_down)
    return _moe_combine(x, wt, ys, dest, g_final, final=final)


def _rglru_layer(x, ln_g, w_in, conv_w, conv_b, w_a, w_x, b_a, b_x, lam, w_out):
    d = x.shape[1]
    c = conv_w.shape[1]
    proj = _norm_matmul(x, ln_g, w_in, name="lru_in_proj")
    yf, yb = _rglru_scan(proj, conv_w, conv_b, w_a, w_x, b_a, b_x, lam)
    return _matmul_residual([(yf, c, 0), (yb, c, 0), (proj, c, 0)], [], w_out, x, _pro_rglru_out,
                            k_dim=c, tm=512, name="lru_out_proj")


def _diff_layer(x, ln_g, rope, w_qkv, lq1, lk1, lq2, lk2, subln_g, w_out, lambda_init):
    qk_cols = 2 * DIFF_HEADS * 2 * HEAD_DIM
    qkv = _norm_matmul(x, ln_g, w_qkv, rope=rope, n_rope_cols=qk_cols, n_q_cols=qk_cols // 2,
                       q_scale=HEAD_DIM ** -0.5 * math.log2(math.e), out_dtype=BF16,
                       name="diff_qkv_proj")
    o = _diff_attention(qkv, lq1, lk1, lq2, lk2, subln_g, lambda_init)
    return _matmul_residual([(o, o.shape[1], 0)], [], w_out, x, _pro_identity,
                            k_dim=o.shape[1], name="diff_out_proj")


def _window_layer(x, ln_g, rope, w_qkv, sink, w_out):
    qk_cols = (WIN_Q_HEADS + WIN_KV_HEADS) * HEAD_DIM
    qkv = _norm_matmul(x, ln_g, w_qkv, rope=rope, n_rope_cols=qk_cols,
                       n_q_cols=WIN_Q_HEADS * HEAD_DIM,
                       q_scale=HEAD_DIM ** -0.5 * math.log2(math.e), out_dtype=BF16, tn=512,
                       name="win_qkv_proj")
    o = _window_attention(qkv, sink)
    return _matmul_residual([(o, o.shape[1], 0)], [], w_out, x, _pro_identity,
                            k_dim=o.shape[1], name="win_out_proj")


def _s5_layer(x, ln_g, w_in, a_re, a_im, log_dt, b_re, b_im, c_re, c_im, d_skip, w_glu, w_out,
              *, gpc=16):
    u = _norm_matmul(x, ln_g, w_in, name="s5_in_proj")
    w = u.shape[1]
    wins, wouts, lams = [], [], []
    for dd in range(2):
        lr, li, bbr, bbi = _s5_discretize(a_re[dd], a_im[dd], log_dt[dd], b_re[dd], b_im[dd])
        wi, wo, lm = _s5_block_weights(lr, li, bbr, bbi, c_re[dd], c_im[dd], gpc)
        wins.append(wi)
        wouts.append(wo)
        lams.append(lm)
    yf, yb = _s5_scan(u, jnp.stack(wins).astype(BF16), jnp.stack(wouts).astype(BF16),
                      jnp.stack(lams))
    z = _fused_mm([(yf, w, 0), (yb, w, 0), (u, w, 0)], [d_skip.reshape(1, w)], w_glu, [],
                  _pro_s5_glu, _epi_glu, out_dtype=BF16, tm=512, tn=w, k_dim=w, keep_f32=True,
                  name="s5_glu")
    return _matmul_residual([(z, w, 0)], [], w_out, x, _pro_identity, k_dim=w, name="s5_out_proj")


def kernel(x, positions, ln_mix, ln_ffn, ln_final, lru_w_in, lru_conv_w, lru_conv_b, lru_w_a, lru_w_x, lru_b_a, lru_b_x, lru_lambda, lru_w_out, diff_w_qkv, diff_lq1, diff_lk1, diff_lq2, diff_lk2, diff_subln, diff_w_out, win_w_qkv, win_sink, win_w_out, s5_w_in, s5_a_re, s5_a_im, s5_log_dt, s5_b_re, s5_b_im, s5_c_re, s5_c_im, s5_d, s5_w_glu, s5_w_out, moe_w_group, moe_w_expert, moe_w_gate, moe_w_up, moe_w_down):
    batch, s, d = x.shape
    depth = ln_mix.shape[0]
    outs = []
    for b in range(batch):
        xb = x[b]
        rope = _rope_tables(positions[b])
        for i in range(depth):
            kind, j = i % 4, i // 4
            if kind == 0:
                xb = _rglru_layer(xb, ln_mix[i], lru_w_in[j], lru_conv_w[j], lru_conv_b[j],
                                  lru_w_a[j], lru_w_x[j], lru_b_a[j], lru_b_x[j], lru_lambda[j],
                                  lru_w_out[j])
            elif kind == 1:
                xb = _diff_layer(xb, ln_mix[i], rope, diff_w_qkv[j], diff_lq1[j], diff_lk1[j],
                                 diff_lq2[j], diff_lk2[j], diff_subln[j], diff_w_out[j],
                                 0.8 - 0.6 * math.exp(-0.3 * i))
            elif kind == 2:
                xb = _window_layer(xb, ln_mix[i], rope, win_w_qkv[j], win_sink[j], win_w_out[j])
            else:
                xb = _s5_layer(xb, ln_mix[i], s5_w_in[j], s5_a_re[j], s5_a_im[j], s5_log_dt[j],
                               s5_b_re[j], s5_b_im[j], s5_c_re[j], s5_c_im[j], s5_d[j],
                               s5_w_glu[j], s5_w_out[j])
            xb = _hier_moe(xb, ln_ffn[i], moe_w_group[i], moe_w_expert[i], i, moe_w_gate,
                           moe_w_up, moe_w_down, ln_final, final=(i == depth - 1))
        outs.append(xb)
    return jnp.stack(outs)
```

```python
import functools
import math

import jax
import jax.numpy as jnp
from jax import lax
from jax.experimental import pallas as pl
from jax.experimental.pallas import tpu as pltpu

F32 = jnp.float32
BF16 = jnp.bfloat16

NORM_EPS = 1e-6
NEG_INF = -1e30
LANES = 128
SUBLANES = 8
VMEM_LIMIT = 56 * 1024 * 1024

HEAD_DIM = 128
ROT_DIM = HEAD_DIM // 4
ROPE_THETA = 500000.0
RGLRU_C = 8.0
CONV_W = 4
LRU_BLOCK_W = 128
WINDOW = 128
DIFF_HEADS = 8
WIN_Q_HEADS = 16
WIN_KV_HEADS = 4
SSM_GROUP_CH = 16
SSM_STATE = 64
MOE_GROUPS = 4
EXPERTS_PER_GROUP = 8
N_EXPERTS = MOE_GROUPS * EXPERTS_PER_GROUP
TOP_K = 2
EXPERT_BLOCK = 256


def _params(*sem):
    return pltpu.CompilerParams(dimension_semantics=sem, vmem_limit_bytes=VMEM_LIMIT)


def _rms(x, g):
    ms = jnp.mean(x * x, axis=-1, keepdims=True)
    return x * lax.rsqrt(ms + NORM_EPS) * g


def _gelu_tanh(x):
    return 0.5 * x * (1.0 + jnp.tanh(math.sqrt(2.0 / math.pi) * (x + 0.044715 * (x * x * x))))


def _sigmoid(x):
    return 1.0 / (1.0 + jnp.exp(-x))


def _fused_mm_kernel(*refs, n_row, n_vec, n_epi, prologue, epilogue, keep_f32):
    row_refs = refs[:n_row]
    vec_refs = refs[n_row:n_row + n_vec]
    w_ref = refs[n_row + n_vec]
    epi_refs = refs[n_row + n_vec + 1:n_row + n_vec + 1 + n_epi]
    o_ref = refs[n_row + n_vec + 1 + n_epi]
    a_s = refs[n_row + n_vec + 2 + n_epi]
    a32_s = refs[n_row + n_vec + 3 + n_epi] if keep_f32 else None
    j = pl.program_id(1)

    @pl.when(j == 0)
    def _():
        a = prologue([r[...] for r in row_refs], [v[...] for v in vec_refs])
        a_s[...] = a.astype(BF16)
        if keep_f32:
            a32_s[...] = a

    acc = jnp.dot(a_s[...], w_ref[...].astype(BF16), preferred_element_type=F32)
    epilogue(acc, epi_refs, o_ref, j, a32_s)


def _fused_mm(row_inputs, vec_inputs, w, epi_inputs, prologue, epilogue, *, out_dtype, tm, tn,
              k_dim, keep_f32=False, name):
    s = row_inputs[0][0].shape[0]
    n = w.shape[1]
    tm = min(tm, s)
    tn = min(tn, n)
    w = w.astype(BF16)
    in_specs = []
    args = []
    for arr, width, cb in row_inputs:
        in_specs.append(pl.BlockSpec((tm, width), lambda i, j, cb=cb: (i, cb)))
        args.append(arr)
    for arr in vec_inputs:
        in_specs.append(pl.BlockSpec(arr.shape, lambda i, j: (0, 0)))
        args.append(arr)
    in_specs.append(pl.BlockSpec((k_dim, tn), lambda i, j: (0, j)))
    args.append(w)
    for arr, width, per_tile in epi_inputs:
        if per_tile:
            in_specs.append(pl.BlockSpec((tm, width), lambda i, j: (i, j)))
        else:
            in_specs.append(pl.BlockSpec((tm, width), lambda i, j: (i, 0)))
        args.append(arr)
    scratch = [pltpu.VMEM((tm, k_dim), BF16)]
    if keep_f32:
        scratch.append(pltpu.VMEM((tm, k_dim), F32))
    kern = functools.partial(_fused_mm_kernel, n_row=len(row_inputs), n_vec=len(vec_inputs),
                             n_epi=len(epi_inputs), prologue=prologue, epilogue=epilogue,
                             keep_f32=keep_f32)
    return pl.pallas_call(
        kern,
        out_shape=jax.ShapeDtypeStruct((s, n), out_dtype),
        grid=(s // tm, n // tn),
        in_specs=in_specs,
        out_specs=pl.BlockSpec((tm, tn), lambda i, j: (i, j)),
        scratch_shapes=scratch,
        compiler_params=_params("parallel", "arbitrary"),
        name=name,
    )(*args)


def _pro_rms(rows, vecs):
    return _rms(rows[0], vecs[0])


def _epi_store(acc, epi_refs, o_ref, j, a32_s):
    o_ref[...] = acc.astype(o_ref.dtype)


def _epi_residual(acc, epi_refs, o_ref, j, a32_s):
    o_ref[...] = (epi_refs[0][...] + acc).astype(o_ref.dtype)


def _make_epi_rope(n_rope_tiles, n_q_tiles, q_scale, tn):
    def epi(acc, epi_refs, o_ref, j, a32_s):
        c_ref, s1_ref, s2_ref = epi_refs

        @pl.when(j < n_rope_tiles)
        def _():
            c = c_ref[...]
            s1 = s1_ref[...]
            s2 = s2_ref[...]
            sc = jnp.where(j < n_q_tiles, q_scale, 1.0).astype(F32)
            for hh in range(tn // HEAD_DIM):
                xs = acc[:, hh * HEAD_DIM:(hh + 1) * HEAD_DIM]
                rot = (xs * c + pltpu.roll(xs, HEAD_DIM - ROT_DIM // 2, 1) * s1
                       + pltpu.roll(xs, ROT_DIM // 2, 1) * s2) * sc
                o_ref[:, hh * HEAD_DIM:(hh + 1) * HEAD_DIM] = rot.astype(o_ref.dtype)

        @pl.when(j >= n_rope_tiles)
        def _():
            o_ref[...] = acc.astype(o_ref.dtype)

    return epi


def _rope_tables(positions):
    half = ROT_DIM // 2
    inv = ROPE_THETA ** (-jnp.arange(0, ROT_DIM, 2, dtype=F32) / ROT_DIM)
    ang = positions.astype(F32)[:, None] * inv
    cos, sin = jnp.cos(ang), jnp.sin(ang)
    s = positions.shape[0]
    ones = jnp.ones((s, HEAD_DIM - ROT_DIM), F32)
    zeros = jnp.zeros((s, HEAD_DIM - ROT_DIM), F32)
    zh = jnp.zeros((s, half), F32)
    c_tab = jnp.concatenate([cos, cos, ones], axis=1)
    s1_tab = jnp.concatenate([-sin, zh, zeros], axis=1)
    s2_tab = jnp.concatenate([zh, sin, zeros], axis=1)
    return c_tab, s1_tab, s2_tab


def _norm_matmul(x, g, w, *, rope=None, n_rope_cols=0, n_q_cols=0, q_scale=1.0, out_dtype=F32,
                 tm=1024, tn=1024, name):
    d = x.shape[1]
    if rope is None:
        epi, epi_inputs = _epi_store, []
    else:
        tn = min(tn, w.shape[1])
        assert n_rope_cols % tn == 0 and n_q_cols % tn == 0
        epi = _make_epi_rope(n_rope_cols // tn, n_q_cols // tn, q_scale, tn)
        epi_inputs = [(t, HEAD_DIM, False) for t in rope]
    return _fused_mm([(x, d, 0)], [g.reshape(1, d)], w, epi_inputs, _pro_rms, epi,
                     out_dtype=out_dtype, tm=tm, tn=tn, k_dim=d, name=name)


def _matmul_residual(row_inputs, vec_inputs, w, res, prologue, *, k_dim, tm=1024, tn=1024, name):
    return _fused_mm(row_inputs, vec_inputs, w, [(res, min(tn, w.shape[1]), True)], prologue,
                     _epi_residual, out_dtype=F32, tm=tm, tn=tn, k_dim=k_dim, name=name)


def _rglru_kernel(xf_ref, xfp_ref, xfn_ref, xb_ref, xbp_ref, xbn_ref, cw_ref, cb_ref, wa_ref,
                  wx_ref, ba_ref, bx_ref, lam_ref, yf_ref, yb_ref,
                  ext_s, af_s, bf_s, ab_s, bb_s, hf_s, hb_s, *, tc, cw, n_t):
    i = pl.program_id(1)
    halo = SUBLANES

    @pl.when(i == 0)
    def _():
        hf_s[...] = jnp.zeros_like(hf_s)
        hb_s[...] = jnp.zeros_like(hb_s)

    def gates(x_ref, xp_ref, xn_ref, chunk, d, a_s, b_s):
        prev = jnp.where(chunk == 0, 0.0, xp_ref[...])
        nxt = jnp.where(chunk == n_t - 1, 0.0, xn_ref[...])
        ext_s[0:halo, :] = prev
        ext_s[halo:halo + tc, :] = x_ref[...]
        ext_s[halo + tc:halo + tc + halo, :] = nxt
        xc = cb_ref[...] + sum(
            cw_ref[k:k + 1, :] * ext_s[halo - 2 + k:halo - 2 + k + tc, :] for k in range(CONV_W))
        lam = lam_ref[d:d + 1, :]
        z = -lam
        sp = jnp.maximum(z, 0.0) + jnp.log1p(jnp.exp(-jnp.abs(z)))
        for blk in range(cw // LRU_BLOCK_W):
            sl = slice(blk * LRU_BLOCK_W, (blk + 1) * LRU_BLOCK_W)
            xb = xc[:, sl]
            xbh = xb.astype(BF16)
            r = _sigmoid(jnp.dot(xbh, wa_ref[d, blk].astype(BF16), preferred_element_type=F32)
                         + ba_ref[d:d + 1, sl])
            ig = _sigmoid(jnp.dot(xbh, wx_ref[d, blk].astype(BF16), preferred_element_type=F32)
                          + bx_ref[d:d + 1, sl])
            log_a = (-RGLRU_C) * r * sp[:, sl]
            a_s[:, sl] = jnp.exp(log_a)
            th = jnp.tanh(log_a)
            b_s[:, sl] = jnp.sqrt(-2.0 * th / (1.0 - th)) * (ig * xb)

    gates(xf_ref, xfp_ref, xfn_ref, i, 0, af_s, bf_s)
    gates(xb_ref, xbp_ref, xbn_ref, n_t - 1 - i, 1, ab_s, bb_s)

    def body(r, carry):
        hf, hb = carry
        hf = af_s[pl.ds(r, 1), :] * hf + bf_s[pl.ds(r, 1), :]
        yf_ref[pl.ds(r, 1), :] = hf
        rb = tc - 1 - r
        hb = ab_s[pl.ds(rb, 1), :] * hb + bb_s[pl.ds(rb, 1), :]
        yb_ref[pl.ds(rb, 1), :] = hb
        return hf, hb

    hf, hb = lax.fori_loop(0, tc, body, (hf_s[...], hb_s[...]), unroll=8)
    hf_s[...] = hf
    hb_s[...] = hb


def _rglru_scan(proj, conv_w, conv_b, w_a, w_x, b_a, b_x, lam, *, tc=256, cw=2048):
    s = proj.shape[0]
    c = conv_w.shape[1]
    tc = min(tc, s)
    n_t = s // tc
    n_c = c // cw
    xoff = c // cw
    hb = tc // SUBLANES
    last_h = s // SUBLANES - 1

    specs = [
        pl.BlockSpec((tc, cw), lambda ci, i: (i, xoff + ci)),
        pl.BlockSpec((SUBLANES, cw), lambda ci, i: (jnp.maximum(i * hb - 1, 0), xoff + ci)),
        pl.BlockSpec((SUBLANES, cw), lambda ci, i: (jnp.minimum((i + 1) * hb, last_h), xoff + ci)),
        pl.BlockSpec((tc, cw), lambda ci, i: (n_t - 1 - i, xoff + ci)),
        pl.BlockSpec((SUBLANES, cw),
                     lambda ci, i: (jnp.maximum((n_t - 1 - i) * hb - 1, 0), xoff + ci)),
        pl.BlockSpec((SUBLANES, cw),
                     lambda ci, i: (jnp.minimum((n_t - i) * hb, last_h), xoff + ci)),
        pl.BlockSpec((CONV_W, cw), lambda ci, i: (0, ci)),
        pl.BlockSpec((1, cw), lambda ci, i: (0, ci)),
        pl.BlockSpec((2, cw // LRU_BLOCK_W, LRU_BLOCK_W, LRU_BLOCK_W), lambda ci, i: (0, ci, 0, 0)),
        pl.BlockSpec((2, cw // LRU_BLOCK_W, LRU_BLOCK_W, LRU_BLOCK_W), lambda ci, i: (0, ci, 0, 0)),
        pl.BlockSpec((2, cw), lambda ci, i: (0, ci)),
        pl.BlockSpec((2, cw), lambda ci, i: (0, ci)),
        pl.BlockSpec((2, cw), lambda ci, i: (0, ci)),
    ]
    kern = functools.partial(_rglru_kernel, tc=tc, cw=cw, n_t=n_t)
    return pl.pallas_call(
        kern,
        out_shape=(jax.ShapeDtypeStruct((s, c), F32), jax.ShapeDtypeStruct((s, c), F32)),
        grid=(n_c, n_t),
        in_specs=specs,
        out_specs=(pl.BlockSpec((tc, cw), lambda ci, i: (i, ci)),
                   pl.BlockSpec((tc, cw), lambda ci, i: (n_t - 1 - i, ci))),
        scratch_shapes=[pltpu.VMEM((tc + 2 * SUBLANES, cw), F32)]
        + [pltpu.VMEM((tc, cw), F32) for _ in range(4)]
        + [pltpu.VMEM((1, cw), F32) for _ in range(2)],
        compiler_params=_params("parallel", "arbitrary"),
        name="rglru_scan",
    )(proj, proj, proj, proj, proj, proj, conv_w, conv_b.reshape(1, c), w_a, w_x, b_a, b_x, lam)


def _pro_rglru_out(rows, vecs):
    yf, yb, gate = rows
    return (yf + yb) * _gelu_tanh(gate)


def _diff_attn_kernel(q_ref, k_ref, v_ref, lq1_ref, lk1_ref, lq2_ref, lk2_ref, g_ref, o_ref,
                      s_buf, p_buf, m_s, a_s, l_s, acc_s, *, tq, tk, rc, pv_rows, n_kv,
                      lambda_init):
    m_s[...] = jnp.full_like(m_s, -jnp.inf)
    l_s[...] = jnp.zeros_like(l_s)
    acc_s[...] = jnp.zeros_like(acc_s)
    q = q_ref[...]
    qs = (q[:, :HEAD_DIM], q[:, HEAD_DIM:])

    def scores(j, slot):
        off = pl.multiple_of(j * tk, tk)
        kb = k_ref[pl.ds(off, tk), :]
        for c in range(2):
            kc = kb[:, c * HEAD_DIM:(c + 1) * HEAD_DIM]
            s_buf[slot, c * tq:(c + 1) * tq, :] = lax.dot_general(
                qs[c], kc, (((1,), (1,)), ((), ())), preferred_element_type=F32)

    def update(j, slot):
        off = pl.multiple_of(j * tk, tk)
        vb = v_ref[pl.ds(off, tk), :]
        for g0 in range(0, 2 * tq, pv_rows):
            for r0 in range(g0, g0 + pv_rows, rc):
                rows = slice(r0, r0 + rc)
                sc = s_buf[slot, rows, :]
                m_old = m_s[rows, :]
                m_new = jnp.maximum(m_old, jnp.max(sc, axis=-1, keepdims=True))
                alpha = jnp.exp2(m_old - m_new)
                p = jnp.exp2(sc - m_new)
                l_s[rows, :] = alpha * l_s[rows, :] + sum(
                    p[:, t * LANES:(t + 1) * LANES] for t in range(tk // LANES))
                p_buf[rows, :] = p.astype(BF16)
                m_s[rows, :] = m_new
                a_s[rows, :] = alpha
            grp = slice(g0, g0 + pv_rows)
            acc_s[grp, :] = a_s[grp, :] * acc_s[grp, :] + jnp.dot(
                p_buf[grp, :], vb, preferred_element_type=F32)

    scores(0, 0)

    def kv_pair(jj, _):
        j = 2 * jj
        scores(j + 1, 1)
        update(j, 0)
        scores(jnp.minimum(j + 2, n_kv - 1), 0)
        update(j + 1, 1)
        return 0

    lax.fori_loop(0, n_kv // 2, kv_pair, 0)
    lam = (jnp.exp(jnp.sum(lq1_ref[...] * lk1_ref[...], axis=-1, keepdims=True))
           - jnp.exp(jnp.sum(lq2_ref[...] * lk2_ref[...], axis=-1, keepdims=True)) + lambda_init)
    l = jnp.sum(l_s[...], axis=-1, keepdims=True)
    o = acc_s[0:tq] / l[0:tq] - lam * (acc_s[tq:2 * tq] / l[tq:2 * tq])
    o_ref[...] = (_rms(o, g_ref[...]) * (1.0 - lambda_init)).astype(o_ref.dtype)


def _diff_attention(qkv, lq1, lk1, lq2, lk2, subln_g, lambda_init, *, tq=512, tk=1024, rc=32):
    s = qkv.shape[0]
    vd = 2 * HEAD_DIM
    tq = min(tq, s)
    tk = min(tk, s // 2)
    assert (s // tk) % 2 == 0
    kern = functools.partial(_diff_attn_kernel, tq=tq, tk=tk, rc=rc, pv_rows=tq, n_kv=s // tk,
                             lambda_init=lambda_init)
    vec = lambda a: a.reshape(1, -1)
    vspec = lambda w: pl.BlockSpec((1, w), lambda h, i: (0, 0))
    return pl.pallas_call(
        kern,
        out_shape=jax.ShapeDtypeStruct((s, DIFF_HEADS * vd), BF16),
        grid=(DIFF_HEADS, s // tq),
        in_specs=[
            pl.BlockSpec((tq, vd), lambda h, i: (i, h)),
            pl.BlockSpec((s, vd), lambda h, i: (0, DIFF_HEADS + h)),
            pl.BlockSpec((s, vd), lambda h, i: (0, 2 * DIFF_HEADS + h)),
            vspec(HEAD_DIM), vspec(HEAD_DIM), vspec(HEAD_DIM), vspec(HEAD_DIM), vspec(vd),
        ],
        out_specs=pl.BlockSpec((tq, vd), lambda h, i: (i, h)),
        scratch_shapes=[pltpu.VMEM((2, 2 * tq, tk), F32), pltpu.VMEM((2 * tq, tk), BF16),
                        pltpu.VMEM((2 * tq, 1), F32), pltpu.VMEM((2 * tq, 1), F32),
                        pltpu.VMEM((2 * tq, LANES), F32), pltpu.VMEM((2 * tq, vd), F32)],
        compiler_params=_params("parallel", "arbitrary"),
        name="diff_attention",
    )(qkv, qkv, qkv, vec(lq1), vec(lk1), vec(lq2), vec(lk2), vec(subln_g))


def _pro_identity(rows, vecs):
    return rows[0]


def _win_attn_kernel(sink_ref, q_ref, k_ref, v_ref, o_ref, *, tq, win, s_len, group):
    kvh = pl.program_id(0)
    i = pl.program_id(1)
    start = jnp.clip(i * tq - WINDOW, 0, s_len - win)
    start = pl.multiple_of(start, WINDOW)
    kw = k_ref[pl.ds(start, win), :]
    vw = v_ref[pl.ds(start, win), :]
    qpos = i * tq + lax.broadcasted_iota(jnp.int32, (tq, win), 0)
    kpos = start + lax.broadcasted_iota(jnp.int32, (tq, win), 1)
    valid = jnp.abs(kpos - qpos) <= WINDOW
    for g in range(group):
        qg = q_ref[:, g * HEAD_DIM:(g + 1) * HEAD_DIM]
        sc = lax.dot_general(qg, kw, (((1,), (1,)), ((), ())), preferred_element_type=F32)
        sc = jnp.where(valid, sc, NEG_INF)
        sink = sink_ref[kvh * group + g] * math.log2(math.e)
        m = jnp.maximum(jnp.max(sc, axis=-1, keepdims=True), sink)
        e = jnp.exp2(sc - m)
        den = jnp.sum(e, axis=-1, keepdims=True) + jnp.exp2(sink - m)
        o = jnp.dot(e.astype(BF16), vw, preferred_element_type=F32) / den
        o_ref[:, g * HEAD_DIM:(g + 1) * HEAD_DIM] = o.astype(o_ref.dtype)


def _window_attention(qkv, sink, *, tq=256):
    s = qkv.shape[0]
    group = WIN_Q_HEADS // WIN_KV_HEADS
    tq = min(tq, s)
    win = min(tq + 2 * WINDOW, s)
    qw = group * HEAD_DIM
    k0 = WIN_Q_HEADS
    v0 = WIN_Q_HEADS + WIN_KV_HEADS
    kern = functools.partial(_win_attn_kernel, tq=tq, win=win, s_len=s, group=group)
    return pl.pallas_call(
        kern,
        out_shape=jax.ShapeDtypeStruct((s, WIN_Q_HEADS * HEAD_DIM), BF16),
        grid=(WIN_KV_HEADS, s // tq),
        in_specs=[
            pl.BlockSpec(memory_space=pltpu.SMEM),
            pl.BlockSpec((tq, qw), lambda h, i: (i, h)),
            pl.BlockSpec((s, HEAD_DIM), lambda h, i: (0, k0 + h)),
            pl.BlockSpec((s, HEAD_DIM), lambda h, i: (0, v0 + h)),
        ],
        out_specs=pl.BlockSpec((tq, qw), lambda h, i: (i, h)),
        compiler_params=_params("parallel", "arbitrary"),
        name="window_attention",
    )(sink.astype(F32), qkv, qkv, qkv)


def _s5_discretize(a_re, a_im, log_dt, b_re, b_im):
    dt = jnp.exp(log_dt)[:, None]
    mag = jnp.exp(dt * a_re)
    lr, li = mag * jnp.cos(dt * a_im), mag * jnp.sin(dt * a_im)
    den = a_re * a_re + a_im * a_im
    nr, ni = lr - 1.0, li
    fr = (nr * a_re + ni * a_im) / den
    fi = (ni * a_re - nr * a_im) / den
    bbr = fr[..., None] * b_re - fi[..., None] * b_im
    bbi = fr[..., None] * b_im + fi[..., None] * b_re
    return lr, li, bbr, bbi


def _s5_block_weights(lr, li, bbr, bbi, c_re, c_im, gpc):
    g, n, c = bbr.shape
    n_k = g // gpc
    eye = jnp.eye(gpc, dtype=F32)

    def w_in(bb):
        t = bb.reshape(n_k, gpc, n, c)
        return jnp.einsum('kgnc,gh->kgchn', t, eye).reshape(n_k, gpc * c, gpc * n)

    def w_out(cc):
        t = cc.reshape(n_k, gpc, c, n)
        return jnp.einsum('kgcn,gh->khngc', t, eye).reshape(n_k, gpc * n, gpc * c)

    win = jnp.concatenate([w_in(bbr), w_in(bbi)], axis=2)
    wout = jnp.concatenate([w_out(c_re), -w_out(c_im)], axis=1)
    lam = jnp.stack([lr.reshape(n_k, gpc * n), li.reshape(n_k, gpc * n)], axis=1)
    return win, wout, lam


def _s5_kernel(uf_ref, ub_ref, win_ref, wout_ref, lam_ref, yf_ref, yb_ref,
               xf_s, xb_s, st_s, *, tc, ns, cw, nsub):
    i = pl.program_id(1)

    @pl.when(i == 0)
    def _():
        st_s[...] = jnp.zeros_like(st_s)

    for c in range(nsub):
        cols = slice(c * cw, (c + 1) * cw)
        xf_s[c] = jnp.dot(uf_ref[:, cols].astype(BF16), win_ref[0, c].astype(BF16),
                          preferred_element_type=F32)
        xb_s[c] = jnp.dot(ub_ref[:, cols].astype(BF16), win_ref[1, c].astype(BF16),
                          preferred_element_type=F32)
    lam = [[(lam_ref[d, c, 0:1, :], lam_ref[d, c, 1:2, :]) for c in range(nsub)]
           for d in range(2)]

    def step(x_s, c, row, lr, li, sr, si):
        nr = lr * sr - li * si + x_s[c, pl.ds(row, 1), 0:ns]
        ni = lr * si + li * sr + x_s[c, pl.ds(row, 1), ns:2 * ns]
        x_s[c, pl.ds(row, 1), 0:ns] = nr
        x_s[c, pl.ds(row, 1), ns:2 * ns] = ni
        return nr, ni

    def body(r, carry):
        out = []
        for c in range(nsub):
            srf, sif, srb, sib = carry[4 * c:4 * c + 4]
            srf, sif = step(xf_s, c, r, *lam[0][c], srf, sif)
            srb, sib = step(xb_s, c, tc - 1 - r, *lam[1][c], srb, sib)
            out += [srf, sif, srb, sib]
        return tuple(out)

    init = tuple(st_s[j:j + 1, :] for j in range(4 * nsub))
    final = lax.fori_loop(0, tc, body, init, unroll=8)
    for j in range(4 * nsub):
        st_s[j:j + 1, :] = final[j]
    for c in range(nsub):
        cols = slice(c * cw, (c + 1) * cw)
        yf_ref[:, cols] = jnp.dot(xf_s[c].astype(BF16), wout_ref[0, c].astype(BF16),
                                  preferred_element_type=F32)
        yb_ref[:, cols] = jnp.dot(xb_s[c].astype(BF16), wout_ref[1, c].astype(BF16),
                                  preferred_element_type=F32)


def _s5_scan(u, win, wout, lam, *, tc=256, nsub=2):
    s, w = u.shape
    _, n_k, cw, ns2 = win.shape
    ns = ns2 // 2
    tc = min(tc, s)
    n_t = s // tc
    nsub = min(nsub, n_k)
    bw = nsub * cw
    kern = functools.partial(_s5_kernel, tc=tc, ns=ns, cw=cw, nsub=nsub)
    return pl.pallas_call(
        kern,
        out_shape=(jax.ShapeDtypeStruct((s, w), F32), jax.ShapeDtypeStruct((s, w), F32)),
        grid=(n_k // nsub, n_t),
        in_specs=[
            pl.BlockSpec((tc, bw), lambda k, i: (i, k)),
            pl.BlockSpec((tc, bw), lambda k, i: (n_t - 1 - i, k)),
            pl.BlockSpec((2, nsub, cw, ns2), lambda k, i: (0, k, 0, 0)),
            pl.BlockSpec((2, nsub, ns2, cw), lambda k, i: (0, k, 0, 0)),
            pl.BlockSpec((2, nsub, 2, ns), lambda k, i: (0, k, 0, 0)),
        ],
        out_specs=(pl.BlockSpec((tc, bw), lambda k, i: (i, k)),
                   pl.BlockSpec((tc, bw), lambda k, i: (n_t - 1 - i, k))),
        scratch_shapes=[pltpu.VMEM((nsub, tc, ns2), F32), pltpu.VMEM((nsub, tc, ns2), F32),
                        pltpu.VMEM((4 * nsub, ns), F32)],
        compiler_params=_params("parallel", "arbitrary"),
        name="s5_scan",
    )(u, u, win, wout, lam)


def _pro_s5_glu(rows, vecs):
    yf, yb, u = rows
    return _gelu_tanh(yf + yb + vecs[0] * u)


def _epi_glu(acc, epi_refs, o_ref, j, a32_s):
    o_ref[...] = (a32_s[...] * _sigmoid(acc)).astype(o_ref.dtype)


def _route(x, g, w_r):
    h = _rms(x, g)
    logits = jnp.dot(h.astype(BF16), w_r, preferred_element_type=F32)
    tm = logits.shape[0]
    lane = lax.broadcasted_iota(jnp.int32, (tm, LANES), 1)
    big = jnp.int32(LANES)
    ninf = -jnp.inf
    gl = jnp.where(lane < MOE_GROUPS, logits, ninf)
    gm = jnp.max(gl, axis=-1, keepdims=True)
    ge = jnp.exp(gl - gm)
    g_prob = ge / jnp.sum(ge, axis=-1, keepdims=True)
    g_p = jnp.max(g_prob, axis=-1, keepdims=True)
    g_idx = jnp.min(jnp.where(g_prob == g_p, lane, big), axis=-1, keepdims=True)
    lo = MOE_GROUPS + g_idx * EXPERTS_PER_GROUP
    in_grp = (lane >= lo) & (lane < lo + EXPERTS_PER_GROUP)
    el = jnp.where(in_grp, logits, ninf)
    em = jnp.max(el, axis=-1, keepdims=True)
    ee = jnp.exp(el - em)
    e_prob = jnp.where(in_grp, ee / jnp.sum(ee, axis=-1, keepdims=True), -1.0)
    p1 = jnp.max(e_prob, axis=-1, keepdims=True)
    i1 = jnp.min(jnp.where(e_prob == p1, lane, big), axis=-1, keepdims=True)
    rest = jnp.where(lane == i1, -1.0, e_prob)
    p2 = jnp.max(rest, axis=-1, keepdims=True)
    i2 = jnp.min(jnp.where(rest == p2, lane, big), axis=-1, keepdims=True)
    denom = p1 + p2
    w1 = g_p * (p1 / denom)
    w2 = g_p * (p2 / denom)
    idx = jnp.where(lane == 0, i1 - MOE_GROUPS, jnp.where(lane == 1, i2 - MOE_GROUPS, 0))
    wt = jnp.where(lane == 0, w1, jnp.where(lane == 1, w2, 0.0))
    return h, idx, wt


def _pack_bf16_pairs(h):
    half = h.shape[1] // 2
    lo = pltpu.bitcast(h[:, :half].astype(BF16).astype(F32), jnp.uint32)
    hi = pltpu.bitcast(h[:, half:].astype(BF16).astype(F32), jnp.uint32)
    return (lo >> 16) | (hi & jnp.uint32(0xFFFF0000))


def _unpack_bf16_pairs(w):
    lo = pltpu.bitcast(w << 16, F32)
    hi = pltpu.bitcast(w & jnp.uint32(0xFFFF0000), F32)
    return jnp.concatenate([lo, hi], axis=1).astype(BF16)


def _router_kernel(x_ref, g_ref, w_ref, h_ref, idx_ref, wt_ref):
    h, idx, wt = _route(x_ref[...], g_ref[...], w_ref[...])
    h_ref[...] = _pack_bf16_pairs(h)
    idx_ref[...] = idx
    wt_ref[...] = wt


def _router(x, g, w_group, w_expert, *, tm=512):
    s, d = x.shape
    tm = min(tm, s)
    w_r = jnp.concatenate(
        [w_group, w_expert, jnp.zeros((d, LANES - MOE_GROUPS - N_EXPERTS), F32)], axis=1)
    return pl.pallas_call(
        _router_kernel,
        out_shape=(jax.ShapeDtypeStruct((s, d // 2), jnp.uint32),
                   jax.ShapeDtypeStruct((s, LANES), jnp.int32),
                   jax.ShapeDtypeStruct((s, LANES), F32)),
        grid=(s // tm,),
        in_specs=[pl.BlockSpec((tm, d), lambda i: (i, 0)), pl.BlockSpec((1, d), lambda i: (0, 0)),
                  pl.BlockSpec((d, LANES), lambda i: (0, 0))],
        out_specs=(pl.BlockSpec((tm, d // 2), lambda i: (i, 0)),
                   pl.BlockSpec((tm, LANES), lambda i: (i, 0)),
                   pl.BlockSpec((tm, LANES), lambda i: (i, 0))),
        compiler_params=_params("parallel"),
        name="moe_router",
    )(x, g.reshape(1, d), w_r.astype(BF16))


def _rank_kernel(idx_ref, dest_ref, cnt_ref, tot_s, pst_s, run_s, *, tm):
    ph = pl.program_id(0)
    i = pl.program_id(1)
    lane = lax.broadcasted_iota(jnp.int32, (tm, LANES), 1)
    idx = idx_ref[...]
    oh0 = (lane == idx[:, 0:1]).astype(F32)
    oh1 = (lane == idx[:, 1:2]).astype(F32)
    c = oh0 + oh1
    csum = jnp.sum(c, axis=0, keepdims=True)

    @pl.when(jnp.logical_and(ph == 0, i == 0))
    def _():
        tot_s[...] = jnp.zeros_like(tot_s)

    @pl.when(ph == 0)
    def _():
        tot_s[...] += csum

    @pl.when(jnp.logical_and(ph == 1, i == 0))
    def _():
        counts = tot_s[...]
        nblk = jnp.floor((counts + (EXPERT_BLOCK - 1)) * (1.0 / EXPERT_BLOCK))
        r = lax.broadcasted_iota(jnp.int32, (LANES, LANES), 0)
        cc = lax.broadcasted_iota(jnp.int32, (LANES, LANES), 1)
        upper = (r < cc).astype(F32)
        excl = jnp.dot(jnp.broadcast_to(nblk, (SUBLANES, LANES)), upper,
                       preferred_element_type=F32, precision=lax.Precision.HIGHEST)
        pst_s[...] = excl[0:1, :] * EXPERT_BLOCK
        run_s[...] = jnp.zeros_like(run_s)
        cnt_ref[...] = jnp.broadcast_to(counts, cnt_ref.shape)

    @pl.when(ph == 1)
    def _():
        rr = lax.broadcasted_iota(jnp.int32, (tm, tm), 0)
        cr = lax.broadcasted_iota(jnp.int32, (tm, tm), 1)
        lower = (rr > cr).astype(BF16)
        before = jnp.dot(lower, c.astype(BF16), preferred_element_type=F32) + run_s[...]
        base = pst_s[...] + before
        d0 = jnp.sum(oh0 * base, axis=-1, keepdims=True)
        d1 = jnp.sum(oh1 * (base + oh0), axis=-1, keepdims=True)
        dest_ref[...] = jnp.where(lane == 0, d0, jnp.where(lane == 1, d1, 0.0)).astype(jnp.int32)
        run_s[...] += csum


def _dispatch_rank(idx, *, tm=512):
    s = idx.shape[0]
    tm = min(tm, s)
    kern = functools.partial(_rank_kernel, tm=tm)
    return pl.pallas_call(
        kern,
        out_shape=(jax.ShapeDtypeStruct((s, LANES), jnp.int32),
                   jax.ShapeDtypeStruct((SUBLANES, LANES), F32)),
        grid=(2, s // tm),
        in_specs=[pl.BlockSpec((tm, LANES), lambda ph, i: (i, 0))],
        out_specs=(pl.BlockSpec((tm, LANES), lambda ph, i: (i * ph, 0)),
                   pl.BlockSpec((SUBLANES, LANES), lambda ph, i: (0, 0))),
        scratch_shapes=[pltpu.VMEM((1, LANES), F32) for _ in range(3)],
        compiler_params=_params("arbitrary", "arbitrary"),
        name="moe_rank",
    )(idx)


def _block_experts(counts, n_blk):
    cnt = counts[0, :N_EXPERTS].astype(jnp.int32)
    padded = ((cnt + EXPERT_BLOCK - 1) // EXPERT_BLOCK) * EXPERT_BLOCK
    pends = jnp.cumsum(padded)
    blk_start = jnp.arange(n_blk, dtype=jnp.int32) * EXPERT_BLOCK
    owner = jnp.sum((pends[None, :] <= blk_start[:, None]).astype(jnp.int32), axis=1)
    ids = jnp.arange(N_EXPERTS, dtype=jnp.int32)
    last_e = jnp.max(jnp.where(cnt > 0, ids, 0))
    n_used = (pends[-1] // EXPERT_BLOCK).reshape(1)
    blk_e = jnp.minimum(owner, last_e).astype(jnp.int32)
    later = jnp.logical_and(ids[None, :] > ids[:, None], cnt[None, :] > 0)
    nxt = jnp.min(jnp.where(later, ids[None, :], N_EXPERTS), axis=1)
    nxt = jnp.where(nxt == N_EXPERTS, ids, nxt).astype(jnp.int32)
    nxt2 = nxt[nxt]
    parity = ((jnp.cumsum((cnt > 0).astype(jnp.int32)) - 1) % 2).astype(jnp.int32)
    pad_start = (pends - padded + cnt).astype(jnp.int32)
    pad_len = (padded - cnt).astype(jnp.int32)
    tables = (blk_e, nxt[blk_e], nxt2[blk_e], parity[blk_e], n_used.astype(jnp.int32))
    return tables, pad_start, pad_len


def _scatter_kernel(dest_ref, ps_ref, pl_ref, nu_ref, h_ref, xs_hbm, h_s, z_s, sem, zsem,
                    *, tb, n_b, n_blk):
    b = pl.program_id(0)
    slot = b % 2
    n = tb * TOP_K
    bits = [1 << k for k in reversed(range(3, (EXPERT_BLOCK - 1).bit_length()))]

    def zero_fill(action):
        def pad_run(e, _):
            start = ps_ref[e]
            head = (-start) & (SUBLANES - 1)
            for r in range(SUBLANES - 1):
                @pl.when(r < head)
                def _():
                    action(pltpu.make_async_copy(z_s.at[pl.ds(0, 1), :],
                                                 xs_hbm.at[pl.ds(start + r, 1), :], zsem.at[0]))
            astart = start + head
            length = pl_ref[e] - head
            for bit in bits:
                @pl.when((length & bit) != 0)
                def _():
                    off = pl.multiple_of(astart + (length & ~(2 * bit - 1)), SUBLANES)
                    action(pltpu.make_async_copy(z_s.at[pl.ds(0, bit), :],
                                                 xs_hbm.at[pl.ds(off, bit), :], zsem.at[0]))
            return 0

        def unused_block(bb, _):
            row0 = pl.multiple_of(bb * EXPERT_BLOCK, EXPERT_BLOCK)
            action(pltpu.make_async_copy(
                z_s, xs_hbm.at[pl.ds(row0, EXPERT_BLOCK), :], zsem.at[0]))
            return 0

        lax.fori_loop(0, N_EXPERTS, pad_run, 0)
        lax.fori_loop(nu_ref[0], n_blk, unused_block, 0)

    @pl.when(b == 0)
    def _():
        z_s[...] = jnp.zeros_like(z_s)
        zero_fill(lambda c: c.start())

    def copy(r, row, sl):
        return pltpu.make_async_copy(h_s.at[sl, pl.ds(r, 1), :], xs_hbm.at[pl.ds(row, 1), :],
                                     sem.at[sl])

    def wait_all(sl):
        def body(r, _):
            copy(0, 0, sl).wait()
            return 0
        lax.fori_loop(0, n, body, 0, unroll=8)

    h_s[slot] = h_ref[...]
    for r in range(tb):
        tok = b * tb + r
        for k in range(TOP_K):
            copy(r, dest_ref[tok * TOP_K + k], slot).start()

    @pl.when(b > 0)
    def _():
        wait_all(1 - slot)

    @pl.when(b == n_b - 1)
    def _():
        wait_all(slot)
        zero_fill(lambda c: c.wait())


def _dispatch_scatter(h, dest_flat, pad_start, pad_len, n_used, n_rows, *, tb=256):
    s, d = h.shape
    tb = min(tb, s)
    n_b = s // tb
    kern = functools.partial(_scatter_kernel, tb=tb, n_b=n_b, n_blk=n_rows // EXPERT_BLOCK)
    grid_spec = pltpu.PrefetchScalarGridSpec(
        num_scalar_prefetch=4,
        grid=(n_b,),
        in_specs=[pl.BlockSpec((tb, d), lambda b, *_: (b, 0))],
        out_specs=pl.BlockSpec(memory_space=pl.ANY),
        scratch_shapes=[pltpu.VMEM((2, tb, d), h.dtype), pltpu.VMEM((EXPERT_BLOCK, d), h.dtype),
                        pltpu.SemaphoreType.DMA((2,)), pltpu.SemaphoreType.DMA((1,))],
    )
    return pl.pallas_call(
        kern,
        out_shape=jax.ShapeDtypeStruct((n_rows, d), h.dtype),
        grid_spec=grid_spec,
        compiler_params=_params("arbitrary"),
        name="moe_scatter",
    )(dest_flat, pad_start, pad_len, n_used, h)


def _row_copy(src_hbm, src_row, dst_buf, slot, dst_row, sem):
    return pltpu.make_async_copy(src_hbm.at[pl.ds(src_row, 1), :],
                                 dst_buf.at[slot, pl.ds(dst_row, 1), :], sem.at[slot])


def _gather_start(idx_ref, base, n, src_hbm, dst_buf, slot, sem):
    for r in range(n):
        _row_copy(src_hbm, idx_ref[base + r], dst_buf, slot, r, sem).start()


def _gather_wait(n, src_hbm, dst_buf, slot, sem):
    def body(r, _):
        _row_copy(src_hbm, 0, dst_buf, slot, r, sem).wait()
        return 0
    lax.fori_loop(0, n, body, 0, unroll=8)


def _expert_kernel(be_ref, nx_ref, nx2_ref, par_ref, nu_ref, x_ref, wg_hbm, wu_hbm, wd_hbm, y_ref,
                   wg_f, wu_f, wd_f, wg_s, wu_s, wd_s, sem, *, layer):
    b = pl.program_id(0)
    e = be_ref[b]
    slot = par_ref[b]

    def fetch(expert, sl):
        return (pltpu.make_async_copy(wg_hbm.at[layer, expert], wg_f.at[sl], sem.at[sl, 0]),
                pltpu.make_async_copy(wu_hbm.at[layer, expert], wu_f.at[sl], sem.at[sl, 1]),
                pltpu.make_async_copy(wd_hbm.at[layer, expert], wd_f.at[sl], sem.at[sl, 2]))

    @pl.when(b == 0)
    def _():
        for c in fetch(e, slot):
            c.start()

        @pl.when(nx_ref[b] != e)
        def _():
            for c in fetch(nx_ref[b], 1 - slot):
                c.start()

    changed = jnp.logical_or(b == 0, e != be_ref[jnp.maximum(b - 1, 0)])

    @pl.when(changed)
    def _():
        for c in fetch(e, slot):
            c.wait()
        wg_s[...] = wg_f[slot].astype(BF16)
        wu_s[...] = wu_f[slot].astype(BF16)
        wd_s[...] = wd_f[slot].astype(BF16)

        @pl.when(nx2_ref[b] != nx_ref[b])
        def _():
            for c in fetch(nx2_ref[b], slot):
                c.start()

    @pl.when(b >= nu_ref[0])
    def _():
        y_ref[...] = jnp.zeros_like(y_ref)

    @pl.when(b < nu_ref[0])
    def _():
        x = _unpack_bf16_pairs(x_ref[...])
        hg = jnp.dot(x, wg_s[...], preferred_element_type=F32)
        hu = jnp.dot(x, wu_s[...], preferred_element_type=F32)
        hdn = (hg * _sigmoid(hg)) * hu
        y_ref[...] = jnp.dot(hdn.astype(BF16), wd_s[...], preferred_element_type=F32)


def _expert_mlp(xs, blk_e, blk_next, blk_next2, blk_par, n_used, layer, w_gate, w_up, w_down):
    n_rows, dp = xs.shape
    d, de = w_gate.shape[2], w_gate.shape[3]
    grid_spec = pltpu.PrefetchScalarGridSpec(
        num_scalar_prefetch=5,
        grid=(n_rows // EXPERT_BLOCK,),
        in_specs=[
            pl.BlockSpec((EXPERT_BLOCK, dp),
                         lambda b, be, nx, nx2, par, nu: (jnp.minimum(b, nu[0] - 1), 0)),
            pl.BlockSpec(memory_space=pl.ANY),
            pl.BlockSpec(memory_space=pl.ANY),
            pl.BlockSpec(memory_space=pl.ANY),
        ],
        out_specs=pl.BlockSpec((EXPERT_BLOCK, d), lambda b, be, nx, nx2, par, nu: (b, 0)),
        scratch_shapes=[pltpu.VMEM((2, d, de), F32), pltpu.VMEM((2, d, de), F32),
                        pltpu.VMEM((2, de, d), F32), pltpu.VMEM((d, de), BF16),
                        pltpu.VMEM((d, de), BF16), pltpu.VMEM((de, d), BF16),
                        pltpu.SemaphoreType.DMA((2, 3))],
    )
    return pl.pallas_call(
        functools.partial(_expert_kernel, layer=layer),
        out_shape=jax.ShapeDtypeStruct((n_rows, d), F32),
        grid_spec=grid_spec,
        compiler_params=_params("arbitrary"),
        name="moe_experts",
    )(blk_e, blk_next, blk_next2, blk_par, n_used, xs, w_gate, w_up, w_down)


def _combine_kernel(dest_ref, x_ref, wt_ref, ys_hbm, g_ref, o_ref, ybuf, sem, *, tb, n_b, final):
    b = pl.program_id(0)
    slot = b % 2
    n = tb * TOP_K

    @pl.when(b == 0)
    def _():
        _gather_start(dest_ref, 0, n, ys_hbm, ybuf, 0, sem)

    _gather_wait(n, ys_hbm, ybuf, slot, sem)
    _gather_start(dest_ref, (b + 1) * n, n, ys_hbm, ybuf, 1 - slot, sem)
    y0 = ybuf[slot, 0:tb, :]
    y1 = ybuf[slot, tb:2 * tb, :]
    wt = wt_ref[...]
    out = x_ref[...] + (y0 * wt[:, 0:1] + y1 * wt[:, 1:2])
    if final:
        out = _rms(out, g_ref[...])
    o_ref[...] = out

    @pl.when(b == n_b - 1)
    def _():
        _gather_wait(n, ys_hbm, ybuf, 1 - slot, sem)


def _moe_combine(x, wt, ys, dest, g_final, *, final, tb=256):
    s, d = x.shape
    tb = min(tb, s)
    n_b = s // tb
    dest = dest.reshape(n_b, tb, TOP_K).transpose(0, 2, 1).reshape(-1)
    dest = jnp.concatenate([dest, jnp.zeros((tb * TOP_K,), jnp.int32)])
    kern = functools.partial(_combine_kernel, tb=tb, n_b=n_b, final=final)
    grid_spec = pltpu.PrefetchScalarGridSpec(
        num_scalar_prefetch=1,
        grid=(n_b,),
        in_specs=[
            pl.BlockSpec((tb, d), lambda b, dr: (b, 0)),
            pl.BlockSpec((tb, LANES), lambda b, dr: (b, 0)),
            pl.BlockSpec(memory_space=pl.ANY),
            pl.BlockSpec((1, d), lambda b, dr: (0, 0)),
        ],
        out_specs=pl.BlockSpec((tb, d), lambda b, dr: (b, 0)),
        scratch_shapes=[pltpu.VMEM((2, tb * TOP_K, d), F32), pltpu.SemaphoreType.DMA((2,))],
    )
    return pl.pallas_call(
        kern,
        out_shape=jax.ShapeDtypeStruct((s, d), F32),
        grid_spec=grid_spec,
        compiler_params=_params("arbitrary"),
        name="moe_combine",
    )(dest, x, wt, ys, g_final.reshape(1, d))


def _hier_moe(x, ln_g, w_group, w_expert, layer, w_gate, w_up, w_down, g_final, *, final):
    s = x.shape[0]
    h, idx, wt = _router(x, ln_g, w_group, w_expert)
    dest, counts = _dispatch_rank(idx)
    dest = dest[:, :TOP_K].reshape(-1)
    n_rows = -(-(s * TOP_K + N_EXPERTS * (EXPERT_BLOCK - 1)) // EXPERT_BLOCK) * EXPERT_BLOCK
    tables, pad_start, pad_len = _block_experts(counts, n_rows // EXPERT_BLOCK)
    xs = _dispatch_scatter(h, dest, pad_start, pad_len, tables[-1], n_rows)
    ys = _expert_mlp(xs, *tables, layer, w_gate, w_up, w_down)
    return _moe_combine(x, wt, ys, dest, g_final, final=final)


def _rglru_layer(x, ln_g, w_in, conv_w, conv_b, w_a, w_x, b_a, b_x, lam, w_out):
    d = x.shape[1]
    c = conv_w.shape[1]
    proj = _norm_matmul(x, ln_g, w_in, name="lru_in_proj")
    yf, yb = _rglru_scan(proj, conv_w, conv_b, w_a, w_x, b_a, b_x, lam)
    return _matmul_residual([(yf, c, 0), (yb, c, 0), (proj, c, 0)], [], w_out, x, _pro_rglru_out,
                            k_dim=c, tm=512, name="lru_out_proj")


def _diff_layer(x, ln_g, rope, w_qkv, lq1, lk1, lq2, lk2, subln_g, w_out, lambda_init):
    qk_cols = 2 * DIFF_HEADS * 2 * HEAD_DIM
    qkv = _norm_matmul(x, ln_g, w_qkv, rope=rope, n_rope_cols=qk_cols, n_q_cols=qk_cols // 2,
                       q_scale=HEAD_DIM ** -0.5 * math.log2(math.e), out_dtype=BF16,
                       name="diff_qkv_proj")
    o = _diff_attention(qkv, lq1, lk1, lq2, lk2, subln_g, lambda_init)
    return _matmul_residual([(o, o.shape[1], 0)], [], w_out, x, _pro_identity,
                            k_dim=o.shape[1], name="diff_out_proj")


def _window_layer(x, ln_g, rope, w_qkv, sink, w_out):
    qk_cols = (WIN_Q_HEADS + WIN_KV_HEADS) * HEAD_DIM
    qkv = _norm_matmul(x, ln_g, w_qkv, rope=rope, n_rope_cols=qk_cols,
                       n_q_cols=WIN_Q_HEADS * HEAD_DIM,
                       q_scale=HEAD_DIM ** -0.5 * math.log2(math.e), out_dtype=BF16, tn=512,
                       name="win_qkv_proj")
    o = _window_attention(qkv, sink)
    return _matmul_residual([(o, o.shape[1], 0)], [], w_out, x, _pro_identity,
                            k_dim=o.shape[1], name="win_out_proj")


def _s5_layer(x, ln_g, w_in, a_re, a_im, log_dt, b_re, b_im, c_re, c_im, d_skip, w_glu, w_out,
              *, gpc=16):
    u = _norm_matmul(x, ln_g, w_in, name="s5_in_proj")
    w = u.shape[1]
    wins, wouts, lams = [], [], []
    for dd in range(2):
        lr, li, bbr, bbi = _s5_discretize(a_re[dd], a_im[dd], log_dt[dd], b_re[dd], b_im[dd])
        wi, wo, lm = _s5_block_weights(lr, li, bbr, bbi, c_re[dd], c_im[dd], gpc)
        wins.append(wi)
        wouts.append(wo)
        lams.append(lm)
    yf, yb = _s5_scan(u, jnp.stack(wins).astype(BF16), jnp.stack(wouts).astype(BF16),
                      jnp.stack(lams))
    z = _fused_mm([(yf, w, 0), (yb, w, 0), (u, w, 0)], [d_skip.reshape(1, w)], w_glu, [],
                  _pro_s5_glu, _epi_glu, out_dtype=BF16, tm=512, tn=w, k_dim=w, keep_f32=True,
                  name="s5_glu")
    return _matmul_residual([(z, w, 0)], [], w_out, x, _pro_identity, k_dim=w, name="s5_out_proj")


def kernel(x, positions, ln_mix, ln_ffn, ln_final, lru_w_in, lru_conv_w, lru_conv_b, lru_w_a, lru_w_x, lru_b_a, lru_b_x, lru_lambda, lru_w_out, diff_w_qkv, diff_lq1, diff_lk1, diff_lq2, diff_lk2, diff_subln, diff_w_out, win_w_qkv, win_sink, win_w_out, s5_w_in, s5_a_re, s5_a_im, s5_log_dt, s5_b_re, s5_b_im, s5_c_re, s5_c_im, s5_d, s5_w_glu, s5_w_out, moe_w_group, moe_w_expert, moe_w_gate, moe_w_up, moe_w_down):
    batch, s, d = x.shape
    depth = ln_mix.shape[0]
    outs = []
    for b in range(batch):
        xb = x[b]
        rope = _rope_tables(positions[b])
        for i in range(depth):
            kind, j = i % 4, i // 4
            if kind == 0:
                xb = _rglru_layer(xb, ln_mix[i], lru_w_in[j], lru_conv_w[j], lru_conv_b[j],
                                  lru_w_a[j], lru_w_x[j], lru_b_a[j], lru_b_x[j], lru_lambda[j],
                                  lru_w_out[j])
            elif kind == 1:
                xb = _diff_layer(xb, ln_mix[i], rope, diff_w_qkv[j], diff_lq1[j], diff_lk1[j],
                                 diff_lq2[j], diff_lk2[j], diff_subln[j], diff_w_out[j],
                                 0.8 - 0.6 * math.exp(-0.3 * i))
            elif kind == 2:
                xb = _window_layer(xb, ln_mix[i], rope, win_w_qkv[j], win_sink[j], win_w_out[j])
            else:
                xb = _s5_layer(xb, ln_mix[i], s5_w_in[j], s5_a_re[j], s5_a_im[j], s5_log_dt[j],
                               s5_b_re[j], s5_b_im[j], s5_c_re[j], s5_c_im[j], s5_d[j],
                               s5_w_glu[j], s5_w_out[j])
            xb = _hier_moe(xb, ln_ffn[i], moe_w_group[i], moe_w_expert[i], i, moe_w_gate,
                           moe_w_up, moe_w_down, ln_final, final=(i == depth - 1))
        outs.append(xb)
    return jnp.stack(outs)
```

```python
import functools
import math

import jax
import jax.numpy as jnp
from jax import lax
from jax.experimental import pallas as pl
from jax.experimental.pallas import tpu as pltpu

F32 = jnp.float32
BF16 = jnp.bfloat16

NORM_EPS = 1e-6
NEG_INF = -1e30
LANES = 128
SUBLANES = 8
VMEM_LIMIT = 56 * 1024 * 1024

HEAD_DIM = 128
ROT_DIM = HEAD_DIM // 4
ROPE_THETA = 500000.0
RGLRU_C = 8.0
CONV_W = 4
LRU_BLOCK_W = 128
WINDOW = 128
DIFF_HEADS = 8
WIN_Q_HEADS = 16
WIN_KV_HEADS = 4
SSM_GROUP_CH = 16
SSM_STATE = 64
MOE_GROUPS = 4
EXPERTS_PER_GROUP = 8
N_EXPERTS = MOE_GROUPS * EXPERTS_PER_GROUP
TOP_K = 2
EXPERT_BLOCK = 256


def _params(*sem):
    return pltpu.CompilerParams(dimension_semantics=sem, vmem_limit_bytes=VMEM_LIMIT)


def _rms(x, g):
    ms = jnp.mean(x * x, axis=-1, keepdims=True)
    return x * lax.rsqrt(ms + NORM_EPS) * g


def _gelu_tanh(x):
    return 0.5 * x * (1.0 + jnp.tanh(math.sqrt(2.0 / math.pi) * (x + 0.044715 * (x * x * x))))


def _sigmoid(x):
    return 1.0 / (1.0 + jnp.exp(-x))


def _fused_mm_kernel(*refs, n_row, n_vec, n_epi, prologue, epilogue, keep_f32):
    row_refs = refs[:n_row]
    vec_refs = refs[n_row:n_row + n_vec]
    w_ref = refs[n_row + n_vec]
    epi_refs = refs[n_row + n_vec + 1:n_row + n_vec + 1 + n_epi]
    o_ref = refs[n_row + n_vec + 1 + n_epi]
    a_s = refs[n_row + n_vec + 2 + n_epi]
    a32_s = refs[n_row + n_vec + 3 + n_epi] if keep_f32 else None
    j = pl.program_id(1)

    @pl.when(j == 0)
    def _():
        a = prologue([r[...] for r in row_refs], [v[...] for v in vec_refs])
        a_s[...] = a.astype(BF16)
        if keep_f32:
            a32_s[...] = a

    acc = jnp.dot(a_s[...], w_ref[...].astype(BF16), preferred_element_type=F32)
    epilogue(acc, epi_refs, o_ref, j, a32_s)


def _fused_mm(row_inputs, vec_inputs, w, epi_inputs, prologue, epilogue, *, out_dtype, tm, tn,
              k_dim, keep_f32=False, name):
    s = row_inputs[0][0].shape[0]
    n = w.shape[1]
    tm = min(tm, s)
    tn = min(tn, n)
    w = w.astype(BF16)
    in_specs = []
    args = []
    for arr, width, cb in row_inputs:
        in_specs.append(pl.BlockSpec((tm, width), lambda i, j, cb=cb: (i, cb)))
        args.append(arr)
    for arr in vec_inputs:
        in_specs.append(pl.BlockSpec(arr.shape, lambda i, j: (0, 0)))
        args.append(arr)
    in_specs.append(pl.BlockSpec((k_dim, tn), lambda i, j: (0, j)))
    args.append(w)
    for arr, width, per_tile in epi_inputs:
        if per_tile:
            in_specs.append(pl.BlockSpec((tm, width), lambda i, j: (i, j)))
        else:
            in_specs.append(pl.BlockSpec((tm, width), lambda i, j: (i, 0)))
        args.append(arr)
    scratch = [pltpu.VMEM((tm, k_dim), BF16)]
    if keep_f32:
        scratch.append(pltpu.VMEM((tm, k_dim), F32))
    kern = functools.partial(_fused_mm_kernel, n_row=len(row_inputs), n_vec=len(vec_inputs),
                             n_epi=len(epi_inputs), prologue=prologue, epilogue=epilogue,
                             keep_f32=keep_f32)
    return pl.pallas_call(
        kern,
        out_shape=jax.ShapeDtypeStruct((s, n), out_dtype),
        grid=(s // tm, n // tn),
        in_specs=in_specs,
        out_specs=pl.BlockSpec((tm, tn), lambda i, j: (i, j)),
        scratch_shapes=scratch,
        compiler_params=_params("parallel", "arbitrary"),
        name=name,
    )(*args)


def _pro_rms(rows, vecs):
    return _rms(rows[0], vecs[0])


def _epi_store(acc, epi_refs, o_ref, j, a32_s):
    o_ref[...] = acc.astype(o_ref.dtype)


def _epi_residual(acc, epi_refs, o_ref, j, a32_s):
    o_ref[...] = (epi_refs[0][...] + acc).astype(o_ref.dtype)


def _make_epi_rope(n_rope_tiles, n_q_tiles, q_scale, tn):
    def epi(acc, epi_refs, o_ref, j, a32_s):
        c_ref, s1_ref, s2_ref = epi_refs

        @pl.when(j < n_rope_tiles)
        def _():
            c = c_ref[...]
            s1 = s1_ref[...]
            s2 = s2_ref[...]
            sc = jnp.where(j < n_q_tiles, q_scale, 1.0).astype(F32)
            for hh in range(tn // HEAD_DIM):
                xs = acc[:, hh * HEAD_DIM:(hh + 1) * HEAD_DIM]
                rot = (xs * c + pltpu.roll(xs, HEAD_DIM - ROT_DIM // 2, 1) * s1
                       + pltpu.roll(xs, ROT_DIM // 2, 1) * s2) * sc
                o_ref[:, hh * HEAD_DIM:(hh + 1) * HEAD_DIM] = rot.astype(o_ref.dtype)

        @pl.when(j >= n_rope_tiles)
        def _():
            o_ref[...] = acc.astype(o_ref.dtype)

    return epi


def _rope_tables(positions):
    half = ROT_DIM // 2
    inv = ROPE_THETA ** (-jnp.arange(0, ROT_DIM, 2, dtype=F32) / ROT_DIM)
    ang = positions.astype(F32)[:, None] * inv
    cos, sin = jnp.cos(ang), jnp.sin(ang)
    s = positions.shape[0]
    ones = jnp.ones((s, HEAD_DIM - ROT_DIM), F32)
    zeros = jnp.zeros((s, HEAD_DIM - ROT_DIM), F32)
    zh = jnp.zeros((s, half), F32)
    c_tab = jnp.concatenate([cos, cos, ones], axis=1)
    s1_tab = jnp.concatenate([-sin, zh, zeros], axis=1)
    s2_tab = jnp.concatenate([zh, sin, zeros], axis=1)
    return c_tab, s1_tab, s2_tab


def _norm_matmul(x, g, w, *, rope=None, n_rope_cols=0, n_q_cols=0, q_scale=1.0, out_dtype=F32,
                 tm=1024, tn=1024, name):
    d = x.shape[1]
    if rope is None:
        epi, epi_inputs = _epi_store, []
    else:
        tn = min(tn, w.shape[1])
        assert n_rope_cols % tn == 0 and n_q_cols % tn == 0
        epi = _make_epi_rope(n_rope_cols // tn, n_q_cols // tn, q_scale, tn)
        epi_inputs = [(t, HEAD_DIM, False) for t in rope]
    return _fused_mm([(x, d, 0)], [g.reshape(1, d)], w, epi_inputs, _pro_rms, epi,
                     out_dtype=out_dtype, tm=tm, tn=tn, k_dim=d, name=name)


def _matmul_residual(row_inputs, vec_inputs, w, res, prologue, *, k_dim, tm=1024, tn=1024, name):
    return _fused_mm(row_inputs, vec_inputs, w, [(res, min(tn, w.shape[1]), True)], prologue,
                     _epi_residual, out_dtype=F32, tm=tm, tn=tn, k_dim=k_dim, name=name)


def _rglru_kernel(xf_ref, xfp_ref, xfn_ref, xb_ref, xbp_ref, xbn_ref, cw_ref, cb_ref, wa_ref,
                  wx_ref, ba_ref, bx_ref, lam_ref, yf_ref, yb_ref,
                  ext_s, af_s, bf_s, ab_s, bb_s, hf_s, hb_s, *, tc, cw, n_t):
    i = pl.program_id(1)
    halo = SUBLANES

    @pl.when(i == 0)
    def _():
        hf_s[...] = jnp.zeros_like(hf_s)
        hb_s[...] = jnp.zeros_like(hb_s)

    def gates(x_ref, xp_ref, xn_ref, chunk, d, a_s, b_s):
        prev = jnp.where(chunk == 0, 0.0, xp_ref[...])
        nxt = jnp.where(chunk == n_t - 1, 0.0, xn_ref[...])
        ext_s[0:halo, :] = prev
        ext_s[halo:halo + tc, :] = x_ref[...]
        ext_s[halo + tc:halo + tc + halo, :] = nxt
        xc = cb_ref[...] + sum(
            cw_ref[k:k + 1, :] * ext_s[halo - 2 + k:halo - 2 + k + tc, :] for k in range(CONV_W))
        lam = lam_ref[d:d + 1, :]
        z = -lam
        sp = jnp.maximum(z, 0.0) + jnp.log1p(jnp.exp(-jnp.abs(z)))
        for blk in range(cw // LRU_BLOCK_W):
            sl = slice(blk * LRU_BLOCK_W, (blk + 1) * LRU_BLOCK_W)
            xb = xc[:, sl]
            xbh = xb.astype(BF16)
            r = _sigmoid(jnp.dot(xbh, wa_ref[d, blk].astype(BF16), preferred_element_type=F32)
                         + ba_ref[d:d + 1, sl])
            ig = _sigmoid(jnp.dot(xbh, wx_ref[d, blk].astype(BF16), preferred_element_type=F32)
                          + bx_ref[d:d + 1, sl])
            log_a = (-RGLRU_C) * r * sp[:, sl]
            a_s[:, sl] = jnp.exp(log_a)
            th = jnp.tanh(log_a)
            b_s[:, sl] = jnp.sqrt(-2.0 * th / (1.0 - th)) * (ig * xb)

    gates(xf_ref, xfp_ref, xfn_ref, i, 0, af_s, bf_s)
    gates(xb_ref, xbp_ref, xbn_ref, n_t - 1 - i, 1, ab_s, bb_s)

    def body(r, carry):
        hf, hb = carry
        hf = af_s[pl.ds(r, 1), :] * hf + bf_s[pl.ds(r, 1), :]
        yf_ref[pl.ds(r, 1), :] = hf
        rb = tc - 1 - r
        hb = ab_s[pl.ds(rb, 1), :] * hb + bb_s[pl.ds(rb, 1), :]
        yb_ref[pl.ds(rb, 1), :] = hb
        return hf, hb

    hf, hb = lax.fori_loop(0, tc, body, (hf_s[...], hb_s[...]), unroll=8)
    hf_s[...] = hf
    hb_s[...] = hb


def _rglru_scan(proj, conv_w, conv_b, w_a, w_x, b_a, b_x, lam, *, tc=256, cw=2048):
    s = proj.shape[0]
    c = conv_w.shape[1]
    tc = min(tc, s)
    n_t = s // tc
    n_c = c // cw
    xoff = c // cw
    hb = tc // SUBLANES
    last_h = s // SUBLANES - 1

    specs = [
        pl.BlockSpec((tc, cw), lambda ci, i: (i, xoff + ci)),
        pl.BlockSpec((SUBLANES, cw), lambda ci, i: (jnp.maximum(i * hb - 1, 0), xoff + ci)),
        pl.BlockSpec((SUBLANES, cw), lambda ci, i: (jnp.minimum((i + 1) * hb, last_h), xoff + ci)),
        pl.BlockSpec((tc, cw), lambda ci, i: (n_t - 1 - i, xoff + ci)),
        pl.BlockSpec((SUBLANES, cw),
                     lambda ci, i: (jnp.maximum((n_t - 1 - i) * hb - 1, 0), xoff + ci)),
        pl.BlockSpec((SUBLANES, cw),
                     lambda ci, i: (jnp.minimum((n_t - i) * hb, last_h), xoff + ci)),
        pl.BlockSpec((CONV_W, cw), lambda ci, i: (0, ci)),
        pl.BlockSpec((1, cw), lambda ci, i: (0, ci)),
        pl.BlockSpec((2, cw // LRU_BLOCK_W, LRU_BLOCK_W, LRU_BLOCK_W), lambda ci, i: (0, ci, 0, 0)),
        pl.BlockSpec((2, cw // LRU_BLOCK_W, LRU_BLOCK_W, LRU_BLOCK_W), lambda ci, i: (0, ci, 0, 0)),
        pl.BlockSpec((2, cw), lambda ci, i: (0, ci)),
        pl.BlockSpec((2, cw), lambda ci, i: (0, ci)),
        pl.BlockSpec((2, cw), lambda ci, i: (0, ci)),
    ]
    kern = functools.partial(_rglru_kernel, tc=tc, cw=cw, n_t=n_t)
    return pl.pallas_call(
        kern,
        out_shape=(jax.ShapeDtypeStruct((s, c), F32), jax.ShapeDtypeStruct((s, c), F32)),
        grid=(n_c, n_t),
        in_specs=specs,
        out_specs=(pl.BlockSpec((tc, cw), lambda ci, i: (i, ci)),
                   pl.BlockSpec((tc, cw), lambda ci, i: (n_t - 1 - i, ci))),
        scratch_shapes=[pltpu.VMEM((tc + 2 * SUBLANES, cw), F32)]
        + [pltpu.VMEM((tc, cw), F32) for _ in range(4)]
        + [pltpu.VMEM((1, cw), F32) for _ in range(2)],
        compiler_params=_params("parallel", "arbitrary"),
        name="rglru_scan",
    )(proj, proj, proj, proj, proj, proj, conv_w, conv_b.reshape(1, c), w_a, w_x, b_a, b_x, lam)


def _pro_rglru_out(rows, vecs):
    yf, yb, gate = rows
    return (yf + yb) * _gelu_tanh(gate)


def _diff_attn_kernel(q_ref, k_ref, v_ref, lq1_ref, lk1_ref, lq2_ref, lk2_ref, g_ref, o_ref,
                      s_buf, p_buf, m_s, a_s, l_s, acc_s, *, tq, tk, rc, pv_rows, n_kv,
                      lambda_init):
    m_s[...] = jnp.full_like(m_s, -jnp.inf)
    l_s[...] = jnp.zeros_like(l_s)
    acc_s[...] = jnp.zeros_like(acc_s)
    q = q_ref[...]
    qs = (q[:, :HEAD_DIM], q[:, HEAD_DIM:])

    def scores(j, slot):
        off = pl.multiple_of(j * tk, tk)
        kb = k_ref[pl.ds(off, tk), :]
        for c in range(2):
            kc = kb[:, c * HEAD_DIM:(c + 1) * HEAD_DIM]
            s_buf[slot, c * tq:(c + 1) * tq, :] = lax.dot_general(
                qs[c], kc, (((1,), (1,)), ((), ())), preferred_element_type=F32)

    def update(j, slot):
        off = pl.multiple_of(j * tk, tk)
        vb = v_ref[pl.ds(off, tk), :]
        for g0 in range(0, 2 * tq, pv_rows):
            for r0 in range(g0, g0 + pv_rows, rc):
                rows = slice(r0, r0 + rc)
                sc = s_buf[slot, rows, :]
                m_old = m_s[rows, :]
                m_new = jnp.maximum(m_old, jnp.max(sc, axis=-1, keepdims=True))
                alpha = jnp.exp2(m_old - m_new)
                p = jnp.exp2(sc - m_new)
                l_s[rows, :] = alpha * l_s[rows, :] + sum(
                    p[:, t * LANES:(t + 1) * LANES] for t in range(tk // LANES))
                p_buf[rows, :] = p.astype(BF16)
                m_s[rows, :] = m_new
                a_s[rows, :] = alpha
            grp = slice(g0, g0 + pv_rows)
            acc_s[grp, :] = a_s[grp, :] * acc_s[grp, :] + jnp.dot(
                p_buf[grp, :], vb, preferred_element_type=F32)

    scores(0, 0)

    def kv_pair(jj, _):
        j = 2 * jj
        scores(j + 1, 1)
        update(j, 0)
        scores(jnp.minimum(j + 2, n_kv - 1), 0)
        update(j + 1, 1)
        return 0

    lax.fori_loop(0, n_kv // 2, kv_pair, 0)
    lam = (jnp.exp(jnp.sum(lq1_ref[...] * lk1_ref[...], axis=-1, keepdims=True))
           - jnp.exp(jnp.sum(lq2_ref[...] * lk2_ref[...], axis=-1, keepdims=True)) + lambda_init)
    l = jnp.sum(l_s[...], axis=-1, keepdims=True)
    o = acc_s[0:tq] / l[0:tq] - lam * (acc_s[tq:2 * tq] / l[tq:2 * tq])
    o_ref[...] = (_rms(o, g_ref[...]) * (1.0 - lambda_init)).astype(o_ref.dtype)


def _diff_attention(qkv, lq1, lk1, lq2, lk2, subln_g, lambda_init, *, tq=512, tk=1024, rc=32):
    s = qkv.shape[0]
    vd = 2 * HEAD_DIM
    tq = min(tq, s)
    tk = min(tk, s // 2)
    assert (s // tk) % 2 == 0
    kern = functools.partial(_diff_attn_kernel, tq=tq, tk=tk, rc=rc, pv_rows=tq, n_kv=s // tk,
                             lambda_init=lambda_init)
    vec = lambda a: a.reshape(1, -1)
    vspec = lambda w: pl.BlockSpec((1, w), lambda h, i: (0, 0))
    return pl.pallas_call(
        kern,
        out_shape=jax.ShapeDtypeStruct((s, DIFF_HEADS * vd), BF16),
        grid=(DIFF_HEADS, s // tq),
        in_specs=[
            pl.BlockSpec((tq, vd), lambda h, i: (i, h)),
            pl.BlockSpec((s, vd), lambda h, i: (0, DIFF_HEADS + h)),
            pl.BlockSpec((s, vd), lambda h, i: (0, 2 * DIFF_HEADS + h)),
            vspec(HEAD_DIM), vspec(HEAD_DIM), vspec(HEAD_DIM), vspec(HEAD_DIM), vspec(vd),
        ],
        out_specs=pl.BlockSpec((tq, vd), lambda h, i: (i, h)),
        scratch_shapes=[pltpu.VMEM((2, 2 * tq, tk), F32), pltpu.VMEM((2 * tq, tk), BF16),
                        pltpu.VMEM((2 * tq, 1), F32), pltpu.VMEM((2 * tq, 1), F32),
                        pltpu.VMEM((2 * tq, LANES), F32), pltpu.VMEM((2 * tq, vd), F32)],
        compiler_params=_params("parallel", "arbitrary"),
        name="diff_attention",
    )(qkv, qkv, qkv, vec(lq1), vec(lk1), vec(lq2), vec(lk2), vec(subln_g))


def _pro_identity(rows, vecs):
    return rows[0]


def _win_attn_kernel(sink_ref, q_ref, k_ref, v_ref, o_ref, *, tq, win, s_len, group):
    kvh = pl.program_id(0)
    i = pl.program_id(1)
    start = jnp.clip(i * tq - WINDOW, 0, s_len - win)
    start = pl.multiple_of(start, WINDOW)
    kw = k_ref[pl.ds(start, win), :]
    vw = v_ref[pl.ds(start, win), :]
    qpos = i * tq + lax.broadcasted_iota(jnp.int32, (tq, win), 0)
    kpos = start + lax.broadcasted_iota(jnp.int32, (tq, win), 1)
    valid = jnp.abs(kpos - qpos) <= WINDOW
    for g in range(group):
        qg = q_ref[:, g * HEAD_DIM:(g + 1) * HEAD_DIM]
        sc = lax.dot_general(qg, kw, (((1,), (1,)), ((), ())), preferred_element_type=F32)
        sc = jnp.where(valid, sc, NEG_INF)
        sink = sink_ref[kvh * group + g] * math.log2(math.e)
        m = jnp.maximum(jnp.max(sc, axis=-1, keepdims=True), sink)
        e = jnp.exp2(sc - m)
        den = jnp.sum(e, axis=-1, keepdims=True) + jnp.exp2(sink - m)
        o = jnp.dot(e.astype(BF16), vw, preferred_element_type=F32) / den
        o_ref[:, g * HEAD_DIM:(g + 1) * HEAD_DIM] = o.astype(o_ref.dtype)


def _window_attention(qkv, sink, *, tq=256):
    s = qkv.shape[0]
    group = WIN_Q_HEADS // WIN_KV_HEADS
    tq = min(tq, s)
    win = min(tq + 2 * WINDOW, s)
    qw = group * HEAD_DIM
    k0 = WIN_Q_HEADS
    v0 = WIN_Q_HEADS + WIN_KV_HEADS
    kern = functools.partial(_win_attn_kernel, tq=tq, win=win, s_len=s, group=group)
    return pl.pallas_call(
        kern,
        out_shape=jax.ShapeDtypeStruct((s, WIN_Q_HEADS * HEAD_DIM), BF16),
        grid=(WIN_KV_HEADS, s // tq),
        in_specs=[
            pl.BlockSpec(memory_space=pltpu.SMEM),
            pl.BlockSpec((tq, qw), lambda h, i: (i, h)),
            pl.BlockSpec((s, HEAD_DIM), lambda h, i: (0, k0 + h)),
            pl.BlockSpec((s, HEAD_DIM), lambda h, i: (0, v0 + h)),
        ],
        out_specs=pl.BlockSpec((tq, qw), lambda h, i: (i, h)),
        compiler_params=_params("parallel", "arbitrary"),
        name="window_attention",
    )(sink.astype(F32), qkv, qkv, qkv)


def _s5_discretize(a_re, a_im, log_dt, b_re, b_im):
    dt = jnp.exp(log_dt)[:, None]
    mag = jnp.exp(dt * a_re)
    lr, li = mag * jnp.cos(dt * a_im), mag * jnp.sin(dt * a_im)
    den = a_re * a_re + a_im * a_im
    nr, ni = lr - 1.0, li
    fr = (nr * a_re + ni * a_im) / den
    fi = (ni * a_re - nr * a_im) / den
    bbr = fr[..., None] * b_re - fi[..., None] * b_im
    bbi = fr[..., None] * b_im + fi[..., None] * b_re
    return lr, li, bbr, bbi


def _s5_block_weights(lr, li, bbr, bbi, c_re, c_im, gpc):
    g, n, c = bbr.shape
    n_k = g // gpc
    eye = jnp.eye(gpc, dtype=F32)

    def w_in(bb):
        t = bb.reshape(n_k, gpc, n, c)
        return jnp.einsum('kgnc,gh->kgchn', t, eye).reshape(n_k, gpc * c, gpc * n)

    def w_out(cc):
        t = cc.reshape(n_k, gpc, c, n)
        return jnp.einsum('kgcn,gh->khngc', t, eye).reshape(n_k, gpc * n, gpc * c)

    win = jnp.concatenate([w_in(bbr), w_in(bbi)], axis=2)
    wout = jnp.concatenate([w_out(c_re), -w_out(c_im)], axis=1)
    lam = jnp.stack([lr.reshape(n_k, gpc * n), li.reshape(n_k, gpc * n)], axis=1)
    return win, wout, lam


def _s5_kernel(uf_ref, ub_ref, win_ref, wout_ref, lam_ref, yf_ref, yb_ref,
               xf_s, xb_s, st_s, *, tc, ns, cw, nsub):
    i = pl.program_id(1)

    @pl.when(i == 0)
    def _():
        st_s[...] = jnp.zeros_like(st_s)

    for c in range(nsub):
        cols = slice(c * cw, (c + 1) * cw)
        xf_s[c] = jnp.dot(uf_ref[:, cols].astype(BF16), win_ref[0, c].astype(BF16),
                          preferred_element_type=F32)
        xb_s[c] = jnp.dot(ub_ref[:, cols].astype(BF16), win_ref[1, c].astype(BF16),
                          preferred_element_type=F32)
    lam = [[(lam_ref[d, c, 0:1, :], lam_ref[d, c, 1:2, :]) for c in range(nsub)]
           for d in range(2)]

    def step(x_s, c, row, lr, li, sr, si):
        nr = lr * sr - li * si + x_s[c, pl.ds(row, 1), 0:ns]
        ni = lr * si + li * sr + x_s[c, pl.ds(row, 1), ns:2 * ns]
        x_s[c, pl.ds(row, 1), 0:ns] = nr
        x_s[c, pl.ds(row, 1), ns:2 * ns] = ni
        return nr, ni

    def body(r, carry):
        out = []
        for c in range(nsub):
            srf, sif, srb, sib = carry[4 * c:4 * c + 4]
            srf, sif = step(xf_s, c, r, *lam[0][c], srf, sif)
            srb, sib = step(xb_s, c, tc - 1 - r, *lam[1][c], srb, sib)
            out += [srf, sif, srb, sib]
        return tuple(out)

    init = tuple(st_s[j:j + 1, :] for j in range(4 * nsub))
    final = lax.fori_loop(0, tc, body, init, unroll=8)
    for j in range(4 * nsub):
        st_s[j:j + 1, :] = final[j]
    for c in range(nsub):
        cols = slice(c * cw, (c + 1) * cw)
        yf_ref[:, cols] = jnp.dot(xf_s[c].astype(BF16), wout_ref[0, c].astype(BF16),
                                  preferred_element_type=F32)
        yb_ref[:, cols] = jnp.dot(xb_s[c].astype(BF16), wout_ref[1, c].astype(BF16),
                                  preferred_element_type=F32)


def _s5_scan(u, win, wout, lam, *, tc=256, nsub=2):
    s, w = u.shape
    _, n_k, cw, ns2 = win.shape
    ns = ns2 // 2
    tc = min(tc, s)
    n_t = s // tc
    nsub = min(nsub, n_k)
    bw = nsub * cw
    kern = functools.partial(_s5_kernel, tc=tc, ns=ns, cw=cw, nsub=nsub)
    return pl.pallas_call(
        kern,
        out_shape=(jax.ShapeDtypeStruct((s, w), F32), jax.ShapeDtypeStruct((s, w), F32)),
        grid=(n_k // nsub, n_t),
        in_specs=[
            pl.BlockSpec((tc, bw), lambda k, i: (i, k)),
            pl.BlockSpec((tc, bw), lambda k, i: (n_t - 1 - i, k)),
            pl.BlockSpec((2, nsub, cw, ns2), lambda k, i: (0, k, 0, 0)),
            pl.BlockSpec((2, nsub, ns2, cw), lambda k, i: (0, k, 0, 0)),
            pl.BlockSpec((2, nsub, 2, ns), lambda k, i: (0, k, 0, 0)),
        ],
        out_specs=(pl.BlockSpec((tc, bw), lambda k, i: (i, k)),
                   pl.BlockSpec((tc, bw), lambda k, i: (n_t - 1 - i, k))),
        scratch_shapes=[pltpu.VMEM((nsub, tc, ns2), F32), pltpu.VMEM((nsub, tc, ns2), F32),
                        pltpu.VMEM((4 * nsub, ns), F32)],
        compiler_params=_params("parallel", "arbitrary"),
        name="s5_scan",
    )(u, u, win, wout, lam)


def _pro_s5_glu(rows, vecs):
    yf, yb, u = rows
    return _gelu_tanh(yf + yb + vecs[0] * u)


def _epi_glu(acc, epi_refs, o_ref, j, a32_s):
    o_ref[...] = (a32_s[...] * _sigmoid(acc)).astype(o_ref.dtype)


def _route(x, g, w_r):
    h = _rms(x, g)
    logits = jnp.dot(h.astype(BF16), w_r, preferred_element_type=F32)
    tm = logits.shape[0]
    lane = lax.broadcasted_iota(jnp.int32, (tm, LANES), 1)
    big = jnp.int32(LANES)
    ninf = -jnp.inf
    gl = jnp.where(lane < MOE_GROUPS, logits, ninf)
    gm = jnp.max(gl, axis=-1, keepdims=True)
    ge = jnp.exp(gl - gm)
    g_prob = ge / jnp.sum(ge, axis=-1, keepdims=True)
    g_p = jnp.max(g_prob, axis=-1, keepdims=True)
    g_idx = jnp.min(jnp.where(g_prob == g_p, lane, big), axis=-1, keepdims=True)
    lo = MOE_GROUPS + g_idx * EXPERTS_PER_GROUP
    in_grp = (lane >= lo) & (lane < lo + EXPERTS_PER_GROUP)
    el = jnp.where(in_grp, logits, ninf)
    em = jnp.max(el, axis=-1, keepdims=True)
    ee = jnp.exp(el - em)
    e_prob = jnp.where(in_grp, ee / jnp.sum(ee, axis=-1, keepdims=True), -1.0)
    p1 = jnp.max(e_prob, axis=-1, keepdims=True)
    i1 = jnp.min(jnp.where(e_prob == p1, lane, big), axis=-1, keepdims=True)
    rest = jnp.where(lane == i1, -1.0, e_prob)
    p2 = jnp.max(rest, axis=-1, keepdims=True)
    i2 = jnp.min(jnp.where(rest == p2, lane, big), axis=-1, keepdims=True)
    denom = p1 + p2
    w1 = g_p * (p1 / denom)
    w2 = g_p * (p2 / denom)
    idx = jnp.where(lane == 0, i1 - MOE_GROUPS, jnp.where(lane == 1, i2 - MOE_GROUPS, 0))
    wt = jnp.where(lane == 0, w1, jnp.where(lane == 1, w2, 0.0))
    return h, idx, wt


def _pack_bf16_pairs(h):
    half = h.shape[1] // 2
    lo = pltpu.bitcast(h[:, :half].astype(BF16).astype(F32), jnp.uint32)
    hi = pltpu.bitcast(h[:, half:].astype(BF16).astype(F32), jnp.uint32)
    return (lo >> 16) | (hi & jnp.uint32(0xFFFF0000))


def _unpack_bf16_pairs(w):
    lo = pltpu.bitcast(w << 16, F32)
    hi = pltpu.bitcast(w & jnp.uint32(0xFFFF0000), F32)
    return jnp.concatenate([lo, hi], axis=1).astype(BF16)


def _router_kernel(x_ref, g_ref, w_ref, h_ref, idx_ref, wt_ref):
    h, idx, wt = _route(x_ref[...], g_ref[...], w_ref[...])
    h_ref[...] = _pack_bf16_pairs(h)
    idx_ref[...] = idx
    wt_ref[...] = wt


def _router(x, g, w_group, w_expert, *, tm=512):
    s, d = x.shape
    tm = min(tm, s)
    w_r = jnp.concatenate(
        [w_group, w_expert, jnp.zeros((d, LANES - MOE_GROUPS - N_EXPERTS), F32)], axis=1)
    return pl.pallas_call(
        _router_kernel,
        out_shape=(jax.ShapeDtypeStruct((s, d // 2), jnp.uint32),
                   jax.ShapeDtypeStruct((s, LANES), jnp.int32),
                   jax.ShapeDtypeStruct((s, LANES), F32)),
        grid=(s // tm,),
        in_specs=[pl.BlockSpec((tm, d), lambda i: (i, 0)), pl.BlockSpec((1, d), lambda i: (0, 0)),
                  pl.BlockSpec((d, LANES), lambda i: (0, 0))],
        out_specs=(pl.BlockSpec((tm, d // 2), lambda i: (i, 0)),
                   pl.BlockSpec((tm, LANES), lambda i: (i, 0)),
                   pl.BlockSpec((tm, LANES), lambda i: (i, 0))),
        compiler_params=_params("parallel"),
        name="moe_router",
    )(x, g.reshape(1, d), w_r.astype(BF16))


def _rank_kernel(idx_ref, dest_ref, cnt_ref, tot_s, pst_s, run_s, *, tm):
    ph = pl.program_id(0)
    i = pl.program_id(1)
    lane = lax.broadcasted_iota(jnp.int32, (tm, LANES), 1)
    idx = idx_ref[...]
    oh0 = (lane == idx[:, 0:1]).astype(F32)
    oh1 = (lane == idx[:, 1:2]).astype(F32)
    c = oh0 + oh1
    csum = jnp.sum(c, axis=0, keepdims=True)

    @pl.when(jnp.logical_and(ph == 0, i == 0))
    def _():
        tot_s[...] = jnp.zeros_like(tot_s)

    @pl.when(ph == 0)
    def _():
        tot_s[...] += csum

    @pl.when(jnp.logical_and(ph == 1, i == 0))
    def _():
        counts = tot_s[...]
        nblk = jnp.floor((counts + (EXPERT_BLOCK - 1)) * (1.0 / EXPERT_BLOCK))
        r = lax.broadcasted_iota(jnp.int32, (LANES, LANES), 0)
        cc = lax.broadcasted_iota(jnp.int32, (LANES, LANES), 1)
        upper = (r < cc).astype(F32)
        excl = jnp.dot(jnp.broadcast_to(nblk, (SUBLANES, LANES)), upper,
                       preferred_element_type=F32, precision=lax.Precision.HIGHEST)
        pst_s[...] = excl[0:1, :] * EXPERT_BLOCK
        run_s[...] = jnp.zeros_like(run_s)
        cnt_ref[...] = jnp.broadcast_to(counts, cnt_ref.shape)

    @pl.when(ph == 1)
    def _():
        rr = lax.broadcasted_iota(jnp.int32, (tm, tm), 0)
        cr = lax.broadcasted_iota(jnp.int32, (tm, tm), 1)
        lower = (rr > cr).astype(BF16)
        before = jnp.dot(lower, c.astype(BF16), preferred_element_type=F32) + run_s[...]
        base = pst_s[...] + before
        d0 = jnp.sum(oh0 * base, axis=-1, keepdims=True)
        d1 = jnp.sum(oh1 * (base + oh0), axis=-1, keepdims=True)
        dest_ref[...] = jnp.where(lane == 0, d0, jnp.where(lane == 1, d1, 0.0)).astype(jnp.int32)
        run_s[...] += csum


def _dispatch_rank(idx, *, tm=512):
    s = idx.shape[0]
    tm = min(tm, s)
    kern = functools.partial(_rank_kernel, tm=tm)
    return pl.pallas_call(
        kern,
        out_shape=(jax.ShapeDtypeStruct((s, LANES), jnp.int32),
                   jax.ShapeDtypeStruct((SUBLANES, LANES), F32)),
        grid=(2, s // tm),
        in_specs=[pl.BlockSpec((tm, LANES), lambda ph, i: (i, 0))],
        out_specs=(pl.BlockSpec((tm, LANES), lambda ph, i: (i * ph, 0)),
                   pl.BlockSpec((SUBLANES, LANES), lambda ph, i: (0, 0))),
        scratch_shapes=[pltpu.VMEM((1, LANES), F32) for _ in range(3)],
        compiler_params=_params("arbitrary", "arbitrary"),
        name="moe_rank",
    )(idx)


def _block_experts(counts, n_blk):
    cnt = counts[0, :N_EXPERTS].astype(jnp.int32)
    padded = ((cnt + EXPERT_BLOCK - 1) // EXPERT_BLOCK) * EXPERT_BLOCK
    pends = jnp.cumsum(padded)
    blk_start = jnp.arange(n_blk, dtype=jnp.int32) * EXPERT_BLOCK
    owner = jnp.sum((pends[None, :] <= blk_start[:, None]).astype(jnp.int32), axis=1)
    ids = jnp.arange(N_EXPERTS, dtype=jnp.int32)
    last_e = jnp.max(jnp.where(cnt > 0, ids, 0))
    n_used = (pends[-1] // EXPERT_BLOCK).reshape(1)
    blk_e = jnp.minimum(owner, last_e).astype(jnp.int32)
    later = jnp.logical_and(ids[None, :] > ids[:, None], cnt[None, :] > 0)
    nxt = jnp.min(jnp.where(later, ids[None, :], N_EXPERTS), axis=1)
    nxt = jnp.where(nxt == N_EXPERTS, ids, nxt).astype(jnp.int32)
    nxt2 = nxt[nxt]
    parity = ((jnp.cumsum((cnt > 0).astype(jnp.int32)) - 1) % 2).astype(jnp.int32)
    pad_start = (pends - padded + cnt).astype(jnp.int32)
    pad_len = (padded - cnt).astype(jnp.int32)
    tables = (blk_e, nxt[blk_e], nxt2[blk_e], parity[blk_e], n_used.astype(jnp.int32))
    return tables, pad_start, pad_len


def _scatter_kernel(dest_ref, ps_ref, pl_ref, nu_ref, h_ref, xs_hbm, h_s, z_s, sem, zsem,
                    *, tb, n_b, n_blk):
    b = pl.program_id(0)
    slot = b % 2
    n = tb * TOP_K
    bits = [1 << k for k in reversed(range(3, (EXPERT_BLOCK - 1).bit_length()))]

    def zero_fill(action):
        def pad_run(e, _):
            start = ps_ref[e]
            head = (-start) & (SUBLANES - 1)
            for r in range(SUBLANES - 1):
                @pl.when(r < head)
                def _():
                    action(pltpu.make_async_copy(z_s.at[pl.ds(0, 1), :],
                                                 xs_hbm.at[pl.ds(start + r, 1), :], zsem.at[0]))
            astart = start + head
            length = pl_ref[e] - head
            for bit in bits:
                @pl.when((length & bit) != 0)
                def _():
                    off = pl.multiple_of(astart + (length & ~(2 * bit - 1)), SUBLANES)
                    action(pltpu.make_async_copy(z_s.at[pl.ds(0, bit), :],
                                                 xs_hbm.at[pl.ds(off, bit), :], zsem.at[0]))
            return 0

        def unused_block(bb, _):
            row0 = pl.multiple_of(bb * EXPERT_BLOCK, EXPERT_BLOCK)
            action(pltpu.make_async_copy(
                z_s, xs_hbm.at[pl.ds(row0, EXPERT_BLOCK), :], zsem.at[0]))
            return 0

        lax.fori_loop(0, N_EXPERTS, pad_run, 0)
        lax.fori_loop(nu_ref[0], n_blk, unused_block, 0)

    @pl.when(b == 0)
    def _():
        z_s[...] = jnp.zeros_like(z_s)
        zero_fill(lambda c: c.start())

    def copy(r, row, sl):
        return pltpu.make_async_copy(h_s.at[sl, pl.ds(r, 1), :], xs_hbm.at[pl.ds(row, 1), :],
                                     sem.at[sl])

    def wait_all(sl):
        def body(r, _):
            copy(0, 0, sl).wait()
            return 0
        lax.fori_loop(0, n, body, 0, unroll=8)

    h_s[slot] = h_ref[...]
    for r in range(tb):
        tok = b * tb + r
        for k in range(TOP_K):
            copy(r, dest_ref[tok * TOP_K + k], slot).start(priority=k % 2)

    @pl.when(b > 0)
    def _():
        wait_all(1 - slot)

    @pl.when(b == n_b - 1)
    def _():
        wait_all(slot)
        zero_fill(lambda c: c.wait())


def _dispatch_scatter(h, dest_flat, pad_start, pad_len, n_used, n_rows, *, tb=256):
    s, d = h.shape
    tb = min(tb, s)
    n_b = s // tb
    kern = functools.partial(_scatter_kernel, tb=tb, n_b=n_b, n_blk=n_rows // EXPERT_BLOCK)
    grid_spec = pltpu.PrefetchScalarGridSpec(
        num_scalar_prefetch=4,
        grid=(n_b,),
        in_specs=[pl.BlockSpec((tb, d), lambda b, *_: (b, 0))],
        out_specs=pl.BlockSpec(memory_space=pl.ANY),
        scratch_shapes=[pltpu.VMEM((2, tb, d), h.dtype), pltpu.VMEM((EXPERT_BLOCK, d), h.dtype),
                        pltpu.SemaphoreType.DMA((2,)), pltpu.SemaphoreType.DMA((1,))],
    )
    return pl.pallas_call(
        kern,
        out_shape=jax.ShapeDtypeStruct((n_rows, d), h.dtype),
        grid_spec=grid_spec,
        compiler_params=_params("arbitrary"),
        name="moe_scatter",
    )(dest_flat, pad_start, pad_len, n_used, h)


def _row_copy(src_hbm, src_row, dst_buf, slot, dst_row, sem):
    return pltpu.make_async_copy(src_hbm.at[pl.ds(src_row, 1), :],
                                 dst_buf.at[slot, pl.ds(dst_row, 1), :], sem.at[slot])


def _gather_start(idx_ref, base, n, src_hbm, dst_buf, slot, sem):
    for r in range(n):
        _row_copy(src_hbm, idx_ref[base + r], dst_buf, slot, r, sem).start(priority=r % 2)


def _gather_wait(n, src_hbm, dst_buf, slot, sem):
    def body(r, _):
        _row_copy(src_hbm, 0, dst_buf, slot, r, sem).wait()
        return 0
    lax.fori_loop(0, n, body, 0, unroll=8)


def _expert_kernel(be_ref, nx_ref, nx2_ref, par_ref, nu_ref, x_ref, wg_hbm, wu_hbm, wd_hbm, y_ref,
                   wg_f, wu_f, wd_f, wg_s, wu_s, wd_s, sem, *, layer):
    b = pl.program_id(0)
    e = be_ref[b]
    slot = par_ref[b]

    def fetch(expert, sl):
        return (pltpu.make_async_copy(wg_hbm.at[layer, expert], wg_f.at[sl], sem.at[sl, 0]),
                pltpu.make_async_copy(wu_hbm.at[layer, expert], wu_f.at[sl], sem.at[sl, 1]),
                pltpu.make_async_copy(wd_hbm.at[layer, expert], wd_f.at[sl], sem.at[sl, 2]))

    @pl.when(b == 0)
    def _():
        for c in fetch(e, slot):
            c.start()

        @pl.when(nx_ref[b] != e)
        def _():
            for c in fetch(nx_ref[b], 1 - slot):
                c.start()

    changed = jnp.logical_or(b == 0, e != be_ref[jnp.maximum(b - 1, 0)])

    @pl.when(changed)
    def _():
        for c in fetch(e, slot):
            c.wait()
        wg_s[...] = wg_f[slot].astype(BF16)
        wu_s[...] = wu_f[slot].astype(BF16)
        wd_s[...] = wd_f[slot].astype(BF16)

        @pl.when(nx2_ref[b] != nx_ref[b])
        def _():
            for c in fetch(nx2_ref[b], slot):
                c.start()

    @pl.when(b >= nu_ref[0])
    def _():
        y_ref[...] = jnp.zeros_like(y_ref)

    @pl.when(b < nu_ref[0])
    def _():
        x = _unpack_bf16_pairs(x_ref[...])
        hg = jnp.dot(x, wg_s[...], preferred_element_type=F32)
        hu = jnp.dot(x, wu_s[...], preferred_element_type=F32)
        hdn = (hg * _sigmoid(hg)) * hu
        y_ref[...] = jnp.dot(hdn.astype(BF16), wd_s[...], preferred_element_type=F32)


def _expert_mlp(xs, blk_e, blk_next, blk_next2, blk_par, n_used, layer, w_gate, w_up, w_down):
    n_rows, dp = xs.shape
    d, de = w_gate.shape[2], w_gate.shape[3]
    grid_spec = pltpu.PrefetchScalarGridSpec(
        num_scalar_prefetch=5,
        grid=(n_rows // EXPERT_BLOCK,),
        in_specs=[
            pl.BlockSpec((EXPERT_BLOCK, dp),
                         lambda b, be, nx, nx2, par, nu: (jnp.minimum(b, nu[0] - 1), 0)),
            pl.BlockSpec(memory_space=pl.ANY),
            pl.BlockSpec(memory_space=pl.ANY),
            pl.BlockSpec(memory_space=pl.ANY),
        ],
        out_specs=pl.BlockSpec((EXPERT_BLOCK, d), lambda b, be, nx, nx2, par, nu: (b, 0)),
        scratch_shapes=[pltpu.VMEM((2, d, de), F32), pltpu.VMEM((2, d, de), F32),
                        pltpu.VMEM((2, de, d), F32), pltpu.VMEM((d, de), BF16),
                        pltpu.VMEM((d, de), BF16), pltpu.VMEM((de, d), BF16),
                        pltpu.SemaphoreType.DMA((2, 3))],
    )
    return pl.pallas_call(
        functools.partial(_expert_kernel, layer=layer),
        out_shape=jax.ShapeDtypeStruct((n_rows, d), F32),
        grid_spec=grid_spec,
        compiler_params=_params("arbitrary"),
        name="moe_experts",
    )(blk_e, blk_next, blk_next2, blk_par, n_used, xs, w_gate, w_up, w_down)


def _combine_kernel(dest_ref, x_ref, wt_ref, ys_hbm, g_ref, o_ref, ybuf, sem, *, tb, n_b, final):
    b = pl.program_id(0)
    slot = b % 2
    n = tb * TOP_K

    @pl.when(b == 0)
    def _():
        _gather_start(dest_ref, 0, n, ys_hbm, ybuf, 0, sem)

    _gather_wait(n, ys_hbm, ybuf, slot, sem)
    _gather_start(dest_ref, (b + 1) * n, n, ys_hbm, ybuf, 1 - slot, sem)
    y0 = ybuf[slot, 0:tb, :]
    y1 = ybuf[slot, tb:2 * tb, :]
    wt = wt_ref[...]
    out = x_ref[...] + (y0 * wt[:, 0:1] + y1 * wt[:, 1:2])
    if final:
        out = _rms(out, g_ref[...])
    o_ref[...] = out

    @pl.when(b == n_b - 1)
    def _():
        _gather_wait(n, ys_hbm, ybuf, 1 - slot, sem)


def _moe_combine(x, wt, ys, dest, g_final, *, final, tb=256):
    s, d = x.shape
    tb = min(tb, s)
    n_b = s // tb
    dest = dest.reshape(n_b, tb, TOP_K).transpose(0, 2, 1).reshape(-1)
    dest = jnp.concatenate([dest, jnp.zeros((tb * TOP_K,), jnp.int32)])
    kern = functools.partial(_combine_kernel, tb=tb, n_b=n_b, final=final)
    grid_spec = pltpu.PrefetchScalarGridSpec(
        num_scalar_prefetch=1,
        grid=(n_b,),
        in_specs=[
            pl.BlockSpec((tb, d), lambda b, dr: (b, 0)),
            pl.BlockSpec((tb, LANES), lambda b, dr: (b, 0)),
            pl.BlockSpec(memory_space=pl.ANY),
            pl.BlockSpec((1, d), lambda b, dr: (0, 0)),
        ],
        out_specs=pl.BlockSpec((tb, d), lambda b, dr: (b, 0)),
        scratch_shapes=[pltpu.VMEM((2, tb * TOP_K, d), F32), pltpu.SemaphoreType.DMA((2,))],
    )
    return pl.pallas_call(
        kern,
        out_shape=jax.ShapeDtypeStruct((s, d), F32),
        grid_spec=grid_spec,
        compiler_params=_params("arbitrary"),
        name="moe_combine",
    )(dest, x, wt, ys, g_final.reshape(1, d))


def _hier_moe(x, ln_g, w_group, w_expert, layer, w_gate, w_up, w_down, g_final, *, final):
    s = x.shape[0]
    h, idx, wt = _router(x, ln_g, w_group, w_expert)
    dest, counts = _dispatch_rank(idx)
    dest = dest[:, :TOP_K].reshape(-1)
    n_rows = -(-(s * TOP_K + N_EXPERTS * (EXPERT_BLOCK - 1)) // EXPERT_BLOCK) * EXPERT_BLOCK
    tables, pad_start, pad_len = _block_experts(counts, n_rows // EXPERT_BLOCK)
    xs = _dispatch_scatter(h, dest, pad_start, pad_len, tables[-1], n_rows)
    ys = _expert_mlp(xs, *tables, layer, w_gate, w_up, w_down)
    return _moe_combine(x, wt, ys, dest, g_final, final=final)


def _rglru_layer(x, ln_g, w_in, conv_w, conv_b, w_a, w_x, b_a, b_x, lam, w_out):
    d = x.shape[1]
    c = conv_w.shape[1]
    proj = _norm_matmul(x, ln_g, w_in, name="lru_in_proj")
    yf, yb = _rglru_scan(proj, conv_w, conv_b, w_a, w_x, b_a, b_x, lam)
    return _matmul_residual([(yf, c, 0), (yb, c, 0), (proj, c, 0)], [], w_out, x, _pro_rglru_out,
                            k_dim=c, tm=512, name="lru_out_proj")


def _diff_layer(x, ln_g, rope, w_qkv, lq1, lk1, lq2, lk2, subln_g, w_out, lambda_init):
    qk_cols = 2 * DIFF_HEADS * 2 * HEAD_DIM
    qkv = _norm_matmul(x, ln_g, w_qkv, rope=rope, n_rope_cols=qk_cols, n_q_cols=qk_cols // 2,
                       q_scale=HEAD_DIM ** -0.5 * math.log2(math.e), out_dtype=BF16,
                       name="diff_qkv_proj")
    o = _diff_attention(qkv, lq1, lk1, lq2, lk2, subln_g, lambda_init)
    return _matmul_residual([(o, o.shape[1], 0)], [], w_out, x, _pro_identity,
                            k_dim=o.shape[1], name="diff_out_proj")


def _window_layer(x, ln_g, rope, w_qkv, sink, w_out):
    qk_cols = (WIN_Q_HEADS + WIN_KV_HEADS) * HEAD_DIM
    qkv = _norm_matmul(x, ln_g, w_qkv, rope=rope, n_rope_cols=qk_cols,
                       n_q_cols=WIN_Q_HEADS * HEAD_DIM,
                       q_scale=HEAD_DIM ** -0.5 * math.log2(math.e), out_dtype=BF16, tn=512,
                       name="win_qkv_proj")
    o = _window_attention(qkv, sink)
    return _matmul_residual([(o, o.shape[1], 0)], [], w_out, x, _pro_identity,
                            k_dim=o.shape[1], name="win_out_proj")


def _s5_layer(x, ln_g, w_in, a_re, a_im, log_dt, b_re, b_im, c_re, c_im, d_skip, w_glu, w_out,
              *, gpc=16):
    u = _norm_matmul(x, ln_g, w_in, name="s5_in_proj")
    w = u.shape[1]
    wins, wouts, lams = [], [], []
    for dd in range(2):
        lr, li, bbr, bbi = _s5_discretize(a_re[dd], a_im[dd], log_dt[dd], b_re[dd], b_im[dd])
        wi, wo, lm = _s5_block_weights(lr, li, bbr, bbi, c_re[dd], c_im[dd], gpc)
        wins.append(wi)
        wouts.append(wo)
        lams.append(lm)
    yf, yb = _s5_scan(u, jnp.stack(wins).astype(BF16), jnp.stack(wouts).astype(BF16),
                      jnp.stack(lams))
    z = _fused_mm([(yf, w, 0), (yb, w, 0), (u, w, 0)], [d_skip.reshape(1, w)], w_glu, [],
                  _pro_s5_glu, _epi_glu, out_dtype=BF16, tm=512, tn=w, k_dim=w, keep_f32=True,
                  name="s5_glu")
    return _matmul_residual([(z, w, 0)], [], w_out, x, _pro_identity, k_dim=w, name="s5_out_proj")


def kernel(x, positions, ln_mix, ln_ffn, ln_final, lru_w_in, lru_conv_w, lru_conv_b, lru_w_a, lru_w_x, lru_b_a, lru_b_x, lru_lambda, lru_w_out, diff_w_qkv, diff_lq1, diff_lk1, diff_lq2, diff_lk2, diff_subln, diff_w_out, win_w_qkv, win_sink, win_w_out, s5_w_in, s5_a_re, s5_a_im, s5_log_dt, s5_b_re, s5_b_im, s5_c_re, s5_c_im, s5_d, s5_w_glu, s5_w_out, moe_w_group, moe_w_expert, moe_w_gate, moe_w_up, moe_w_down):
    batch, s, d = x.shape
    depth = ln_mix.shape[0]
    outs = []
    for b in range(batch):
        xb = x[b]
        rope = _rope_tables(positions[b])
        for i in range(depth):
            kind, j = i % 4, i // 4
            if kind == 0:
                xb = _rglru_layer(xb, ln_mix[i], lru_w_in[j], lru_conv_w[j], lru_conv_b[j],
                                  lru_w_a[j], lru_w_x[j], lru_b_a[j], lru_b_x[j], lru_lambda[j],
                                  lru_w_out[j])
            elif kind == 1:
                xb = _diff_layer(xb, ln_mix[i], rope, diff_w_qkv[j], diff_lq1[j], diff_lk1[j],
                                 diff_lq2[j], diff_lk2[j], diff_subln[j], diff_w_out[j],
                                 0.8 - 0.6 * math.exp(-0.3 * i))
            elif kind == 2:
                xb = _window_layer(xb, ln_mix[i], rope, win_w_qkv[j], win_sink[j], win_w_out[j])
            else:
                xb = _s5_layer(xb, ln_mix[i], s5_w_in[j], s5_a_re[j], s5_a_im[j], s5_log_dt[j],
                               s5_b_re[j], s5_b_im[j], s5_c_re[j], s5_c_im[j], s5_d[j],
                               s5_w_glu[j], s5_w_out[j])
            xb = _hier_moe(xb, ln_ffn[i], moe_w_group[i], moe_w_expert[i], i, moe_w_gate,
                           moe_w_up, moe_w_down, ln_final, final=(i == depth - 1))
        outs.append(xb)
    return jnp.stack(outs)
```
